```python
import math
import jax, jax.numpy as jnp
from jax import lax
import numpy as np

D_MODEL = 1024
BATCH = 4
SEQ = 4096
DEPTH = 2

N_MIXERS = 2
PLE_DIM = 256
FOX_HEADS = 16
FOX_HEAD_DIM = D_MODEL // FOX_HEADS
FOX_BLOCK = 128
FOX_IN = 3 * D_MODEL + FOX_HEADS
RET_HEADS = 4
RET_KEY_DIM = D_MODEL // RET_HEADS
RET_VAL_DIM = 2 * D_MODEL // RET_HEADS
RET_CHUNK = 128
RET_IN = 2 * D_MODEL + 2 * (2 * D_MODEL)
ROPE_BASE = 10000.0
N_GROUPS = 4
EXPERTS_PER_GROUP = 4
N_EXPERTS = N_GROUPS * EXPERTS_PER_GROUP
D_EXPERT = 512
TOP_K_IN_GROUP = 2
DEEPNORM_ALPHA = (2.0 * DEPTH) ** 0.25
DEEPNORM_BETA = (8.0 * DEPTH) ** -0.25
LN_EPS = 1e-5
N_FOX_LAYERS = (DEPTH + 1) // 2
N_RET_LAYERS = DEPTH // 2

kernel_name = "fox_retnet_interleaved_hmoe_deepnorm"


def layer_norm(x, g, b):
    xf = x.astype(jnp.float32)
    mu = jnp.mean(xf, axis=-1, keepdims=True)
    var = jnp.mean(jnp.square(xf - mu), axis=-1, keepdims=True)
    y = (xf - mu) * lax.rsqrt(var + LN_EPS)
    return (y * g.astype(jnp.float32) + b.astype(jnp.float32)).astype(x.dtype)


def fox_attention(x, w_in, b_f, w_out):
    B, S, _ = x.shape
    H, dh = FOX_HEADS, FOX_HEAD_DIM
    proj = x @ w_in
    q, k, v, f = jnp.split(proj, [D_MODEL, 2 * D_MODEL, 3 * D_MODEL], axis=-1)
    q = q.reshape(B, S, H, dh).transpose(0, 2, 1, 3)
    k = k.reshape(B, S, H, dh).transpose(0, 2, 1, 3)
    v = v.reshape(B, S, H, dh).transpose(0, 2, 1, 3)
    log_f = jax.nn.log_sigmoid((f + b_f).astype(jnp.float32))
    c = jnp.cumsum(log_f, axis=1).transpose(0, 2, 1)
    nb = S // FOX_BLOCK
    q_blocks = q.reshape(B, H, nb, FOX_BLOCK, dh).transpose(2, 0, 1, 3, 4)
    c_blocks = c.reshape(B, H, nb, FOX_BLOCK).transpose(2, 0, 1, 3)
    key_pos = jnp.arange(S)
    scale = FOX_HEAD_DIM ** -0.5

    def one_block(args):
        qb, cb, blk = args
        q_pos = blk * FOX_BLOCK + jnp.arange(FOX_BLOCK)
        s = jnp.einsum('bhqd,bhkd->bhqk', qb, k).astype(jnp.float32) * scale
        s = s + (cb[..., :, None] - c[:, :, None, :])
        s = jnp.where(key_pos[None, :] <= q_pos[:, None], s, -jnp.inf)
        pr = jax.nn.softmax(s, axis=-1).astype(v.dtype)
        return jnp.einsum('bhqk,bhkd->bhqd', pr, v)

    o = lax.map(one_block, (q_blocks, c_blocks, jnp.arange(nb)))
    o = o.transpose(1, 0, 3, 2, 4).reshape(B, S, D_MODEL)
    return o @ w_out


def rotary(x, positions):
    d = x.shape[-1]
    half = d // 2
    inv_freq = ROPE_BASE ** (-jnp.arange(0, d, 2, dtype=jnp.float32) / d)
    ang = positions.astype(jnp.float32)[..., None] * inv_freq
    cos = jnp.cos(ang)[:, :, None, :].astype(x.dtype)
    sin = jnp.sin(ang)[:, :, None, :].astype(x.dtype)
    x1, x2 = x[..., :half], x[..., half:]
    return jnp.concatenate([x1 * cos - x2 * sin, x2 * cos + x1 * sin], axis=-1)


def retention(x, positions, w_in, w_out):
    B, S, _ = x.shape
    H, dk, dv, C = RET_HEADS, RET_KEY_DIM, RET_VAL_DIM, RET_CHUNK
    nc = S // C
    proj = x @ w_in
    q, k, v, g = jnp.split(proj, [D_MODEL, 2 * D_MODEL, 4 * D_MODEL], axis=-1)
    q = rotary(q.reshape(B, S, H, dk), positions)
    k = rotary(k.reshape(B, S, H, dk), positions) * (dk ** -0.5)
    v = v.reshape(B, S, H, dv)

    def to_chunks(t, d):
        return t.reshape(B, nc, C, H, d).transpose(1, 0, 3, 2, 4).astype(jnp.float32)

    qc, kc, vc = to_chunks(q, dk), to_chunks(k, dk), to_chunks(v, dv)
    log_gamma = jnp.log(1.0 - 2.0 ** (-5.0 - jnp.arange(H, dtype=jnp.float32)))
    idx = jnp.arange(C, dtype=jnp.float32)
    diff = idx[:, None] - idx[None, :]
    decay_intra = jnp.where(diff >= 0, jnp.exp(jnp.maximum(diff, 0.0)[None] * log_gamma[:, None, None]), 0.0)
    q_decay = jnp.exp((idx + 1.0)[None, :] * log_gamma[:, None])
    k_decay = jnp.exp((C - 1.0 - idx)[None, :] * log_gamma[:, None])
    chunk_decay = jnp.exp(C * log_gamma)

    def step(state, inp):
        qi, ki, vi = inp
        scores = jnp.einsum('bhid,bhjd->bhij', qi, ki) * decay_intra[None]
        o = jnp.einsum('bhij,bhje->bhie', scores, vi)
        o = o + jnp.einsum('bhid,bhde->bhie', qi * q_decay[None, :, :, None], state)
        new_state = state * chunk_decay[None, :, None, None] + jnp.einsum(
            'bhjd,bhje->bhde', ki * k_decay[None, :, :, None], vi)
        return new_state, o

    state0 = jnp.zeros((B, H, dk, dv), jnp.float32)
    _, o = lax.scan(step, state0, (qc, kc, vc))
    o = o.transpose(1, 0, 3, 2, 4).reshape(B, S, H, dv)
    mu = jnp.mean(o, axis=-1, keepdims=True)
    var = jnp.mean(jnp.square(o - mu), axis=-1, keepdims=True)
    o = ((o - mu) * lax.rsqrt(var + LN_EPS)).reshape(B, S, 2 * D_MODEL).astype(x.dtype)
    return (jax.nn.silu(g) * o) @ w_out


def hierarchical_moe(x, w_group, b_group, w_router, b_router, w_gate, w_up, w_down):
    B, S, _ = x.shape
    xt = x.reshape(B * S, D_MODEL)
    xf = xt.astype(jnp.float32)
    pg = jax.nn.softmax(xf @ w_group.astype(jnp.float32) + b_group.astype(jnp.float32), axis=-1)
    g_val, g_idx = lax.top_k(pg, 1)
    el = (xf @ w_router.astype(jnp.float32) + b_router.astype(jnp.float32)).reshape(-1, N_GROUPS, EXPERTS_PER_GROUP)
    el = jnp.take_along_axis(el, g_idx[:, :, None], axis=1)[:, 0]
    pe = jax.nn.softmax(el, axis=-1)
    e_val, e_idx = lax.top_k(pe, TOP_K_IN_GROUP)
    e_val = e_val / jnp.sum(e_val, axis=-1, keepdims=True)
    weights = g_val * e_val
    expert_ids = g_idx * EXPERTS_PER_GROUP + e_idx
    combine = jnp.sum(jax.nn.one_hot(expert_ids, N_EXPERTS, dtype=jnp.float32) * weights[..., None], axis=1)
    combine = combine.astype(xt.dtype)
    y = jnp.zeros_like(xt)
    for e in range(N_EXPERTS):
        h = jax.nn.silu(xt @ w_gate[e]) * (xt @ w_up[e])
        y = y + combine[:, e:e + 1] * (h @ w_down[e])
    return y.reshape(B, S, D_MODEL)


def setup_inputs(seed: int = 0) -> dict:
    key = jax.random.key(seed)
    ks = jax.random.split(key, 24)
    nrm = jax.random.normal
    D = D_MODEL
    beta = DEEPNORM_BETA
    x = nrm(ks[0], (BATCH, SEQ, D), jnp.float32)
    p = nrm(ks[1], (DEPTH, BATCH, SEQ, PLE_DIM), jnp.float32)
    positions = jnp.broadcast_to(jnp.arange(SEQ, dtype=jnp.int32)[None, :], (BATCH, SEQ))
    fox_w_in = nrm(ks[2], (N_FOX_LAYERS, D, FOX_IN), jnp.float32) * D ** -0.5
    fox_w_in = fox_w_in.at[:, :, 2 * D:3 * D].multiply(beta)
    fox_b_f = jax.random.uniform(ks[3], (N_FOX_LAYERS, FOX_HEADS), jnp.float32, 1.0, 5.0)
    fox_w_out = nrm(ks[4], (N_FOX_LAYERS, D, D), jnp.float32) * D ** -0.5 * beta
    ret_w_in = nrm(ks[5], (N_RET_LAYERS, D, RET_IN), jnp.float32) * D ** -0.5
    ret_w_in = ret_w_in.at[:, :, 2 * D:4 * D].multiply(beta)
    ret_w_out = nrm(ks[6], (N_RET_LAYERS, 2 * D, D), jnp.float32) * (2 * D) ** -0.5 * beta
    ln1_g = 1.0 + 0.05 * nrm(ks[7], (DEPTH, D), jnp.float32)
    ln1_b = 0.02 * nrm(ks[8], (DEPTH, D), jnp.float32)
    ln2_g = 1.0 + 0.05 * nrm(ks[9], (DEPTH, D), jnp.float32)
    ln2_b = 0.02 * nrm(ks[10], (DEPTH, D), jnp.float32)
    moe_w_group = nrm(ks[11], (DEPTH, D, N_GROUPS), jnp.float32) * D ** -0.5
    moe_b_group = 0.01 * nrm(ks[12], (DEPTH, N_GROUPS), jnp.float32)
    moe_w_router = nrm(ks[13], (DEPTH, D, N_EXPERTS), jnp.float32) * D ** -0.5
    moe_b_router = 0.01 * nrm(ks[14], (DEPTH, N_EXPERTS), jnp.float32)
    moe_w_gate = nrm(ks[15], (DEPTH, N_EXPERTS, D, D_EXPERT), jnp.float32) * D ** -0.5
    moe_w_up = nrm(ks[16], (DEPTH, N_EXPERTS, D, D_EXPERT), jnp.float32) * D ** -0.5 * beta
    moe_w_down = nrm(ks[17], (DEPTH, N_EXPERTS, D_EXPERT, D), jnp.float32) * D_EXPERT ** -0.5 * beta
    ple_w_proj = nrm(ks[18], (DEPTH, PLE_DIM, D), jnp.float32) * PLE_DIM ** -0.5
    ple_w_gate = nrm(ks[19], (DEPTH, D, D), jnp.float32) * D ** -0.5
    ple_b_gate = 0.02 * nrm(ks[20], (DEPTH, D), jnp.float32)
    return {"x": x, "p": p, "positions": positions,
            "fox_w_in": fox_w_in, "fox_b_f": fox_b_f, "fox_w_out": fox_w_out,
            "ret_w_in": ret_w_in, "ret_w_out": ret_w_out,
            "ln1_g": ln1_g, "ln1_b": ln1_b, "ln2_g": ln2_g, "ln2_b": ln2_b,
            "moe_w_group": moe_w_group, "moe_b_group": moe_b_group,
            "moe_w_router": moe_w_router, "moe_b_router": moe_b_router,
            "moe_w_gate": moe_w_gate, "moe_w_up": moe_w_up, "moe_w_down": moe_w_down,
            "ple_w_proj": ple_w_proj, "ple_w_gate": ple_w_gate, "ple_b_gate": ple_b_gate}


def reference(x, p, positions, fox_w_in, fox_b_f, fox_w_out, ret_w_in, ret_w_out,
              ln1_g, ln1_b, ln2_g, ln2_b, moe_w_group, moe_b_group, moe_w_router, moe_b_router,
              moe_w_gate, moe_w_up, moe_w_down, ple_w_proj, ple_w_gate, ple_b_gate):
    for i in range(DEPTH):
        j = i // N_MIXERS
        if i % N_MIXERS == 0:
            h = fox_attention(x, fox_w_in[j], fox_b_f[j], fox_w_out[j])
        else:
            h = retention(x, positions, ret_w_in[j], ret_w_out[j])
        x = layer_norm(DEEPNORM_ALPHA * x + h, ln1_g[i], ln1_b[i])
        m = hierarchical_moe(x, moe_w_group[i], moe_b_group[i], moe_w_router[i], moe_b_router[i],
                             moe_w_gate[i], moe_w_up[i], moe_w_down[i])
        x = layer_norm(DEEPNORM_ALPHA * x + m, ln2_g[i], ln2_b[i])
        gate = jax.nn.sigmoid(x @ ple_w_gate[i] + ple_b_gate[i])
        x = x + gate * (p[i] @ ple_w_proj[i])
    return x
```

```python
import functools

import jax
import jax.numpy as jnp
from jax import lax
from jax.experimental import pallas as pl
from jax.experimental.pallas import tpu as pltpu

F32 = jnp.float32
BF16 = jnp.bfloat16

D_MODEL = 1024
PLE_DIM = 256
FOX_HEADS = 16
FOX_HEAD_DIM = D_MODEL // FOX_HEADS
RET_HEADS = 4
RET_KEY_DIM = D_MODEL // RET_HEADS
RET_VAL_DIM = 2 * D_MODEL // RET_HEADS
ROPE_BASE = 10000.0
N_GROUPS = 4
EXPERTS_PER_GROUP = 4
N_EXPERTS = N_GROUPS * EXPERTS_PER_GROUP
D_EXPERT = 512
DEPTH = 2
DEEPNORM_ALPHA = (2.0 * DEPTH) ** 0.25
LN_EPS = 1e-5

LANES = 128
EXT = LANES
ROW_W = D_MODEL + EXT
GIDX_LANE = 0
CW_LANE0 = 4
NEG_BIG = -1e30
VMEM_LIMIT = 56 * 1024 * 1024

TM_PROJ = 1024
TS_GATE = 512
TQ_ATTN = 256
TM_OUT = 256
TM_ROWS = 256
TM_MOE = 512
RET_CHUNK = 256


def _cparams(sem):
    return pltpu.CompilerParams(dimension_semantics=sem, vmem_limit_bytes=VMEM_LIMIT)


def _proj_kernel(a_ref, w_ref, o_ref):
    a = a_ref[...].astype(BF16)
    o_ref[...] = jnp.dot(a, w_ref[...], preferred_element_type=F32).astype(o_ref.dtype)


def _proj(a, w, tn):
    m, k = a.shape
    n = w.shape[1]
    tm = min(TM_PROJ, m)
    return pl.pallas_call(
        _proj_kernel,
        out_shape=jax.ShapeDtypeStruct((m, n), BF16),
        grid=(m // tm, n // tn),
        in_specs=[pl.BlockSpec((tm, k), lambda i, j: (i, 0)),
                  pl.BlockSpec((k, tn), lambda i, j: (0, j))],
        out_specs=pl.BlockSpec((tm, tn), lambda i, j: (i, j)),
        compiler_params=_cparams(("parallel", "arbitrary")),
        name="proj",
    )(a, w)


def _fgate_kernel(x_ref, w_ref, b_ref, c_ref, carry_ref):
    @pl.when(pl.program_id(1) == 0)
    def _():
        carry_ref[...] = jnp.zeros_like(carry_ref)

    ts = x_ref.shape[0]
    z = jnp.dot(x_ref[...], w_ref[...], preferred_element_type=F32,
                precision=lax.Precision.HIGHEST) + b_ref[...]
    logf = jnp.minimum(z, 0.0) - jnp.log1p(jnp.exp(-jnp.abs(z)))
    row = lax.broadcasted_iota(jnp.int32, (ts, ts), 0)
    col = lax.broadcasted_iota(jnp.int32, (ts, ts), 1)
    tri = (row >= col).astype(F32)
    c = jnp.dot(tri, logf, preferred_element_type=F32,
                precision=lax.Precision.HIGHEST) + carry_ref[...]
    c_ref[...] = c
    carry_ref[...] = c[ts - 1:ts, :]


def _fgate(x2d, w_f, b_f, batch, seq):
    ts = min(TS_GATE, seq)
    ns = seq // ts
    h = w_f.shape[1]
    return pl.pallas_call(
        _fgate_kernel,
        out_shape=jax.ShapeDtypeStruct((batch * seq, h), F32),
        grid=(batch, ns),
        in_specs=[pl.BlockSpec((ts, D_MODEL), lambda b, s: (b * ns + s, 0)),
                  pl.BlockSpec((D_MODEL, h), lambda b, s: (0, 0)),
                  pl.BlockSpec((1, h), lambda b, s: (0, 0))],
        out_specs=pl.BlockSpec((ts, h), lambda b, s: (b * ns + s, 0)),
        scratch_shapes=[pltpu.VMEM((1, h), F32)],
        compiler_params=_cparams(("parallel", "arbitrary")),
        name="fgate",
    )(x2d, w_f, b_f)


def _fox_attn_kernel(q_ref, k_ref, v_ref, cq_ref, ck_ref, o_ref, *, tq):
    qi = pl.program_id(2)
    q = q_ref[...]
    lane = lax.broadcasted_iota(jnp.int32, (tq, LANES), 1)
    row = lax.broadcasted_iota(jnp.int32, (tq, tq), 0)
    col = lax.broadcasted_iota(jnp.int32, (tq, tq), 1)
    causal = row >= col
    head_out = []
    for h in range(2):
        in_head = (lane < FOX_HEAD_DIM) if h == 0 else (lane >= FOX_HEAD_DIM)
        qh = jnp.where(in_head, q, jnp.zeros_like(q))
        cq = cq_ref[h]

        def step(kb, carry, masked, qh=qh, cq=cq, h=h):
            m, l, acc = carry
            ks = pl.multiple_of(kb * tq, tq)
            kblk = k_ref[pl.ds(ks, tq), :]
            s = lax.dot_general(qh, kblk, (((1,), (1,)), ((), ())),
                                preferred_element_type=F32)
            s = s + (cq - ck_ref[h, :, pl.ds(ks, tq)])
            if masked:
                s = jnp.where(causal, s, NEG_BIG)
            m_new = jnp.maximum(m, jnp.max(s, axis=1, keepdims=True))
            alpha = jnp.exp(m - m_new)
            p = jnp.exp(s - m_new)
            l_new = alpha * l + jnp.sum(p, axis=1, keepdims=True)
            pv = jnp.dot(p.astype(BF16), v_ref[pl.ds(ks, tq), :],
                         preferred_element_type=F32)
            return m_new, l_new, alpha * acc + pv

        init = (jnp.full((tq, 1), NEG_BIG, F32), jnp.zeros((tq, 1), F32),
                jnp.zeros((tq, LANES), F32))
        carry = lax.fori_loop(0, qi, functools.partial(step, masked=False), init)
        m, l, acc = step(qi, carry, True)
        head_out.append(acc / l)
    o_ref[...] = jnp.where(lane < FOX_HEAD_DIM, head_out[0], head_out[1]).astype(o_ref.dtype)


def _fox_attention(qkv, cq, ck, batch, seq):
    tq = min(TQ_ATTN, seq)
    nq = seq // tq
    hp = FOX_HEADS // 2
    return pl.pallas_call(
        functools.partial(_fox_attn_kernel, tq=tq),
        out_shape=jax.ShapeDtypeStruct((batch * seq, D_MODEL), BF16),
        grid=(batch, hp, nq),
        in_specs=[pl.BlockSpec((tq, LANES), lambda b, p, i: (b * nq + i, p)),
                  pl.BlockSpec((seq, LANES), lambda b, p, i: (b, hp + p)),
                  pl.BlockSpec((seq, LANES), lambda b, p, i: (b, 2 * hp + p)),
                  pl.BlockSpec((None, 2, tq, 1), lambda b, p, i: (b, p, i, 0)),
                  pl.BlockSpec((None, 2, 1, seq), lambda b, p, i: (b, p, 0, 0))],
        out_specs=pl.BlockSpec((tq, LANES), lambda b, p, i: (b * nq + i, p)),
        compiler_params=_cparams(("parallel", "parallel", "arbitrary")),
        name="fox_attn",
    )(qkv, qkv, qkv, cq, ck)


def _rope_kernel(pos_ref, freq_ref, cos_ref, sin_ref):
    ang = pos_ref[...] * freq_ref[...]
    cos_ref[...] = jnp.cos(ang)
    sin_ref[...] = jnp.sin(ang)


def _rope_tables(pos_f, inv_freq):
    t = pos_f.shape[0]
    half = inv_freq.shape[1]
    tm = min(1024, t)
    return pl.pallas_call(
        _rope_kernel,
        out_shape=(jax.ShapeDtypeStruct((t, half), F32), jax.ShapeDtypeStruct((t, half), F32)),
        grid=(t // tm,),
        in_specs=[pl.BlockSpec((tm, 1), lambda i: (i, 0)),
                  pl.BlockSpec((1, half), lambda i: (0, 0))],
        out_specs=(pl.BlockSpec((tm, half), lambda i: (i, 0)),
                   pl.BlockSpec((tm, half), lambda i: (i, 0))),
        compiler_params=_cparams(("parallel",)),
        name="rope_tables",
    )(pos_f, inv_freq)


def _ret_proj_kernel(a_ref, w_ref, cos_ref, sin_ref, o_ref):
    j = pl.program_id(1)
    a = a_ref[...].astype(BF16)
    acc = jnp.dot(a, w_ref[...], preferred_element_type=F32)

    @pl.when(j < 2)
    def _():
        c = cos_ref[...]
        s = sin_ref[...]
        half = RET_KEY_DIM // 2
        for h in range(RET_HEADS):
            x1 = acc[:, h * RET_KEY_DIM:h * RET_KEY_DIM + half]
            x2 = acc[:, h * RET_KEY_DIM + half:(h + 1) * RET_KEY_DIM]
            o_ref[:, h * RET_KEY_DIM:h * RET_KEY_DIM + half] = (x1 * c - x2 * s).astype(o_ref.dtype)
            o_ref[:, h * RET_KEY_DIM + half:(h + 1) * RET_KEY_DIM] = (x2 * c + x1 * s).astype(o_ref.dtype)

    @pl.when(j >= 2)
    def _():
        o_ref[...] = acc.astype(o_ref.dtype)


def _ret_proj(x2d, w, cos, sin):
    t = x2d.shape[0]
    n = w.shape[1]
    tm = min(TM_PROJ, t)
    tn = D_MODEL
    half = RET_KEY_DIM // 2
    return pl.pallas_call(
        _ret_proj_kernel,
        out_shape=jax.ShapeDtypeStruct((t, n), BF16),
        grid=(t // tm, n // tn),
        in_specs=[pl.BlockSpec((tm, D_MODEL), lambda i, j: (i, 0)),
                  pl.BlockSpec((D_MODEL, tn), lambda i, j: (0, j)),
                  pl.BlockSpec((tm, half), lambda i, j: (i, 0)),
                  pl.BlockSpec((tm, half), lambda i, j: (i, 0))],
        out_specs=pl.BlockSpec((tm, tn), lambda i, j: (i, j)),
        compiler_params=_cparams(("parallel", "arbitrary")),
        name="ret_proj",
    )(x2d, w, cos, sin)


def _retention_kernel(lg_ref, q_ref, k_ref, v_ref, g_ref, o_ref, state_ref, *, chunk):
    @pl.when(pl.program_id(2) == 0)
    def _():
        state_ref[...] = jnp.zeros_like(state_ref)

    lg = lg_ref[pl.program_id(1)]
    q = q_ref[...]
    k = k_ref[...]
    v = v_ref[...]
    row = lax.broadcasted_iota(jnp.int32, (chunk, chunk), 0)
    col = lax.broadcasted_iota(jnp.int32, (chunk, chunk), 1)
    diff = (row - col).astype(F32)
    decay = jnp.where(diff >= 0, jnp.exp(jnp.maximum(diff, 0.0) * lg), 0.0)
    idx = lax.broadcasted_iota(jnp.int32, (chunk, 1), 0).astype(F32)
    scores = lax.dot_general(q, k, (((1,), (1,)), ((), ())), preferred_element_type=F32) * decay
    o = jnp.dot(scores.astype(BF16), v, preferred_element_type=F32)
    state = state_ref[...]
    qd = (q.astype(F32) * jnp.exp((idx + 1.0) * lg)).astype(BF16)
    o = o + jnp.dot(qd, state.astype(BF16), preferred_element_type=F32)
    kd = (k.astype(F32) * jnp.exp((chunk - 1.0 - idx) * lg)).astype(BF16)
    state_ref[...] = state * jnp.exp(chunk * lg) + lax.dot_general(
        kd, v, (((0,), (0,)), ((), ())), preferred_element_type=F32)
    mu = jnp.mean(o, axis=1, keepdims=True)
    oc = o - mu
    var = jnp.mean(oc * oc, axis=1, keepdims=True)
    on = oc * lax.rsqrt(var + LN_EPS)
    g = g_ref[...].astype(F32)
    o_ref[...] = (g * jax.nn.sigmoid(g) * on).astype(o_ref.dtype)


def _retention(proj, log_gamma, batch, seq):
    chunk = min(RET_CHUNK, seq)
    nc = seq // chunk
    kb = D_MODEL // RET_KEY_DIM
    vb = 2 * D_MODEL // RET_VAL_DIM
    grid_spec = pltpu.PrefetchScalarGridSpec(
        num_scalar_prefetch=1,
        grid=(batch, RET_HEADS, nc),
        in_specs=[pl.BlockSpec((chunk, RET_KEY_DIM), lambda b, h, c, lg: (b * nc + c, h)),
                  pl.BlockSpec((chunk, RET_KEY_DIM), lambda b, h, c, lg: (b * nc + c, kb + h)),
                  pl.BlockSpec((chunk, RET_VAL_DIM), lambda b, h, c, lg: (b * nc + c, vb + h)),
                  pl.BlockSpec((chunk, RET_VAL_DIM), lambda b, h, c, lg: (b * nc + c, vb + RET_HEADS + h))],
        out_specs=pl.BlockSpec((chunk, RET_VAL_DIM), lambda b, h, c, lg: (b * nc + c, h)),
        scratch_shapes=[pltpu.VMEM((RET_KEY_DIM, RET_VAL_DIM), F32)],
    )
    return pl.pallas_call(
        functools.partial(_retention_kernel, chunk=chunk),
        out_shape=jax.ShapeDtypeStruct((batch * seq, 2 * D_MODEL), BF16),
        grid_spec=grid_spec,
        compiler_params=_cparams(("parallel", "parallel", "arbitrary")),
        name="retention",
    )(log_gamma, proj, proj, proj, proj)


def _layer_norm(y, g, b):
    mu = jnp.mean(y, axis=1, keepdims=True)
    yc = y - mu
    var = jnp.mean(yc * yc, axis=1, keepdims=True)
    return yc * lax.rsqrt(var + LN_EPS) * g + b


def _out_router_kernel(a_ref, w_ref, x_ref, g_ref, b_ref, wr_ref, br_ref, o_ref):
    tm = a_ref.shape[0]
    h = jnp.dot(a_ref[...], w_ref[...], preferred_element_type=F32)
    x1 = _layer_norm(DEEPNORM_ALPHA * x_ref[...] + h, g_ref[...], b_ref[...])
    o_ref[:, :D_MODEL] = x1

    logits = jnp.dot(x1, wr_ref[...], preferred_element_type=F32,
                     precision=lax.Precision.HIGHEST) + br_ref[...]
    lane = lax.broadcasted_iota(jnp.int32, (tm, LANES), 1)

    def first_argmax(vals, vmax):
        return jnp.min(jnp.where(vals == vmax, lane, LANES), axis=1, keepdims=True)

    gl = jnp.where(lane < N_GROUPS, logits, NEG_BIG)
    gmax = jnp.max(gl, axis=1, keepdims=True)
    gsum = jnp.sum(jnp.where(lane < N_GROUPS, jnp.exp(gl - gmax), 0.0), axis=1, keepdims=True)
    g_val = 1.0 / gsum
    g_idx = first_argmax(gl, gmax)
    lo = CW_LANE0 + EXPERTS_PER_GROUP * g_idx
    in_group = (lane >= lo) & (lane < lo + EXPERTS_PER_GROUP)
    el = jnp.where(in_group, logits, NEG_BIG)
    e1 = jnp.max(el, axis=1, keepdims=True)
    i1 = first_argmax(el, e1)
    el2 = jnp.where(lane == i1, NEG_BIG, el)
    e2 = jnp.max(el2, axis=1, keepdims=True)
    i2 = first_argmax(el2, e2)
    r = jnp.exp(e2 - e1)
    w1 = g_val / (1.0 + r)
    w2 = g_val * r / (1.0 + r)
    ext = jnp.where(lane == i1, w1, 0.0) + jnp.where(lane == i2, w2, 0.0)
    ext = jnp.where(lane == GIDX_LANE, g_idx.astype(F32), ext)
    o_ref[:, D_MODEL:] = ext


def _out_router(a, w_out, x2d, ln_g, ln_b, w_r, b_r):
    t, din = a.shape
    tm = min(TM_OUT, t)
    return pl.pallas_call(
        _out_router_kernel,
        out_shape=jax.ShapeDtypeStruct((t, ROW_W), F32),
        grid=(t // tm,),
        in_specs=[pl.BlockSpec((tm, din), lambda i: (i, 0)),
                  pl.BlockSpec((din, D_MODEL), lambda i: (0, 0)),
                  pl.BlockSpec((tm, D_MODEL), lambda i: (i, 0)),
                  pl.BlockSpec((1, D_MODEL), lambda i: (0, 0)),
                  pl.BlockSpec((1, D_MODEL), lambda i: (0, 0)),
                  pl.BlockSpec((D_MODEL, LANES), lambda i: (0, 0)),
                  pl.BlockSpec((1, LANES), lambda i: (0, 0))],
        out_specs=pl.BlockSpec((tm, ROW_W), lambda i: (i, 0)),
        compiler_params=_cparams(("parallel",)),
        name="out_router",
    )(a, w_out, x2d, ln_g, ln_b, w_r, b_r)


def _row_copy(src_ref, src_row, dst_ref, dst_row, sem):
    return pltpu.make_async_copy(src_ref.at[pl.ds(src_row, 1)], dst_ref.at[pl.ds(dst_row, 1)], sem)


def _dispatch_kernel(pos_ref, x_ref, init_ref, xs_ref, sem):
    del init_ref
    tm = x_ref.shape[0]
    base = pl.program_id(0) * tm

    def issue(r, _):
        _row_copy(x_ref, r, xs_ref, pos_ref[base + r], sem).start()
        return 0

    lax.fori_loop(0, tm, issue, 0, unroll=8)

    def drain(r, _):
        _row_copy(x_ref, r, xs_ref, pos_ref[base + r], sem).wait()
        return 0

    lax.fori_loop(0, tm, drain, 0, unroll=8)


def _dispatch(pos, x1ext, n_rows):
    t = x1ext.shape[0]
    tm = min(TM_ROWS, t)
    grid_spec = pltpu.PrefetchScalarGridSpec(
        num_scalar_prefetch=1,
        grid=(t // tm,),
        in_specs=[pl.BlockSpec((tm, ROW_W), lambda i, pos: (i, 0)),
                  pl.BlockSpec(memory_space=pl.ANY)],
        out_specs=pl.BlockSpec(memory_space=pl.ANY),
        scratch_shapes=[pltpu.SemaphoreType.DMA(())],
    )
    return pl.pallas_call(
        _dispatch_kernel,
        out_shape=jax.ShapeDtypeStruct((n_rows, ROW_W), F32),
        grid_spec=grid_spec,
        input_output_aliases={2: 0},
        compiler_params=_cparams(("arbitrary",)),
        name="dispatch",
    )(pos, x1ext, jnp.zeros((n_rows, ROW_W), F32))


def _moe_kernel(tg_ref, xs_ref, wg_ref, wu_ref, wd_ref, g_ref, b_ref, o_ref, xb_ref, acc_ref):
    i = pl.program_id(0)
    e = pl.program_id(1)
    tm = xs_ref.shape[0]

    @pl.when(e == 0)
    def _():
        xb_ref[...] = xs_ref[:, :D_MODEL].astype(BF16)
        acc_ref[...] = jnp.zeros_like(acc_ref)

    lane = lax.broadcasted_iota(jnp.int32, (tm, EXT), 1)
    cw_lane = CW_LANE0 + EXPERTS_PER_GROUP * tg_ref[i] + e
    cw = jnp.sum(jnp.where(lane == cw_lane, xs_ref[:, D_MODEL:], 0.0), axis=1, keepdims=True)
    xb = xb_ref[...]
    hg = jnp.dot(xb, wg_ref[...], preferred_element_type=F32)
    hu = jnp.dot(xb, wu_ref[...], preferred_element_type=F32)
    hh = (hg * jax.nn.sigmoid(hg) * hu * cw).astype(BF16)
    acc_ref[...] += jnp.dot(hh, wd_ref[...], preferred_element_type=F32)

    @pl.when(e == EXPERTS_PER_GROUP - 1)
    def _():
        y = DEEPNORM_ALPHA * xs_ref[:, :D_MODEL] + acc_ref[...]
        o_ref[...] = _layer_norm(y, g_ref[...], b_ref[...])


def _moe(tile_group, xs, w_gate, w_up, w_down, ln_g, ln_b):
    n_rows = xs.shape[0]
    tm = TM_MOE
    epg = EXPERTS_PER_GROUP
    grid_spec = pltpu.PrefetchScalarGridSpec(
        num_scalar_prefetch=1,
        grid=(n_rows // tm, epg),
        in_specs=[pl.BlockSpec((tm, ROW_W), lambda i, e, tg: (i, 0)),
                  pl.BlockSpec((None, D_MODEL, D_EXPERT), lambda i, e, tg: (tg[i] * epg + e, 0, 0)),
                  pl.BlockSpec((None, D_MODEL, D_EXPERT), lambda i, e, tg: (tg[i] * epg + e, 0, 0)),
                  pl.BlockSpec((None, D_EXPERT, D_MODEL), lambda i, e, tg: (tg[i] * epg + e, 0, 0)),
                  pl.BlockSpec((1, D_MODEL), lambda i, e, tg: (0, 0)),
                  pl.BlockSpec((1, D_MODEL), lambda i, e, tg: (0, 0))],
        out_specs=pl.BlockSpec((tm, D_MODEL), lambda i, e, tg: (i, 0)),
        scratch_shapes=[pltpu.VMEM((tm, D_MODEL), BF16), pltpu.VMEM((tm, D_MODEL), F32)],
    )
    return pl.pallas_call(
        _moe_kernel,
        out_shape=jax.ShapeDtypeStruct((n_rows, D_MODEL), F32),
        grid_spec=grid_spec,
        compiler_params=_cparams(("parallel", "arbitrary")),
        name="moe",
    )(tile_group, xs, w_gate, w_up, w_down, ln_g, ln_b)


def _combine_ple_kernel(pos_ref, x2s_ref, p_ref, wg_ref, bg_ref, wp_ref, o_ref, buf_ref, sem):
    tm = o_ref.shape[0]
    base = pl.program_id(0) * tm

    def issue(r, _):
        _row_copy(x2s_ref, pos_ref[base + r], buf_ref, r, sem).start()
        return 0

    lax.fori_loop(0, tm, issue, 0, unroll=8)

    def drain(r, _):
        _row_copy(x2s_ref, pos_ref[base + r], buf_ref, r, sem).wait()
        return 0

    lax.fori_loop(0, tm, drain, 0, unroll=8)

    x2 = buf_ref[...]
    gate = jax.nn.sigmoid(jnp.dot(x2.astype(BF16), wg_ref[...], preferred_element_type=F32) + bg_ref[...])
    pp = jnp.dot(p_ref[...].astype(BF16), wp_ref[...], preferred_element_type=F32)
    o_ref[...] = x2 + gate * pp


def _combine_ple(pos, x2s, p2d, w_gate, b_gate, w_proj):
    t = p2d.shape[0]
    tm = min(TM_ROWS, t)
    grid_spec = pltpu.PrefetchScalarGridSpec(
        num_scalar_prefetch=1,
        grid=(t // tm,),
        in_specs=[pl.BlockSpec(memory_space=pl.ANY),
                  pl.BlockSpec((tm, PLE_DIM), lambda i, pos: (i, 0)),
                  pl.BlockSpec((D_MODEL, D_MODEL), lambda i, pos: (0, 0)),
                  pl.BlockSpec((1, D_MODEL), lambda i, pos: (0, 0)),
                  pl.BlockSpec((PLE_DIM, D_MODEL), lambda i, pos: (0, 0))],
        out_specs=pl.BlockSpec((tm, D_MODEL), lambda i, pos: (i, 0)),
        scratch_shapes=[pltpu.VMEM((tm, D_MODEL), F32), pltpu.SemaphoreType.DMA(())],
    )
    return pl.pallas_call(
        _combine_ple_kernel,
        out_shape=jax.ShapeDtypeStruct((t, D_MODEL), F32),
        grid_spec=grid_spec,
        compiler_params=_cparams(("arbitrary",)),
        name="combine_ple",
    )(pos, x2s, p2d, w_gate, b_gate, w_proj)


def _sorted_positions(g_idx, n_tiles):
    onehot = (g_idx[:, None] == jnp.arange(N_GROUPS, dtype=jnp.int32)[None, :]).astype(jnp.int32)
    counts = jnp.sum(onehot, axis=0)
    tiles = (counts + TM_MOE - 1) // TM_MOE
    tile_end = jnp.cumsum(tiles)
    tile_start = tile_end - tiles
    rank = jnp.sum((jnp.cumsum(onehot, axis=0) - onehot) * onehot, axis=1)
    pos = jnp.sum(onehot * (tile_start * TM_MOE)[None, :], axis=1) + rank
    tile_ids = jnp.arange(n_tiles, dtype=jnp.int32)
    tile_group = jnp.sum((tile_ids[:, None] >= tile_end[None, :]).astype(jnp.int32), axis=1)
    return pos.astype(jnp.int32), jnp.minimum(tile_group, N_GROUPS - 1).astype(jnp.int32)


def _moe_block(a, w_out, x2d, p2d, ln1_g, ln1_b, ln2_g, ln2_b, w_group, b_group, w_router, b_router,
               w_gate, w_up, w_down, ple_w_proj, ple_w_gate, ple_b_gate):
    t = x2d.shape[0]
    n_tiles = t // TM_MOE + N_GROUPS
    pad = LANES - CW_LANE0 - N_EXPERTS
    w_r = jnp.concatenate([w_group, w_router, jnp.zeros((D_MODEL, pad), F32)], axis=1)
    b_r = jnp.concatenate([b_group, b_router, jnp.zeros((pad,), F32)])[None, :]
    x1ext = _out_router(a, w_out.astype(BF16), x2d, ln1_g[None, :], ln1_b[None, :], w_r, b_r)
    g_idx = x1ext[:, D_MODEL + GIDX_LANE].astype(jnp.int32)
    pos, tile_group = _sorted_positions(g_idx, n_tiles)
    xs = _dispatch(pos, x1ext, n_tiles * TM_MOE)
    x2s = _moe(tile_group, xs, w_gate.astype(BF16), w_up.astype(BF16), w_down.astype(BF16),
               ln2_g[None, :], ln2_b[None, :])
    return _combine_ple(pos, x2s, p2d, ple_w_gate.astype(BF16), ple_b_gate[None, :],
                        ple_w_proj.astype(BF16))


@jax.jit
def _forward(x, p, positions, fox_w_in, fox_b_f, fox_w_out, ret_w_in, ret_w_out, ln1_g, ln1_b, ln2_g, ln2_b,
             moe_w_group, moe_b_group, moe_w_router, moe_b_router, moe_w_gate, moe_w_up, moe_w_down,
             ple_w_proj, ple_w_gate, ple_b_gate):
    batch, seq, d = x.shape
    t = batch * seq
    x2d = x.reshape(t, d)

    def moe_args(i):
        return (ln1_g[i], ln1_b[i], ln2_g[i], ln2_b[i], moe_w_group[i], moe_b_group[i], moe_w_router[i],
                moe_b_router[i], moe_w_gate[i], moe_w_up[i], moe_w_down[i], ple_w_proj[i], ple_w_gate[i],
                ple_b_gate[i])

    w_in = fox_w_in[0]
    scale = jnp.concatenate([jnp.full((D_MODEL,), FOX_HEAD_DIM ** -0.5, F32), jnp.ones((2 * D_MODEL,), F32)])
    w_qkv = (w_in[:, :3 * D_MODEL] * scale[None, :]).astype(BF16)
    qkv = _proj(x2d, w_qkv, D_MODEL)
    c = _fgate(x2d, w_in[:, 3 * D_MODEL:], fox_b_f[0][None, :], batch, seq)
    c3 = c.reshape(batch, seq, FOX_HEADS).transpose(0, 2, 1)
    attn = _fox_attention(qkv, c3[:, :, :, None], c3[:, :, None, :], batch, seq)
    x2d = _moe_block(attn, fox_w_out[0], x2d, p[0].reshape(t, PLE_DIM), *moe_args(0))

    w_in = ret_w_in[0]
    scale = jnp.concatenate([jnp.ones((D_MODEL,), F32), jnp.full((D_MODEL,), RET_KEY_DIM ** -0.5, F32),
                             jnp.ones((4 * D_MODEL,), F32)])
    inv_freq = ROPE_BASE ** (-jnp.arange(0, RET_KEY_DIM, 2, dtype=F32) / RET_KEY_DIM)
    cos, sin = _rope_tables(positions.astype(F32).reshape(t, 1), inv_freq[None, :])
    proj = _ret_proj(x2d, (w_in * scale[None, :]).astype(BF16), cos, sin)
    log_gamma = jnp.log(1.0 - 2.0 ** (-5.0 - jnp.arange(RET_HEADS, dtype=F32)))
    ret = _retention(proj, log_gamma, batch, seq)
    x2d = _moe_block(ret, ret_w_out[0], x2d, p[1].reshape(t, PLE_DIM), *moe_args(1))
    return x2d.reshape(batch, seq, d)


def kernel(x, p, positions, fox_w_in, fox_b_f, fox_w_out, ret_w_in, ret_w_out, ln1_g, ln1_b, ln2_g, ln2_b,
           moe_w_group, moe_b_group, moe_w_router, moe_b_router, moe_w_gate, moe_w_up, moe_w_down,
           ple_w_proj, ple_w_gate, ple_b_gate):
    return _forward(x, p, positions, fox_w_in, fox_b_f, fox_w_out, ret_w_in, ret_w_out, ln1_g, ln1_b,
                    ln2_g, ln2_b, moe_w_group, moe_b_group, moe_w_router, moe_b_router, moe_w_gate,
                    moe_w_up, moe_w_down, ple_w_proj, ple_w_gate, ple_b_gate)
```

```python
import functools

import jax
import jax.numpy as jnp
from jax import lax
from jax.experimental import pallas as pl
from jax.experimental.pallas import tpu as pltpu

F32 = jnp.float32
BF16 = jnp.bfloat16

D_MODEL = 1024
PLE_DIM = 256
FOX_HEADS = 16
FOX_HEAD_DIM = D_MODEL // FOX_HEADS
RET_HEADS = 4
RET_KEY_DIM = D_MODEL // RET_HEADS
RET_VAL_DIM = 2 * D_MODEL // RET_HEADS
ROPE_BASE = 10000.0
N_GROUPS = 4
EXPERTS_PER_GROUP = 4
N_EXPERTS = N_GROUPS * EXPERTS_PER_GROUP
D_EXPERT = 512
DEPTH = 2
DEEPNORM_ALPHA = (2.0 * DEPTH) ** 0.25
LN_EPS = 1e-5

LANES = 128
EXT = LANES
ROW_W = D_MODEL + EXT
GIDX_LANE = 0
CW_LANE0 = 4
NEG_BIG = -1e30
LOG2E = 1.4426950408889634
VMEM_LIMIT = 56 * 1024 * 1024

TM_PROJ = 1024
TS_GATE = 512
TQ_ATTN = 512
ATTN_ROW_BLOCK = 32
TM_OUT = 256
TM_ROWS = 256
TM_MOE = 512
RET_CHUNK = 256


def _cparams(sem):
    return pltpu.CompilerParams(dimension_semantics=sem, vmem_limit_bytes=VMEM_LIMIT)


def _proj_kernel(a_ref, w_ref, o_ref):
    a = a_ref[...].astype(BF16)
    o_ref[...] = jnp.dot(a, w_ref[...], preferred_element_type=F32).astype(o_ref.dtype)


def _proj(a, w, tn):
    m, k = a.shape
    n = w.shape[1]
    tm = min(TM_PROJ, m)
    return pl.pallas_call(
        _proj_kernel,
        out_shape=jax.ShapeDtypeStruct((m, n), BF16),
        grid=(m // tm, n // tn),
        in_specs=[pl.BlockSpec((tm, k), lambda i, j: (i, 0)),
                  pl.BlockSpec((k, tn), lambda i, j: (0, j))],
        out_specs=pl.BlockSpec((tm, tn), lambda i, j: (i, j)),
        compiler_params=_cparams(("parallel", "arbitrary")),
        name="proj",
    )(a, w)


def _fgate_kernel(x_ref, w_ref, b_ref, c_ref, carry_ref):
    @pl.when(pl.program_id(1) == 0)
    def _():
        carry_ref[...] = jnp.zeros_like(carry_ref)

    ts = x_ref.shape[0]
    z = jnp.dot(x_ref[...], w_ref[...], preferred_element_type=F32,
                precision=lax.Precision.HIGHEST) + b_ref[...]
    logf = (jnp.minimum(z, 0.0) - jnp.log1p(jnp.exp(-jnp.abs(z)))) * LOG2E
    row = lax.broadcasted_iota(jnp.int32, (ts, ts), 0)
    col = lax.broadcasted_iota(jnp.int32, (ts, ts), 1)
    tri = (row >= col).astype(F32)
    c = jnp.dot(tri, logf, preferred_element_type=F32,
                precision=lax.Precision.HIGHEST) + carry_ref[...]
    c_ref[...] = c
    carry_ref[...] = c[ts - 1:ts, :]


def _fgate(x2d, w_f, b_f, batch, seq):
    ts = min(TS_GATE, seq)
    ns = seq // ts
    h = w_f.shape[1]
    return pl.pallas_call(
        _fgate_kernel,
        out_shape=jax.ShapeDtypeStruct((batch * seq, h), F32),
        grid=(batch, ns),
        in_specs=[pl.BlockSpec((ts, D_MODEL), lambda b, s: (b * ns + s, 0)),
                  pl.BlockSpec((D_MODEL, h), lambda b, s: (0, 0)),
                  pl.BlockSpec((1, h), lambda b, s: (0, 0))],
        out_specs=pl.BlockSpec((ts, h), lambda b, s: (b * ns + s, 0)),
        scratch_shapes=[pltpu.VMEM((1, h), F32)],
        compiler_params=_cparams(("parallel", "arbitrary")),
        name="fgate",
    )(x2d, w_f, b_f)


def _fox_attn_kernel(q_ref, k_ref, v_ref, cq_ref, ck_ref, o_ref, m_ref, acc_ref, *, tq):
    qi = pl.program_id(2)
    q = q_ref[...]
    lane = lax.broadcasted_iota(jnp.int32, (tq, LANES), 1)
    row = lax.broadcasted_iota(jnp.int32, (tq, tq), 0)
    col = lax.broadcasted_iota(jnp.int32, (tq, tq), 1)
    causal = row >= col
    head0 = lane < FOX_HEAD_DIM
    q2 = jnp.concatenate([jnp.where(head0, q, jnp.zeros_like(q)),
                          jnp.where(head0, jnp.zeros_like(q), q)], axis=0)
    m_ref[...] = jnp.full(m_ref.shape, NEG_BIG, F32)
    acc_ref[...] = jnp.zeros(acc_ref.shape, F32)
    lane_k = lax.broadcasted_iota(jnp.int32, (tq, LANES), 1)
    rb = ATTN_ROW_BLOCK

    def step(kb, masked):
        ks = pl.multiple_of(kb * tq, tq)
        kblk = k_ref[pl.ds(ks, tq), :]
        vblk = v_ref[pl.ds(ks, tq), :]
        s2 = lax.dot_general(q2, kblk, (((1,), (1,)), ((), ())), preferred_element_type=F32)
        for h in range(2):
            ck = ck_ref[h, :, pl.ds(ks, tq)]
            vh = jnp.where(lane_k < FOX_HEAD_DIM if h == 0 else lane_k >= FOX_HEAD_DIM,
                           vblk, jnp.ones_like(vblk))
            p_blocks = []
            m_blocks = []
            cq_all = cq_ref[h]
            m_all = m_ref[h]
            for r0 in range(0, tq, rb):
                t = s2[h * tq + r0:h * tq + r0 + rb, :] - ck
                if masked:
                    t = jnp.where(causal[r0:r0 + rb, :], t, NEG_BIG)
                cq = cq_all[r0:r0 + rb, :]
                m_new = jnp.maximum(m_all[r0:r0 + rb, :], cq + jnp.max(t, axis=1, keepdims=True))
                m_blocks.append(m_new)
                p_blocks.append(jnp.exp2(t + (cq - m_new)).astype(BF16))
            m_new_all = jnp.concatenate(m_blocks, axis=0)
            pv = jnp.dot(jnp.concatenate(p_blocks, axis=0), vh, preferred_element_type=F32)
            acc_ref[h] = jnp.exp2(m_all - m_new_all) * acc_ref[h] + pv
            m_ref[h] = m_new_all

    def body(kb, _):
        step(kb, False)
        return 0

    lax.fori_loop(0, qi, body, 0)
    step(qi, True)
    a0 = acc_ref[0]
    a1 = acc_ref[1]
    o0 = a0 / pltpu.roll(a0, FOX_HEAD_DIM, 1)
    o1 = a1 / pltpu.roll(a1, FOX_HEAD_DIM, 1)
    o_ref[...] = jnp.where(head0, o0, o1).astype(o_ref.dtype)


def _fox_attention(qkv, cq, ck, batch, seq):
    tq = min(TQ_ATTN, seq)
    nq = seq // tq
    hp = FOX_HEADS // 2
    return pl.pallas_call(
        functools.partial(_fox_attn_kernel, tq=tq),
        out_shape=jax.ShapeDtypeStruct((batch * seq, D_MODEL), BF16),
        grid=(batch, hp, nq),
        in_specs=[pl.BlockSpec((tq, LANES), lambda b, p, i: (b * nq + i, p)),
                  pl.BlockSpec((seq, LANES), lambda b, p, i: (b, hp + p)),
                  pl.BlockSpec((seq, LANES), lambda b, p, i: (b, 2 * hp + p)),
                  pl.BlockSpec((None, 2, tq, 1), lambda b, p, i: (b, p, i, 0)),
                  pl.BlockSpec((None, 2, 1, seq), lambda b, p, i: (b, p, 0, 0))],
        out_specs=pl.BlockSpec((tq, LANES), lambda b, p, i: (b * nq + i, p)),
        scratch_shapes=[pltpu.VMEM((2, tq, 1), F32), pltpu.VMEM((2, tq, LANES), F32)],
        compiler_params=_cparams(("parallel", "parallel", "arbitrary")),
        name="fox_attn",
    )(qkv, qkv, qkv, cq, ck)


def _rope_kernel(pos_ref, freq_ref, cos_ref, sin_ref):
    ang = pos_ref[...] * freq_ref[...]
    cos_ref[...] = jnp.cos(ang)
    sin_ref[...] = jnp.sin(ang)


def _rope_tables(pos_f, inv_freq):
    t = pos_f.shape[0]
    half = inv_freq.shape[1]
    tm = min(1024, t)
    return pl.pallas_call(
        _rope_kernel,
        out_shape=(jax.ShapeDtypeStruct((t, half), F32), jax.ShapeDtypeStruct((t, half), F32)),
        grid=(t // tm,),
        in_specs=[pl.BlockSpec((tm, 1), lambda i: (i, 0)),
                  pl.BlockSpec((1, half), lambda i: (0, 0))],
        out_specs=(pl.BlockSpec((tm, half), lambda i: (i, 0)),
                   pl.BlockSpec((tm, half), lambda i: (i, 0))),
        compiler_params=_cparams(("parallel",)),
        name="rope_tables",
    )(pos_f, inv_freq)


def _ret_proj_kernel(a_ref, w_ref, cos_ref, sin_ref, o_ref):
    j = pl.program_id(1)
    a = a_ref[...].astype(BF16)
    acc = jnp.dot(a, w_ref[...], preferred_element_type=F32)

    @pl.when(j < 2)
    def _():
        c = cos_ref[...]
        s = sin_ref[...]
        half = RET_KEY_DIM // 2
        for h in range(RET_HEADS):
            x1 = acc[:, h * RET_KEY_DIM:h * RET_KEY_DIM + half]
            x2 = acc[:, h * RET_KEY_DIM + half:(h + 1) * RET_KEY_DIM]
            o_ref[:, h * RET_KEY_DIM:h * RET_KEY_DIM + half] = (x1 * c - x2 * s).astype(o_ref.dtype)
            o_ref[:, h * RET_KEY_DIM + half:(h + 1) * RET_KEY_DIM] = (x2 * c + x1 * s).astype(o_ref.dtype)

    @pl.when(j >= 2)
    def _():
        o_ref[...] = acc.astype(o_ref.dtype)


def _ret_proj(x2d, w, cos, sin):
    t = x2d.shape[0]
    n = w.shape[1]
    tm = min(TM_PROJ, t)
    tn = D_MODEL
    half = RET_KEY_DIM // 2
    return pl.pallas_call(
        _ret_proj_kernel,
        out_shape=jax.ShapeDtypeStruct((t, n), BF16),
        grid=(t // tm, n // tn),
        in_specs=[pl.BlockSpec((tm, D_MODEL), lambda i, j: (i, 0)),
                  pl.BlockSpec((D_MODEL, tn), lambda i, j: (0, j)),
                  pl.BlockSpec((tm, half), lambda i, j: (i, 0)),
                  pl.BlockSpec((tm, half), lambda i, j: (i, 0))],
        out_specs=pl.BlockSpec((tm, tn), lambda i, j: (i, j)),
        compiler_params=_cparams(("parallel", "arbitrary")),
        name="ret_proj",
    )(x2d, w, cos, sin)


def _retention_kernel(lg_ref, q_ref, k_ref, v_ref, g_ref, o_ref, state_ref, *, chunk):
    @pl.when(pl.program_id(2) == 0)
    def _():
        state_ref[...] = jnp.zeros_like(state_ref)

    lg = lg_ref[pl.program_id(1)]
    q = q_ref[...]
    k = k_ref[...]
    v = v_ref[...]
    row = lax.broadcasted_iota(jnp.int32, (chunk, chunk), 0)
    col = lax.broadcasted_iota(jnp.int32, (chunk, chunk), 1)
    diff = (row - col).astype(F32)
    decay = jnp.where(diff >= 0, jnp.exp(jnp.maximum(diff, 0.0) * lg), 0.0)
    idx = lax.broadcasted_iota(jnp.int32, (chunk, 1), 0).astype(F32)
    scores = lax.dot_general(q, k, (((1,), (1,)), ((), ())), preferred_element_type=F32) * decay
    o = jnp.dot(scores.astype(BF16), v, preferred_element_type=F32)
    state = state_ref[...]
    qd = (q.astype(F32) * jnp.exp((idx + 1.0) * lg)).astype(BF16)
    o = o + jnp.dot(qd, state.astype(BF16), preferred_element_type=F32)
    kd = (k.astype(F32) * jnp.exp((chunk - 1.0 - idx) * lg)).astype(BF16)
    state_ref[...] = state * jnp.exp(chunk * lg) + lax.dot_general(
        kd, v, (((0,), (0,)), ((), ())), preferred_element_type=F32)
    mu = jnp.mean(o, axis=1, keepdims=True)
    oc = o - mu
    var = jnp.mean(oc * oc, axis=1, keepdims=True)
    on = oc * lax.rsqrt(var + LN_EPS)
    g = g_ref[...].astype(F32)
    o_ref[...] = (g * jax.nn.sigmoid(g) * on).astype(o_ref.dtype)


def _retention(proj, log_gamma, batch, seq):
    chunk = min(RET_CHUNK, seq)
    nc = seq // chunk
    kb = D_MODEL // RET_KEY_DIM
    vb = 2 * D_MODEL // RET_VAL_DIM
    grid_spec = pltpu.PrefetchScalarGridSpec(
        num_scalar_prefetch=1,
        grid=(batch, RET_HEADS, nc),
        in_specs=[pl.BlockSpec((chunk, RET_KEY_DIM), lambda b, h, c, lg: (b * nc + c, h)),
                  pl.BlockSpec((chunk, RET_KEY_DIM), lambda b, h, c, lg: (b * nc + c, kb + h)),
                  pl.BlockSpec((chunk, RET_VAL_DIM), lambda b, h, c, lg: (b * nc + c, vb + h)),
                  pl.BlockSpec((chunk, RET_VAL_DIM), lambda b, h, c, lg: (b * nc + c, vb + RET_HEADS + h))],
        out_specs=pl.BlockSpec((chunk, RET_VAL_DIM), lambda b, h, c, lg: (b * nc + c, h)),
        scratch_shapes=[pltpu.VMEM((RET_KEY_DIM, RET_VAL_DIM), F32)],
    )
    return pl.pallas_call(
        functools.partial(_retention_kernel, chunk=chunk),
        out_shape=jax.ShapeDtypeStruct((batch * seq, 2 * D_MODEL), BF16),
        grid_spec=grid_spec,
        compiler_params=_cparams(("parallel", "parallel", "arbitrary")),
        name="retention",
    )(log_gamma, proj, proj, proj, proj)


def _layer_norm(y, g, b):
    mu = jnp.mean(y, axis=1, keepdims=True)
    yc = y - mu
    var = jnp.mean(yc * yc, axis=1, keepdims=True)
    return yc * lax.rsqrt(var + LN_EPS) * g + b


def _out_router_kernel(a_ref, w_ref, x_ref, g_ref, b_ref, wr_ref, br_ref, o_ref):
    tm = a_ref.shape[0]
    h = jnp.dot(a_ref[...], w_ref[...], preferred_element_type=F32)
    x1 = _layer_norm(DEEPNORM_ALPHA * x_ref[...] + h, g_ref[...], b_ref[...])
    o_ref[:, :D_MODEL] = x1

    logits = jnp.dot(x1, wr_ref[...], preferred_element_type=F32,
                     precision=lax.Precision.HIGHEST) + br_ref[...]
    lane = lax.broadcasted_iota(jnp.int32, (tm, LANES), 1)

    def first_argmax(vals, vmax):
        return jnp.min(jnp.where(vals == vmax, lane, LANES), axis=1, keepdims=True)

    gl = jnp.where(lane < N_GROUPS, logits, NEG_BIG)
    gmax = jnp.max(gl, axis=1, keepdims=True)
    gsum = jnp.sum(jnp.where(lane < N_GROUPS, jnp.exp(gl - gmax), 0.0), axis=1, keepdims=True)
    g_val = 1.0 / gsum
    g_idx = first_argmax(gl, gmax)
    lo = CW_LANE0 + EXPERTS_PER_GROUP * g_idx
    in_group = (lane >= lo) & (lane < lo + EXPERTS_PER_GROUP)
    el = jnp.where(in_group, logits, NEG_BIG)
    e1 = jnp.max(el, axis=1, keepdims=True)
    i1 = first_argmax(el, e1)
    el2 = jnp.where(lane == i1, NEG_BIG, el)
    e2 = jnp.max(el2, axis=1, keepdims=True)
    i2 = first_argmax(el2, e2)
    r = jnp.exp(e2 - e1)
    w1 = g_val / (1.0 + r)
    w2 = g_val * r / (1.0 + r)
    ext = jnp.where(lane == i1, w1, 0.0) + jnp.where(lane == i2, w2, 0.0)
    ext = jnp.where(lane == GIDX_LANE, g_idx.astype(F32), ext)
    o_ref[:, D_MODEL:] = ext


def _out_router(a, w_out, x2d, ln_g, ln_b, w_r, b_r):
    t, din = a.shape
    tm = min(TM_OUT, t)
    return pl.pallas_call(
        _out_router_kernel,
        out_shape=jax.ShapeDtypeStruct((t, ROW_W), F32),
        grid=(t // tm,),
        in_specs=[pl.BlockSpec((tm, din), lambda i: (i, 0)),
                  pl.BlockSpec((din, D_MODEL), lambda i: (0, 0)),
                  pl.BlockSpec((tm, D_MODEL), lambda i: (i, 0)),
                  pl.BlockSpec((1, D_MODEL), lambda i: (0, 0)),
                  pl.BlockSpec((1, D_MODEL), lambda i: (0, 0)),
                  pl.BlockSpec((D_MODEL, LANES), lambda i: (0, 0)),
                  pl.BlockSpec((1, LANES), lambda i: (0, 0))],
        out_specs=pl.BlockSpec((tm, ROW_W), lambda i: (i, 0)),
        compiler_params=_cparams(("parallel",)),
        name="out_router",
    )(a, w_out, x2d, ln_g, ln_b, w_r, b_r)


def _row_copy(src_ref, src_row, dst_ref, dst_row, sem):
    return pltpu.make_async_copy(src_ref.at[pl.ds(src_row, 1)], dst_ref.at[pl.ds(dst_row, 1)], sem)


def _dispatch_kernel(pos_ref, x_ref, init_ref, xs_ref, sem):
    del init_ref
    tm = x_ref.shape[0]
    base = pl.program_id(0) * tm

    def issue(r, _):
        _row_copy(x_ref, r, xs_ref, pos_ref[base + r], sem).start()
        return 0

    lax.fori_loop(0, tm, issue, 0, unroll=8)

    def drain(r, _):
        _row_copy(x_ref, r, xs_ref, pos_ref[base + r], sem).wait()
        return 0

    lax.fori_loop(0, tm, drain, 0, unroll=8)


def _dispatch(pos, x1ext, n_rows):
    t = x1ext.shape[0]
    tm = min(TM_ROWS, t)
    grid_spec = pltpu.PrefetchScalarGridSpec(
        num_scalar_prefetch=1,
        grid=(t // tm,),
        in_specs=[pl.BlockSpec((tm, ROW_W), lambda i, pos: (i, 0)),
                  pl.BlockSpec(memory_space=pl.ANY)],
        out_specs=pl.BlockSpec(memory_space=pl.ANY),
        scratch_shapes=[pltpu.SemaphoreType.DMA(())],
    )
    return pl.pallas_call(
        _dispatch_kernel,
        out_shape=jax.ShapeDtypeStruct((n_rows, ROW_W), F32),
        grid_spec=grid_spec,
        input_output_aliases={2: 0},
        compiler_params=_cparams(("arbitrary",)),
        name="dispatch",
    )(pos, x1ext, jnp.zeros((n_rows, ROW_W), F32))


def _moe_kernel(tg_ref, xs_ref, wg_ref, wu_ref, wd_ref, g_ref, b_ref, o_ref, xb_ref, acc_ref):
    i = pl.program_id(0)
    e = pl.program_id(1)
    tm = xs_ref.shape[0]

    @pl.when(e == 0)
    def _():
        xb_ref[...] = xs_ref[:, :D_MODEL].astype(BF16)
        acc_ref[...] = jnp.zeros_like(acc_ref)

    lane = lax.broadcasted_iota(jnp.int32, (tm, EXT), 1)
    cw_lane = CW_LANE0 + EXPERTS_PER_GROUP * tg_ref[i] + e
    cw = jnp.sum(jnp.where(lane == cw_lane, xs_ref[:, D_MODEL:], 0.0), axis=1, keepdims=True)
    xb = xb_ref[...]
    hg = jnp.dot(xb, wg_ref[...], preferred_element_type=F32)
    hu = jnp.dot(xb, wu_ref[...], preferred_element_type=F32)
    hh = (hg * jax.nn.sigmoid(hg) * hu * cw).astype(BF16)
    acc_ref[...] += jnp.dot(hh, wd_ref[...], preferred_element_type=F32)

    @pl.when(e == EXPERTS_PER_GROUP - 1)
    def _():
        y = DEEPNORM_ALPHA * xs_ref[:, :D_MODEL] + acc_ref[...]
        o_ref[...] = _layer_norm(y, g_ref[...], b_ref[...])


def _moe(tile_group, xs, w_gate, w_up, w_down, ln_g, ln_b):
    n_rows = xs.shape[0]
    tm = TM_MOE
    epg = EXPERTS_PER_GROUP
    grid_spec = pltpu.PrefetchScalarGridSpec(
        num_scalar_prefetch=1,
        grid=(n_rows // tm, epg),
        in_specs=[pl.BlockSpec((tm, ROW_W), lambda i, e, tg: (i, 0)),
                  pl.BlockSpec((None, D_MODEL, D_EXPERT), lambda i, e, tg: (tg[i] * epg + e, 0, 0)),
                  pl.BlockSpec((None, D_MODEL, D_EXPERT), lambda i, e, tg: (tg[i] * epg + e, 0, 0)),
                  pl.BlockSpec((None, D_EXPERT, D_MODEL), lambda i, e, tg: (tg[i] * epg + e, 0, 0)),
                  pl.BlockSpec((1, D_MODEL), lambda i, e, tg: (0, 0)),
                  pl.BlockSpec((1, D_MODEL), lambda i, e, tg: (0, 0))],
        out_specs=pl.BlockSpec((tm, D_MODEL), lambda i, e, tg: (i, 0)),
        scratch_shapes=[pltpu.VMEM((tm, D_MODEL), BF16), pltpu.VMEM((tm, D_MODEL), F32)],
    )
    return pl.pallas_call(
        _moe_kernel,
        out_shape=jax.ShapeDtypeStruct((n_rows, D_MODEL), F32),
        grid_spec=grid_spec,
        compiler_params=_cparams(("parallel", "arbitrary")),
        name="moe",
    )(tile_group, xs, w_gate, w_up, w_down, ln_g, ln_b)


def _combine_ple_kernel(pos_ref, x2s_ref, p_ref, wg_ref, bg_ref, wp_ref, o_ref, buf_ref, sem):
    tm = o_ref.shape[0]
    base = pl.program_id(0) * tm

    def issue(r, _):
        _row_copy(x2s_ref, pos_ref[base + r], buf_ref, r, sem).start()
        return 0

    lax.fori_loop(0, tm, issue, 0, unroll=8)

    def drain(r, _):
        _row_copy(x2s_ref, pos_ref[base + r], buf_ref, r, sem).wait()
        return 0

    lax.fori_loop(0, tm, drain, 0, unroll=8)

    x2 = buf_ref[...]
    gate = jax.nn.sigmoid(jnp.dot(x2.astype(BF16), wg_ref[...], preferred_element_type=F32) + bg_ref[...])
    pp = jnp.dot(p_ref[...].astype(BF16), wp_ref[...], preferred_element_type=F32)
    o_ref[...] = x2 + gate * pp


def _combine_ple(pos, x2s, p2d, w_gate, b_gate, w_proj):
    t = p2d.shape[0]
    tm = min(TM_ROWS, t)
    grid_spec = pltpu.PrefetchScalarGridSpec(
        num_scalar_prefetch=1,
        grid=(t // tm,),
        in_specs=[pl.BlockSpec(memory_space=pl.ANY),
                  pl.BlockSpec((tm, PLE_DIM), lambda i, pos: (i, 0)),
                  pl.BlockSpec((D_MODEL, D_MODEL), lambda i, pos: (0, 0)),
                  pl.BlockSpec((1, D_MODEL), lambda i, pos: (0, 0)),
                  pl.BlockSpec((PLE_DIM, D_MODEL), lambda i, pos: (0, 0))],
        out_specs=pl.BlockSpec((tm, D_MODEL), lambda i, pos: (i, 0)),
        scratch_shapes=[pltpu.VMEM((tm, D_MODEL), F32), pltpu.SemaphoreType.DMA(())],
    )
    return pl.pallas_call(
        _combine_ple_kernel,
        out_shape=jax.ShapeDtypeStruct((t, D_MODEL), F32),
        grid_spec=grid_spec,
        compiler_params=_cparams(("arbitrary",)),
        name="combine_ple",
    )(pos, x2s, p2d, w_gate, b_gate, w_proj)


def _sorted_positions(g_idx, n_tiles):
    onehot = (g_idx[:, None] == jnp.arange(N_GROUPS, dtype=jnp.int32)[None, :]).astype(jnp.int32)
    counts = jnp.sum(onehot, axis=0)
    tiles = (counts + TM_MOE - 1) // TM_MOE
    tile_end = jnp.cumsum(tiles)
    tile_start = tile_end - tiles
    rank = jnp.sum((jnp.cumsum(onehot, axis=0) - onehot) * onehot, axis=1)
    pos = jnp.sum(onehot * (tile_start * TM_MOE)[None, :], axis=1) + rank
    tile_ids = jnp.arange(n_tiles, dtype=jnp.int32)
    tile_group = jnp.sum((tile_ids[:, None] >= tile_end[None, :]).astype(jnp.int32), axis=1)
    return pos.astype(jnp.int32), jnp.minimum(tile_group, N_GROUPS - 1).astype(jnp.int32)


def _moe_block(a, w_out, x2d, p2d, ln1_g, ln1_b, ln2_g, ln2_b, w_group, b_group, w_router, b_router,
               w_gate, w_up, w_down, ple_w_proj, ple_w_gate, ple_b_gate):
    t = x2d.shape[0]
    n_tiles = t // TM_MOE + N_GROUPS
    pad = LANES - CW_LANE0 - N_EXPERTS
    w_r = jnp.concatenate([w_group, w_router, jnp.zeros((D_MODEL, pad), F32)], axis=1)
    b_r = jnp.concatenate([b_group, b_router, jnp.zeros((pad,), F32)])[None, :]
    x1ext = _out_router(a, w_out.astype(BF16), x2d, ln1_g[None, :], ln1_b[None, :], w_r, b_r)
    g_idx = x1ext[:, D_MODEL + GIDX_LANE].astype(jnp.int32)
    pos, tile_group = _sorted_positions(g_idx, n_tiles)
    xs = _dispatch(pos, x1ext, n_tiles * TM_MOE)
    x2s = _moe(tile_group, xs, w_gate.astype(BF16), w_up.astype(BF16), w_down.astype(BF16),
               ln2_g[None, :], ln2_b[None, :])
    return _combine_ple(pos, x2s, p2d, ple_w_gate.astype(BF16), ple_b_gate[None, :],
                        ple_w_proj.astype(BF16))


@jax.jit
def _forward(x, p, positions, fox_w_in, fox_b_f, fox_w_out, ret_w_in, ret_w_out, ln1_g, ln1_b, ln2_g, ln2_b,
             moe_w_group, moe_b_group, moe_w_router, moe_b_router, moe_w_gate, moe_w_up, moe_w_down,
             ple_w_proj, ple_w_gate, ple_b_gate):
    batch, seq, d = x.shape
    t = batch * seq
    x2d = x.reshape(t, d)

    def moe_args(i):
        return (ln1_g[i], ln1_b[i], ln2_g[i], ln2_b[i], moe_w_group[i], moe_b_group[i], moe_w_router[i],
                moe_b_router[i], moe_w_gate[i], moe_w_up[i], moe_w_down[i], ple_w_proj[i], ple_w_gate[i],
                ple_b_gate[i])

    w_in = fox_w_in[0]
    scale = jnp.concatenate([jnp.full((D_MODEL,), FOX_HEAD_DIM ** -0.5 * LOG2E, F32),
                             jnp.ones((2 * D_MODEL,), F32)])
    w_qkv = (w_in[:, :3 * D_MODEL] * scale[None, :]).astype(BF16)
    qkv = _proj(x2d, w_qkv, D_MODEL)
    c = _fgate(x2d, w_in[:, 3 * D_MODEL:], fox_b_f[0][None, :], batch, seq)
    c3 = c.reshape(batch, seq, FOX_HEADS).transpose(0, 2, 1)
    attn = _fox_attention(qkv, c3[:, :, :, None], c3[:, :, None, :], batch, seq)
    x2d = _moe_block(attn, fox_w_out[0], x2d, p[0].reshape(t, PLE_DIM), *moe_args(0))

    w_in = ret_w_in[0]
    scale = jnp.concatenate([jnp.ones((D_MODEL,), F32), jnp.full((D_MODEL,), RET_KEY_DIM ** -0.5, F32),
                             jnp.ones((4 * D_MODEL,), F32)])
    inv_freq = ROPE_BASE ** (-jnp.arange(0, RET_KEY_DIM, 2, dtype=F32) / RET_KEY_DIM)
    cos, sin = _rope_tables(positions.astype(F32).reshape(t, 1), inv_freq[None, :])
    proj = _ret_proj(x2d, (w_in * scale[None, :]).astype(BF16), cos, sin)
    log_gamma = jnp.log(1.0 - 2.0 ** (-5.0 - jnp.arange(RET_HEADS, dtype=F32)))
    ret = _retention(proj, log_gamma, batch, seq)
    x2d = _moe_block(ret, ret_w_out[0], x2d, p[1].reshape(t, PLE_DIM), *moe_args(1))
    return x2d.reshape(batch, seq, d)


def kernel(x, p, positions, fox_w_in, fox_b_f, fox_w_out, ret_w_in, ret_w_out, ln1_g, ln1_b, ln2_g, ln2_b,
           moe_w_group, moe_b_group, moe_w_router, moe_b_router, moe_w_gate, moe_w_up, moe_w_down,
           ple_w_proj, ple_w_gate, ple_b_gate):
    return _forward(x, p, positions, fox_w_in, fox_b_f, fox_w_out, ret_w_in, ret_w_out, ln1_g, ln1_b,
                    ln2_g, ln2_b, moe_w_group, moe_b_group, moe_w_router, moe_b_router, moe_w_gate,
                    moe_w_up, moe_w_down, ple_w_proj, ple_w_gate, ple_b_gate)
```

```python
import functools

import jax
import jax.numpy as jnp
from jax import lax
from jax.experimental import pallas as pl
from jax.experimental.pallas import tpu as pltpu

F32 = jnp.float32
BF16 = jnp.bfloat16

D_MODEL = 1024
PLE_DIM = 256
FOX_HEADS = 16
FOX_HEAD_DIM = D_MODEL // FOX_HEADS
RET_HEADS = 4
RET_KEY_DIM = D_MODEL // RET_HEADS
RET_VAL_DIM = 2 * D_MODEL // RET_HEADS
ROPE_BASE = 10000.0
N_GROUPS = 4
EXPERTS_PER_GROUP = 4
N_EXPERTS = N_GROUPS * EXPERTS_PER_GROUP
D_EXPERT = 512
DEPTH = 2
DEEPNORM_ALPHA = (2.0 * DEPTH) ** 0.25
LN_EPS = 1e-5

LANES = 128
EXT = LANES
ROW_W = D_MODEL + EXT
GIDX_LANE = 0
CW_LANE0 = 4
NEG_BIG = -1e30
LOG2E = 1.4426950408889634
VMEM_LIMIT = 56 * 1024 * 1024

TM_PROJ = 1024
TS_GATE = 512
TQ_ATTN = 512
ATTN_ROW_BLOCK = 32
TM_OUT = 256
TM_ROWS = 256
TM_MOE = 512
RET_CHUNK = 256


def _cparams(sem):
    return pltpu.CompilerParams(dimension_semantics=sem, vmem_limit_bytes=VMEM_LIMIT)


def _proj_kernel(a_ref, w_ref, o_ref):
    a = a_ref[...].astype(BF16)
    o_ref[...] = jnp.dot(a, w_ref[...], preferred_element_type=F32).astype(o_ref.dtype)


def _proj(a, w, tn):
    m, k = a.shape
    n = w.shape[1]
    tm = min(TM_PROJ, m)
    return pl.pallas_call(
        _proj_kernel,
        out_shape=jax.ShapeDtypeStruct((m, n), BF16),
        grid=(m // tm, n // tn),
        in_specs=[pl.BlockSpec((tm, k), lambda i, j: (i, 0)),
                  pl.BlockSpec((k, tn), lambda i, j: (0, j))],
        out_specs=pl.BlockSpec((tm, tn), lambda i, j: (i, j)),
        compiler_params=_cparams(("parallel", "arbitrary")),
        name="proj",
    )(a, w)


def _split_bf16(x):
    hi = x.astype(BF16)
    return hi, (x - hi.astype(F32)).astype(BF16)


def _split_weight(w):
    hi, lo = _split_bf16(w)
    return jnp.concatenate([hi, lo], axis=1), hi


def _dot_split(x, w_hilo, w_hi):
    x_hi, x_lo = _split_bf16(x)
    r = jnp.dot(x_hi, w_hilo, preferred_element_type=F32)
    return r[:, :LANES] + r[:, LANES:] + jnp.dot(x_lo, w_hi, preferred_element_type=F32)


def _fgate_kernel(x_ref, whl_ref, wh_ref, b_ref, c_ref, carry_ref):
    @pl.when(pl.program_id(1) == 0)
    def _():
        carry_ref[...] = jnp.zeros_like(carry_ref)

    ts = x_ref.shape[0]
    z = _dot_split(x_ref[...], whl_ref[...], wh_ref[...]) + b_ref[...]
    logf = (jnp.minimum(z, 0.0) - jnp.log1p(jnp.exp(-jnp.abs(z)))) * LOG2E
    l1 = logf.astype(BF16)
    rest = logf - l1.astype(F32)
    l2, l3 = _split_bf16(rest)
    row = lax.broadcasted_iota(jnp.int32, (ts, ts), 0)
    col = lax.broadcasted_iota(jnp.int32, (ts, ts), 1)
    tri = jnp.where(row >= col, 1.0, 0.0).astype(BF16)
    parts = jnp.dot(tri, jnp.concatenate([l1, l2, l3], axis=1), preferred_element_type=F32)
    c = parts[:, :LANES] + parts[:, LANES:2 * LANES] + parts[:, 2 * LANES:] + carry_ref[...]
    c_ref[...] = c[:, :c_ref.shape[1]]
    carry_ref[...] = c[ts - 1:ts, :]


def _fgate(x2d, w_f, b_f, batch, seq):
    ts = min(TS_GATE, seq)
    ns = seq // ts
    h = w_f.shape[1]
    w_hilo, w_hi = _split_weight(jnp.pad(w_f, ((0, 0), (0, LANES - h))))
    b_pad = jnp.pad(b_f, ((0, 0), (0, LANES - h)))
    return pl.pallas_call(
        _fgate_kernel,
        out_shape=jax.ShapeDtypeStruct((batch * seq, h), F32),
        grid=(batch, ns),
        in_specs=[pl.BlockSpec((ts, D_MODEL), lambda b, s: (b * ns + s, 0)),
                  pl.BlockSpec((D_MODEL, 2 * LANES), lambda b, s: (0, 0)),
                  pl.BlockSpec((D_MODEL, LANES), lambda b, s: (0, 0)),
                  pl.BlockSpec((1, LANES), lambda b, s: (0, 0))],
        out_specs=pl.BlockSpec((ts, h), lambda b, s: (b * ns + s, 0)),
        scratch_shapes=[pltpu.VMEM((1, LANES), F32)],
        compiler_params=_cparams(("parallel", "arbitrary")),
        name="fgate",
    )(x2d, w_hilo, w_hi, b_pad)


def _fox_attn_kernel(q_ref, k_ref, v_ref, cq_ref, ck_ref, o_ref, m_ref, acc_ref, *, tq):
    qi = pl.program_id(2)
    q = q_ref[...]
    lane = lax.broadcasted_iota(jnp.int32, (tq, LANES), 1)
    row = lax.broadcasted_iota(jnp.int32, (tq, tq), 0)
    col = lax.broadcasted_iota(jnp.int32, (tq, tq), 1)
    causal = row >= col
    head0 = lane < FOX_HEAD_DIM
    q2 = jnp.concatenate([jnp.where(head0, q, jnp.zeros_like(q)),
                          jnp.where(head0, jnp.zeros_like(q), q)], axis=0)
    m_ref[...] = jnp.full(m_ref.shape, NEG_BIG, F32)
    acc_ref[...] = jnp.zeros(acc_ref.shape, F32)
    lane_k = lax.broadcasted_iota(jnp.int32, (tq, LANES), 1)
    rb = ATTN_ROW_BLOCK

    def step(kb, masked):
        ks = pl.multiple_of(kb * tq, tq)
        kblk = k_ref[pl.ds(ks, tq), :]
        vblk = v_ref[pl.ds(ks, tq), :]
        s2 = lax.dot_general(q2, kblk, (((1,), (1,)), ((), ())), preferred_element_type=F32)
        for h in range(2):
            ck = ck_ref[h, :, pl.ds(ks, tq)]
            vh = jnp.where(lane_k < FOX_HEAD_DIM if h == 0 else lane_k >= FOX_HEAD_DIM,
                           vblk, jnp.ones_like(vblk))
            p_blocks = []
            m_blocks = []
            cq_all = cq_ref[h]
            m_all = m_ref[h]
            for r0 in range(0, tq, rb):
                t = s2[h * tq + r0:h * tq + r0 + rb, :] - ck
                if masked:
                    t = jnp.where(causal[r0:r0 + rb, :], t, NEG_BIG)
                cq = cq_all[r0:r0 + rb, :]
                m_new = jnp.maximum(m_all[r0:r0 + rb, :], cq + jnp.max(t, axis=1, keepdims=True))
                m_blocks.append(m_new)
                p_blocks.append(jnp.exp2(t + (cq - m_new)).astype(BF16))
            m_new_all = jnp.concatenate(m_blocks, axis=0)
            pv = jnp.dot(jnp.concatenate(p_blocks, axis=0), vh, preferred_element_type=F32)
            acc_ref[h] = jnp.exp2(m_all - m_new_all) * acc_ref[h] + pv
            m_ref[h] = m_new_all

    def body(kb, _):
        step(kb, False)
        return 0

    lax.fori_loop(0, qi, body, 0)
    step(qi, True)
    a0 = acc_ref[0]
    a1 = acc_ref[1]
    o0 = a0 / pltpu.roll(a0, FOX_HEAD_DIM, 1)
    o1 = a1 / pltpu.roll(a1, FOX_HEAD_DIM, 1)
    o_ref[...] = jnp.where(head0, o0, o1).astype(o_ref.dtype)


def _fox_attention(qkv, cq, ck, batch, seq):
    tq = min(TQ_ATTN, seq)
    nq = seq // tq
    hp = FOX_HEADS // 2
    return pl.pallas_call(
        functools.partial(_fox_attn_kernel, tq=tq),
        out_shape=jax.ShapeDtypeStruct((batch * seq, D_MODEL), BF16),
        grid=(batch, hp, nq),
        in_specs=[pl.BlockSpec((tq, LANES), lambda b, p, i: (b * nq + i, p)),
                  pl.BlockSpec((seq, LANES), lambda b, p, i: (b, hp + p)),
                  pl.BlockSpec((seq, LANES), lambda b, p, i: (b, 2 * hp + p)),
                  pl.BlockSpec((None, 2, tq, 1), lambda b, p, i: (b, p, i, 0)),
                  pl.BlockSpec((None, 2, 1, seq), lambda b, p, i: (b, p, 0, 0))],
        out_specs=pl.BlockSpec((tq, LANES), lambda b, p, i: (b * nq + i, p)),
        scratch_shapes=[pltpu.VMEM((2, tq, 1), F32), pltpu.VMEM((2, tq, LANES), F32)],
        compiler_params=_cparams(("parallel", "parallel", "arbitrary")),
        name="fox_attn",
    )(qkv, qkv, qkv, cq, ck)


def _rope_kernel(pos_ref, freq_ref, cos_ref, sin_ref):
    ang = pos_ref[...] * freq_ref[...]
    cos_ref[...] = jnp.cos(ang)
    sin_ref[...] = jnp.sin(ang)


def _rope_tables(pos_f, inv_freq):
    t = pos_f.shape[0]
    half = inv_freq.shape[1]
    tm = min(1024, t)
    return pl.pallas_call(
        _rope_kernel,
        out_shape=(jax.ShapeDtypeStruct((t, half), F32), jax.ShapeDtypeStruct((t, half), F32)),
        grid=(t // tm,),
        in_specs=[pl.BlockSpec((tm, 1), lambda i: (i, 0)),
                  pl.BlockSpec((1, half), lambda i: (0, 0))],
        out_specs=(pl.BlockSpec((tm, half), lambda i: (i, 0)),
                   pl.BlockSpec((tm, half), lambda i: (i, 0))),
        compiler_params=_cparams(("parallel",)),
        name="rope_tables",
    )(pos_f, inv_freq)


def _ret_proj_kernel(a_ref, w_ref, cos_ref, sin_ref, o_ref):
    j = pl.program_id(1)
    a = a_ref[...].astype(BF16)
    acc = jnp.dot(a, w_ref[...], preferred_element_type=F32)

    @pl.when(j < 2)
    def _():
        c = cos_ref[...]
        s = sin_ref[...]
        half = RET_KEY_DIM // 2
        for h in range(RET_HEADS):
            x1 = acc[:, h * RET_KEY_DIM:h * RET_KEY_DIM + half]
            x2 = acc[:, h * RET_KEY_DIM + half:(h + 1) * RET_KEY_DIM]
            o_ref[:, h * RET_KEY_DIM:h * RET_KEY_DIM + half] = (x1 * c - x2 * s).astype(o_ref.dtype)
            o_ref[:, h * RET_KEY_DIM + half:(h + 1) * RET_KEY_DIM] = (x2 * c + x1 * s).astype(o_ref.dtype)

    @pl.when(j >= 2)
    def _():
        o_ref[...] = acc.astype(o_ref.dtype)


def _ret_proj(x2d, w, cos, sin):
    t = x2d.shape[0]
    n = w.shape[1]
    tm = min(TM_PROJ, t)
    tn = D_MODEL
    half = RET_KEY_DIM // 2
    return pl.pallas_call(
        _ret_proj_kernel,
        out_shape=jax.ShapeDtypeStruct((t, n), BF16),
        grid=(t // tm, n // tn),
        in_specs=[pl.BlockSpec((tm, D_MODEL), lambda i, j: (i, 0)),
                  pl.BlockSpec((D_MODEL, tn), lambda i, j: (0, j)),
                  pl.BlockSpec((tm, half), lambda i, j: (i, 0)),
                  pl.BlockSpec((tm, half), lambda i, j: (i, 0))],
        out_specs=pl.BlockSpec((tm, tn), lambda i, j: (i, j)),
        compiler_params=_cparams(("parallel", "arbitrary")),
        name="ret_proj",
    )(x2d, w, cos, sin)


def _retention_kernel(lg_ref, q_ref, k_ref, v_ref, g_ref, o_ref, state_ref, decay_ref, *, chunk):
    dk, dv = RET_KEY_DIM, RET_VAL_DIM

    @pl.when(pl.program_id(1) == 0)
    def _():
        state_ref[...] = jnp.zeros_like(state_ref)
        row = lax.broadcasted_iota(jnp.int32, (chunk, chunk), 0)
        col = lax.broadcasted_iota(jnp.int32, (chunk, chunk), 1)
        diff = (row - col).astype(F32)
        for h in range(RET_HEADS):
            decay_ref[h] = jnp.where(diff >= 0, jnp.exp(jnp.maximum(diff, 0.0) * lg_ref[h]), 0.0)

    idx = lax.broadcasted_iota(jnp.int32, (chunk, 1), 0).astype(F32)
    for h in range(RET_HEADS):
        lg = lg_ref[h]
        q = q_ref[:, h * dk:(h + 1) * dk]
        k = k_ref[:, h * dk:(h + 1) * dk]
        v = v_ref[:, h * dv:(h + 1) * dv]
        scores = lax.dot_general(q, k, (((1,), (1,)), ((), ())), preferred_element_type=F32) * decay_ref[h]
        o = jnp.dot(scores.astype(BF16), v, preferred_element_type=F32)
        state = state_ref[h]
        qd = (q.astype(F32) * jnp.exp((idx + 1.0) * lg)).astype(BF16)
        o = o + jnp.dot(qd, state.astype(BF16), preferred_element_type=F32)
        kd = (k.astype(F32) * jnp.exp((chunk - 1.0 - idx) * lg)).astype(BF16)
        state_ref[h] = state * jnp.exp(chunk * lg) + lax.dot_general(
            kd, v, (((0,), (0,)), ((), ())), preferred_element_type=F32)
        mu = jnp.mean(o, axis=1, keepdims=True)
        oc = o - mu
        var = jnp.mean(oc * oc, axis=1, keepdims=True)
        on = oc * lax.rsqrt(var + LN_EPS)
        g = g_ref[:, h * dv:(h + 1) * dv].astype(F32)
        o_ref[:, h * dv:(h + 1) * dv] = (g * jax.nn.sigmoid(g) * on).astype(o_ref.dtype)


def _retention(proj, log_gamma, batch, seq):
    chunk = min(RET_CHUNK, seq)
    nc = seq // chunk
    dq = RET_HEADS * RET_KEY_DIM
    dvt = RET_HEADS * RET_VAL_DIM
    grid_spec = pltpu.PrefetchScalarGridSpec(
        num_scalar_prefetch=1,
        grid=(batch, nc),
        in_specs=[pl.BlockSpec((chunk, dq), lambda b, c, lg: (b * nc + c, 0)),
                  pl.BlockSpec((chunk, dq), lambda b, c, lg: (b * nc + c, 1)),
                  pl.BlockSpec((chunk, dvt), lambda b, c, lg: (b * nc + c, 1)),
                  pl.BlockSpec((chunk, dvt), lambda b, c, lg: (b * nc + c, 2))],
        out_specs=pl.BlockSpec((chunk, dvt), lambda b, c, lg: (b * nc + c, 0)),
        scratch_shapes=[pltpu.VMEM((RET_HEADS, RET_KEY_DIM, RET_VAL_DIM), F32),
                        pltpu.VMEM((RET_HEADS, chunk, chunk), F32)],
    )
    return pl.pallas_call(
        functools.partial(_retention_kernel, chunk=chunk),
        out_shape=jax.ShapeDtypeStruct((batch * seq, dvt), BF16),
        grid_spec=grid_spec,
        compiler_params=_cparams(("parallel", "arbitrary")),
        name="retention",
    )(log_gamma, proj, proj, proj, proj)


def _layer_norm(y, g, b):
    mu = jnp.mean(y, axis=1, keepdims=True)
    yc = y - mu
    var = jnp.mean(yc * yc, axis=1, keepdims=True)
    return yc * lax.rsqrt(var + LN_EPS) * g + b


def _out_router_kernel(a_ref, w_ref, x_ref, g_ref, b_ref, wrhl_ref, wrh_ref, br_ref, o_ref):
    tm = a_ref.shape[0]
    h = jnp.dot(a_ref[...], w_ref[...], preferred_element_type=F32)
    x1 = _layer_norm(DEEPNORM_ALPHA * x_ref[...] + h, g_ref[...], b_ref[...])
    o_ref[:, :D_MODEL] = x1

    logits = _dot_split(x1, wrhl_ref[...], wrh_ref[...]) + br_ref[...]
    lane = lax.broadcasted_iota(jnp.int32, (tm, LANES), 1)

    def first_argmax(vals, vmax):
        return jnp.min(jnp.where(vals == vmax, lane, LANES), axis=1, keepdims=True)

    gl = jnp.where(lane < N_GROUPS, logits, NEG_BIG)
    gmax = jnp.max(gl, axis=1, keepdims=True)
    gsum = jnp.sum(jnp.where(lane < N_GROUPS, jnp.exp(gl - gmax), 0.0), axis=1, keepdims=True)
    g_val = 1.0 / gsum
    g_idx = first_argmax(gl, gmax)
    lo = CW_LANE0 + EXPERTS_PER_GROUP * g_idx
    in_group = (lane >= lo) & (lane < lo + EXPERTS_PER_GROUP)
    el = jnp.where(in_group, logits, NEG_BIG)
    e1 = jnp.max(el, axis=1, keepdims=True)
    i1 = first_argmax(el, e1)
    el2 = jnp.where(lane == i1, NEG_BIG, el)
    e2 = jnp.max(el2, axis=1, keepdims=True)
    i2 = first_argmax(el2, e2)
    r = jnp.exp(e2 - e1)
    w1 = g_val / (1.0 + r)
    w2 = g_val * r / (1.0 + r)
    ext = jnp.where(lane == i1, w1, 0.0) + jnp.where(lane == i2, w2, 0.0)
    ext = jnp.where(lane == GIDX_LANE, g_idx.astype(F32), ext)
    o_ref[:, D_MODEL:] = ext


def _out_router(a, w_out, x2d, ln_g, ln_b, w_r, b_r):
    t, din = a.shape
    tm = min(TM_OUT, t)
    w_r_hilo, w_r_hi = _split_weight(w_r)
    return pl.pallas_call(
        _out_router_kernel,
        out_shape=jax.ShapeDtypeStruct((t, ROW_W), F32),
        grid=(t // tm,),
        in_specs=[pl.BlockSpec((tm, din), lambda i: (i, 0)),
                  pl.BlockSpec((din, D_MODEL), lambda i: (0, 0)),
                  pl.BlockSpec((tm, D_MODEL), lambda i: (i, 0)),
                  pl.BlockSpec((1, D_MODEL), lambda i: (0, 0)),
                  pl.BlockSpec((1, D_MODEL), lambda i: (0, 0)),
                  pl.BlockSpec((D_MODEL, 2 * LANES), lambda i: (0, 0)),
                  pl.BlockSpec((D_MODEL, LANES), lambda i: (0, 0)),
                  pl.BlockSpec((1, LANES), lambda i: (0, 0))],
        out_specs=pl.BlockSpec((tm, ROW_W), lambda i: (i, 0)),
        compiler_params=_cparams(("parallel",)),
        name="out_router",
    )(a, w_out, x2d, ln_g, ln_b, w_r_hilo, w_r_hi, b_r)


def _row_copy(src_ref, src_row, dst_ref, dst_row, sem):
    return pltpu.make_async_copy(src_ref.at[pl.ds(src_row, 1)], dst_ref.at[pl.ds(dst_row, 1)], sem)


def _dispatch_kernel(pos_ref, x_ref, init_ref, xs_ref, sem):
    del init_ref
    tm = x_ref.shape[0]
    base = pl.program_id(0) * tm

    def issue(r, _):
        _row_copy(x_ref, r, xs_ref, pos_ref[base + r], sem).start()
        return 0

    lax.fori_loop(0, tm, issue, 0, unroll=8)

    def drain(r, _):
        _row_copy(x_ref, r, xs_ref, pos_ref[base + r], sem).wait()
        return 0

    lax.fori_loop(0, tm, drain, 0, unroll=8)


def _dispatch(pos, x1ext, n_rows):
    t = x1ext.shape[0]
    tm = min(TM_ROWS, t)
    grid_spec = pltpu.PrefetchScalarGridSpec(
        num_scalar_prefetch=1,
        grid=(t // tm,),
        in_specs=[pl.BlockSpec((tm, ROW_W), lambda i, pos: (i, 0)),
                  pl.BlockSpec(memory_space=pl.ANY)],
        out_specs=pl.BlockSpec(memory_space=pl.ANY),
        scratch_shapes=[pltpu.SemaphoreType.DMA(())],
    )
    return pl.pallas_call(
        _dispatch_kernel,
        out_shape=jax.ShapeDtypeStruct((n_rows, ROW_W), F32),
        grid_spec=grid_spec,
        input_output_aliases={2: 0},
        compiler_params=_cparams(("arbitrary",)),
        name="dispatch",
    )(pos, x1ext, jnp.zeros((n_rows, ROW_W), F32))


def _moe_kernel(tg_ref, xs_ref, wg_ref, wu_ref, wd_ref, g_ref, b_ref, o_ref, xb_ref, acc_ref):
    i = pl.program_id(0)
    e = pl.program_id(1)
    tm = xs_ref.shape[0]

    @pl.when(e == 0)
    def _():
        xb_ref[...] = xs_ref[:, :D_MODEL].astype(BF16)
        acc_ref[...] = jnp.zeros_like(acc_ref)

    lane = lax.broadcasted_iota(jnp.int32, (tm, EXT), 1)
    cw_lane = CW_LANE0 + EXPERTS_PER_GROUP * tg_ref[i] + e
    cw = jnp.sum(jnp.where(lane == cw_lane, xs_ref[:, D_MODEL:], 0.0), axis=1, keepdims=True)
    xb = xb_ref[...]
    hg = jnp.dot(xb, wg_ref[...], preferred_element_type=F32)
    hu = jnp.dot(xb, wu_ref[...], preferred_element_type=F32)
    hh = (hg * jax.nn.sigmoid(hg) * hu * cw).astype(BF16)
    acc_ref[...] += jnp.dot(hh, wd_ref[...], preferred_element_type=F32)

    @pl.when(e == EXPERTS_PER_GROUP - 1)
    def _():
        y = DEEPNORM_ALPHA * xs_ref[:, :D_MODEL] + acc_ref[...]
        o_ref[...] = _layer_norm(y, g_ref[...], b_ref[...])


def _moe(tile_group, xs, w_gate, w_up, w_down, ln_g, ln_b):
    n_rows = xs.shape[0]
    tm = TM_MOE
    epg = EXPERTS_PER_GROUP
    grid_spec = pltpu.PrefetchScalarGridSpec(
        num_scalar_prefetch=1,
        grid=(n_rows // tm, epg),
        in_specs=[pl.BlockSpec((tm, ROW_W), lambda i, e, tg: (i, 0)),
                  pl.BlockSpec((None, D_MODEL, D_EXPERT), lambda i, e, tg: (tg[i] * epg + e, 0, 0)),
                  pl.BlockSpec((None, D_MODEL, D_EXPERT), lambda i, e, tg: (tg[i] * epg + e, 0, 0)),
                  pl.BlockSpec((None, D_EXPERT, D_MODEL), lambda i, e, tg: (tg[i] * epg + e, 0, 0)),
                  pl.BlockSpec((1, D_MODEL), lambda i, e, tg: (0, 0)),
                  pl.BlockSpec((1, D_MODEL), lambda i, e, tg: (0, 0))],
        out_specs=pl.BlockSpec((tm, D_MODEL), lambda i, e, tg: (i, 0)),
        scratch_shapes=[pltpu.VMEM((tm, D_MODEL), BF16), pltpu.VMEM((tm, D_MODEL), F32)],
    )
    return pl.pallas_call(
        _moe_kernel,
        out_shape=jax.ShapeDtypeStruct((n_rows, D_MODEL), F32),
        grid_spec=grid_spec,
        compiler_params=_cparams(("parallel", "arbitrary")),
        name="moe",
    )(tile_group, xs, w_gate, w_up, w_down, ln_g, ln_b)


def _combine_ple_kernel(pos_ref, x2s_ref, p_ref, wg_ref, bg_ref, wp_ref, o_ref, buf_ref, sem):
    tm = o_ref.shape[0]
    base = pl.program_id(0) * tm

    def issue(r, _):
        _row_copy(x2s_ref, pos_ref[base + r], buf_ref, r, sem).start()
        return 0

    lax.fori_loop(0, tm, issue, 0, unroll=8)

    def drain(r, _):
        _row_copy(x2s_ref, pos_ref[base + r], buf_ref, r, sem).wait()
        return 0

    lax.fori_loop(0, tm, drain, 0, unroll=8)

    x2 = buf_ref[...]
    gate = jax.nn.sigmoid(jnp.dot(x2.astype(BF16), wg_ref[...], preferred_element_type=F32) + bg_ref[...])
    pp = jnp.dot(p_ref[...].astype(BF16), wp_ref[...], preferred_element_type=F32)
    o_ref[...] = x2 + gate * pp


def _combine_ple(pos, x2s, p2d, w_gate, b_gate, w_proj):
    t = p2d.shape[0]
    tm = min(TM_ROWS, t)
    grid_spec = pltpu.PrefetchScalarGridSpec(
        num_scalar_prefetch=1,
        grid=(t // tm,),
        in_specs=[pl.BlockSpec(memory_space=pl.ANY),
                  pl.BlockSpec((tm, PLE_DIM), lambda i, pos: (i, 0)),
                  pl.BlockSpec((D_MODEL, D_MODEL), lambda i, pos: (0, 0)),
                  pl.BlockSpec((1, D_MODEL), lambda i, pos: (0, 0)),
                  pl.BlockSpec((PLE_DIM, D_MODEL), lambda i, pos: (0, 0))],
        out_specs=pl.BlockSpec((tm, D_MODEL), lambda i, pos: (i, 0)),
        scratch_shapes=[pltpu.VMEM((tm, D_MODEL), F32), pltpu.SemaphoreType.DMA(())],
    )
    return pl.pallas_call(
        _combine_ple_kernel,
        out_shape=jax.ShapeDtypeStruct((t, D_MODEL), F32),
        grid_spec=grid_spec,
        compiler_params=_cparams(("arbitrary",)),
        name="combine_ple",
    )(pos, x2s, p2d, w_gate, b_gate, w_proj)


def _sorted_positions(g_idx, n_tiles):
    onehot = (g_idx[:, None] == jnp.arange(N_GROUPS, dtype=jnp.int32)[None, :]).astype(jnp.int32)
    counts = jnp.sum(onehot, axis=0)
    tiles = (counts + TM_MOE - 1) // TM_MOE
    tile_end = jnp.cumsum(tiles)
    tile_start = tile_end - tiles
    rank = jnp.sum((jnp.cumsum(onehot, axis=0) - onehot) * onehot, axis=1)
    pos = jnp.sum(onehot * (tile_start * TM_MOE)[None, :], axis=1) + rank
    tile_ids = jnp.arange(n_tiles, dtype=jnp.int32)
    tile_group = jnp.sum((tile_ids[:, None] >= tile_end[None, :]).astype(jnp.int32), axis=1)
    return pos.astype(jnp.int32), jnp.minimum(tile_group, N_GROUPS - 1).astype(jnp.int32)


def _moe_block(a, w_out, x2d, p2d, ln1_g, ln1_b, ln2_g, ln2_b, w_group, b_group, w_router, b_router,
               w_gate, w_up, w_down, ple_w_proj, ple_w_gate, ple_b_gate):
    t = x2d.shape[0]
    n_tiles = t // TM_MOE + N_GROUPS
    pad = LANES - CW_LANE0 - N_EXPERTS
    w_r = jnp.concatenate([w_group, w_router, jnp.zeros((D_MODEL, pad), F32)], axis=1)
    b_r = jnp.concatenate([b_group, b_router, jnp.zeros((pad,), F32)])[None, :]
    x1ext = _out_router(a, w_out.astype(BF16), x2d, ln1_g[None, :], ln1_b[None, :], w_r, b_r)
    g_idx = x1ext[:, D_MODEL + GIDX_LANE].astype(jnp.int32)
    pos, tile_group = _sorted_positions(g_idx, n_tiles)
    xs = _dispatch(pos, x1ext, n_tiles * TM_MOE)
    x2s = _moe(tile_group, xs, w_gate.astype(BF16), w_up.astype(BF16), w_down.astype(BF16),
               ln2_g[None, :], ln2_b[None, :])
    return _combine_ple(pos, x2s, p2d, ple_w_gate.astype(BF16), ple_b_gate[None, :],
                        ple_w_proj.astype(BF16))


@jax.jit
def _forward(x, p, positions, fox_w_in, fox_b_f, fox_w_out, ret_w_in, ret_w_out, ln1_g, ln1_b, ln2_g, ln2_b,
             moe_w_group, moe_b_group, moe_w_router, moe_b_router, moe_w_gate, moe_w_up, moe_w_down,
             ple_w_proj, ple_w_gate, ple_b_gate):
    batch, seq, d = x.shape
    t = batch * seq
    x2d = x.reshape(t, d)

    def moe_args(i):
        return (ln1_g[i], ln1_b[i], ln2_g[i], ln2_b[i], moe_w_group[i], moe_b_group[i], moe_w_router[i],
                moe_b_router[i], moe_w_gate[i], moe_w_up[i], moe_w_down[i], ple_w_proj[i], ple_w_gate[i],
                ple_b_gate[i])

    w_in = fox_w_in[0]
    scale = jnp.concatenate([jnp.full((D_MODEL,), FOX_HEAD_DIM ** -0.5 * LOG2E, F32),
                             jnp.ones((2 * D_MODEL,), F32)])
    w_qkv = (w_in[:, :3 * D_MODEL] * scale[None, :]).astype(BF16)
    qkv = _proj(x2d, w_qkv, D_MODEL)
    c = _fgate(x2d, w_in[:, 3 * D_MODEL:], fox_b_f[0][None, :], batch, seq)
    c3 = c.reshape(batch, seq, FOX_HEADS).transpose(0, 2, 1)
    attn = _fox_attention(qkv, c3[:, :, :, None], c3[:, :, None, :], batch, seq)
    x2d = _moe_block(attn, fox_w_out[0], x2d, p[0].reshape(t, PLE_DIM), *moe_args(0))

    w_in = ret_w_in[0]
    scale = jnp.concatenate([jnp.ones((D_MODEL,), F32), jnp.full((D_MODEL,), RET_KEY_DIM ** -0.5, F32),
                             jnp.ones((4 * D_MODEL,), F32)])
    inv_freq = ROPE_BASE ** (-jnp.arange(0, RET_KEY_DIM, 2, dtype=F32) / RET_KEY_DIM)
    cos, sin = _rope_tables(positions.astype(F32).reshape(t, 1), inv_freq[None, :])
    proj = _ret_proj(x2d, (w_in * scale[None, :]).astype(BF16), cos, sin)
    log_gamma = jnp.log(1.0 - 2.0 ** (-5.0 - jnp.arange(RET_HEADS, dtype=F32)))
    ret = _retention(proj, log_gamma, batch, seq)
    x2d = _moe_block(ret, ret_w_out[0], x2d, p[1].reshape(t, PLE_DIM), *moe_args(1))
    return x2d.reshape(batch, seq, d)


def kernel(x, p, positions, fox_w_in, fox_b_f, fox_w_out, ret_w_in, ret_w_out, ln1_g, ln1_b, ln2_g, ln2_b,
           moe_w_group, moe_b_group, moe_w_router, moe_b_router, moe_w_gate, moe_w_up, moe_w_down,
           ple_w_proj, ple_w_gate, ple_b_gate):
    return _forward(x, p, positions, fox_w_in, fox_b_f, fox_w_out, ret_w_in, ret_w_out, ln1_g, ln1_b,
                    ln2_g, ln2_b, moe_w_group, moe_b_group, moe_w_router, moe_b_router, moe_w_gate,
                    moe_w_up, moe_w_down, ple_w_proj, ple_w_gate, ple_b_gate)
```

```python
import functools

import jax
import jax.numpy as jnp
from jax import lax
from jax.experimental import pallas as pl
from jax.experimental.pallas import tpu as pltpu

F32 = jnp.float32
BF16 = jnp.bfloat16

D_MODEL = 1024
PLE_DIM = 256
FOX_HEADS = 16
FOX_HEAD_DIM = D_MODEL // FOX_HEADS
RET_HEADS = 4
RET_KEY_DIM = D_MODEL // RET_HEADS
RET_VAL_DIM = 2 * D_MODEL // RET_HEADS
ROPE_BASE = 10000.0
N_GROUPS = 4
EXPERTS_PER_GROUP = 4
N_EXPERTS = N_GROUPS * EXPERTS_PER_GROUP
D_EXPERT = 512
DEPTH = 2
DEEPNORM_ALPHA = (2.0 * DEPTH) ** 0.25
LN_EPS = 1e-5

LANES = 128
EXT = LANES
ROW_W = D_MODEL + EXT
GIDX_LANE = 0
CW_LANE0 = 4
NEG_BIG = -1e30
LOG2E = 1.4426950408889634
VMEM_LIMIT = 56 * 1024 * 1024

TM_PROJ = 1024
TS_GATE = 512
TQ_ATTN = 512
ATTN_ROW_BLOCK = 32
TM_OUT = 256
TM_ROWS = 256
TM_MOE = 1024
RET_CHUNK = 256


def _cparams(sem):
    return pltpu.CompilerParams(dimension_semantics=sem, vmem_limit_bytes=VMEM_LIMIT)


def _proj_kernel(a_ref, w_ref, o_ref):
    a = a_ref[...].astype(BF16)
    o_ref[...] = jnp.dot(a, w_ref[...], preferred_element_type=F32).astype(o_ref.dtype)


def _proj(a, w, tn):
    m, k = a.shape
    n = w.shape[1]
    tm = min(TM_PROJ, m)
    return pl.pallas_call(
        _proj_kernel,
        out_shape=jax.ShapeDtypeStruct((m, n), BF16),
        grid=(m // tm, n // tn),
        in_specs=[pl.BlockSpec((tm, k), lambda i, j: (i, 0)),
                  pl.BlockSpec((k, tn), lambda i, j: (0, j))],
        out_specs=pl.BlockSpec((tm, tn), lambda i, j: (i, j)),
        compiler_params=_cparams(("parallel", "arbitrary")),
        name="proj",
    )(a, w)


def _split_bf16(x):
    hi = x.astype(BF16)
    return hi, (x - hi.astype(F32)).astype(BF16)


def _split_weight(w):
    hi, lo = _split_bf16(w)
    return jnp.concatenate([hi, lo], axis=1), hi


def _dot_split(x, w_hilo, w_hi):
    x_hi, x_lo = _split_bf16(x)
    r = jnp.dot(x_hi, w_hilo, preferred_element_type=F32)
    return r[:, :LANES] + r[:, LANES:] + jnp.dot(x_lo, w_hi, preferred_element_type=F32)


def _fgate_kernel(x_ref, whl_ref, wh_ref, b_ref, c_ref, carry_ref):
    @pl.when(pl.program_id(1) == 0)
    def _():
        carry_ref[...] = jnp.zeros_like(carry_ref)

    ts = x_ref.shape[0]
    z = _dot_split(x_ref[...], whl_ref[...], wh_ref[...]) + b_ref[...]
    logf = (jnp.minimum(z, 0.0) - jnp.log1p(jnp.exp(-jnp.abs(z)))) * LOG2E
    l1 = logf.astype(BF16)
    rest = logf - l1.astype(F32)
    l2, l3 = _split_bf16(rest)
    row = lax.broadcasted_iota(jnp.int32, (ts, ts), 0)
    col = lax.broadcasted_iota(jnp.int32, (ts, ts), 1)
    tri = jnp.where(row >= col, 1.0, 0.0).astype(BF16)
    parts = jnp.dot(tri, jnp.concatenate([l1, l2, l3], axis=1), preferred_element_type=F32)
    c = parts[:, :LANES] + parts[:, LANES:2 * LANES] + parts[:, 2 * LANES:] + carry_ref[...]
    c_ref[...] = c[:, :c_ref.shape[1]]
    carry_ref[...] = c[ts - 1:ts, :]


def _fgate(x2d, w_f, b_f, batch, seq):
    ts = min(TS_GATE, seq)
    ns = seq // ts
    h = w_f.shape[1]
    w_hilo, w_hi = _split_weight(jnp.pad(w_f, ((0, 0), (0, LANES - h))))
    b_pad = jnp.pad(b_f, ((0, 0), (0, LANES - h)))
    return pl.pallas_call(
        _fgate_kernel,
        out_shape=jax.ShapeDtypeStruct((batch * seq, h), F32),
        grid=(batch, ns),
        in_specs=[pl.BlockSpec((ts, D_MODEL), lambda b, s: (b * ns + s, 0)),
                  pl.BlockSpec((D_MODEL, 2 * LANES), lambda b, s: (0, 0)),
                  pl.BlockSpec((D_MODEL, LANES), lambda b, s: (0, 0)),
                  pl.BlockSpec((1, LANES), lambda b, s: (0, 0))],
        out_specs=pl.BlockSpec((ts, h), lambda b, s: (b * ns + s, 0)),
        scratch_shapes=[pltpu.VMEM((1, LANES), F32)],
        compiler_params=_cparams(("parallel", "arbitrary")),
        name="fgate",
    )(x2d, w_hilo, w_hi, b_pad)


def _fox_attn_kernel(q_ref, k_ref, v_ref, cq_ref, ck_ref, o_ref, m_ref, acc_ref, *, tq):
    qi = pl.program_id(2)
    q = q_ref[...]
    lane = lax.broadcasted_iota(jnp.int32, (tq, LANES), 1)
    row = lax.broadcasted_iota(jnp.int32, (tq, tq), 0)
    col = lax.broadcasted_iota(jnp.int32, (tq, tq), 1)
    causal = row >= col
    head0 = lane < FOX_HEAD_DIM
    q2 = jnp.concatenate([jnp.where(head0, q, jnp.zeros_like(q)),
                          jnp.where(head0, jnp.zeros_like(q), q)], axis=0)
    m_ref[...] = jnp.full(m_ref.shape, NEG_BIG, F32)
    acc_ref[...] = jnp.zeros(acc_ref.shape, F32)
    lane_k = lax.broadcasted_iota(jnp.int32, (tq, LANES), 1)
    c_tok = cq_ref[...]
    lane_h = lax.broadcasted_iota(jnp.int32, c_tok.shape, 1)
    cq_heads = [jnp.sum(jnp.where(lane_h == 2 * pl.program_id(1) + h, c_tok, 0.0), axis=1, keepdims=True)
                for h in range(2)]
    rb = ATTN_ROW_BLOCK

    def step(kb, masked):
        ks = pl.multiple_of(kb * tq, tq)
        kblk = k_ref[pl.ds(ks, tq), :]
        vblk = v_ref[pl.ds(ks, tq), :]
        s2 = lax.dot_general(q2, kblk, (((1,), (1,)), ((), ())), preferred_element_type=F32)
        for h in range(2):
            ck = ck_ref[h, :, pl.ds(ks, tq)]
            vh = jnp.where(lane_k < FOX_HEAD_DIM if h == 0 else lane_k >= FOX_HEAD_DIM,
                           vblk, jnp.ones_like(vblk))
            p_blocks = []
            m_blocks = []
            cq_all = cq_heads[h]
            m_all = m_ref[h]
            for r0 in range(0, tq, rb):
                t = s2[h * tq + r0:h * tq + r0 + rb, :] - ck
                if masked:
                    t = jnp.where(causal[r0:r0 + rb, :], t, NEG_BIG)
                cq = cq_all[r0:r0 + rb, :]
                m_new = jnp.maximum(m_all[r0:r0 + rb, :], cq + jnp.max(t, axis=1, keepdims=True))
                m_blocks.append(m_new)
                p_blocks.append(jnp.exp2(t + (cq - m_new)).astype(BF16))
            m_new_all = jnp.concatenate(m_blocks, axis=0)
            pv = jnp.dot(jnp.concatenate(p_blocks, axis=0), vh, preferred_element_type=F32)
            acc_ref[h] = jnp.exp2(m_all - m_new_all) * acc_ref[h] + pv
            m_ref[h] = m_new_all

    def body(kb, _):
        step(kb, False)
        return 0

    lax.fori_loop(0, qi, body, 0)
    step(qi, True)
    a0 = acc_ref[0]
    a1 = acc_ref[1]
    o0 = a0 / pltpu.roll(a0, FOX_HEAD_DIM, 1)
    o1 = a1 / pltpu.roll(a1, FOX_HEAD_DIM, 1)
    o_ref[...] = jnp.where(head0, o0, o1).astype(o_ref.dtype)


def _fox_attention(qkv, cq, ck, batch, seq):
    tq = min(TQ_ATTN, seq)
    nq = seq // tq
    hp = FOX_HEADS // 2
    return pl.pallas_call(
        functools.partial(_fox_attn_kernel, tq=tq),
        out_shape=jax.ShapeDtypeStruct((batch * seq, D_MODEL), BF16),
        grid=(batch, hp, nq),
        in_specs=[pl.BlockSpec((tq, LANES), lambda b, p, i: (b * nq + i, p)),
                  pl.BlockSpec((seq, LANES), lambda b, p, i: (b, hp + p)),
                  pl.BlockSpec((seq, LANES), lambda b, p, i: (b, 2 * hp + p)),
                  pl.BlockSpec((tq, FOX_HEADS), lambda b, p, i: (b * nq + i, 0)),
                  pl.BlockSpec((None, 2, 1, seq), lambda b, p, i: (b, p, 0, 0))],
        out_specs=pl.BlockSpec((tq, LANES), lambda b, p, i: (b * nq + i, p)),
        scratch_shapes=[pltpu.VMEM((2, tq, 1), F32), pltpu.VMEM((2, tq, LANES), F32)],
        compiler_params=_cparams(("parallel", "parallel", "arbitrary")),
        name="fox_attn",
    )(qkv, qkv, qkv, cq, ck)


def _rope_kernel(pos_ref, freq_ref, cos_ref, sin_ref):
    ang = pos_ref[...] * freq_ref[...]
    cos_ref[...] = jnp.cos(ang)
    sin_ref[...] = jnp.sin(ang)


def _rope_tables(pos_f, inv_freq):
    t = pos_f.shape[0]
    half = inv_freq.shape[1]
    tm = min(1024, t)
    return pl.pallas_call(
        _rope_kernel,
        out_shape=(jax.ShapeDtypeStruct((t, half), F32), jax.ShapeDtypeStruct((t, half), F32)),
        grid=(t // tm,),
        in_specs=[pl.BlockSpec((tm, 1), lambda i: (i, 0)),
                  pl.BlockSpec((1, half), lambda i: (0, 0))],
        out_specs=(pl.BlockSpec((tm, half), lambda i: (i, 0)),
                   pl.BlockSpec((tm, half), lambda i: (i, 0))),
        compiler_params=_cparams(("parallel",)),
        name="rope_tables",
    )(pos_f, inv_freq)


def _ret_proj_kernel(a_ref, w_ref, cos_ref, sin_ref, o_ref):
    j = pl.program_id(1)
    a = a_ref[...].astype(BF16)
    acc = jnp.dot(a, w_ref[...], preferred_element_type=F32)

    @pl.when(j < 2)
    def _():
        c = cos_ref[...]
        s = sin_ref[...]
        half = RET_KEY_DIM // 2
        for h in range(RET_HEADS):
            x1 = acc[:, h * RET_KEY_DIM:h * RET_KEY_DIM + half]
            x2 = acc[:, h * RET_KEY_DIM + half:(h + 1) * RET_KEY_DIM]
            o_ref[:, h * RET_KEY_DIM:h * RET_KEY_DIM + half] = (x1 * c - x2 * s).astype(o_ref.dtype)
            o_ref[:, h * RET_KEY_DIM + half:(h + 1) * RET_KEY_DIM] = (x2 * c + x1 * s).astype(o_ref.dtype)

    @pl.when(j >= 2)
    def _():
        o_ref[...] = acc.astype(o_ref.dtype)


def _ret_proj(x2d, w, cos, sin):
    t = x2d.shape[0]
    n = w.shape[1]
    tm = min(TM_PROJ, t)
    tn = D_MODEL
    half = RET_KEY_DIM // 2
    return pl.pallas_call(
        _ret_proj_kernel,
        out_shape=jax.ShapeDtypeStruct((t, n), BF16),
        grid=(t // tm, n // tn),
        in_specs=[pl.BlockSpec((tm, D_MODEL), lambda i, j: (i, 0)),
                  pl.BlockSpec((D_MODEL, tn), lambda i, j: (0, j)),
                  pl.BlockSpec((tm, half), lambda i, j: (i, 0)),
                  pl.BlockSpec((tm, half), lambda i, j: (i, 0))],
        out_specs=pl.BlockSpec((tm, tn), lambda i, j: (i, j)),
        compiler_params=_cparams(("parallel", "arbitrary")),
        name="ret_proj",
    )(x2d, w, cos, sin)


def _retention_kernel(lg_ref, q_ref, k_ref, v_ref, g_ref, o_ref, state_ref, decay_ref, *, chunk):
    dk, dv = RET_KEY_DIM, RET_VAL_DIM

    @pl.when(pl.program_id(1) == 0)
    def _():
        state_ref[...] = jnp.zeros_like(state_ref)
        row = lax.broadcasted_iota(jnp.int32, (chunk, chunk), 0)
        col = lax.broadcasted_iota(jnp.int32, (chunk, chunk), 1)
        diff = (row - col).astype(F32)
        for h in range(RET_HEADS):
            decay_ref[h] = jnp.where(diff >= 0, jnp.exp(jnp.maximum(diff, 0.0) * lg_ref[h]), 0.0)

    idx = lax.broadcasted_iota(jnp.int32, (chunk, 1), 0).astype(F32)
    for h in range(RET_HEADS):
        lg = lg_ref[h]
        q = q_ref[:, h * dk:(h + 1) * dk]
        k = k_ref[:, h * dk:(h + 1) * dk]
        v = v_ref[:, h * dv:(h + 1) * dv]
        scores = lax.dot_general(q, k, (((1,), (1,)), ((), ())), preferred_element_type=F32) * decay_ref[h]
        o = jnp.dot(scores.astype(BF16), v, preferred_element_type=F32)
        state = state_ref[h]
        qd = (q.astype(F32) * jnp.exp((idx + 1.0) * lg)).astype(BF16)
        o = o + jnp.dot(qd, state.astype(BF16), preferred_element_type=F32)
        kd = (k.astype(F32) * jnp.exp((chunk - 1.0 - idx) * lg)).astype(BF16)
        state_ref[h] = state * jnp.exp(chunk * lg) + lax.dot_general(
            kd, v, (((0,), (0,)), ((), ())), preferred_element_type=F32)
        mu = jnp.mean(o, axis=1, keepdims=True)
        oc = o - mu
        var = jnp.mean(oc * oc, axis=1, keepdims=True)
        on = oc * lax.rsqrt(var + LN_EPS)
        g = g_ref[:, h * dv:(h + 1) * dv].astype(F32)
        o_ref[:, h * dv:(h + 1) * dv] = (g * jax.nn.sigmoid(g) * on).astype(o_ref.dtype)


def _retention(proj, log_gamma, batch, seq):
    chunk = min(RET_CHUNK, seq)
    nc = seq // chunk
    dq = RET_HEADS * RET_KEY_DIM
    dvt = RET_HEADS * RET_VAL_DIM
    grid_spec = pltpu.PrefetchScalarGridSpec(
        num_scalar_prefetch=1,
        grid=(batch, nc),
        in_specs=[pl.BlockSpec((chunk, dq), lambda b, c, lg: (b * nc + c, 0)),
                  pl.BlockSpec((chunk, dq), lambda b, c, lg: (b * nc + c, 1)),
                  pl.BlockSpec((chunk, dvt), lambda b, c, lg: (b * nc + c, 1)),
                  pl.BlockSpec((chunk, dvt), lambda b, c, lg: (b * nc + c, 2))],
        out_specs=pl.BlockSpec((chunk, dvt), lambda b, c, lg: (b * nc + c, 0)),
        scratch_shapes=[pltpu.VMEM((RET_HEADS, RET_KEY_DIM, RET_VAL_DIM), F32),
                        pltpu.VMEM((RET_HEADS, chunk, chunk), F32)],
    )
    return pl.pallas_call(
        functools.partial(_retention_kernel, chunk=chunk),
        out_shape=jax.ShapeDtypeStruct((batch * seq, dvt), BF16),
        grid_spec=grid_spec,
        compiler_params=_cparams(("parallel", "arbitrary")),
        name="retention",
    )(log_gamma, proj, proj, proj, proj)


def _layer_norm(y, g, b):
    mu = jnp.mean(y, axis=1, keepdims=True)
    yc = y - mu
    var = jnp.mean(yc * yc, axis=1, keepdims=True)
    return yc * lax.rsqrt(var + LN_EPS) * g + b


def _out_router_kernel(a_ref, w_ref, x_ref, g_ref, b_ref, wrhl_ref, wrh_ref, br_ref, o_ref, ext_ref):
    tm = a_ref.shape[0]
    h = jnp.dot(a_ref[...], w_ref[...], preferred_element_type=F32)
    x1 = _layer_norm(DEEPNORM_ALPHA * x_ref[...] + h, g_ref[...], b_ref[...])
    o_ref[:, :D_MODEL] = x1

    logits = _dot_split(x1, wrhl_ref[...], wrh_ref[...]) + br_ref[...]
    lane = lax.broadcasted_iota(jnp.int32, (tm, LANES), 1)

    def first_argmax(vals, vmax):
        return jnp.min(jnp.where(vals == vmax, lane, LANES), axis=1, keepdims=True)

    gl = jnp.where(lane < N_GROUPS, logits, NEG_BIG)
    gmax = jnp.max(gl, axis=1, keepdims=True)
    gsum = jnp.sum(jnp.where(lane < N_GROUPS, jnp.exp(gl - gmax), 0.0), axis=1, keepdims=True)
    g_val = 1.0 / gsum
    g_idx = first_argmax(gl, gmax)
    lo = CW_LANE0 + EXPERTS_PER_GROUP * g_idx
    in_group = (lane >= lo) & (lane < lo + EXPERTS_PER_GROUP)
    el = jnp.where(in_group, logits, NEG_BIG)
    e1 = jnp.max(el, axis=1, keepdims=True)
    i1 = first_argmax(el, e1)
    el2 = jnp.where(lane == i1, NEG_BIG, el)
    e2 = jnp.max(el2, axis=1, keepdims=True)
    i2 = first_argmax(el2, e2)
    r = jnp.exp(e2 - e1)
    w1 = g_val / (1.0 + r)
    w2 = g_val * r / (1.0 + r)
    ext = jnp.where(lane == i1, w1, 0.0) + jnp.where(lane == i2, w2, 0.0)
    ext = jnp.where(lane == GIDX_LANE, g_idx.astype(F32), ext)
    o_ref[:, D_MODEL:] = ext
    ext_ref[...] = ext


def _out_router(a, w_out, x2d, ln_g, ln_b, w_r, b_r):
    t, din = a.shape
    tm = min(TM_OUT, t)
    w_r_hilo, w_r_hi = _split_weight(w_r)
    return pl.pallas_call(
        _out_router_kernel,
        out_shape=(jax.ShapeDtypeStruct((t, ROW_W), F32), jax.ShapeDtypeStruct((t, EXT), F32)),
        grid=(t // tm,),
        in_specs=[pl.BlockSpec((tm, din), lambda i: (i, 0)),
                  pl.BlockSpec((din, D_MODEL), lambda i: (0, 0)),
                  pl.BlockSpec((tm, D_MODEL), lambda i: (i, 0)),
                  pl.BlockSpec((1, D_MODEL), lambda i: (0, 0)),
                  pl.BlockSpec((1, D_MODEL), lambda i: (0, 0)),
                  pl.BlockSpec((D_MODEL, 2 * LANES), lambda i: (0, 0)),
                  pl.BlockSpec((D_MODEL, LANES), lambda i: (0, 0)),
                  pl.BlockSpec((1, LANES), lambda i: (0, 0))],
        out_specs=(pl.BlockSpec((tm, ROW_W), lambda i: (i, 0)), pl.BlockSpec((tm, EXT), lambda i: (i, 0))),
        compiler_params=_cparams(("parallel",)),
        name="out_router",
    )(a, w_out, x2d, ln_g, ln_b, w_r_hilo, w_r_hi, b_r)


def _row_copy(src_ref, src_row, dst_ref, dst_row, sem):
    return pltpu.make_async_copy(src_ref.at[pl.ds(src_row, 1)], dst_ref.at[pl.ds(dst_row, 1)], sem)


def _dispatch_kernel(pos_ref, x_ref, init_ref, xs_ref, sem):
    del init_ref
    tm = x_ref.shape[0]
    base = pl.program_id(0) * tm

    def issue(r, _):
        _row_copy(x_ref, r, xs_ref, pos_ref[base + r], sem).start()
        return 0

    lax.fori_loop(0, tm, issue, 0, unroll=8)
    pltpu.make_async_copy(x_ref, xs_ref.at[pl.ds(0, tm)], sem).wait()


def _dispatch(pos, x1ext, n_rows):
    t = x1ext.shape[0]
    tm = min(TM_ROWS, t)
    grid_spec = pltpu.PrefetchScalarGridSpec(
        num_scalar_prefetch=1,
        grid=(t // tm,),
        in_specs=[pl.BlockSpec((tm, ROW_W), lambda i, pos: (i, 0)),
                  pl.BlockSpec(memory_space=pl.ANY)],
        out_specs=pl.BlockSpec(memory_space=pl.ANY),
        scratch_shapes=[pltpu.SemaphoreType.DMA(())],
    )
    return pl.pallas_call(
        _dispatch_kernel,
        out_shape=jax.ShapeDtypeStruct((n_rows, ROW_W), F32),
        grid_spec=grid_spec,
        input_output_aliases={2: 0},
        compiler_params=_cparams(("arbitrary",)),
        name="dispatch",
    )(pos, x1ext, jnp.zeros((n_rows, ROW_W), F32))


def _moe_kernel(tg_ref, xs_ref, wg_ref, wu_ref, wd_ref, g_ref, b_ref, o_ref, xb_ref, acc_ref):
    i = pl.program_id(0)
    e = pl.program_id(1)
    tm = xs_ref.shape[0]
    group = tg_ref[i]
    valid = group < N_GROUPS

    @pl.when(valid)
    def _():
        @pl.when(e == 0)
        def _():
            xb_ref[...] = xs_ref[:, :D_MODEL].astype(BF16)
            acc_ref[...] = jnp.zeros_like(acc_ref)

        lane = lax.broadcasted_iota(jnp.int32, (tm, EXT), 1)
        cw_lane = CW_LANE0 + EXPERTS_PER_GROUP * group + e
        cw = jnp.sum(jnp.where(lane == cw_lane, xs_ref[:, D_MODEL:], 0.0), axis=1, keepdims=True)
        xb = xb_ref[...]
        hg = jnp.dot(xb, wg_ref[...].astype(BF16), preferred_element_type=F32)
        hu = jnp.dot(xb, wu_ref[...].astype(BF16), preferred_element_type=F32)
        hh = (hg * jax.nn.sigmoid(hg) * hu * cw).astype(BF16)
        acc_ref[...] += jnp.dot(hh, wd_ref[...].astype(BF16), preferred_element_type=F32)

        @pl.when(e == EXPERTS_PER_GROUP - 1)
        def _():
            y = DEEPNORM_ALPHA * xs_ref[:, :D_MODEL] + acc_ref[...]
            o_ref[...] = _layer_norm(y, g_ref[...], b_ref[...])

    @pl.when(jnp.logical_not(valid) & (e == EXPERTS_PER_GROUP - 1))
    def _():
        o_ref[...] = jnp.zeros_like(o_ref)


def _moe(tile_group, xs, w_gate, w_up, w_down, ln_g, ln_b):
    n_rows = xs.shape[0]
    tm = TM_MOE
    epg = EXPERTS_PER_GROUP

    def expert(i, e, tg):
        return (jnp.where(tg[i] < N_GROUPS, tg[i] * epg + e, N_EXPERTS - 1), 0, 0)

    grid_spec = pltpu.PrefetchScalarGridSpec(
        num_scalar_prefetch=1,
        grid=(n_rows // tm, epg),
        in_specs=[pl.BlockSpec((tm, ROW_W), lambda i, e, tg: (i, 0)),
                  pl.BlockSpec((None, D_MODEL, D_EXPERT), expert),
                  pl.BlockSpec((None, D_MODEL, D_EXPERT), expert),
                  pl.BlockSpec((None, D_EXPERT, D_MODEL), expert),
                  pl.BlockSpec((1, D_MODEL), lambda i, e, tg: (0, 0)),
                  pl.BlockSpec((1, D_MODEL), lambda i, e, tg: (0, 0))],
        out_specs=pl.BlockSpec((tm, D_MODEL), lambda i, e, tg: (i, 0)),
        scratch_shapes=[pltpu.VMEM((tm, D_MODEL), BF16), pltpu.VMEM((tm, D_MODEL), F32)],
    )
    return pl.pallas_call(
        _moe_kernel,
        out_shape=jax.ShapeDtypeStruct((n_rows, D_MODEL), F32),
        grid_spec=grid_spec,
        compiler_params=_cparams(("parallel", "arbitrary")),
        name="moe",
    )(tile_group, xs, w_gate, w_up, w_down, ln_g, ln_b)


def _combine_ple_kernel(pos_ref, x2s_ref, p_ref, wg_ref, bg_ref, wp_ref, o_ref, buf_ref, sem):
    tm = o_ref.shape[0]
    i = pl.program_id(0)

    def gather(step, slot):
        base = step * tm

        def issue(r, _):
            _row_copy(x2s_ref, pos_ref[base + r], buf_ref.at[slot], r, sem.at[slot]).start()
            return 0

        lax.fori_loop(0, tm, issue, 0, unroll=8)

    @pl.when(i == 0)
    def _():
        gather(0, 0)

    @pl.when(i + 1 < pl.num_programs(0))
    def _():
        gather(i + 1, (i + 1) % 2)

    slot = i % 2
    pltpu.make_async_copy(x2s_ref.at[pl.ds(0, tm)], buf_ref.at[slot], sem.at[slot]).wait()
    x2 = buf_ref[slot]
    gate = jax.nn.sigmoid(jnp.dot(x2.astype(BF16), wg_ref[...], preferred_element_type=F32) + bg_ref[...])
    pp = jnp.dot(p_ref[...].astype(BF16), wp_ref[...], preferred_element_type=F32)
    o_ref[...] = x2 + gate * pp


def _combine_ple(pos, x2s, p2d, w_gate, b_gate, w_proj):
    t = p2d.shape[0]
    tm = min(TM_ROWS, t)
    grid_spec = pltpu.PrefetchScalarGridSpec(
        num_scalar_prefetch=1,
        grid=(t // tm,),
        in_specs=[pl.BlockSpec(memory_space=pl.ANY),
                  pl.BlockSpec((tm, PLE_DIM), lambda i, pos: (i, 0)),
                  pl.BlockSpec((D_MODEL, D_MODEL), lambda i, pos: (0, 0)),
                  pl.BlockSpec((1, D_MODEL), lambda i, pos: (0, 0)),
                  pl.BlockSpec((PLE_DIM, D_MODEL), lambda i, pos: (0, 0))],
        out_specs=pl.BlockSpec((tm, D_MODEL), lambda i, pos: (i, 0)),
        scratch_shapes=[pltpu.VMEM((2, tm, D_MODEL), F32), pltpu.SemaphoreType.DMA((2,))],
    )
    return pl.pallas_call(
        _combine_ple_kernel,
        out_shape=jax.ShapeDtypeStruct((t, D_MODEL), F32),
        grid_spec=grid_spec,
        compiler_params=_cparams(("arbitrary",)),
        name="combine_ple",
    )(pos, x2s, p2d, w_gate, b_gate, w_proj)


def _sorted_positions(g_idx, n_tiles):
    onehot = (g_idx[:, None] == jnp.arange(N_GROUPS, dtype=jnp.int32)[None, :]).astype(jnp.int32)
    counts = jnp.sum(onehot, axis=0)
    tiles = (counts + TM_MOE - 1) // TM_MOE
    tile_end = jnp.cumsum(tiles)
    tile_start = tile_end - tiles
    rank = jnp.sum((jnp.cumsum(onehot, axis=0) - onehot) * onehot, axis=1)
    pos = jnp.sum(onehot * (tile_start * TM_MOE)[None, :], axis=1) + rank
    tile_ids = jnp.arange(n_tiles, dtype=jnp.int32)
    tile_group = jnp.sum((tile_ids[:, None] >= tile_end[None, :]).astype(jnp.int32), axis=1)
    return pos.astype(jnp.int32), tile_group.astype(jnp.int32)


def _moe_block(a, w_out, x2d, p2d, ln1_g, ln1_b, ln2_g, ln2_b, w_group, b_group, w_router, b_router,
               w_gate, w_up, w_down, ple_w_proj, ple_w_gate, ple_b_gate):
    t = x2d.shape[0]
    n_tiles = t // TM_MOE + N_GROUPS
    pad = LANES - CW_LANE0 - N_EXPERTS
    w_r = jnp.concatenate([w_group, w_router, jnp.zeros((D_MODEL, pad), F32)], axis=1)
    b_r = jnp.concatenate([b_group, b_router, jnp.zeros((pad,), F32)])[None, :]
    x1ext, ext = _out_router(a, w_out.astype(BF16), x2d, ln1_g[None, :], ln1_b[None, :], w_r, b_r)
    g_idx = ext[:, GIDX_LANE].astype(jnp.int32)
    pos, tile_group = _sorted_positions(g_idx, n_tiles)
    xs = _dispatch(pos, x1ext, n_tiles * TM_MOE)
    x2s = _moe(tile_group, xs, w_gate, w_up, w_down, ln2_g[None, :], ln2_b[None, :])
    return _combine_ple(pos, x2s, p2d, ple_w_gate.astype(BF16), ple_b_gate[None, :],
                        ple_w_proj.astype(BF16))


@jax.jit
def _forward(x, p, positions, fox_w_in, fox_b_f, fox_w_out, ret_w_in, ret_w_out, ln1_g, ln1_b, ln2_g, ln2_b,
             moe_w_group, moe_b_group, moe_w_router, moe_b_router, moe_w_gate, moe_w_up, moe_w_down,
             ple_w_proj, ple_w_gate, ple_b_gate):
    batch, seq, d = x.shape
    t = batch * seq
    x2d = x.reshape(t, d)

    def moe_args(i):
        return (ln1_g[i], ln1_b[i], ln2_g[i], ln2_b[i], moe_w_group[i], moe_b_group[i], moe_w_router[i],
                moe_b_router[i], moe_w_gate[i], moe_w_up[i], moe_w_down[i], ple_w_proj[i], ple_w_gate[i],
                ple_b_gate[i])

    w_in = fox_w_in[0]
    scale = jnp.concatenate([jnp.full((D_MODEL,), FOX_HEAD_DIM ** -0.5 * LOG2E, F32),
                             jnp.ones((2 * D_MODEL,), F32)])
    w_qkv = (w_in[:, :3 * D_MODEL] * scale[None, :]).astype(BF16)
    qkv = _proj(x2d, w_qkv, D_MODEL)
    c = _fgate(x2d, w_in[:, 3 * D_MODEL:], fox_b_f[0][None, :], batch, seq)
    c3 = c.reshape(batch, seq, FOX_HEADS).transpose(0, 2, 1)
    attn = _fox_attention(qkv, c, c3[:, :, None, :], batch, seq)
    x2d = _moe_block(attn, fox_w_out[0], x2d, p[0].reshape(t, PLE_DIM), *moe_args(0))

    w_in = ret_w_in[0]
    scale = jnp.concatenate([jnp.ones((D_MODEL,), F32), jnp.full((D_MODEL,), RET_KEY_DIM ** -0.5, F32),
                             jnp.ones((4 * D_MODEL,), F32)])
    inv_freq = ROPE_BASE ** (-jnp.arange(0, RET_KEY_DIM, 2, dtype=F32) / RET_KEY_DIM)
    cos, sin = _rope_tables(positions.astype(F32).reshape(t, 1), inv_freq[None, :])
    proj = _ret_proj(x2d, (w_in * scale[None, :]).astype(BF16), cos, sin)
    log_gamma = jnp.log(1.0 - 2.0 ** (-5.0 - jnp.arange(RET_HEADS, dtype=F32)))
    ret = _retention(proj, log_gamma, batch, seq)
    x2d = _moe_block(ret, ret_w_out[0], x2d, p[1].reshape(t, PLE_DIM), *moe_args(1))
    return x2d.reshape(batch, seq, d)


def kernel(x, p, positions, fox_w_in, fox_b_f, fox_w_out, ret_w_in, ret_w_out, ln1_g, ln1_b, ln2_g, ln2_b,
           moe_w_group, moe_b_group, moe_w_router, moe_b_router, moe_w_gate, moe_w_up, moe_w_down,
           ple_w_proj, ple_w_gate, ple_b_gate):
    return _forward(x, p, positions, fox_w_in, fox_b_f, fox_w_out, ret_w_in, ret_w_out, ln1_g, ln1_b,
                    ln2_g, ln2_b, moe_w_group, moe_b_group, moe_w_router, moe_b_router, moe_w_gate,
                    moe_w_up, moe_w_down, ple_w_proj, ple_w_gate, ple_b_gate)
```

```python
import functools

import jax
import jax.numpy as jnp
from jax import lax
from jax.experimental import pallas as pl
from jax.experimental.pallas import tpu as pltpu

F32 = jnp.float32
BF16 = jnp.bfloat16

D_MODEL = 1024
PLE_DIM = 256
FOX_HEADS = 16
FOX_HEAD_DIM = D_MODEL // FOX_HEADS
RET_HEADS = 4
RET_KEY_DIM = D_MODEL // RET_HEADS
RET_VAL_DIM = 2 * D_MODEL // RET_HEADS
ROPE_BASE = 10000.0
N_GROUPS = 4
EXPERTS_PER_GROUP = 4
N_EXPERTS = N_GROUPS * EXPERTS_PER_GROUP
D_EXPERT = 512
DEPTH = 2
DEEPNORM_ALPHA = (2.0 * DEPTH) ** 0.25
LN_EPS = 1e-5

LANES = 128
EXT = LANES
ROW_W = D_MODEL + EXT
GIDX_LANE = 0
CW_LANE0 = 4
NEG_BIG = -1e30
LOG2E = 1.4426950408889634
VMEM_LIMIT = 56 * 1024 * 1024

TM_PROJ = 1024
TS_GATE = 512
TQ_ATTN = 512
ATTN_ROW_BLOCK = 32
TM_OUT = 256
TM_ROWS = 256
TM_MOE = 1024
RET_CHUNK = 256


def _cparams(sem):
    return pltpu.CompilerParams(dimension_semantics=sem, vmem_limit_bytes=VMEM_LIMIT)


def _proj_kernel(a_ref, w_ref, o_ref):
    a = a_ref[...].astype(BF16)
    o_ref[...] = jnp.dot(a, w_ref[...], preferred_element_type=F32).astype(o_ref.dtype)


def _proj(a, w, tn):
    m, k = a.shape
    n = w.shape[1]
    tm = min(TM_PROJ, m)
    return pl.pallas_call(
        _proj_kernel,
        out_shape=jax.ShapeDtypeStruct((m, n), BF16),
        grid=(m // tm, n // tn),
        in_specs=[pl.BlockSpec((tm, k), lambda i, j: (i, 0)),
                  pl.BlockSpec((k, tn), lambda i, j: (0, j))],
        out_specs=pl.BlockSpec((tm, tn), lambda i, j: (i, j)),
        compiler_params=_cparams(("parallel", "arbitrary")),
        name="proj",
    )(a, w)


def _split_bf16(x):
    hi = x.astype(BF16)
    return hi, (x - hi.astype(F32)).astype(BF16)


def _split_weight(w):
    hi, lo = _split_bf16(w)
    return jnp.concatenate([hi, lo], axis=1), hi


def _dot_split(x, w_hilo, w_hi):
    x_hi, x_lo = _split_bf16(x)
    r = jnp.dot(x_hi, w_hilo, preferred_element_type=F32)
    return r[:, :LANES] + r[:, LANES:] + jnp.dot(x_lo, w_hi, preferred_element_type=F32)


def _fgate_kernel(x_ref, whl_ref, wh_ref, b_ref, c_ref, carry_ref):
    @pl.when(pl.program_id(1) == 0)
    def _():
        carry_ref[...] = jnp.zeros_like(carry_ref)

    ts = x_ref.shape[0]
    z = _dot_split(x_ref[...], whl_ref[...], wh_ref[...]) + b_ref[...]
    logf = (jnp.minimum(z, 0.0) - jnp.log1p(jnp.exp(-jnp.abs(z)))) * LOG2E
    l1 = logf.astype(BF16)
    rest = logf - l1.astype(F32)
    l2, l3 = _split_bf16(rest)
    row = lax.broadcasted_iota(jnp.int32, (ts, ts), 0)
    col = lax.broadcasted_iota(jnp.int32, (ts, ts), 1)
    tri = jnp.where(row >= col, 1.0, 0.0).astype(BF16)
    parts = jnp.dot(tri, jnp.concatenate([l1, l2, l3], axis=1), preferred_element_type=F32)
    c = parts[:, :LANES] + parts[:, LANES:2 * LANES] + parts[:, 2 * LANES:] + carry_ref[...]
    c_ref[...] = c[:, :c_ref.shape[1]]
    carry_ref[...] = c[ts - 1:ts, :]


def _fgate(x2d, w_f, b_f, batch, seq):
    ts = min(TS_GATE, seq)
    ns = seq // ts
    h = w_f.shape[1]
    w_hilo, w_hi = _split_weight(jnp.pad(w_f, ((0, 0), (0, LANES - h))))
    b_pad = jnp.pad(b_f, ((0, 0), (0, LANES - h)))
    return pl.pallas_call(
        _fgate_kernel,
        out_shape=jax.ShapeDtypeStruct((batch * seq, h), F32),
        grid=(batch, ns),
        in_specs=[pl.BlockSpec((ts, D_MODEL), lambda b, s: (b * ns + s, 0)),
                  pl.BlockSpec((D_MODEL, 2 * LANES), lambda b, s: (0, 0)),
                  pl.BlockSpec((D_MODEL, LANES), lambda b, s: (0, 0)),
                  pl.BlockSpec((1, LANES), lambda b, s: (0, 0))],
        out_specs=pl.BlockSpec((ts, h), lambda b, s: (b * ns + s, 0)),
        scratch_shapes=[pltpu.VMEM((1, LANES), F32)],
        compiler_params=_cparams(("parallel", "arbitrary")),
        name="fgate",
    )(x2d, w_hilo, w_hi, b_pad)


def _fox_attn_kernel(q_ref, k_ref, v_ref, cq_ref, ck_ref, o_ref, m_ref, acc_ref, sa_ref, sb_ref, *, tq):
    qi = pl.program_id(2)
    q = q_ref[...]
    lane = lax.broadcasted_iota(jnp.int32, (tq, LANES), 1)
    row = lax.broadcasted_iota(jnp.int32, (tq, tq), 0)
    col = lax.broadcasted_iota(jnp.int32, (tq, tq), 1)
    causal = row >= col
    head0 = lane < FOX_HEAD_DIM
    q2 = jnp.concatenate([jnp.where(head0, q, jnp.zeros_like(q)),
                          jnp.where(head0, jnp.zeros_like(q), q)], axis=0)
    m_ref[...] = jnp.full(m_ref.shape, NEG_BIG, F32)
    acc_ref[...] = jnp.zeros(acc_ref.shape, F32)
    lane_k = lax.broadcasted_iota(jnp.int32, (tq, LANES), 1)
    c_tok = cq_ref[...]
    lane_h = lax.broadcasted_iota(jnp.int32, c_tok.shape, 1)
    cq_heads = [jnp.sum(jnp.where(lane_h == 2 * pl.program_id(1) + h, c_tok, 0.0), axis=1, keepdims=True)
                for h in range(2)]
    rb = ATTN_ROW_BLOCK

    def scores(kb, s_ref):
        ks = pl.multiple_of(kb * tq, tq)
        s_ref[...] = lax.dot_general(q2, k_ref[pl.ds(ks, tq), :], (((1,), (1,)), ((), ())),
                                     preferred_element_type=F32)

    def softmax_pv(kb, s_ref, masked):
        ks = pl.multiple_of(kb * tq, tq)
        vblk = v_ref[pl.ds(ks, tq), :]
        for h in range(2):
            ck = ck_ref[h, :, pl.ds(ks, tq)]
            vh = jnp.where(lane_k < FOX_HEAD_DIM if h == 0 else lane_k >= FOX_HEAD_DIM,
                           vblk, jnp.ones_like(vblk))
            p_blocks = []
            m_blocks = []
            cq_all = cq_heads[h]
            m_all = m_ref[h]
            for r0 in range(0, tq, rb):
                t = s_ref[h * tq + r0:h * tq + r0 + rb, :] - ck
                if masked:
                    t = jnp.where(causal[r0:r0 + rb, :], t, NEG_BIG)
                cq = cq_all[r0:r0 + rb, :]
                m_new = jnp.maximum(m_all[r0:r0 + rb, :], cq + jnp.max(t, axis=1, keepdims=True))
                m_blocks.append(m_new)
                p_blocks.append(jnp.exp2(t + (cq - m_new)).astype(BF16))
            m_new_all = jnp.concatenate(m_blocks, axis=0)
            pv = jnp.dot(jnp.concatenate(p_blocks, axis=0), vh, preferred_element_type=F32)
            acc_ref[h] = jnp.exp2(m_all - m_new_all) * acc_ref[h] + pv
            m_ref[h] = m_new_all

    scores(0, sa_ref)

    def pair(j, _):
        scores(2 * j + 1, sb_ref)
        softmax_pv(2 * j, sa_ref, False)
        scores(2 * j + 2, sa_ref)
        softmax_pv(2 * j + 1, sb_ref, False)
        return 0

    lax.fori_loop(0, qi // 2, pair, 0)

    @pl.when(qi % 2 == 0)
    def _():
        softmax_pv(qi, sa_ref, True)

    @pl.when(qi % 2 == 1)
    def _():
        scores(qi, sb_ref)
        softmax_pv(qi - 1, sa_ref, False)
        softmax_pv(qi, sb_ref, True)

    a0 = acc_ref[0]
    a1 = acc_ref[1]
    o0 = a0 / pltpu.roll(a0, FOX_HEAD_DIM, 1)
    o1 = a1 / pltpu.roll(a1, FOX_HEAD_DIM, 1)
    o_ref[...] = jnp.where(head0, o0, o1).astype(o_ref.dtype)


def _fox_attention(qkv, cq, ck, batch, seq):
    tq = min(TQ_ATTN, seq)
    nq = seq // tq
    hp = FOX_HEADS // 2
    return pl.pallas_call(
        functools.partial(_fox_attn_kernel, tq=tq),
        out_shape=jax.ShapeDtypeStruct((batch * seq, D_MODEL), BF16),
        grid=(batch, hp, nq),
        in_specs=[pl.BlockSpec((tq, LANES), lambda b, p, i: (b * nq + i, p)),
                  pl.BlockSpec((seq, LANES), lambda b, p, i: (b, hp + p)),
                  pl.BlockSpec((seq, LANES), lambda b, p, i: (b, 2 * hp + p)),
                  pl.BlockSpec((tq, FOX_HEADS), lambda b, p, i: (b * nq + i, 0)),
                  pl.BlockSpec((None, 2, 1, seq), lambda b, p, i: (b, p, 0, 0))],
        out_specs=pl.BlockSpec((tq, LANES), lambda b, p, i: (b * nq + i, p)),
        scratch_shapes=[pltpu.VMEM((2, tq, 1), F32), pltpu.VMEM((2, tq, LANES), F32),
                        pltpu.VMEM((2 * tq, tq), F32), pltpu.VMEM((2 * tq, tq), F32)],
        compiler_params=_cparams(("parallel", "parallel", "arbitrary")),
        name="fox_attn",
    )(qkv, qkv, qkv, cq, ck)


def _rope_kernel(pos_ref, freq_ref, cos_ref, sin_ref):
    ang = pos_ref[...] * freq_ref[...]
    cos_ref[...] = jnp.cos(ang)
    sin_ref[...] = jnp.sin(ang)


def _rope_tables(pos_f, inv_freq):
    t = pos_f.shape[0]
    half = inv_freq.shape[1]
    tm = min(1024, t)
    return pl.pallas_call(
        _rope_kernel,
        out_shape=(jax.ShapeDtypeStruct((t, half), F32), jax.ShapeDtypeStruct((t, half), F32)),
        grid=(t // tm,),
        in_specs=[pl.BlockSpec((tm, 1), lambda i: (i, 0)),
                  pl.BlockSpec((1, half), lambda i: (0, 0))],
        out_specs=(pl.BlockSpec((tm, half), lambda i: (i, 0)),
                   pl.BlockSpec((tm, half), lambda i: (i, 0))),
        compiler_params=_cparams(("parallel",)),
        name="rope_tables",
    )(pos_f, inv_freq)


def _ret_proj_kernel(a_ref, w_ref, cos_ref, sin_ref, o_ref):
    j = pl.program_id(1)
    a = a_ref[...].astype(BF16)
    acc = jnp.dot(a, w_ref[...], preferred_element_type=F32)

    @pl.when(j < 2)
    def _():
        c = cos_ref[...]
        s = sin_ref[...]
        half = RET_KEY_DIM // 2
        for h in range(RET_HEADS):
            x1 = acc[:, h * RET_KEY_DIM:h * RET_KEY_DIM + half]
            x2 = acc[:, h * RET_KEY_DIM + half:(h + 1) * RET_KEY_DIM]
            o_ref[:, h * RET_KEY_DIM:h * RET_KEY_DIM + half] = (x1 * c - x2 * s).astype(o_ref.dtype)
            o_ref[:, h * RET_KEY_DIM + half:(h + 1) * RET_KEY_DIM] = (x2 * c + x1 * s).astype(o_ref.dtype)

    @pl.when(j >= 2)
    def _():
        o_ref[...] = acc.astype(o_ref.dtype)


def _ret_proj(x2d, w, cos, sin):
    t = x2d.shape[0]
    n = w.shape[1]
    tm = min(TM_PROJ, t)
    tn = D_MODEL
    half = RET_KEY_DIM // 2
    return pl.pallas_call(
        _ret_proj_kernel,
        out_shape=jax.ShapeDtypeStruct((t, n), BF16),
        grid=(t // tm, n // tn),
        in_specs=[pl.BlockSpec((tm, D_MODEL), lambda i, j: (i, 0)),
                  pl.BlockSpec((D_MODEL, tn), lambda i, j: (0, j)),
                  pl.BlockSpec((tm, half), lambda i, j: (i, 0)),
                  pl.BlockSpec((tm, half), lambda i, j: (i, 0))],
        out_specs=pl.BlockSpec((tm, tn), lambda i, j: (i, j)),
        compiler_params=_cparams(("parallel", "arbitrary")),
        name="ret_proj",
    )(x2d, w, cos, sin)


def _retention_kernel(lg_ref, q_ref, k_ref, v_ref, g_ref, o_ref, state_ref, decay_ref, *, chunk):
    dk, dv = RET_KEY_DIM, RET_VAL_DIM

    @pl.when(pl.program_id(1) == 0)
    def _():
        state_ref[...] = jnp.zeros_like(state_ref)
        row = lax.broadcasted_iota(jnp.int32, (chunk, chunk), 0)
        col = lax.broadcasted_iota(jnp.int32, (chunk, chunk), 1)
        diff = (row - col).astype(F32)
        for h in range(RET_HEADS):
            decay_ref[h] = jnp.where(diff >= 0, jnp.exp(jnp.maximum(diff, 0.0) * lg_ref[h]), 0.0)

    idx = lax.broadcasted_iota(jnp.int32, (chunk, 1), 0).astype(F32)
    for h in range(RET_HEADS):
        lg = lg_ref[h]
        q = q_ref[:, h * dk:(h + 1) * dk]
        k = k_ref[:, h * dk:(h + 1) * dk]
        v = v_ref[:, h * dv:(h + 1) * dv]
        scores = lax.dot_general(q, k, (((1,), (1,)), ((), ())), preferred_element_type=F32) * decay_ref[h]
        o = jnp.dot(scores.astype(BF16), v, preferred_element_type=F32)
        state = state_ref[h]
        qd = (q.astype(F32) * jnp.exp((idx + 1.0) * lg)).astype(BF16)
        o = o + jnp.dot(qd, state.astype(BF16), preferred_element_type=F32)
        kd = (k.astype(F32) * jnp.exp((chunk - 1.0 - idx) * lg)).astype(BF16)
        state_ref[h] = state * jnp.exp(chunk * lg) + lax.dot_general(
            kd, v, (((0,), (0,)), ((), ())), preferred_element_type=F32)
        mu = jnp.mean(o, axis=1, keepdims=True)
        oc = o - mu
        var = jnp.mean(oc * oc, axis=1, keepdims=True)
        on = oc * lax.rsqrt(var + LN_EPS)
        g = g_ref[:, h * dv:(h + 1) * dv].astype(F32)
        o_ref[:, h * dv:(h + 1) * dv] = (g * jax.nn.sigmoid(g) * on).astype(o_ref.dtype)


def _retention(proj, log_gamma, batch, seq):
    chunk = min(RET_CHUNK, seq)
    nc = seq // chunk
    dq = RET_HEADS * RET_KEY_DIM
    dvt = RET_HEADS * RET_VAL_DIM
    grid_spec = pltpu.PrefetchScalarGridSpec(
        num_scalar_prefetch=1,
        grid=(batch, nc),
        in_specs=[pl.BlockSpec((chunk, dq), lambda b, c, lg: (b * nc + c, 0)),
                  pl.BlockSpec((chunk, dq), lambda b, c, lg: (b * nc + c, 1)),
                  pl.BlockSpec((chunk, dvt), lambda b, c, lg: (b * nc + c, 1)),
                  pl.BlockSpec((chunk, dvt), lambda b, c, lg: (b * nc + c, 2))],
        out_specs=pl.BlockSpec((chunk, dvt), lambda b, c, lg: (b * nc + c, 0)),
        scratch_shapes=[pltpu.VMEM((RET_HEADS, RET_KEY_DIM, RET_VAL_DIM), F32),
                        pltpu.VMEM((RET_HEADS, chunk, chunk), F32)],
    )
    return pl.pallas_call(
        functools.partial(_retention_kernel, chunk=chunk),
        out_shape=jax.ShapeDtypeStruct((batch * seq, dvt), BF16),
        grid_spec=grid_spec,
        compiler_params=_cparams(("parallel", "arbitrary")),
        name="retention",
    )(log_gamma, proj, proj, proj, proj)


def _layer_norm(y, g, b):
    mu = jnp.mean(y, axis=1, keepdims=True)
    yc = y - mu
    var = jnp.mean(yc * yc, axis=1, keepdims=True)
    return yc * lax.rsqrt(var + LN_EPS) * g + b


def _out_router_kernel(a_ref, w_ref, x_ref, g_ref, b_ref, wrhl_ref, wrh_ref, br_ref, o_ref, ext_ref):
    tm = a_ref.shape[0]
    h = jnp.dot(a_ref[...], w_ref[...], preferred_element_type=F32)
    x1 = _layer_norm(DEEPNORM_ALPHA * x_ref[...] + h, g_ref[...], b_ref[...])
    o_ref[:, :D_MODEL] = x1

    logits = _dot_split(x1, wrhl_ref[...], wrh_ref[...]) + br_ref[...]
    lane = lax.broadcasted_iota(jnp.int32, (tm, LANES), 1)

    def first_argmax(vals, vmax):
        return jnp.min(jnp.where(vals == vmax, lane, LANES), axis=1, keepdims=True)

    gl = jnp.where(lane < N_GROUPS, logits, NEG_BIG)
    gmax = jnp.max(gl, axis=1, keepdims=True)
    gsum = jnp.sum(jnp.where(lane < N_GROUPS, jnp.exp(gl - gmax), 0.0), axis=1, keepdims=True)
    g_val = 1.0 / gsum
    g_idx = first_argmax(gl, gmax)
    lo = CW_LANE0 + EXPERTS_PER_GROUP * g_idx
    in_group = (lane >= lo) & (lane < lo + EXPERTS_PER_GROUP)
    el = jnp.where(in_group, logits, NEG_BIG)
    e1 = jnp.max(el, axis=1, keepdims=True)
    i1 = first_argmax(el, e1)
    el2 = jnp.where(lane == i1, NEG_BIG, el)
    e2 = jnp.max(el2, axis=1, keepdims=True)
    i2 = first_argmax(el2, e2)
    r = jnp.exp(e2 - e1)
    w1 = g_val / (1.0 + r)
    w2 = g_val * r / (1.0 + r)
    ext = jnp.where(lane == i1, w1, 0.0) + jnp.where(lane == i2, w2, 0.0)
    ext = jnp.where(lane == GIDX_LANE, g_idx.astype(F32), ext)
    o_ref[:, D_MODEL:] = ext
    ext_ref[...] = ext


def _out_router(a, w_out, x2d, ln_g, ln_b, w_r, b_r):
    t, din = a.shape
    tm = min(TM_OUT, t)
    w_r_hilo, w_r_hi = _split_weight(w_r)
    return pl.pallas_call(
        _out_router_kernel,
        out_shape=(jax.ShapeDtypeStruct((t, ROW_W), F32), jax.ShapeDtypeStruct((t, EXT), F32)),
        grid=(t // tm,),
        in_specs=[pl.BlockSpec((tm, din), lambda i: (i, 0)),
                  pl.BlockSpec((din, D_MODEL), lambda i: (0, 0)),
                  pl.BlockSpec((tm, D_MODEL), lambda i: (i, 0)),
                  pl.BlockSpec((1, D_MODEL), lambda i: (0, 0)),
                  pl.BlockSpec((1, D_MODEL), lambda i: (0, 0)),
                  pl.BlockSpec((D_MODEL, 2 * LANES), lambda i: (0, 0)),
                  pl.BlockSpec((D_MODEL, LANES), lambda i: (0, 0)),
                  pl.BlockSpec((1, LANES), lambda i: (0, 0))],
        out_specs=(pl.BlockSpec((tm, ROW_W), lambda i: (i, 0)), pl.BlockSpec((tm, EXT), lambda i: (i, 0))),
        compiler_params=_cparams(("parallel",)),
        name="out_router",
    )(a, w_out, x2d, ln_g, ln_b, w_r_hilo, w_r_hi, b_r)


def _row_copy(src_ref, src_row, dst_ref, dst_row, sem):
    return pltpu.make_async_copy(src_ref.at[pl.ds(src_row, 1)], dst_ref.at[pl.ds(dst_row, 1)], sem)


def _dispatch_kernel(pos_ref, x_ref, init_ref, xs_ref, sem):
    del init_ref
    tm = x_ref.shape[0]
    base = pl.program_id(0) * tm

    def issue(r, _):
        _row_copy(x_ref, r, xs_ref, pos_ref[base + r], sem).start()
        return 0

    lax.fori_loop(0, tm, issue, 0, unroll=8)
    pltpu.make_async_copy(x_ref, xs_ref.at[pl.ds(0, tm)], sem).wait()


def _dispatch(pos, x1ext, n_rows):
    t = x1ext.shape[0]
    tm = min(TM_ROWS, t)
    grid_spec = pltpu.PrefetchScalarGridSpec(
        num_scalar_prefetch=1,
        grid=(t // tm,),
        in_specs=[pl.BlockSpec((tm, ROW_W), lambda i, pos: (i, 0)),
                  pl.BlockSpec(memory_space=pl.ANY)],
        out_specs=pl.BlockSpec(memory_space=pl.ANY),
        scratch_shapes=[pltpu.SemaphoreType.DMA(())],
    )
    return pl.pallas_call(
        _dispatch_kernel,
        out_shape=jax.ShapeDtypeStruct((n_rows, ROW_W), F32),
        grid_spec=grid_spec,
        input_output_aliases={2: 0},
        compiler_params=_cparams(("arbitrary",)),
        name="dispatch",
    )(pos, x1ext, jnp.zeros((n_rows, ROW_W), F32))


def _moe_kernel(tg_ref, xs_ref, wg_ref, wu_ref, wd_ref, g_ref, b_ref, o_ref, xb_ref, acc_ref):
    i = pl.program_id(0)
    e = pl.program_id(1)
    tm = xs_ref.shape[0]
    group = tg_ref[i]
    valid = group < N_GROUPS

    @pl.when(valid)
    def _():
        @pl.when(e == 0)
        def _():
            xb_ref[...] = xs_ref[:, :D_MODEL].astype(BF16)
            acc_ref[...] = jnp.zeros_like(acc_ref)

        lane = lax.broadcasted_iota(jnp.int32, (tm, EXT), 1)
        cw_lane = CW_LANE0 + EXPERTS_PER_GROUP * group + e
        cw = jnp.sum(jnp.where(lane == cw_lane, xs_ref[:, D_MODEL:], 0.0), axis=1, keepdims=True)
        xb = xb_ref[...]
        hg = jnp.dot(xb, wg_ref[...].astype(BF16), preferred_element_type=F32)
        hu = jnp.dot(xb, wu_ref[...].astype(BF16), preferred_element_type=F32)
        hh = (hg * jax.nn.sigmoid(hg) * hu * cw).astype(BF16)
        acc_ref[...] += jnp.dot(hh, wd_ref[...].astype(BF16), preferred_element_type=F32)

        @pl.when(e == EXPERTS_PER_GROUP - 1)
        def _():
            y = DEEPNORM_ALPHA * xs_ref[:, :D_MODEL] + acc_ref[...]
            o_ref[...] = _layer_norm(y, g_ref[...], b_ref[...])

    @pl.when(jnp.logical_not(valid) & (e == EXPERTS_PER_GROUP - 1))
    def _():
        o_ref[...] = jnp.zeros_like(o_ref)


def _moe(tile_group, xs, w_gate, w_up, w_down, ln_g, ln_b, layer):
    n_rows = xs.shape[0]
    tm = TM_MOE
    epg = EXPERTS_PER_GROUP

    def expert(i, e, tg):
        return (layer, jnp.where(tg[i] < N_GROUPS, tg[i] * epg + e, N_EXPERTS - 1), 0, 0)

    grid_spec = pltpu.PrefetchScalarGridSpec(
        num_scalar_prefetch=1,
        grid=(n_rows // tm, epg),
        in_specs=[pl.BlockSpec((tm, ROW_W), lambda i, e, tg: (i, 0)),
                  pl.BlockSpec((None, None, D_MODEL, D_EXPERT), expert),
                  pl.BlockSpec((None, None, D_MODEL, D_EXPERT), expert),
                  pl.BlockSpec((None, None, D_EXPERT, D_MODEL), expert),
                  pl.BlockSpec((1, D_MODEL), lambda i, e, tg: (0, 0)),
                  pl.BlockSpec((1, D_MODEL), lambda i, e, tg: (0, 0))],
        out_specs=pl.BlockSpec((tm, D_MODEL), lambda i, e, tg: (i, 0)),
        scratch_shapes=[pltpu.VMEM((tm, D_MODEL), BF16), pltpu.VMEM((tm, D_MODEL), F32)],
    )
    return pl.pallas_call(
        _moe_kernel,
        out_shape=jax.ShapeDtypeStruct((n_rows, D_MODEL), F32),
        grid_spec=grid_spec,
        compiler_params=_cparams(("parallel", "arbitrary")),
        name="moe",
    )(tile_group, xs, w_gate, w_up, w_down, ln_g, ln_b)


def _combine_ple_kernel(pos_ref, x2s_ref, p_ref, wg_ref, bg_ref, wp_ref, o_ref, buf_ref, sem):
    tm = o_ref.shape[0]
    i = pl.program_id(0)

    def gather(step, slot):
        base = step * tm

        def issue(r, _):
            _row_copy(x2s_ref, pos_ref[base + r], buf_ref.at[slot], r, sem.at[slot]).start()
            return 0

        lax.fori_loop(0, tm, issue, 0, unroll=8)

    @pl.when(i == 0)
    def _():
        gather(0, 0)

    @pl.when(i + 1 < pl.num_programs(0))
    def _():
        gather(i + 1, (i + 1) % 2)

    slot = i % 2
    pltpu.make_async_copy(x2s_ref.at[pl.ds(0, tm)], buf_ref.at[slot], sem.at[slot]).wait()
    x2 = buf_ref[slot]
    gate = jax.nn.sigmoid(jnp.dot(x2.astype(BF16), wg_ref[...], preferred_element_type=F32) + bg_ref[...])
    pp = jnp.dot(p_ref[...].astype(BF16), wp_ref[...], preferred_element_type=F32)
    o_ref[...] = x2 + gate * pp


def _combine_ple(pos, x2s, p2d, w_gate, b_gate, w_proj):
    t = p2d.shape[0]
    tm = min(TM_ROWS, t)
    grid_spec = pltpu.PrefetchScalarGridSpec(
        num_scalar_prefetch=1,
        grid=(t // tm,),
        in_specs=[pl.BlockSpec(memory_space=pl.ANY),
                  pl.BlockSpec((tm, PLE_DIM), lambda i, pos: (i, 0)),
                  pl.BlockSpec((D_MODEL, D_MODEL), lambda i, pos: (0, 0)),
                  pl.BlockSpec((1, D_MODEL), lambda i, pos: (0, 0)),
                  pl.BlockSpec((PLE_DIM, D_MODEL), lambda i, pos: (0, 0))],
        out_specs=pl.BlockSpec((tm, D_MODEL), lambda i, pos: (i, 0)),
        scratch_shapes=[pltpu.VMEM((2, tm, D_MODEL), F32), pltpu.SemaphoreType.DMA((2,))],
    )
    return pl.pallas_call(
        _combine_ple_kernel,
        out_shape=jax.ShapeDtypeStruct((t, D_MODEL), F32),
        grid_spec=grid_spec,
        compiler_params=_cparams(("arbitrary",)),
        name="combine_ple",
    )(pos, x2s, p2d, w_gate, b_gate, w_proj)


def _sorted_positions(g_idx, n_tiles):
    onehot = (g_idx[:, None] == jnp.arange(N_GROUPS, dtype=jnp.int32)[None, :]).astype(jnp.int32)
    counts = jnp.sum(onehot, axis=0)
    tiles = (counts + TM_MOE - 1) // TM_MOE
    tile_end = jnp.cumsum(tiles)
    tile_start = tile_end - tiles
    rank = jnp.sum((jnp.cumsum(onehot, axis=0) - onehot) * onehot, axis=1)
    pos = jnp.sum(onehot * (tile_start * TM_MOE)[None, :], axis=1) + rank
    tile_ids = jnp.arange(n_tiles, dtype=jnp.int32)
    tile_group = jnp.sum((tile_ids[:, None] >= tile_end[None, :]).astype(jnp.int32), axis=1)
    return pos.astype(jnp.int32), tile_group.astype(jnp.int32)


def _moe_block(a, w_out, x2d, p2d, ln1_g, ln1_b, ln2_g, ln2_b, w_group, b_group, w_router, b_router,
               w_gate, w_up, w_down, ple_w_proj, ple_w_gate, ple_b_gate, layer):
    t = x2d.shape[0]
    n_tiles = t // TM_MOE + N_GROUPS
    pad = LANES - CW_LANE0 - N_EXPERTS
    w_r = jnp.concatenate([w_group, w_router, jnp.zeros((D_MODEL, pad), F32)], axis=1)
    b_r = jnp.concatenate([b_group, b_router, jnp.zeros((pad,), F32)])[None, :]
    x1ext, ext = _out_router(a, w_out.astype(BF16), x2d, ln1_g[None, :], ln1_b[None, :], w_r, b_r)
    g_idx = ext[:, GIDX_LANE].astype(jnp.int32)
    pos, tile_group = _sorted_positions(g_idx, n_tiles)
    xs = _dispatch(pos, x1ext, n_tiles * TM_MOE)
    x2s = _moe(tile_group, xs, w_gate, w_up, w_down, ln2_g[None, :], ln2_b[None, :], layer)
    return _combine_ple(pos, x2s, p2d, ple_w_gate.astype(BF16), ple_b_gate[None, :],
                        ple_w_proj.astype(BF16))


@jax.jit
def _forward(x, p, positions, fox_w_in, fox_b_f, fox_w_out, ret_w_in, ret_w_out, ln1_g, ln1_b, ln2_g, ln2_b,
             moe_w_group, moe_b_group, moe_w_router, moe_b_router, moe_w_gate, moe_w_up, moe_w_down,
             ple_w_proj, ple_w_gate, ple_b_gate):
    batch, seq, d = x.shape
    t = batch * seq
    x2d = x.reshape(t, d)

    def moe_args(i):
        return (ln1_g[i], ln1_b[i], ln2_g[i], ln2_b[i], moe_w_group[i], moe_b_group[i], moe_w_router[i],
                moe_b_router[i], moe_w_gate, moe_w_up, moe_w_down, ple_w_proj[i], ple_w_gate[i],
                ple_b_gate[i], i)

    w_in = fox_w_in[0]
    scale = jnp.concatenate([jnp.full((D_MODEL,), FOX_HEAD_DIM ** -0.5 * LOG2E, F32),
                             jnp.ones((2 * D_MODEL,), F32)])
    w_qkv = (w_in[:, :3 * D_MODEL] * scale[None, :]).astype(BF16)
    qkv = _proj(x2d, w_qkv, D_MODEL)
    c = _fgate(x2d, w_in[:, 3 * D_MODEL:], fox_b_f[0][None, :], batch, seq)
    c3 = c.reshape(batch, seq, FOX_HEADS).transpose(0, 2, 1)
    attn = _fox_attention(qkv, c, c3[:, :, None, :], batch, seq)
    x2d = _moe_block(attn, fox_w_out[0], x2d, p[0].reshape(t, PLE_DIM), *moe_args(0))

    w_in = ret_w_in[0]
    scale = jnp.concatenate([jnp.ones((D_MODEL,), F32), jnp.full((D_MODEL,), RET_KEY_DIM ** -0.5, F32),
                             jnp.ones((4 * D_MODEL,), F32)])
    inv_freq = ROPE_BASE ** (-jnp.arange(0, RET_KEY_DIM, 2, dtype=F32) / RET_KEY_DIM)
    cos, sin = _rope_tables(positions.astype(F32).reshape(t, 1), inv_freq[None, :])
    proj = _ret_proj(x2d, (w_in * scale[None, :]).astype(BF16), cos, sin)
    log_gamma = jnp.log(1.0 - 2.0 ** (-5.0 - jnp.arange(RET_HEADS, dtype=F32)))
    ret = _retention(proj, log_gamma, batch, seq)
    x2d = _moe_block(ret, ret_w_out[0], x2d, p[1].reshape(t, PLE_DIM), *moe_args(1))
    return x2d.reshape(batch, seq, d)


def kernel(x, p, positions, fox_w_in, fox_b_f, fox_w_out, ret_w_in, ret_w_out, ln1_g, ln1_b, ln2_g, ln2_b,
           moe_w_group, moe_b_group, moe_w_router, moe_b_router, moe_w_gate, moe_w_up, moe_w_down,
           ple_w_proj, ple_w_gate, ple_b_gate):
    return _forward(x, p, positions, fox_w_in, fox_b_f, fox_w_out, ret_w_in, ret_w_out, ln1_g, ln1_b,
                    ln2_g, ln2_b, moe_w_group, moe_b_group, moe_w_router, moe_b_router, moe_w_gate,
                    moe_w_up, moe_w_down, ple_w_proj, ple_w_gate, ple_b_gate)
```

```python
import functools

import jax
import jax.numpy as jnp
from jax import lax
from jax.experimental import pallas as pl
from jax.experimental.pallas import tpu as pltpu

F32 = jnp.float32
BF16 = jnp.bfloat16

D_MODEL = 1024
PLE_DIM = 256
FOX_HEADS = 16
FOX_HEAD_DIM = D_MODEL // FOX_HEADS
RET_HEADS = 4
RET_KEY_DIM = D_MODEL // RET_HEADS
RET_VAL_DIM = 2 * D_MODEL // RET_HEADS
ROPE_BASE = 10000.0
N_GROUPS = 4
EXPERTS_PER_GROUP = 4
N_EXPERTS = N_GROUPS * EXPERTS_PER_GROUP
D_EXPERT = 512
DEPTH = 2
DEEPNORM_ALPHA = (2.0 * DEPTH) ** 0.25
LN_EPS = 1e-5

LANES = 128
EXT = LANES
ROW_W = D_MODEL + EXT
GIDX_LANE = 0
CW_LANE0 = 4
NEG_BIG = -1e30
LOG2E = 1.4426950408889634
VMEM_LIMIT = 56 * 1024 * 1024

TM_PROJ = 1024
TS_GATE = 512
TQ_ATTN = 512
ATTN_ROW_BLOCK = 32
TM_OUT = 256
TM_ROWS = 256
TM_MOE = 1024
RET_CHUNK = 256


def _cparams(sem):
    return pltpu.CompilerParams(dimension_semantics=sem, vmem_limit_bytes=VMEM_LIMIT)


def _proj_kernel(a_ref, w_ref, o_ref):
    a = a_ref[...].astype(BF16)
    o_ref[...] = jnp.dot(a, w_ref[...], preferred_element_type=F32).astype(o_ref.dtype)


def _proj(a, w, tn):
    m, k = a.shape
    n = w.shape[1]
    tm = min(TM_PROJ, m)
    return pl.pallas_call(
        _proj_kernel,
        out_shape=jax.ShapeDtypeStruct((m, n), BF16),
        grid=(m // tm, n // tn),
        in_specs=[pl.BlockSpec((tm, k), lambda i, j: (i, 0)),
                  pl.BlockSpec((k, tn), lambda i, j: (0, j))],
        out_specs=pl.BlockSpec((tm, tn), lambda i, j: (i, j)),
        compiler_params=_cparams(("parallel", "arbitrary")),
        name="proj",
    )(a, w)


def _split_bf16(x):
    hi = x.astype(BF16)
    return hi, (x - hi.astype(F32)).astype(BF16)


def _split_weight(w):
    hi, lo = _split_bf16(w)
    return jnp.concatenate([hi, lo], axis=1), hi


def _dot_split(x, w_hilo, w_hi):
    x_hi, x_lo = _split_bf16(x)
    r = jnp.dot(x_hi, w_hilo, preferred_element_type=F32)
    return r[:, :LANES] + r[:, LANES:] + jnp.dot(x_lo, w_hi, preferred_element_type=F32)


def _fgate_kernel(x_ref, whl_ref, wh_ref, b_ref, c_ref, carry_ref):
    @pl.when(pl.program_id(1) == 0)
    def _():
        carry_ref[...] = jnp.zeros_like(carry_ref)

    ts = x_ref.shape[0]
    z = _dot_split(x_ref[...], whl_ref[...], wh_ref[...]) + b_ref[...]
    logf = (jnp.minimum(z, 0.0) - jnp.log1p(jnp.exp(-jnp.abs(z)))) * LOG2E
    l1 = logf.astype(BF16)
    rest = logf - l1.astype(F32)
    l2, l3 = _split_bf16(rest)
    row = lax.broadcasted_iota(jnp.int32, (ts, ts), 0)
    col = lax.broadcasted_iota(jnp.int32, (ts, ts), 1)
    tri = jnp.where(row >= col, 1.0, 0.0).astype(BF16)
    parts = jnp.dot(tri, jnp.concatenate([l1, l2, l3], axis=1), preferred_element_type=F32)
    c = parts[:, :LANES] + parts[:, LANES:2 * LANES] + parts[:, 2 * LANES:] + carry_ref[...]
    c_ref[...] = c[:, :c_ref.shape[1]]
    carry_ref[...] = c[ts - 1:ts, :]


def _fgate(x2d, w_f, b_f, batch, seq):
    ts = min(TS_GATE, seq)
    ns = seq // ts
    h = w_f.shape[1]
    w_hilo, w_hi = _split_weight(jnp.pad(w_f, ((0, 0), (0, LANES - h))))
    b_pad = jnp.pad(b_f, ((0, 0), (0, LANES - h)))
    return pl.pallas_call(
        _fgate_kernel,
        out_shape=jax.ShapeDtypeStruct((batch * seq, h), F32),
        grid=(batch, ns),
        in_specs=[pl.BlockSpec((ts, D_MODEL), lambda b, s: (b * ns + s, 0)),
                  pl.BlockSpec((D_MODEL, 2 * LANES), lambda b, s: (0, 0)),
                  pl.BlockSpec((D_MODEL, LANES), lambda b, s: (0, 0)),
                  pl.BlockSpec((1, LANES), lambda b, s: (0, 0))],
        out_specs=pl.BlockSpec((ts, h), lambda b, s: (b * ns + s, 0)),
        scratch_shapes=[pltpu.VMEM((1, LANES), F32)],
        compiler_params=_cparams(("parallel", "arbitrary")),
        name="fgate",
    )(x2d, w_hilo, w_hi, b_pad)


def _fox_attn_kernel(q_ref, k_ref, v_ref, cq_ref, ck_ref, o_ref, m_ref, acc_ref, sa_ref, sb_ref, *, tq):
    qi = pl.program_id(2)
    q = q_ref[...]
    lane = lax.broadcasted_iota(jnp.int32, (tq, LANES), 1)
    row = lax.broadcasted_iota(jnp.int32, (tq, tq), 0)
    col = lax.broadcasted_iota(jnp.int32, (tq, tq), 1)
    causal = row >= col
    head0 = lane < FOX_HEAD_DIM
    q2 = jnp.concatenate([jnp.where(head0, q, jnp.zeros_like(q)),
                          jnp.where(head0, jnp.zeros_like(q), q)], axis=0)
    m_ref[...] = jnp.full(m_ref.shape, NEG_BIG, F32)
    acc_ref[...] = jnp.zeros(acc_ref.shape, F32)
    lane_k = lax.broadcasted_iota(jnp.int32, (tq, LANES), 1)
    c_tok = cq_ref[...]
    lane_h = lax.broadcasted_iota(jnp.int32, c_tok.shape, 1)
    cq_heads = [jnp.sum(jnp.where(lane_h == 2 * pl.program_id(1) + h, c_tok, 0.0), axis=1, keepdims=True)
                for h in range(2)]
    rb = ATTN_ROW_BLOCK

    def scores(kb, s_ref):
        ks = pl.multiple_of(kb * tq, tq)
        s_ref[...] = lax.dot_general(q2, k_ref[pl.ds(ks, tq), :], (((1,), (1,)), ((), ())),
                                     preferred_element_type=F32)

    def softmax_pv(kb, s_ref, masked):
        ks = pl.multiple_of(kb * tq, tq)
        vblk = v_ref[pl.ds(ks, tq), :]
        for h in range(2):
            ck = ck_ref[h, :, pl.ds(ks, tq)]
            vh = jnp.where(lane_k < FOX_HEAD_DIM if h == 0 else lane_k >= FOX_HEAD_DIM,
                           vblk, jnp.ones_like(vblk))
            p_blocks = []
            m_blocks = []
            cq_all = cq_heads[h]
            m_all = m_ref[h]
            for r0 in range(0, tq, rb):
                t = s_ref[h * tq + r0:h * tq + r0 + rb, :] - ck
                if masked:
                    t = jnp.where(causal[r0:r0 + rb, :], t, NEG_BIG)
                cq = cq_all[r0:r0 + rb, :]
                m_new = jnp.maximum(m_all[r0:r0 + rb, :], cq + jnp.max(t, axis=1, keepdims=True))
                m_blocks.append(m_new)
                p_blocks.append(jnp.exp2(t + (cq - m_new)).astype(BF16))
            m_new_all = jnp.concatenate(m_blocks, axis=0)
            pv = jnp.dot(jnp.concatenate(p_blocks, axis=0), vh, preferred_element_type=F32)
            acc_ref[h] = jnp.exp2(m_all - m_new_all) * acc_ref[h] + pv
            m_ref[h] = m_new_all

    scores(0, sa_ref)

    def pair(j, _):
        scores(2 * j + 1, sb_ref)
        softmax_pv(2 * j, sa_ref, False)
        scores(2 * j + 2, sa_ref)
        softmax_pv(2 * j + 1, sb_ref, False)
        return 0

    lax.fori_loop(0, qi // 2, pair, 0)

    @pl.when(qi % 2 == 0)
    def _():
        softmax_pv(qi, sa_ref, True)

    @pl.when(qi % 2 == 1)
    def _():
        scores(qi, sb_ref)
        softmax_pv(qi - 1, sa_ref, False)
        softmax_pv(qi, sb_ref, True)

    a0 = acc_ref[0]
    a1 = acc_ref[1]
    o0 = a0 / pltpu.roll(a0, FOX_HEAD_DIM, 1)
    o1 = a1 / pltpu.roll(a1, FOX_HEAD_DIM, 1)
    o_ref[...] = jnp.where(head0, o0, o1).astype(o_ref.dtype)


def _fox_attention(qkv, cq, ck, batch, seq):
    tq = min(TQ_ATTN, seq)
    nq = seq // tq
    hp = FOX_HEADS // 2
    return pl.pallas_call(
        functools.partial(_fox_attn_kernel, tq=tq),
        out_shape=jax.ShapeDtypeStruct((batch * seq, D_MODEL), BF16),
        grid=(batch, hp, nq),
        in_specs=[pl.BlockSpec((tq, LANES), lambda b, p, i: (b * nq + i, p)),
                  pl.BlockSpec((seq, LANES), lambda b, p, i: (b, hp + p)),
                  pl.BlockSpec((seq, LANES), lambda b, p, i: (b, 2 * hp + p)),
                  pl.BlockSpec((tq, FOX_HEADS), lambda b, p, i: (b * nq + i, 0)),
                  pl.BlockSpec((None, 2, 1, seq), lambda b, p, i: (b, p, 0, 0))],
        out_specs=pl.BlockSpec((tq, LANES), lambda b, p, i: (b * nq + i, p)),
        scratch_shapes=[pltpu.VMEM((2, tq, 1), F32), pltpu.VMEM((2, tq, LANES), F32),
                        pltpu.VMEM((2 * tq, tq), F32), pltpu.VMEM((2 * tq, tq), F32)],
        compiler_params=_cparams(("parallel", "parallel", "arbitrary")),
        name="fox_attn",
    )(qkv, qkv, qkv, cq, ck)


def _rope_kernel(pos_ref, freq_ref, cos_ref, sin_ref):
    ang = pos_ref[...] * freq_ref[...]
    cos_ref[...] = jnp.cos(ang)
    sin_ref[...] = jnp.sin(ang)


def _rope_tables(pos_f, inv_freq):
    t = pos_f.shape[0]
    half = inv_freq.shape[1]
    tm = min(1024, t)
    return pl.pallas_call(
        _rope_kernel,
        out_shape=(jax.ShapeDtypeStruct((t, half), F32), jax.ShapeDtypeStruct((t, half), F32)),
        grid=(t // tm,),
        in_specs=[pl.BlockSpec((tm, 1), lambda i: (i, 0)),
                  pl.BlockSpec((1, half), lambda i: (0, 0))],
        out_specs=(pl.BlockSpec((tm, half), lambda i: (i, 0)),
                   pl.BlockSpec((tm, half), lambda i: (i, 0))),
        compiler_params=_cparams(("parallel",)),
        name="rope_tables",
    )(pos_f, inv_freq)


def _ret_proj_kernel(a_ref, w_ref, cos_ref, sin_ref, o_ref):
    j = pl.program_id(1)
    a = a_ref[...].astype(BF16)
    acc = jnp.dot(a, w_ref[...], preferred_element_type=F32)

    @pl.when(j < 2)
    def _():
        c = cos_ref[...]
        s = sin_ref[...]
        half = RET_KEY_DIM // 2
        for h in range(RET_HEADS):
            x1 = acc[:, h * RET_KEY_DIM:h * RET_KEY_DIM + half]
            x2 = acc[:, h * RET_KEY_DIM + half:(h + 1) * RET_KEY_DIM]
            o_ref[:, h * RET_KEY_DIM:h * RET_KEY_DIM + half] = (x1 * c - x2 * s).astype(o_ref.dtype)
            o_ref[:, h * RET_KEY_DIM + half:(h + 1) * RET_KEY_DIM] = (x2 * c + x1 * s).astype(o_ref.dtype)

    @pl.when(j >= 2)
    def _():
        o_ref[...] = acc.astype(o_ref.dtype)


def _ret_proj(x2d, w, cos, sin):
    t = x2d.shape[0]
    n = w.shape[1]
    tm = min(TM_PROJ, t)
    tn = D_MODEL
    half = RET_KEY_DIM // 2
    return pl.pallas_call(
        _ret_proj_kernel,
        out_shape=jax.ShapeDtypeStruct((t, n), BF16),
        grid=(t // tm, n // tn),
        in_specs=[pl.BlockSpec((tm, D_MODEL), lambda i, j: (i, 0)),
                  pl.BlockSpec((D_MODEL, tn), lambda i, j: (0, j)),
                  pl.BlockSpec((tm, half), lambda i, j: (i, 0)),
                  pl.BlockSpec((tm, half), lambda i, j: (i, 0))],
        out_specs=pl.BlockSpec((tm, tn), lambda i, j: (i, j)),
        compiler_params=_cparams(("parallel", "arbitrary")),
        name="ret_proj",
    )(x2d, w, cos, sin)


def _retention_kernel(lg_ref, q_ref, k_ref, v_ref, g_ref, o_ref, state_ref, decay_ref, *, chunk):
    dk, dv = RET_KEY_DIM, RET_VAL_DIM

    @pl.when(pl.program_id(1) == 0)
    def _():
        state_ref[...] = jnp.zeros_like(state_ref)
        row = lax.broadcasted_iota(jnp.int32, (chunk, chunk), 0)
        col = lax.broadcasted_iota(jnp.int32, (chunk, chunk), 1)
        diff = (row - col).astype(F32)
        for h in range(RET_HEADS):
            decay_ref[h] = jnp.where(diff >= 0, jnp.exp(jnp.maximum(diff, 0.0) * lg_ref[h]), 0.0)

    idx = lax.broadcasted_iota(jnp.int32, (chunk, 1), 0).astype(F32)
    for h in range(RET_HEADS):
        lg = lg_ref[h]
        q = q_ref[:, h * dk:(h + 1) * dk]
        k = k_ref[:, h * dk:(h + 1) * dk]
        v = v_ref[:, h * dv:(h + 1) * dv]
        scores = lax.dot_general(q, k, (((1,), (1,)), ((), ())), preferred_element_type=F32) * decay_ref[h]
        o = jnp.dot(scores.astype(BF16), v, preferred_element_type=F32)
        state = state_ref[h]
        qd = (q.astype(F32) * jnp.exp((idx + 1.0) * lg)).astype(BF16)
        o = o + jnp.dot(qd, state.astype(BF16), preferred_element_type=F32)
        kd = (k.astype(F32) * jnp.exp((chunk - 1.0 - idx) * lg)).astype(BF16)
        state_ref[h] = state * jnp.exp(chunk * lg) + lax.dot_general(
            kd, v, (((0,), (0,)), ((), ())), preferred_element_type=F32)
        mu = jnp.mean(o, axis=1, keepdims=True)
        oc = o - mu
        var = jnp.mean(oc * oc, axis=1, keepdims=True)
        on = oc * lax.rsqrt(var + LN_EPS)
        g = g_ref[:, h * dv:(h + 1) * dv].astype(F32)
        o_ref[:, h * dv:(h + 1) * dv] = (g * jax.nn.sigmoid(g) * on).astype(o_ref.dtype)


def _retention(proj, log_gamma, batch, seq):
    chunk = min(RET_CHUNK, seq)
    nc = seq // chunk
    dq = RET_HEADS * RET_KEY_DIM
    dvt = RET_HEADS * RET_VAL_DIM
    grid_spec = pltpu.PrefetchScalarGridSpec(
        num_scalar_prefetch=1,
        grid=(batch, nc),
        in_specs=[pl.BlockSpec((chunk, dq), lambda b, c, lg: (b * nc + c, 0)),
                  pl.BlockSpec((chunk, dq), lambda b, c, lg: (b * nc + c, 1)),
                  pl.BlockSpec((chunk, dvt), lambda b, c, lg: (b * nc + c, 1)),
                  pl.BlockSpec((chunk, dvt), lambda b, c, lg: (b * nc + c, 2))],
        out_specs=pl.BlockSpec((chunk, dvt), lambda b, c, lg: (b * nc + c, 0)),
        scratch_shapes=[pltpu.VMEM((RET_HEADS, RET_KEY_DIM, RET_VAL_DIM), F32),
                        pltpu.VMEM((RET_HEADS, chunk, chunk), F32)],
    )
    return pl.pallas_call(
        functools.partial(_retention_kernel, chunk=chunk),
        out_shape=jax.ShapeDtypeStruct((batch * seq, dvt), BF16),
        grid_spec=grid_spec,
        compiler_params=_cparams(("parallel", "arbitrary")),
        name="retention",
    )(log_gamma, proj, proj, proj, proj)


def _layer_norm(y, g, b):
    mu = jnp.mean(y, axis=1, keepdims=True)
    yc = y - mu
    var = jnp.mean(yc * yc, axis=1, keepdims=True)
    return yc * lax.rsqrt(var + LN_EPS) * g + b


def _out_router_kernel(a_ref, w_ref, x_ref, g_ref, b_ref, wrhl_ref, wrh_ref, br_ref, o_ref, ext_ref):
    tm = a_ref.shape[0]
    h = jnp.dot(a_ref[...], w_ref[...], preferred_element_type=F32)
    x1 = _layer_norm(DEEPNORM_ALPHA * x_ref[...] + h, g_ref[...], b_ref[...])
    o_ref[:, :D_MODEL] = x1

    logits = _dot_split(x1, wrhl_ref[...], wrh_ref[...]) + br_ref[...]
    lane = lax.broadcasted_iota(jnp.int32, (tm, LANES), 1)

    def first_argmax(vals, vmax):
        return jnp.min(jnp.where(vals == vmax, lane, LANES), axis=1, keepdims=True)

    gl = jnp.where(lane < N_GROUPS, logits, NEG_BIG)
    gmax = jnp.max(gl, axis=1, keepdims=True)
    gsum = jnp.sum(jnp.where(lane < N_GROUPS, jnp.exp(gl - gmax), 0.0), axis=1, keepdims=True)
    g_val = 1.0 / gsum
    g_idx = first_argmax(gl, gmax)
    lo = CW_LANE0 + EXPERTS_PER_GROUP * g_idx
    in_group = (lane >= lo) & (lane < lo + EXPERTS_PER_GROUP)
    el = jnp.where(in_group, logits, NEG_BIG)
    e1 = jnp.max(el, axis=1, keepdims=True)
    i1 = first_argmax(el, e1)
    el2 = jnp.where(lane == i1, NEG_BIG, el)
    e2 = jnp.max(el2, axis=1, keepdims=True)
    i2 = first_argmax(el2, e2)
    r = jnp.exp(e2 - e1)
    w1 = g_val / (1.0 + r)
    w2 = g_val * r / (1.0 + r)
    ext = jnp.where(lane == i1, w1, 0.0) + jnp.where(lane == i2, w2, 0.0)
    ext = jnp.where(lane == GIDX_LANE, g_idx.astype(F32), ext)
    o_ref[:, D_MODEL:] = ext
    ext_ref[...] = ext


def _out_router(a, w_out, x2d, ln_g, ln_b, w_r, b_r):
    t, din = a.shape
    tm = min(TM_OUT, t)
    w_r_hilo, w_r_hi = _split_weight(w_r)
    return pl.pallas_call(
        _out_router_kernel,
        out_shape=(jax.ShapeDtypeStruct((t, ROW_W), F32), jax.ShapeDtypeStruct((t, EXT), F32)),
        grid=(t // tm,),
        in_specs=[pl.BlockSpec((tm, din), lambda i: (i, 0)),
                  pl.BlockSpec((din, D_MODEL), lambda i: (0, 0)),
                  pl.BlockSpec((tm, D_MODEL), lambda i: (i, 0)),
                  pl.BlockSpec((1, D_MODEL), lambda i: (0, 0)),
                  pl.BlockSpec((1, D_MODEL), lambda i: (0, 0)),
                  pl.BlockSpec((D_MODEL, 2 * LANES), lambda i: (0, 0)),
                  pl.BlockSpec((D_MODEL, LANES), lambda i: (0, 0)),
                  pl.BlockSpec((1, LANES), lambda i: (0, 0))],
        out_specs=(pl.BlockSpec((tm, ROW_W), lambda i: (i, 0)), pl.BlockSpec((tm, EXT), lambda i: (i, 0))),
        compiler_params=_cparams(("parallel",)),
        name="out_router",
    )(a, w_out, x2d, ln_g, ln_b, w_r_hilo, w_r_hi, b_r)


def _row_copy(src_ref, src_row, dst_ref, dst_row, sem):
    return pltpu.make_async_copy(src_ref.at[pl.ds(src_row, 1)], dst_ref.at[pl.ds(dst_row, 1)], sem)


def _moe_kernel(tg_ref, nv_ref, src_ref, x_hbm, wg_ref, wu_ref, wd_ref, g_ref, b_ref, o_ref,
                xs_ref, sem, xb_ref, acc_ref):
    i = pl.program_id(0)
    e = pl.program_id(1)
    n_tiles = pl.num_programs(0)
    tm = TM_MOE
    rows_per_step = tm // EXPERTS_PER_GROUP
    group = tg_ref[i]
    valid = group < N_GROUPS
    slot = i % 2

    def wait_tile(s):
        pltpu.make_async_copy(x_hbm.at[pl.ds(0, tm)], xs_ref.at[s], sem.at[s]).wait()

    @pl.when((i == 0) & (e == 0))
    def _():
        def issue(r, _):
            _row_copy(x_hbm, src_ref[r], xs_ref.at[0], r, sem.at[0]).start()
            return 0

        lax.fori_loop(0, tm, issue, 0, unroll=8)

    @pl.when(valid)
    def _():
        @pl.when(e == 0)
        def _():
            wait_tile(slot)
            xb_ref[...] = xs_ref[slot, :, :D_MODEL].astype(BF16)
            acc_ref[...] = jnp.zeros_like(acc_ref)

        nxt = jnp.minimum(i + 1, n_tiles - 1)
        for k in range(rows_per_step):
            r = e * rows_per_step + k
            _row_copy(x_hbm, src_ref[nxt * tm + r], xs_ref.at[1 - slot], r, sem.at[1 - slot]).start()

        lane = lax.broadcasted_iota(jnp.int32, (tm, EXT), 1)
        row = lax.broadcasted_iota(jnp.int32, (tm, 1), 0)
        cw_lane = CW_LANE0 + EXPERTS_PER_GROUP * group + e
        cw = jnp.sum(jnp.where(lane == cw_lane, xs_ref[slot, :, D_MODEL:], 0.0), axis=1, keepdims=True)
        cw = jnp.where(row < nv_ref[i], cw, 0.0)
        xb = xb_ref[...]
        hg = jnp.dot(xb, wg_ref[...].astype(BF16), preferred_element_type=F32)
        hu = jnp.dot(xb, wu_ref[...].astype(BF16), preferred_element_type=F32)
        hh = (hg * jax.nn.sigmoid(hg) * hu * cw).astype(BF16)
        acc_ref[...] += jnp.dot(hh, wd_ref[...].astype(BF16), preferred_element_type=F32)

        @pl.when(e == EXPERTS_PER_GROUP - 1)
        def _():
            y = DEEPNORM_ALPHA * xs_ref[slot, :, :D_MODEL] + acc_ref[...]
            o_ref[...] = _layer_norm(y, g_ref[...], b_ref[...])

            @pl.when((i + 1 == n_tiles) | (tg_ref[nxt] >= N_GROUPS))
            def _():
                wait_tile(1 - slot)

    @pl.when(jnp.logical_not(valid) & (e == EXPERTS_PER_GROUP - 1))
    def _():
        o_ref[...] = jnp.zeros_like(o_ref)


def _moe(tile_group, n_valid, src, x1ext, w_gate, w_up, w_down, ln_g, ln_b, layer):
    n_tiles = tile_group.shape[0]
    tm = TM_MOE
    epg = EXPERTS_PER_GROUP

    def expert(i, e, tg, nv, src):
        return (layer, jnp.where(tg[i] < N_GROUPS, tg[i] * epg + e, N_EXPERTS - 1), 0, 0)

    grid_spec = pltpu.PrefetchScalarGridSpec(
        num_scalar_prefetch=3,
        grid=(n_tiles, epg),
        in_specs=[pl.BlockSpec(memory_space=pl.ANY),
                  pl.BlockSpec((None, None, D_MODEL, D_EXPERT), expert),
                  pl.BlockSpec((None, None, D_MODEL, D_EXPERT), expert),
                  pl.BlockSpec((None, None, D_EXPERT, D_MODEL), expert),
                  pl.BlockSpec((1, D_MODEL), lambda i, e, tg, nv, src: (0, 0)),
                  pl.BlockSpec((1, D_MODEL), lambda i, e, tg, nv, src: (0, 0))],
        out_specs=pl.BlockSpec((tm, D_MODEL), lambda i, e, tg, nv, src: (i, 0)),
        scratch_shapes=[pltpu.VMEM((2, tm, ROW_W), F32), pltpu.SemaphoreType.DMA((2,)),
                        pltpu.VMEM((tm, D_MODEL), BF16), pltpu.VMEM((tm, D_MODEL), F32)],
    )
    return pl.pallas_call(
        _moe_kernel,
        out_shape=jax.ShapeDtypeStruct((n_tiles * tm, D_MODEL), F32),
        grid_spec=grid_spec,
        compiler_params=_cparams(("arbitrary", "arbitrary")),
        name="moe",
    )(tile_group, n_valid, src, x1ext, w_gate, w_up, w_down, ln_g, ln_b)


def _combine_ple_kernel(pos_ref, x2s_ref, p_ref, wg_ref, bg_ref, wp_ref, o_ref, buf_ref, sem):
    tm = o_ref.shape[0]
    i = pl.program_id(0)

    def gather(step, slot):
        base = step * tm

        def issue(r, _):
            _row_copy(x2s_ref, pos_ref[base + r], buf_ref.at[slot], r, sem.at[slot]).start()
            return 0

        lax.fori_loop(0, tm, issue, 0, unroll=8)

    @pl.when(i == 0)
    def _():
        gather(0, 0)

    @pl.when(i + 1 < pl.num_programs(0))
    def _():
        gather(i + 1, (i + 1) % 2)

    slot = i % 2
    pltpu.make_async_copy(x2s_ref.at[pl.ds(0, tm)], buf_ref.at[slot], sem.at[slot]).wait()
    x2 = buf_ref[slot]
    gate = jax.nn.sigmoid(jnp.dot(x2.astype(BF16), wg_ref[...], preferred_element_type=F32) + bg_ref[...])
    pp = jnp.dot(p_ref[...].astype(BF16), wp_ref[...], preferred_element_type=F32)
    o_ref[...] = x2 + gate * pp


def _combine_ple(pos, x2s, p2d, w_gate, b_gate, w_proj):
    t = p2d.shape[0]
    tm = min(TM_ROWS, t)
    grid_spec = pltpu.PrefetchScalarGridSpec(
        num_scalar_prefetch=1,
        grid=(t // tm,),
        in_specs=[pl.BlockSpec(memory_space=pl.ANY),
                  pl.BlockSpec((tm, PLE_DIM), lambda i, pos: (i, 0)),
                  pl.BlockSpec((D_MODEL, D_MODEL), lambda i, pos: (0, 0)),
                  pl.BlockSpec((1, D_MODEL), lambda i, pos: (0, 0)),
                  pl.BlockSpec((PLE_DIM, D_MODEL), lambda i, pos: (0, 0))],
        out_specs=pl.BlockSpec((tm, D_MODEL), lambda i, pos: (i, 0)),
        scratch_shapes=[pltpu.VMEM((2, tm, D_MODEL), F32), pltpu.SemaphoreType.DMA((2,))],
    )
    return pl.pallas_call(
        _combine_ple_kernel,
        out_shape=jax.ShapeDtypeStruct((t, D_MODEL), F32),
        grid_spec=grid_spec,
        compiler_params=_cparams(("arbitrary",)),
        name="combine_ple",
    )(pos, x2s, p2d, w_gate, b_gate, w_proj)


def _sorted_positions(g_idx, n_tiles):
    onehot = (g_idx[:, None] == jnp.arange(N_GROUPS, dtype=jnp.int32)[None, :]).astype(jnp.int32)
    counts = jnp.sum(onehot, axis=0)
    tiles = (counts + TM_MOE - 1) // TM_MOE
    tile_end = jnp.cumsum(tiles)
    tile_start = tile_end - tiles
    rank = jnp.sum((jnp.cumsum(onehot, axis=0) - onehot) * onehot, axis=1)
    pos = jnp.sum(onehot * (tile_start * TM_MOE)[None, :], axis=1) + rank
    tile_ids = jnp.arange(n_tiles, dtype=jnp.int32)
    tile_group = jnp.sum((tile_ids[:, None] >= tile_end[None, :]).astype(jnp.int32), axis=1)
    g_of_tile = jnp.minimum(tile_group, N_GROUPS - 1)
    n_valid = jnp.clip(counts[g_of_tile] - (tile_ids - tile_start[g_of_tile]) * TM_MOE, 0, TM_MOE)
    n_valid = jnp.where(tile_group < N_GROUPS, n_valid, 0)
    pos = pos.astype(jnp.int32)
    src = jnp.zeros((n_tiles * TM_MOE,), jnp.int32).at[pos].set(jnp.arange(g_idx.shape[0], dtype=jnp.int32))
    return pos, tile_group.astype(jnp.int32), n_valid.astype(jnp.int32), src


def _moe_block(a, w_out, x2d, p2d, ln1_g, ln1_b, ln2_g, ln2_b, w_group, b_group, w_router, b_router,
               w_gate, w_up, w_down, ple_w_proj, ple_w_gate, ple_b_gate, layer):
    t = x2d.shape[0]
    n_tiles = t // TM_MOE + N_GROUPS
    pad = LANES - CW_LANE0 - N_EXPERTS
    w_r = jnp.concatenate([w_group, w_router, jnp.zeros((D_MODEL, pad), F32)], axis=1)
    b_r = jnp.concatenate([b_group, b_router, jnp.zeros((pad,), F32)])[None, :]
    x1ext, ext = _out_router(a, w_out.astype(BF16), x2d, ln1_g[None, :], ln1_b[None, :], w_r, b_r)
    g_idx = ext[:, GIDX_LANE].astype(jnp.int32)
    pos, tile_group, n_valid, src = _sorted_positions(g_idx, n_tiles)
    x2s = _moe(tile_group, n_valid, src, x1ext, w_gate, w_up, w_down, ln2_g[None, :], ln2_b[None, :], layer)
    return _combine_ple(pos, x2s, p2d, ple_w_gate.astype(BF16), ple_b_gate[None, :],
                        ple_w_proj.astype(BF16))


@jax.jit
def _forward(x, p, positions, fox_w_in, fox_b_f, fox_w_out, ret_w_in, ret_w_out, ln1_g, ln1_b, ln2_g, ln2_b,
             moe_w_group, moe_b_group, moe_w_router, moe_b_router, moe_w_gate, moe_w_up, moe_w_down,
             ple_w_proj, ple_w_gate, ple_b_gate):
    batch, seq, d = x.shape
    t = batch * seq
    x2d = x.reshape(t, d)

    def moe_args(i):
        return (ln1_g[i], ln1_b[i], ln2_g[i], ln2_b[i], moe_w_group[i], moe_b_group[i], moe_w_router[i],
                moe_b_router[i], moe_w_gate, moe_w_up, moe_w_down, ple_w_proj[i], ple_w_gate[i],
                ple_b_gate[i], i)

    w_in = fox_w_in[0]
    scale = jnp.concatenate([jnp.full((D_MODEL,), FOX_HEAD_DIM ** -0.5 * LOG2E, F32),
                             jnp.ones((2 * D_MODEL,), F32)])
    w_qkv = (w_in[:, :3 * D_MODEL] * scale[None, :]).astype(BF16)
    qkv = _proj(x2d, w_qkv, D_MODEL)
    c = _fgate(x2d, w_in[:, 3 * D_MODEL:], fox_b_f[0][None, :], batch, seq)
    c3 = c.reshape(batch, seq, FOX_HEADS).transpose(0, 2, 1)
    attn = _fox_attention(qkv, c, c3[:, :, None, :], batch, seq)
    x2d = _moe_block(attn, fox_w_out[0], x2d, p[0].reshape(t, PLE_DIM), *moe_args(0))

    w_in = ret_w_in[0]
    scale = jnp.concatenate([jnp.ones((D_MODEL,), F32), jnp.full((D_MODEL,), RET_KEY_DIM ** -0.5, F32),
                             jnp.ones((4 * D_MODEL,), F32)])
    inv_freq = ROPE_BASE ** (-jnp.arange(0, RET_KEY_DIM, 2, dtype=F32) / RET_KEY_DIM)
    cos, sin = _rope_tables(positions.astype(F32).reshape(t, 1), inv_freq[None, :])
    proj = _ret_proj(x2d, (w_in * scale[None, :]).astype(BF16), cos, sin)
    log_gamma = jnp.log(1.0 - 2.0 ** (-5.0 - jnp.arange(RET_HEADS, dtype=F32)))
    ret = _retention(proj, log_gamma, batch, seq)
    x2d = _moe_block(ret, ret_w_out[0], x2d, p[1].reshape(t, PLE_DIM), *moe_args(1))
    return x2d.reshape(batch, seq, d)


def kernel(x, p, positions, fox_w_in, fox_b_f, fox_w_out, ret_w_in, ret_w_out, ln1_g, ln1_b, ln2_g, ln2_b,
           moe_w_group, moe_b_group, moe_w_router, moe_b_router, moe_w_gate, moe_w_up, moe_w_down,
           ple_w_proj, ple_w_gate, ple_b_gate):
    return _forward(x, p, positions, fox_w_in, fox_b_f, fox_w_out, ret_w_in, ret_w_out, ln1_g, ln1_b,
                    ln2_g, ln2_b, moe_w_group, moe_b_group, moe_w_router, moe_b_router, moe_w_gate,
                    moe_w_up, moe_w_down, ple_w_proj, ple_w_gate, ple_b_gate)
```

```python
import functools

import jax
import jax.numpy as jnp
from jax import lax
from jax.experimental import pallas as pl
from jax.experimental.pallas import tpu as pltpu

F32 = jnp.float32
BF16 = jnp.bfloat16

D_MODEL = 1024
PLE_DIM = 256
FOX_HEADS = 16
FOX_HEAD_DIM = D_MODEL // FOX_HEADS
RET_HEADS = 4
RET_KEY_DIM = D_MODEL // RET_HEADS
RET_VAL_DIM = 2 * D_MODEL // RET_HEADS
ROPE_BASE = 10000.0
N_GROUPS = 4
EXPERTS_PER_GROUP = 4
N_EXPERTS = N_GROUPS * EXPERTS_PER_GROUP
D_EXPERT = 512
DEPTH = 2
DEEPNORM_ALPHA = (2.0 * DEPTH) ** 0.25
LN_EPS = 1e-5

LANES = 128
EXT = LANES
ROW_W = D_MODEL + EXT
GIDX_LANE = 0
CW_LANE0 = 4
NEG_BIG = -1e30
LOG2E = 1.4426950408889634
VMEM_LIMIT = 56 * 1024 * 1024

TM_PROJ = 1024
TS_GATE = 512
TQ_ATTN = 512
ATTN_ROW_BLOCK = 32
TM_OUT = 1024
TM_ROWS = 512
TM_MOE = 1024
RET_CHUNK = 256


def _cparams(sem):
    return pltpu.CompilerParams(dimension_semantics=sem, vmem_limit_bytes=VMEM_LIMIT)


def _proj_kernel(a_ref, w_ref, o_ref):
    a = a_ref[...].astype(BF16)
    o_ref[...] = jnp.dot(a, w_ref[...], preferred_element_type=F32).astype(o_ref.dtype)


def _proj(a, w, tn):
    m, k = a.shape
    n = w.shape[1]
    tm = min(TM_PROJ, m)
    return pl.pallas_call(
        _proj_kernel,
        out_shape=jax.ShapeDtypeStruct((m, n), BF16),
        grid=(m // tm, n // tn),
        in_specs=[pl.BlockSpec((tm, k), lambda i, j: (i, 0)),
                  pl.BlockSpec((k, tn), lambda i, j: (0, j))],
        out_specs=pl.BlockSpec((tm, tn), lambda i, j: (i, j)),
        compiler_params=_cparams(("parallel", "arbitrary")),
        name="proj",
    )(a, w)


def _split_bf16(x):
    hi = x.astype(BF16)
    return hi, (x - hi.astype(F32)).astype(BF16)


def _split_weight(w):
    hi, lo = _split_bf16(w)
    return jnp.concatenate([hi, lo], axis=1), hi


def _dot_split(x, w_hilo, w_hi):
    x_hi, x_lo = _split_bf16(x)
    r = jnp.dot(x_hi, w_hilo, preferred_element_type=F32)
    return r[:, :LANES] + r[:, LANES:] + jnp.dot(x_lo, w_hi, preferred_element_type=F32)


def _fgate_kernel(x_ref, whl_ref, wh_ref, b_ref, c_ref, carry_ref):
    @pl.when(pl.program_id(1) == 0)
    def _():
        carry_ref[...] = jnp.zeros_like(carry_ref)

    ts = x_ref.shape[0]
    z = _dot_split(x_ref[...], whl_ref[...], wh_ref[...]) + b_ref[...]
    logf = (jnp.minimum(z, 0.0) - jnp.log1p(jnp.exp(-jnp.abs(z)))) * LOG2E
    l1 = logf.astype(BF16)
    rest = logf - l1.astype(F32)
    l2, l3 = _split_bf16(rest)
    row = lax.broadcasted_iota(jnp.int32, (ts, ts), 0)
    col = lax.broadcasted_iota(jnp.int32, (ts, ts), 1)
    tri = jnp.where(row >= col, 1.0, 0.0).astype(BF16)
    parts = jnp.dot(tri, jnp.concatenate([l1, l2, l3], axis=1), preferred_element_type=F32)
    c = parts[:, :LANES] + parts[:, LANES:2 * LANES] + parts[:, 2 * LANES:] + carry_ref[...]
    c_ref[...] = c[:, :c_ref.shape[1]]
    carry_ref[...] = c[ts - 1:ts, :]


def _fgate(x2d, w_f, b_f, batch, seq):
    ts = min(TS_GATE, seq)
    ns = seq // ts
    h = w_f.shape[1]
    w_hilo, w_hi = _split_weight(jnp.pad(w_f, ((0, 0), (0, LANES - h))))
    b_pad = jnp.pad(b_f, ((0, 0), (0, LANES - h)))
    return pl.pallas_call(
        _fgate_kernel,
        out_shape=jax.ShapeDtypeStruct((batch * seq, h), F32),
        grid=(batch, ns),
        in_specs=[pl.BlockSpec((ts, D_MODEL), lambda b, s: (b * ns + s, 0)),
                  pl.BlockSpec((D_MODEL, 2 * LANES), lambda b, s: (0, 0)),
                  pl.BlockSpec((D_MODEL, LANES), lambda b, s: (0, 0)),
                  pl.BlockSpec((1, LANES), lambda b, s: (0, 0))],
        out_specs=pl.BlockSpec((ts, h), lambda b, s: (b * ns + s, 0)),
        scratch_shapes=[pltpu.VMEM((1, LANES), F32)],
        compiler_params=_cparams(("parallel", "arbitrary")),
        name="fgate",
    )(x2d, w_hilo, w_hi, b_pad)


def _fox_attn_kernel(q_ref, k_ref, v_ref, cq_ref, ck_ref, o_ref, m_ref, acc_ref, sa_ref, sb_ref, *, tq):
    qi = pl.program_id(2)
    q = q_ref[...]
    lane = lax.broadcasted_iota(jnp.int32, (tq, LANES), 1)
    row = lax.broadcasted_iota(jnp.int32, (tq, tq), 0)
    col = lax.broadcasted_iota(jnp.int32, (tq, tq), 1)
    causal = row >= col
    head0 = lane < FOX_HEAD_DIM
    q2 = jnp.concatenate([jnp.where(head0, q, jnp.zeros_like(q)),
                          jnp.where(head0, jnp.zeros_like(q), q)], axis=0)
    m_ref[...] = jnp.full(m_ref.shape, NEG_BIG, F32)
    acc_ref[...] = jnp.zeros(acc_ref.shape, F32)
    lane_k = lax.broadcasted_iota(jnp.int32, (tq, LANES), 1)
    c_tok = cq_ref[...]
    lane_h = lax.broadcasted_iota(jnp.int32, c_tok.shape, 1)
    cq_heads = [jnp.sum(jnp.where(lane_h == 2 * pl.program_id(1) + h, c_tok, 0.0), axis=1, keepdims=True)
                for h in range(2)]
    rb = ATTN_ROW_BLOCK

    def scores(kb, s_ref):
        ks = pl.multiple_of(kb * tq, tq)
        s_ref[...] = lax.dot_general(q2, k_ref[pl.ds(ks, tq), :], (((1,), (1,)), ((), ())),
                                     preferred_element_type=F32)

    def softmax_pv(kb, s_ref, masked):
        ks = pl.multiple_of(kb * tq, tq)
        vblk = v_ref[pl.ds(ks, tq), :]
        for h in range(2):
            ck = ck_ref[h, :, pl.ds(ks, tq)]
            vh = jnp.where(lane_k < FOX_HEAD_DIM if h == 0 else lane_k >= FOX_HEAD_DIM,
                           vblk, jnp.ones_like(vblk))
            p_blocks = []
            m_blocks = []
            cq_all = cq_heads[h]
            m_all = m_ref[h]
            for r0 in range(0, tq, rb):
                t = s_ref[h * tq + r0:h * tq + r0 + rb, :] - ck
                if masked:
                    t = jnp.where(causal[r0:r0 + rb, :], t, NEG_BIG)
                cq = cq_all[r0:r0 + rb, :]
                m_new = jnp.maximum(m_all[r0:r0 + rb, :], cq + jnp.max(t, axis=1, keepdims=True))
                m_blocks.append(m_new)
                p_blocks.append(jnp.exp2(t + (cq - m_new)).astype(BF16))
            m_new_all = jnp.concatenate(m_blocks, axis=0)
            pv = jnp.dot(jnp.concatenate(p_blocks, axis=0), vh, preferred_element_type=F32)
            acc_ref[h] = jnp.exp2(m_all - m_new_all) * acc_ref[h] + pv
            m_ref[h] = m_new_all

    scores(0, sa_ref)

    def pair(j, _):
        scores(2 * j + 1, sb_ref)
        softmax_pv(2 * j, sa_ref, False)
        scores(2 * j + 2, sa_ref)
        softmax_pv(2 * j + 1, sb_ref, False)
        return 0

    lax.fori_loop(0, qi // 2, pair, 0)

    @pl.when(qi % 2 == 0)
    def _():
        softmax_pv(qi, sa_ref, True)

    @pl.when(qi % 2 == 1)
    def _():
        scores(qi, sb_ref)
        softmax_pv(qi - 1, sa_ref, False)
        softmax_pv(qi, sb_ref, True)

    a0 = acc_ref[0]
    a1 = acc_ref[1]
    o0 = a0 / pltpu.roll(a0, FOX_HEAD_DIM, 1)
    o1 = a1 / pltpu.roll(a1, FOX_HEAD_DIM, 1)
    o_ref[...] = jnp.where(head0, o0, o1).astype(o_ref.dtype)


def _fox_attention(qkv, cq, ck, batch, seq):
    tq = min(TQ_ATTN, seq)
    nq = seq // tq
    hp = FOX_HEADS // 2
    return pl.pallas_call(
        functools.partial(_fox_attn_kernel, tq=tq),
        out_shape=jax.ShapeDtypeStruct((batch * seq, D_MODEL), BF16),
        grid=(batch, hp, nq),
        in_specs=[pl.BlockSpec((tq, LANES), lambda b, p, i: (b * nq + i, p)),
                  pl.BlockSpec((seq, LANES), lambda b, p, i: (b, hp + p)),
                  pl.BlockSpec((seq, LANES), lambda b, p, i: (b, 2 * hp + p)),
                  pl.BlockSpec((tq, FOX_HEADS), lambda b, p, i: (b * nq + i, 0)),
                  pl.BlockSpec((None, 2, 1, seq), lambda b, p, i: (b, p, 0, 0))],
        out_specs=pl.BlockSpec((tq, LANES), lambda b, p, i: (b * nq + i, p)),
        scratch_shapes=[pltpu.VMEM((2, tq, 1), F32), pltpu.VMEM((2, tq, LANES), F32),
                        pltpu.VMEM((2 * tq, tq), F32), pltpu.VMEM((2 * tq, tq), F32)],
        compiler_params=_cparams(("parallel", "parallel", "arbitrary")),
        name="fox_attn",
    )(qkv, qkv, qkv, cq, ck)


def _rope_kernel(pos_ref, freq_ref, cos_ref, sin_ref):
    ang = pos_ref[...] * freq_ref[...]
    cos_ref[...] = jnp.cos(ang)
    sin_ref[...] = jnp.sin(ang)


def _rope_tables(pos_f, inv_freq):
    t = pos_f.shape[0]
    half = inv_freq.shape[1]
    tm = min(1024, t)
    return pl.pallas_call(
        _rope_kernel,
        out_shape=(jax.ShapeDtypeStruct((t, half), F32), jax.ShapeDtypeStruct((t, half), F32)),
        grid=(t // tm,),
        in_specs=[pl.BlockSpec((tm, 1), lambda i: (i, 0)),
                  pl.BlockSpec((1, half), lambda i: (0, 0))],
        out_specs=(pl.BlockSpec((tm, half), lambda i: (i, 0)),
                   pl.BlockSpec((tm, half), lambda i: (i, 0))),
        compiler_params=_cparams(("parallel",)),
        name="rope_tables",
    )(pos_f, inv_freq)


def _ret_proj_kernel(a_ref, w_ref, cos_ref, sin_ref, o_ref):
    j = pl.program_id(1)
    a = a_ref[...].astype(BF16)
    acc = jnp.dot(a, w_ref[...], preferred_element_type=F32)

    @pl.when(j < 2)
    def _():
        c = cos_ref[...]
        s = sin_ref[...]
        half = RET_KEY_DIM // 2
        for h in range(RET_HEADS):
            x1 = acc[:, h * RET_KEY_DIM:h * RET_KEY_DIM + half]
            x2 = acc[:, h * RET_KEY_DIM + half:(h + 1) * RET_KEY_DIM]
            o_ref[:, h * RET_KEY_DIM:h * RET_KEY_DIM + half] = (x1 * c - x2 * s).astype(o_ref.dtype)
            o_ref[:, h * RET_KEY_DIM + half:(h + 1) * RET_KEY_DIM] = (x2 * c + x1 * s).astype(o_ref.dtype)

    @pl.when(j >= 2)
    def _():
        o_ref[...] = acc.astype(o_ref.dtype)


def _ret_proj(x2d, w, cos, sin):
    t = x2d.shape[0]
    n = w.shape[1]
    tm = min(TM_PROJ, t)
    tn = D_MODEL
    half = RET_KEY_DIM // 2
    return pl.pallas_call(
        _ret_proj_kernel,
        out_shape=jax.ShapeDtypeStruct((t, n), BF16),
        grid=(t // tm, n // tn),
        in_specs=[pl.BlockSpec((tm, D_MODEL), lambda i, j: (i, 0)),
                  pl.BlockSpec((D_MODEL, tn), lambda i, j: (0, j)),
                  pl.BlockSpec((tm, half), lambda i, j: (i, 0)),
                  pl.BlockSpec((tm, half), lambda i, j: (i, 0))],
        out_specs=pl.BlockSpec((tm, tn), lambda i, j: (i, j)),
        compiler_params=_cparams(("parallel", "arbitrary")),
        name="ret_proj",
    )(x2d, w, cos, sin)


def _retention_kernel(lg_ref, q_ref, k_ref, v_ref, g_ref, o_ref, state_ref, decay_ref, *, chunk):
    dk, dv = RET_KEY_DIM, RET_VAL_DIM

    @pl.when(pl.program_id(1) == 0)
    def _():
        state_ref[...] = jnp.zeros_like(state_ref)
        row = lax.broadcasted_iota(jnp.int32, (chunk, chunk), 0)
        col = lax.broadcasted_iota(jnp.int32, (chunk, chunk), 1)
        diff = (row - col).astype(F32)
        for h in range(RET_HEADS):
            decay_ref[h] = jnp.where(diff >= 0, jnp.exp(jnp.maximum(diff, 0.0) * lg_ref[h]), 0.0)

    idx = lax.broadcasted_iota(jnp.int32, (chunk, 1), 0).astype(F32)
    for h in range(RET_HEADS):
        lg = lg_ref[h]
        q = q_ref[:, h * dk:(h + 1) * dk]
        k = k_ref[:, h * dk:(h + 1) * dk]
        v = v_ref[:, h * dv:(h + 1) * dv]
        scores = lax.dot_general(q, k, (((1,), (1,)), ((), ())), preferred_element_type=F32) * decay_ref[h]
        o = jnp.dot(scores.astype(BF16), v, preferred_element_type=F32)
        state = state_ref[h]
        qd = (q.astype(F32) * jnp.exp((idx + 1.0) * lg)).astype(BF16)
        o = o + jnp.dot(qd, state.astype(BF16), preferred_element_type=F32)
        kd = (k.astype(F32) * jnp.exp((chunk - 1.0 - idx) * lg)).astype(BF16)
        state_ref[h] = state * jnp.exp(chunk * lg) + lax.dot_general(
            kd, v, (((0,), (0,)), ((), ())), preferred_element_type=F32)
        mu = jnp.mean(o, axis=1, keepdims=True)
        oc = o - mu
        var = jnp.mean(oc * oc, axis=1, keepdims=True)
        on = oc * lax.rsqrt(var + LN_EPS)
        g = g_ref[:, h * dv:(h + 1) * dv].astype(F32)
        o_ref[:, h * dv:(h + 1) * dv] = (g * jax.nn.sigmoid(g) * on).astype(o_ref.dtype)


def _retention(proj, log_gamma, batch, seq):
    chunk = min(RET_CHUNK, seq)
    nc = seq // chunk
    dq = RET_HEADS * RET_KEY_DIM
    dvt = RET_HEADS * RET_VAL_DIM
    grid_spec = pltpu.PrefetchScalarGridSpec(
        num_scalar_prefetch=1,
        grid=(batch, nc),
        in_specs=[pl.BlockSpec((chunk, dq), lambda b, c, lg: (b * nc + c, 0)),
                  pl.BlockSpec((chunk, dq), lambda b, c, lg: (b * nc + c, 1)),
                  pl.BlockSpec((chunk, dvt), lambda b, c, lg: (b * nc + c, 1)),
                  pl.BlockSpec((chunk, dvt), lambda b, c, lg: (b * nc + c, 2))],
        out_specs=pl.BlockSpec((chunk, dvt), lambda b, c, lg: (b * nc + c, 0)),
        scratch_shapes=[pltpu.VMEM((RET_HEADS, RET_KEY_DIM, RET_VAL_DIM), F32),
                        pltpu.VMEM((RET_HEADS, chunk, chunk), F32)],
    )
    return pl.pallas_call(
        functools.partial(_retention_kernel, chunk=chunk),
        out_shape=jax.ShapeDtypeStruct((batch * seq, dvt), BF16),
        grid_spec=grid_spec,
        compiler_params=_cparams(("parallel", "arbitrary")),
        name="retention",
    )(log_gamma, proj, proj, proj, proj)


def _layer_norm(y, g, b):
    mu = jnp.mean(y, axis=1, keepdims=True)
    yc = y - mu
    var = jnp.mean(yc * yc, axis=1, keepdims=True)
    return yc * lax.rsqrt(var + LN_EPS) * g + b


def _out_router_kernel(a_ref, w_ref, x_ref, g_ref, b_ref, wrhl_ref, wrh_ref, br_ref, o_ref, ext_ref):
    tm = a_ref.shape[0]
    h = jnp.dot(a_ref[...], w_ref[...], preferred_element_type=F32)
    x1 = _layer_norm(DEEPNORM_ALPHA * x_ref[...] + h, g_ref[...], b_ref[...])
    o_ref[:, :D_MODEL] = x1

    logits = _dot_split(x1, wrhl_ref[...], wrh_ref[...]) + br_ref[...]
    lane = lax.broadcasted_iota(jnp.int32, (tm, LANES), 1)

    def first_argmax(vals, vmax):
        return jnp.min(jnp.where(vals == vmax, lane, LANES), axis=1, keepdims=True)

    gl = jnp.where(lane < N_GROUPS, logits, NEG_BIG)
    gmax = jnp.max(gl, axis=1, keepdims=True)
    gsum = jnp.sum(jnp.where(lane < N_GROUPS, jnp.exp(gl - gmax), 0.0), axis=1, keepdims=True)
    g_val = 1.0 / gsum
    g_idx = first_argmax(gl, gmax)
    lo = CW_LANE0 + EXPERTS_PER_GROUP * g_idx
    in_group = (lane >= lo) & (lane < lo + EXPERTS_PER_GROUP)
    el = jnp.where(in_group, logits, NEG_BIG)
    e1 = jnp.max(el, axis=1, keepdims=True)
    i1 = first_argmax(el, e1)
    el2 = jnp.where(lane == i1, NEG_BIG, el)
    e2 = jnp.max(el2, axis=1, keepdims=True)
    i2 = first_argmax(el2, e2)
    r = jnp.exp(e2 - e1)
    w1 = g_val / (1.0 + r)
    w2 = g_val * r / (1.0 + r)
    ext = jnp.where(lane == i1, w1, 0.0) + jnp.where(lane == i2, w2, 0.0)
    ext = jnp.where(lane == GIDX_LANE, g_idx.astype(F32), ext)
    o_ref[:, D_MODEL:] = ext
    ext_ref[...] = ext


def _out_router(a, w_out, x2d, ln_g, ln_b, w_r, b_r):
    t, din = a.shape
    tm = min(TM_OUT, t)
    w_r_hilo, w_r_hi = _split_weight(w_r)
    return pl.pallas_call(
        _out_router_kernel,
        out_shape=(jax.ShapeDtypeStruct((t, ROW_W), F32), jax.ShapeDtypeStruct((t, EXT), F32)),
        grid=(t // tm,),
        in_specs=[pl.BlockSpec((tm, din), lambda i: (i, 0)),
                  pl.BlockSpec((din, D_MODEL), lambda i: (0, 0)),
                  pl.BlockSpec((tm, D_MODEL), lambda i: (i, 0)),
                  pl.BlockSpec((1, D_MODEL), lambda i: (0, 0)),
                  pl.BlockSpec((1, D_MODEL), lambda i: (0, 0)),
                  pl.BlockSpec((D_MODEL, 2 * LANES), lambda i: (0, 0)),
                  pl.BlockSpec((D_MODEL, LANES), lambda i: (0, 0)),
                  pl.BlockSpec((1, LANES), lambda i: (0, 0))],
        out_specs=(pl.BlockSpec((tm, ROW_W), lambda i: (i, 0)), pl.BlockSpec((tm, EXT), lambda i: (i, 0))),
        compiler_params=_cparams(("parallel",)),
        name="out_router",
    )(a, w_out, x2d, ln_g, ln_b, w_r_hilo, w_r_hi, b_r)


def _row_copy(src_ref, src_row, dst_ref, dst_row, sem):
    return pltpu.make_async_copy(src_ref.at[pl.ds(src_row, 1)], dst_ref.at[pl.ds(dst_row, 1)], sem)


def _dispatch_kernel(pos_ref, x_ref, init_ref, xs_ref, sem):
    del init_ref
    tm = x_ref.shape[0]
    base = pl.program_id(0) * tm

    def issue(r, _):
        _row_copy(x_ref, r, xs_ref, pos_ref[base + r], sem).start()
        return 0

    lax.fori_loop(0, tm, issue, 0, unroll=8)
    pltpu.make_async_copy(x_ref, xs_ref.at[pl.ds(0, tm)], sem).wait()


def _dispatch(pos, x1ext, n_rows):
    t = x1ext.shape[0]
    tm = min(TM_ROWS, t)
    grid_spec = pltpu.PrefetchScalarGridSpec(
        num_scalar_prefetch=1,
        grid=(t // tm,),
        in_specs=[pl.BlockSpec((tm, ROW_W), lambda i, pos: (i, 0)),
                  pl.BlockSpec(memory_space=pl.ANY)],
        out_specs=pl.BlockSpec(memory_space=pl.ANY),
        scratch_shapes=[pltpu.SemaphoreType.DMA(())],
    )
    return pl.pallas_call(
        _dispatch_kernel,
        out_shape=jax.ShapeDtypeStruct((n_rows, ROW_W), F32),
        grid_spec=grid_spec,
        input_output_aliases={2: 0},
        compiler_params=_cparams(("arbitrary",)),
        name="dispatch",
    )(pos, x1ext, jnp.zeros((n_rows, ROW_W), F32))


def _moe_kernel(tg_ref, xs_ref, wg_ref, wu_ref, wd_ref, g_ref, b_ref, o_ref, xb_ref, acc_ref):
    i = pl.program_id(0)
    e = pl.program_id(1)
    tm = xs_ref.shape[0]
    group = tg_ref[i]
    valid = group < N_GROUPS

    @pl.when(valid)
    def _():
        @pl.when(e == 0)
        def _():
            xb_ref[...] = xs_ref[:, :D_MODEL].astype(BF16)
            acc_ref[...] = jnp.zeros_like(acc_ref)

        lane = lax.broadcasted_iota(jnp.int32, (tm, EXT), 1)
        cw_lane = CW_LANE0 + EXPERTS_PER_GROUP * group + e
        cw = jnp.sum(jnp.where(lane == cw_lane, xs_ref[:, D_MODEL:], 0.0), axis=1, keepdims=True)
        xb = xb_ref[...]
        hg = jnp.dot(xb, wg_ref[...].astype(BF16), preferred_element_type=F32)
        hu = jnp.dot(xb, wu_ref[...].astype(BF16), preferred_element_type=F32)
        hh = (hg * jax.nn.sigmoid(hg) * hu * cw).astype(BF16)
        acc_ref[...] += jnp.dot(hh, wd_ref[...].astype(BF16), preferred_element_type=F32)

        @pl.when(e == EXPERTS_PER_GROUP - 1)
        def _():
            y = DEEPNORM_ALPHA * xs_ref[:, :D_MODEL] + acc_ref[...]
            o_ref[...] = _layer_norm(y, g_ref[...], b_ref[...])

    @pl.when(jnp.logical_not(valid) & (e == EXPERTS_PER_GROUP - 1))
    def _():
        o_ref[...] = jnp.zeros_like(o_ref)


def _moe(tile_group, xs, w_gate, w_up, w_down, ln_g, ln_b, layer):
    n_rows = xs.shape[0]
    tm = TM_MOE
    epg = EXPERTS_PER_GROUP

    def expert(i, e, tg):
        return (layer, jnp.where(tg[i] < N_GROUPS, tg[i] * epg + e, N_EXPERTS - 1), 0, 0)

    grid_spec = pltpu.PrefetchScalarGridSpec(
        num_scalar_prefetch=1,
        grid=(n_rows // tm, epg),
        in_specs=[pl.BlockSpec((tm, ROW_W), lambda i, e, tg: (i, 0)),
                  pl.BlockSpec((None, None, D_MODEL, D_EXPERT), expert),
                  pl.BlockSpec((None, None, D_MODEL, D_EXPERT), expert),
                  pl.BlockSpec((None, None, D_EXPERT, D_MODEL), expert),
                  pl.BlockSpec((1, D_MODEL), lambda i, e, tg: (0, 0)),
                  pl.BlockSpec((1, D_MODEL), lambda i, e, tg: (0, 0))],
        out_specs=pl.BlockSpec((tm, D_MODEL), lambda i, e, tg: (i, 0)),
        scratch_shapes=[pltpu.VMEM((tm, D_MODEL), BF16), pltpu.VMEM((tm, D_MODEL), F32)],
    )
    return pl.pallas_call(
        _moe_kernel,
        out_shape=jax.ShapeDtypeStruct((n_rows, D_MODEL), F32),
        grid_spec=grid_spec,
        compiler_params=_cparams(("parallel", "arbitrary")),
        name="moe",
    )(tile_group, xs, w_gate, w_up, w_down, ln_g, ln_b)


def _combine_ple_kernel(pos_ref, x2s_ref, p_ref, wg_ref, bg_ref, wp_ref, o_ref, buf_ref, sem):
    tm = o_ref.shape[0]
    i = pl.program_id(0)

    def gather(step, slot):
        base = step * tm

        def issue(r, _):
            _row_copy(x2s_ref, pos_ref[base + r], buf_ref.at[slot], r, sem.at[slot]).start()
            return 0

        lax.fori_loop(0, tm, issue, 0, unroll=8)

    @pl.when(i == 0)
    def _():
        gather(0, 0)

    @pl.when(i + 1 < pl.num_programs(0))
    def _():
        gather(i + 1, (i + 1) % 2)

    slot = i % 2
    pltpu.make_async_copy(x2s_ref.at[pl.ds(0, tm)], buf_ref.at[slot], sem.at[slot]).wait()
    x2 = buf_ref[slot]
    gate = jax.nn.sigmoid(jnp.dot(x2.astype(BF16), wg_ref[...], preferred_element_type=F32) + bg_ref[...])
    pp = jnp.dot(p_ref[...].astype(BF16), wp_ref[...], preferred_element_type=F32)
    o_ref[...] = x2 + gate * pp


def _combine_ple(pos, x2s, p2d, w_gate, b_gate, w_proj):
    t = p2d.shape[0]
    tm = min(TM_ROWS, t)
    grid_spec = pltpu.PrefetchScalarGridSpec(
        num_scalar_prefetch=1,
        grid=(t // tm,),
        in_specs=[pl.BlockSpec(memory_space=pl.ANY),
                  pl.BlockSpec((tm, PLE_DIM), lambda i, pos: (i, 0)),
                  pl.BlockSpec((D_MODEL, D_MODEL), lambda i, pos: (0, 0)),
                  pl.BlockSpec((1, D_MODEL), lambda i, pos: (0, 0)),
                  pl.BlockSpec((PLE_DIM, D_MODEL), lambda i, pos: (0, 0))],
        out_specs=pl.BlockSpec((tm, D_MODEL), lambda i, pos: (i, 0)),
        scratch_shapes=[pltpu.VMEM((2, tm, D_MODEL), F32), pltpu.SemaphoreType.DMA((2,))],
    )
    return pl.pallas_call(
        _combine_ple_kernel,
        out_shape=jax.ShapeDtypeStruct((t, D_MODEL), F32),
        grid_spec=grid_spec,
        compiler_params=_cparams(("arbitrary",)),
        name="combine_ple",
    )(pos, x2s, p2d, w_gate, b_gate, w_proj)


def _sorted_positions(g_idx, n_tiles):
    onehot = (g_idx[:, None] == jnp.arange(N_GROUPS, dtype=jnp.int32)[None, :]).astype(jnp.int32)
    counts = jnp.sum(onehot, axis=0)
    tiles = (counts + TM_MOE - 1) // TM_MOE
    tile_end = jnp.cumsum(tiles)
    tile_start = tile_end - tiles
    rank = jnp.sum((jnp.cumsum(onehot, axis=0) - onehot) * onehot, axis=1)
    pos = jnp.sum(onehot * (tile_start * TM_MOE)[None, :], axis=1) + rank
    tile_ids = jnp.arange(n_tiles, dtype=jnp.int32)
    tile_group = jnp.sum((tile_ids[:, None] >= tile_end[None, :]).astype(jnp.int32), axis=1)
    return pos.astype(jnp.int32), tile_group.astype(jnp.int32)


def _moe_block(a, w_out, x2d, p2d, ln1_g, ln1_b, ln2_g, ln2_b, w_group, b_group, w_router, b_router,
               w_gate, w_up, w_down, ple_w_proj, ple_w_gate, ple_b_gate, layer):
    t = x2d.shape[0]
    n_tiles = t // TM_MOE + N_GROUPS
    pad = LANES - CW_LANE0 - N_EXPERTS
    w_r = jnp.concatenate([w_group, w_router, jnp.zeros((D_MODEL, pad), F32)], axis=1)
    b_r = jnp.concatenate([b_group, b_router, jnp.zeros((pad,), F32)])[None, :]
    x1ext, ext = _out_router(a, w_out.astype(BF16), x2d, ln1_g[None, :], ln1_b[None, :], w_r, b_r)
    g_idx = ext[:, GIDX_LANE].astype(jnp.int32)
    pos, tile_group = _sorted_positions(g_idx, n_tiles)
    xs = _dispatch(pos, x1ext, n_tiles * TM_MOE)
    x2s = _moe(tile_group, xs, w_gate, w_up, w_down, ln2_g[None, :], ln2_b[None, :], layer)
    return _combine_ple(pos, x2s, p2d, ple_w_gate.astype(BF16), ple_b_gate[None, :],
                        ple_w_proj.astype(BF16))


@jax.jit
def _forward(x, p, positions, fox_w_in, fox_b_f, fox_w_out, ret_w_in, ret_w_out, ln1_g, ln1_b, ln2_g, ln2_b,
             moe_w_group, moe_b_group, moe_w_router, moe_b_router, moe_w_gate, moe_w_up, moe_w_down,
             ple_w_proj, ple_w_gate, ple_b_gate):
    batch, seq, d = x.shape
    t = batch * seq
    x2d = x.reshape(t, d)

    def moe_args(i):
        return (ln1_g[i], ln1_b[i], ln2_g[i], ln2_b[i], moe_w_group[i], moe_b_group[i], moe_w_router[i],
                moe_b_router[i], moe_w_gate, moe_w_up, moe_w_down, ple_w_proj[i], ple_w_gate[i],
                ple_b_gate[i], i)

    w_in = fox_w_in[0]
    scale = jnp.concatenate([jnp.full((D_MODEL,), FOX_HEAD_DIM ** -0.5 * LOG2E, F32),
                             jnp.ones((2 * D_MODEL,), F32)])
    w_qkv = (w_in[:, :3 * D_MODEL] * scale[None, :]).astype(BF16)
    qkv = _proj(x2d, w_qkv, D_MODEL)
    c = _fgate(x2d, w_in[:, 3 * D_MODEL:], fox_b_f[0][None, :], batch, seq)
    c3 = c.reshape(batch, seq, FOX_HEADS).transpose(0, 2, 1)
    attn = _fox_attention(qkv, c, c3[:, :, None, :], batch, seq)
    x2d = _moe_block(attn, fox_w_out[0], x2d, p[0].reshape(t, PLE_DIM), *moe_args(0))

    w_in = ret_w_in[0]
    scale = jnp.concatenate([jnp.ones((D_MODEL,), F32), jnp.full((D_MODEL,), RET_KEY_DIM ** -0.5, F32),
                             jnp.ones((4 * D_MODEL,), F32)])
    inv_freq = ROPE_BASE ** (-jnp.arange(0, RET_KEY_DIM, 2, dtype=F32) / RET_KEY_DIM)
    cos, sin = _rope_tables(positions.astype(F32).reshape(t, 1), inv_freq[None, :])
    proj = _ret_proj(x2d, (w_in * scale[None, :]).astype(BF16), cos, sin)
    log_gamma = jnp.log(1.0 - 2.0 ** (-5.0 - jnp.arange(RET_HEADS, dtype=F32)))
    ret = _retention(proj, log_gamma, batch, seq)
    x2d = _moe_block(ret, ret_w_out[0], x2d, p[1].reshape(t, PLE_DIM), *moe_args(1))
    return x2d.reshape(batch, seq, d)


def kernel(x, p, positions, fox_w_in, fox_b_f, fox_w_out, ret_w_in, ret_w_out, ln1_g, ln1_b, ln2_g, ln2_b,
           moe_w_group, moe_b_group, moe_w_router, moe_b_router, moe_w_gate, moe_w_up, moe_w_down,
           ple_w_proj, ple_w_gate, ple_b_gate):
    return _forward(x, p, positions, fox_w_in, fox_b_f, fox_w_out, ret_w_in, ret_w_out, ln1_g, ln1_b,
                    ln2_g, ln2_b, moe_w_group, moe_b_group, moe_w_router, moe_b_router, moe_w_gate,
                    moe_w_up, moe_w_down, ple_w_proj, ple_w_gate, ple_b_gate)
```

```python
import functools

import jax
import jax.numpy as jnp
from jax import lax
from jax.experimental import pallas as pl
from jax.experimental.pallas import tpu as pltpu

F32 = jnp.float32
BF16 = jnp.bfloat16

D_MODEL = 1024
PLE_DIM = 256
FOX_HEADS = 16
FOX_HEAD_DIM = D_MODEL // FOX_HEADS
RET_HEADS = 4
RET_KEY_DIM = D_MODEL // RET_HEADS
RET_VAL_DIM = 2 * D_MODEL // RET_HEADS
ROPE_BASE = 10000.0
N_GROUPS = 4
EXPERTS_PER_GROUP = 4
N_EXPERTS = N_GROUPS * EXPERTS_PER_GROUP
D_EXPERT = 512
DEPTH = 2
DEEPNORM_ALPHA = (2.0 * DEPTH) ** 0.25
LN_EPS = 1e-5

LANES = 128
EXT = LANES
ROW_W = D_MODEL + EXT
GIDX_LANE = 0
CW_LANE0 = 4
NEG_BIG = -1e30
LOG2E = 1.4426950408889634
VMEM_LIMIT = 56 * 1024 * 1024

TM_PROJ = 1024
TS_GATE = 512
TQ_ATTN = 512
ATTN_ROW_BLOCK = 128
TM_OUT = 1024
TM_ROWS = 1024
TM_MOE = 1024
RET_CHUNK = 256


def _cparams(sem):
    return pltpu.CompilerParams(dimension_semantics=sem, vmem_limit_bytes=VMEM_LIMIT)


def _proj_kernel(a_ref, w_ref, o_ref):
    a = a_ref[...].astype(BF16)
    o_ref[...] = jnp.dot(a, w_ref[...], preferred_element_type=F32).astype(o_ref.dtype)


def _proj(a, w, tn):
    m, k = a.shape
    n = w.shape[1]
    tm = min(TM_PROJ, m)
    return pl.pallas_call(
        _proj_kernel,
        out_shape=jax.ShapeDtypeStruct((m, n), BF16),
        grid=(m // tm, n // tn),
        in_specs=[pl.BlockSpec((tm, k), lambda i, j: (i, 0)),
                  pl.BlockSpec((k, tn), lambda i, j: (0, j))],
        out_specs=pl.BlockSpec((tm, tn), lambda i, j: (i, j)),
        compiler_params=_cparams(("parallel", "arbitrary")),
        name="proj",
    )(a, w)


def _split_bf16(x):
    hi = x.astype(BF16)
    return hi, (x - hi.astype(F32)).astype(BF16)


def _split_weight(w):
    hi, lo = _split_bf16(w)
    return jnp.concatenate([hi, lo], axis=1), hi


def _dot_split(x, w_hilo, w_hi):
    x_hi, x_lo = _split_bf16(x)
    r = jnp.dot(x_hi, w_hilo, preferred_element_type=F32)
    return r[:, :LANES] + r[:, LANES:] + jnp.dot(x_lo, w_hi, preferred_element_type=F32)


def _fgate_kernel(x_ref, whl_ref, wh_ref, b_ref, c_ref, carry_ref):
    @pl.when(pl.program_id(1) == 0)
    def _():
        carry_ref[...] = jnp.zeros_like(carry_ref)

    ts = x_ref.shape[0]
    z = _dot_split(x_ref[...], whl_ref[...], wh_ref[...]) + b_ref[...]
    logf = (jnp.minimum(z, 0.0) - jnp.log1p(jnp.exp(-jnp.abs(z)))) * LOG2E
    l1 = logf.astype(BF16)
    rest = logf - l1.astype(F32)
    l2, l3 = _split_bf16(rest)
    row = lax.broadcasted_iota(jnp.int32, (ts, ts), 0)
    col = lax.broadcasted_iota(jnp.int32, (ts, ts), 1)
    tri = jnp.where(row >= col, 1.0, 0.0).astype(BF16)
    parts = jnp.dot(tri, jnp.concatenate([l1, l2, l3], axis=1), preferred_element_type=F32)
    c = parts[:, :LANES] + parts[:, LANES:2 * LANES] + parts[:, 2 * LANES:] + carry_ref[...]
    c_ref[...] = c[:, :c_ref.shape[1]]
    carry_ref[...] = c[ts - 1:ts, :]


def _fgate(x2d, w_f, b_f, batch, seq):
    ts = min(TS_GATE, seq)
    ns = seq // ts
    h = w_f.shape[1]
    w_hilo, w_hi = _split_weight(jnp.pad(w_f, ((0, 0), (0, LANES - h))))
    b_pad = jnp.pad(b_f, ((0, 0), (0, LANES - h)))
    return pl.pallas_call(
        _fgate_kernel,
        out_shape=jax.ShapeDtypeStruct((batch * seq, h), F32),
        grid=(batch, ns),
        in_specs=[pl.BlockSpec((ts, D_MODEL), lambda b, s: (b * ns + s, 0)),
                  pl.BlockSpec((D_MODEL, 2 * LANES), lambda b, s: (0, 0)),
                  pl.BlockSpec((D_MODEL, LANES), lambda b, s: (0, 0)),
                  pl.BlockSpec((1, LANES), lambda b, s: (0, 0))],
        out_specs=pl.BlockSpec((ts, h), lambda b, s: (b * ns + s, 0)),
        scratch_shapes=[pltpu.VMEM((1, LANES), F32)],
        compiler_params=_cparams(("parallel", "arbitrary")),
        name="fgate",
    )(x2d, w_hilo, w_hi, b_pad)


def _fox_attn_kernel(q_ref, k_ref, v_ref, cq_ref, ck_ref, o_ref, m_ref, acc_ref, sa_ref, sb_ref, *, tq):
    qi = pl.program_id(2)
    q = q_ref[...]
    lane = lax.broadcasted_iota(jnp.int32, (tq, LANES), 1)
    row = lax.broadcasted_iota(jnp.int32, (tq, tq), 0)
    col = lax.broadcasted_iota(jnp.int32, (tq, tq), 1)
    causal = row >= col
    head0 = lane < FOX_HEAD_DIM
    q2 = jnp.concatenate([jnp.where(head0, q, jnp.zeros_like(q)),
                          jnp.where(head0, jnp.zeros_like(q), q)], axis=0)
    m_ref[...] = jnp.full(m_ref.shape, NEG_BIG, F32)
    acc_ref[...] = jnp.zeros(acc_ref.shape, F32)
    lane_k = lax.broadcasted_iota(jnp.int32, (tq, LANES), 1)
    c_tok = cq_ref[...]
    lane_h = lax.broadcasted_iota(jnp.int32, c_tok.shape, 1)
    cq_heads = [jnp.sum(jnp.where(lane_h == 2 * pl.program_id(1) + h, c_tok, 0.0), axis=1, keepdims=True)
                for h in range(2)]
    rb = ATTN_ROW_BLOCK

    def scores(kb, s_ref):
        ks = pl.multiple_of(kb * tq, tq)
        s_ref[...] = lax.dot_general(q2, k_ref[pl.ds(ks, tq), :], (((1,), (1,)), ((), ())),
                                     preferred_element_type=F32)

    def softmax_pv(kb, s_ref, masked):
        ks = pl.multiple_of(kb * tq, tq)
        vblk = v_ref[pl.ds(ks, tq), :]
        for h in range(2):
            ck = ck_ref[h, :, pl.ds(ks, tq)]
            vh = jnp.where(lane_k < FOX_HEAD_DIM if h == 0 else lane_k >= FOX_HEAD_DIM,
                           vblk, jnp.ones_like(vblk))
            p_blocks = []
            m_blocks = []
            cq_all = cq_heads[h]
            m_all = m_ref[h]
            for r0 in range(0, tq, rb):
                t = s_ref[h * tq + r0:h * tq + r0 + rb, :] - ck
                if masked:
                    t = jnp.where(causal[r0:r0 + rb, :], t, NEG_BIG)
                cq = cq_all[r0:r0 + rb, :]
                m_new = jnp.maximum(m_all[r0:r0 + rb, :], cq + jnp.max(t, axis=1, keepdims=True))
                m_blocks.append(m_new)
                p_blocks.append(jnp.exp2(t + (cq - m_new)).astype(BF16))
            m_new_all = jnp.concatenate(m_blocks, axis=0)
            pv = jnp.dot(jnp.concatenate(p_blocks, axis=0), vh, preferred_element_type=F32)
            acc_ref[h] = jnp.exp2(m_all - m_new_all) * acc_ref[h] + pv
            m_ref[h] = m_new_all

    scores(0, sa_ref)

    def pair(j, _):
        scores(2 * j + 1, sb_ref)
        softmax_pv(2 * j, sa_ref, False)
        scores(2 * j + 2, sa_ref)
        softmax_pv(2 * j + 1, sb_ref, False)
        return 0

    lax.fori_loop(0, qi // 2, pair, 0)

    @pl.when(qi % 2 == 0)
    def _():
        softmax_pv(qi, sa_ref, True)

    @pl.when(qi % 2 == 1)
    def _():
        scores(qi, sb_ref)
        softmax_pv(qi - 1, sa_ref, False)
        softmax_pv(qi, sb_ref, True)

    a0 = acc_ref[0]
    a1 = acc_ref[1]
    o0 = a0 / pltpu.roll(a0, FOX_HEAD_DIM, 1)
    o1 = a1 / pltpu.roll(a1, FOX_HEAD_DIM, 1)
    o_ref[...] = jnp.where(head0, o0, o1).astype(o_ref.dtype)


def _fox_attention(qkv, cq, ck, batch, seq):
    tq = min(TQ_ATTN, seq)
    nq = seq // tq
    hp = FOX_HEADS // 2
    return pl.pallas_call(
        functools.partial(_fox_attn_kernel, tq=tq),
        out_shape=jax.ShapeDtypeStruct((batch * seq, D_MODEL), BF16),
        grid=(batch, hp, nq),
        in_specs=[pl.BlockSpec((tq, LANES), lambda b, p, i: (b * nq + i, p)),
                  pl.BlockSpec((seq, LANES), lambda b, p, i: (b, hp + p)),
                  pl.BlockSpec((seq, LANES), lambda b, p, i: (b, 2 * hp + p)),
                  pl.BlockSpec((tq, FOX_HEADS), lambda b, p, i: (b * nq + i, 0)),
                  pl.BlockSpec((None, 2, 1, seq), lambda b, p, i: (b, p, 0, 0))],
        out_specs=pl.BlockSpec((tq, LANES), lambda b, p, i: (b * nq + i, p)),
        scratch_shapes=[pltpu.VMEM((2, tq, 1), F32), pltpu.VMEM((2, tq, LANES), F32),
                        pltpu.VMEM((2 * tq, tq), F32), pltpu.VMEM((2 * tq, tq), F32)],
        compiler_params=_cparams(("parallel", "parallel", "arbitrary")),
        name="fox_attn",
    )(qkv, qkv, qkv, cq, ck)


def _rope_kernel(pos_ref, freq_ref, cos_ref, sin_ref):
    ang = pos_ref[...] * freq_ref[...]
    cos_ref[...] = jnp.cos(ang)
    sin_ref[...] = jnp.sin(ang)


def _rope_tables(pos_f, inv_freq):
    t = pos_f.shape[0]
    half = inv_freq.shape[1]
    tm = min(1024, t)
    return pl.pallas_call(
        _rope_kernel,
        out_shape=(jax.ShapeDtypeStruct((t, half), F32), jax.ShapeDtypeStruct((t, half), F32)),
        grid=(t // tm,),
        in_specs=[pl.BlockSpec((tm, 1), lambda i: (i, 0)),
                  pl.BlockSpec((1, half), lambda i: (0, 0))],
        out_specs=(pl.BlockSpec((tm, half), lambda i: (i, 0)),
                   pl.BlockSpec((tm, half), lambda i: (i, 0))),
        compiler_params=_cparams(("parallel",)),
        name="rope_tables",
    )(pos_f, inv_freq)


def _ret_proj_kernel(a_ref, w_ref, cos_ref, sin_ref, o_ref):
    j = pl.program_id(1)
    a = a_ref[...].astype(BF16)
    acc = jnp.dot(a, w_ref[...], preferred_element_type=F32)

    @pl.when(j < 2)
    def _():
        c = cos_ref[...]
        s = sin_ref[...]
        half = RET_KEY_DIM // 2
        for h in range(RET_HEADS):
            x1 = acc[:, h * RET_KEY_DIM:h * RET_KEY_DIM + half]
            x2 = acc[:, h * RET_KEY_DIM + half:(h + 1) * RET_KEY_DIM]
            o_ref[:, h * RET_KEY_DIM:h * RET_KEY_DIM + half] = (x1 * c - x2 * s).astype(o_ref.dtype)
            o_ref[:, h * RET_KEY_DIM + half:(h + 1) * RET_KEY_DIM] = (x2 * c + x1 * s).astype(o_ref.dtype)

    @pl.when(j >= 2)
    def _():
        o_ref[...] = acc.astype(o_ref.dtype)


def _ret_proj(x2d, w, cos, sin):
    t = x2d.shape[0]
    n = w.shape[1]
    tm = min(TM_PROJ, t)
    tn = D_MODEL
    half = RET_KEY_DIM // 2
    return pl.pallas_call(
        _ret_proj_kernel,
        out_shape=jax.ShapeDtypeStruct((t, n), BF16),
        grid=(t // tm, n // tn),
        in_specs=[pl.BlockSpec((tm, D_MODEL), lambda i, j: (i, 0)),
                  pl.BlockSpec((D_MODEL, tn), lambda i, j: (0, j)),
                  pl.BlockSpec((tm, half), lambda i, j: (i, 0)),
                  pl.BlockSpec((tm, half), lambda i, j: (i, 0))],
        out_specs=pl.BlockSpec((tm, tn), lambda i, j: (i, j)),
        compiler_params=_cparams(("parallel", "arbitrary")),
        name="ret_proj",
    )(x2d, w, cos, sin)


def _retention_kernel(lg_ref, q_ref, k_ref, v_ref, g_ref, o_ref, state_ref, decay_ref, *, chunk):
    dk, dv = RET_KEY_DIM, RET_VAL_DIM

    @pl.when(pl.program_id(1) == 0)
    def _():
        state_ref[...] = jnp.zeros_like(state_ref)
        row = lax.broadcasted_iota(jnp.int32, (chunk, chunk), 0)
        col = lax.broadcasted_iota(jnp.int32, (chunk, chunk), 1)
        diff = (row - col).astype(F32)
        for h in range(RET_HEADS):
            decay_ref[h] = jnp.where(diff >= 0, jnp.exp(jnp.maximum(diff, 0.0) * lg_ref[h]), 0.0)

    idx = lax.broadcasted_iota(jnp.int32, (chunk, 1), 0).astype(F32)
    for h in range(RET_HEADS):
        lg = lg_ref[h]
        q = q_ref[:, h * dk:(h + 1) * dk]
        k = k_ref[:, h * dk:(h + 1) * dk]
        v = v_ref[:, h * dv:(h + 1) * dv]
        scores = lax.dot_general(q, k, (((1,), (1,)), ((), ())), preferred_element_type=F32) * decay_ref[h]
        o = jnp.dot(scores.astype(BF16), v, preferred_element_type=F32)
        state = state_ref[h]
        qd = (q.astype(F32) * jnp.exp((idx + 1.0) * lg)).astype(BF16)
        o = o + jnp.dot(qd, state.astype(BF16), preferred_element_type=F32)
        kd = (k.astype(F32) * jnp.exp((chunk - 1.0 - idx) * lg)).astype(BF16)
        state_ref[h] = state * jnp.exp(chunk * lg) + lax.dot_general(
            kd, v, (((0,), (0,)), ((), ())), preferred_element_type=F32)
        mu = jnp.mean(o, axis=1, keepdims=True)
        oc = o - mu
        var = jnp.mean(oc * oc, axis=1, keepdims=True)
        on = oc * lax.rsqrt(var + LN_EPS)
        g = g_ref[:, h * dv:(h + 1) * dv].astype(F32)
        o_ref[:, h * dv:(h + 1) * dv] = (g * jax.nn.sigmoid(g) * on).astype(o_ref.dtype)


def _retention(proj, log_gamma, batch, seq):
    chunk = min(RET_CHUNK, seq)
    nc = seq // chunk
    dq = RET_HEADS * RET_KEY_DIM
    dvt = RET_HEADS * RET_VAL_DIM
    grid_spec = pltpu.PrefetchScalarGridSpec(
        num_scalar_prefetch=1,
        grid=(batch, nc),
        in_specs=[pl.BlockSpec((chunk, dq), lambda b, c, lg: (b * nc + c, 0)),
                  pl.BlockSpec((chunk, dq), lambda b, c, lg: (b * nc + c, 1)),
                  pl.BlockSpec((chunk, dvt), lambda b, c, lg: (b * nc + c, 1)),
                  pl.BlockSpec((chunk, dvt), lambda b, c, lg: (b * nc + c, 2))],
        out_specs=pl.BlockSpec((chunk, dvt), lambda b, c, lg: (b * nc + c, 0)),
        scratch_shapes=[pltpu.VMEM((RET_HEADS, RET_KEY_DIM, RET_VAL_DIM), F32),
                        pltpu.VMEM((RET_HEADS, chunk, chunk), F32)],
    )
    return pl.pallas_call(
        functools.partial(_retention_kernel, chunk=chunk),
        out_shape=jax.ShapeDtypeStruct((batch * seq, dvt), BF16),
        grid_spec=grid_spec,
        compiler_params=_cparams(("parallel", "arbitrary")),
        name="retention",
    )(log_gamma, proj, proj, proj, proj)


def _layer_norm(y, g, b):
    mu = jnp.mean(y, axis=1, keepdims=True)
    yc = y - mu
    var = jnp.mean(yc * yc, axis=1, keepdims=True)
    return yc * lax.rsqrt(var + LN_EPS) * g + b


def _out_router_kernel(a_ref, w_ref, x_ref, g_ref, b_ref, wrhl_ref, wrh_ref, br_ref, o_ref, ext_ref):
    tm = a_ref.shape[0]
    h = jnp.dot(a_ref[...], w_ref[...], preferred_element_type=F32)
    x1 = _layer_norm(DEEPNORM_ALPHA * x_ref[...] + h, g_ref[...], b_ref[...])
    o_ref[:, :D_MODEL] = x1

    logits = _dot_split(x1, wrhl_ref[...], wrh_ref[...]) + br_ref[...]
    lane = lax.broadcasted_iota(jnp.int32, (tm, LANES), 1)

    def first_argmax(vals, vmax):
        return jnp.min(jnp.where(vals == vmax, lane, LANES), axis=1, keepdims=True)

    gl = jnp.where(lane < N_GROUPS, logits, NEG_BIG)
    gmax = jnp.max(gl, axis=1, keepdims=True)
    gsum = jnp.sum(jnp.where(lane < N_GROUPS, jnp.exp(gl - gmax), 0.0), axis=1, keepdims=True)
    g_val = 1.0 / gsum
    g_idx = first_argmax(gl, gmax)
    lo = CW_LANE0 + EXPERTS_PER_GROUP * g_idx
    in_group = (lane >= lo) & (lane < lo + EXPERTS_PER_GROUP)
    el = jnp.where(in_group, logits, NEG_BIG)
    e1 = jnp.max(el, axis=1, keepdims=True)
    i1 = first_argmax(el, e1)
    el2 = jnp.where(lane == i1, NEG_BIG, el)
    e2 = jnp.max(el2, axis=1, keepdims=True)
    i2 = first_argmax(el2, e2)
    r = jnp.exp(e2 - e1)
    w1 = g_val / (1.0 + r)
    w2 = g_val * r / (1.0 + r)
    ext = jnp.where(lane == i1, w1, 0.0) + jnp.where(lane == i2, w2, 0.0)
    ext = jnp.where(lane == GIDX_LANE, g_idx.astype(F32), ext)
    o_ref[:, D_MODEL:] = ext
    ext_ref[...] = ext


def _out_router(a, w_out, x2d, ln_g, ln_b, w_r, b_r):
    t, din = a.shape
    tm = min(TM_OUT, t)
    w_r_hilo, w_r_hi = _split_weight(w_r)
    return pl.pallas_call(
        _out_router_kernel,
        out_shape=(jax.ShapeDtypeStruct((t, ROW_W), F32), jax.ShapeDtypeStruct((t, EXT), F32)),
        grid=(t // tm,),
        in_specs=[pl.BlockSpec((tm, din), lambda i: (i, 0)),
                  pl.BlockSpec((din, D_MODEL), lambda i: (0, 0)),
                  pl.BlockSpec((tm, D_MODEL), lambda i: (i, 0)),
                  pl.BlockSpec((1, D_MODEL), lambda i: (0, 0)),
                  pl.BlockSpec((1, D_MODEL), lambda i: (0, 0)),
                  pl.BlockSpec((D_MODEL, 2 * LANES), lambda i: (0, 0)),
                  pl.BlockSpec((D_MODEL, LANES), lambda i: (0, 0)),
                  pl.BlockSpec((1, LANES), lambda i: (0, 0))],
        out_specs=(pl.BlockSpec((tm, ROW_W), lambda i: (i, 0)), pl.BlockSpec((tm, EXT), lambda i: (i, 0))),
        compiler_params=_cparams(("parallel",)),
        name="out_router",
    )(a, w_out, x2d, ln_g, ln_b, w_r_hilo, w_r_hi, b_r)


def _row_copy(src_ref, src_row, dst_ref, dst_row, sem):
    return pltpu.make_async_copy(src_ref.at[pl.ds(src_row, 1)], dst_ref.at[pl.ds(dst_row, 1)], sem)


def _dispatch_kernel(pos_ref, x_ref, init_ref, xs_ref, sem):
    del init_ref
    tm = x_ref.shape[0]
    base = pl.program_id(0) * tm

    def issue(r, _):
        _row_copy(x_ref, r, xs_ref, pos_ref[base + r], sem).start()
        return 0

    lax.fori_loop(0, tm, issue, 0, unroll=8)
    pltpu.make_async_copy(x_ref, xs_ref.at[pl.ds(0, tm)], sem).wait()


def _dispatch(pos, x1ext, n_rows):
    t = x1ext.shape[0]
    tm = min(TM_ROWS, t)
    grid_spec = pltpu.PrefetchScalarGridSpec(
        num_scalar_prefetch=1,
        grid=(t // tm,),
        in_specs=[pl.BlockSpec((tm, ROW_W), lambda i, pos: (i, 0)),
                  pl.BlockSpec(memory_space=pl.ANY)],
        out_specs=pl.BlockSpec(memory_space=pl.ANY),
        scratch_shapes=[pltpu.SemaphoreType.DMA(())],
    )
    return pl.pallas_call(
        _dispatch_kernel,
        out_shape=jax.ShapeDtypeStruct((n_rows, ROW_W), F32),
        grid_spec=grid_spec,
        input_output_aliases={2: 0},
        compiler_params=_cparams(("arbitrary",)),
        name="dispatch",
    )(pos, x1ext, jnp.zeros((n_rows, ROW_W), F32))


def _moe_kernel(tg_ref, xs_ref, wg_ref, wu_ref, wd_ref, g_ref, b_ref, o_ref, xb_ref, acc_ref):
    i = pl.program_id(0)
    e = pl.program_id(1)
    tm = xs_ref.shape[0]
    group = tg_ref[i]
    valid = group < N_GROUPS

    @pl.when(valid)
    def _():
        @pl.when(e == 0)
        def _():
            xb_ref[...] = xs_ref[:, :D_MODEL].astype(BF16)
            acc_ref[...] = jnp.zeros_like(acc_ref)

        lane = lax.broadcasted_iota(jnp.int32, (tm, EXT), 1)
        cw_lane = CW_LANE0 + EXPERTS_PER_GROUP * group + e
        cw = jnp.sum(jnp.where(lane == cw_lane, xs_ref[:, D_MODEL:], 0.0), axis=1, keepdims=True)
        xb = xb_ref[...]
        hg = jnp.dot(xb, wg_ref[...].astype(BF16), preferred_element_type=F32)
        hu = jnp.dot(xb, wu_ref[...].astype(BF16), preferred_element_type=F32)
        hh = (hg * jax.nn.sigmoid(hg) * hu * cw).astype(BF16)
        acc_ref[...] += jnp.dot(hh, wd_ref[...].astype(BF16), preferred_element_type=F32)

        @pl.when(e == EXPERTS_PER_GROUP - 1)
        def _():
            y = DEEPNORM_ALPHA * xs_ref[:, :D_MODEL] + acc_ref[...]
            o_ref[...] = _layer_norm(y, g_ref[...], b_ref[...])

    @pl.when(jnp.logical_not(valid) & (e == EXPERTS_PER_GROUP - 1))
    def _():
        o_ref[...] = jnp.zeros_like(o_ref)


def _moe(tile_group, xs, w_gate, w_up, w_down, ln_g, ln_b, layer):
    n_rows = xs.shape[0]
    tm = TM_MOE
    epg = EXPERTS_PER_GROUP

    def expert(i, e, tg):
        return (layer, jnp.where(tg[i] < N_GROUPS, tg[i] * epg + e, N_EXPERTS - 1), 0, 0)

    grid_spec = pltpu.PrefetchScalarGridSpec(
        num_scalar_prefetch=1,
        grid=(n_rows // tm, epg),
        in_specs=[pl.BlockSpec((tm, ROW_W), lambda i, e, tg: (i, 0)),
                  pl.BlockSpec((None, None, D_MODEL, D_EXPERT), expert),
                  pl.BlockSpec((None, None, D_MODEL, D_EXPERT), expert),
                  pl.BlockSpec((None, None, D_EXPERT, D_MODEL), expert),
                  pl.BlockSpec((1, D_MODEL), lambda i, e, tg: (0, 0)),
                  pl.BlockSpec((1, D_MODEL), lambda i, e, tg: (0, 0))],
        out_specs=pl.BlockSpec((tm, D_MODEL), lambda i, e, tg: (i, 0)),
        scratch_shapes=[pltpu.VMEM((tm, D_MODEL), BF16), pltpu.VMEM((tm, D_MODEL), F32)],
    )
    return pl.pallas_call(
        _moe_kernel,
        out_shape=jax.ShapeDtypeStruct((n_rows, D_MODEL), F32),
        grid_spec=grid_spec,
        compiler_params=_cparams(("parallel", "arbitrary")),
        name="moe",
    )(tile_group, xs, w_gate, w_up, w_down, ln_g, ln_b)


def _combine_ple_kernel(pos_ref, x2s_ref, p_ref, wg_ref, bg_ref, wp_ref, o_ref, buf_ref, sem):
    tm = o_ref.shape[0]
    i = pl.program_id(0)

    def gather(step, slot):
        base = step * tm

        def issue(r, _):
            _row_copy(x2s_ref, pos_ref[base + r], buf_ref.at[slot], r, sem.at[slot]).start()
            return 0

        lax.fori_loop(0, tm, issue, 0, unroll=8)

    @pl.when(i == 0)
    def _():
        gather(0, 0)

    @pl.when(i + 1 < pl.num_programs(0))
    def _():
        gather(i + 1, (i + 1) % 2)

    slot = i % 2
    pltpu.make_async_copy(x2s_ref.at[pl.ds(0, tm)], buf_ref.at[slot], sem.at[slot]).wait()
    x2 = buf_ref[slot]
    gate = jax.nn.sigmoid(jnp.dot(x2.astype(BF16), wg_ref[...], preferred_element_type=F32) + bg_ref[...])
    pp = jnp.dot(p_ref[...].astype(BF16), wp_ref[...], preferred_element_type=F32)
    o_ref[...] = x2 + gate * pp


def _combine_ple(pos, x2s, p2d, w_gate, b_gate, w_proj):
    t = p2d.shape[0]
    tm = min(TM_ROWS, t)
    grid_spec = pltpu.PrefetchScalarGridSpec(
        num_scalar_prefetch=1,
        grid=(t // tm,),
        in_specs=[pl.BlockSpec(memory_space=pl.ANY),
                  pl.BlockSpec((tm, PLE_DIM), lambda i, pos: (i, 0)),
                  pl.BlockSpec((D_MODEL, D_MODEL), lambda i, pos: (0, 0)),
                  pl.BlockSpec((1, D_MODEL), lambda i, pos: (0, 0)),
                  pl.BlockSpec((PLE_DIM, D_MODEL), lambda i, pos: (0, 0))],
        out_specs=pl.BlockSpec((tm, D_MODEL), lambda i, pos: (i, 0)),
        scratch_shapes=[pltpu.VMEM((2, tm, D_MODEL), F32), pltpu.SemaphoreType.DMA((2,))],
    )
    return pl.pallas_call(
        _combine_ple_kernel,
        out_shape=jax.ShapeDtypeStruct((t, D_MODEL), F32),
        grid_spec=grid_spec,
        compiler_params=_cparams(("arbitrary",)),
        name="combine_ple",
    )(pos, x2s, p2d, w_gate, b_gate, w_proj)


def _sorted_positions(g_idx, n_tiles):
    onehot = (g_idx[:, None] == jnp.arange(N_GROUPS, dtype=jnp.int32)[None, :]).astype(jnp.int32)
    counts = jnp.sum(onehot, axis=0)
    tiles = (counts + TM_MOE - 1) // TM_MOE
    tile_end = jnp.cumsum(tiles)
    tile_start = tile_end - tiles
    rank = jnp.sum((jnp.cumsum(onehot, axis=0) - onehot) * onehot, axis=1)
    pos = jnp.sum(onehot * (tile_start * TM_MOE)[None, :], axis=1) + rank
    tile_ids = jnp.arange(n_tiles, dtype=jnp.int32)
    tile_group = jnp.sum((tile_ids[:, None] >= tile_end[None, :]).astype(jnp.int32), axis=1)
    return pos.astype(jnp.int32), tile_group.astype(jnp.int32)


def _moe_block(a, w_out, x2d, p2d, ln1_g, ln1_b, ln2_g, ln2_b, w_group, b_group, w_router, b_router,
               w_gate, w_up, w_down, ple_w_proj, ple_w_gate, ple_b_gate, layer):
    t = x2d.shape[0]
    n_tiles = t // TM_MOE + N_GROUPS
    pad = LANES - CW_LANE0 - N_EXPERTS
    w_r = jnp.concatenate([w_group, w_router, jnp.zeros((D_MODEL, pad), F32)], axis=1)
    b_r = jnp.concatenate([b_group, b_router, jnp.zeros((pad,), F32)])[None, :]
    x1ext, ext = _out_router(a, w_out.astype(BF16), x2d, ln1_g[None, :], ln1_b[None, :], w_r, b_r)
    g_idx = ext[:, GIDX_LANE].astype(jnp.int32)
    pos, tile_group = _sorted_positions(g_idx, n_tiles)
    xs = _dispatch(pos, x1ext, n_tiles * TM_MOE)
    x2s = _moe(tile_group, xs, w_gate, w_up, w_down, ln2_g[None, :], ln2_b[None, :], layer)
    return _combine_ple(pos, x2s, p2d, ple_w_gate.astype(BF16), ple_b_gate[None, :],
                        ple_w_proj.astype(BF16))


@jax.jit
def _forward(x, p, positions, fox_w_in, fox_b_f, fox_w_out, ret_w_in, ret_w_out, ln1_g, ln1_b, ln2_g, ln2_b,
             moe_w_group, moe_b_group, moe_w_router, moe_b_router, moe_w_gate, moe_w_up, moe_w_down,
             ple_w_proj, ple_w_gate, ple_b_gate):
    batch, seq, d = x.shape
    t = batch * seq
    x2d = x.reshape(t, d)

    def moe_args(i):
        return (ln1_g[i], ln1_b[i], ln2_g[i], ln2_b[i], moe_w_group[i], moe_b_group[i], moe_w_router[i],
                moe_b_router[i], moe_w_gate, moe_w_up, moe_w_down, ple_w_proj[i], ple_w_gate[i],
                ple_b_gate[i], i)

    w_in = fox_w_in[0]
    scale = jnp.concatenate([jnp.full((D_MODEL,), FOX_HEAD_DIM ** -0.5 * LOG2E, F32),
                             jnp.ones((2 * D_MODEL,), F32)])
    w_qkv = (w_in[:, :3 * D_MODEL] * scale[None, :]).astype(BF16)
    qkv = _proj(x2d, w_qkv, D_MODEL)
    c = _fgate(x2d, w_in[:, 3 * D_MODEL:], fox_b_f[0][None, :], batch, seq)
    c3 = c.reshape(batch, seq, FOX_HEADS).transpose(0, 2, 1)
    attn = _fox_attention(qkv, c, c3[:, :, None, :], batch, seq)
    x2d = _moe_block(attn, fox_w_out[0], x2d, p[0].reshape(t, PLE_DIM), *moe_args(0))

    w_in = ret_w_in[0]
    scale = jnp.concatenate([jnp.ones((D_MODEL,), F32), jnp.full((D_MODEL,), RET_KEY_DIM ** -0.5, F32),
                             jnp.ones((4 * D_MODEL,), F32)])
    inv_freq = ROPE_BASE ** (-jnp.arange(0, RET_KEY_DIM, 2, dtype=F32) / RET_KEY_DIM)
    cos, sin = _rope_tables(positions.astype(F32).reshape(t, 1), inv_freq[None, :])
    proj = _ret_proj(x2d, (w_in * scale[None, :]).astype(BF16), cos, sin)
    log_gamma = jnp.log(1.0 - 2.0 ** (-5.0 - jnp.arange(RET_HEADS, dtype=F32)))
    ret = _retention(proj, log_gamma, batch, seq)
    x2d = _moe_block(ret, ret_w_out[0], x2d, p[1].reshape(t, PLE_DIM), *moe_args(1))
    return x2d.reshape(batch, seq, d)


def kernel(x, p, positions, fox_w_in, fox_b_f, fox_w_out, ret_w_in, ret_w_out, ln1_g, ln1_b, ln2_g, ln2_b,
           moe_w_group, moe_b_group, moe_w_router, moe_b_router, moe_w_gate, moe_w_up, moe_w_down,
           ple_w_proj, ple_w_gate, ple_b_gate):
    return _forward(x, p, positions, fox_w_in, fox_b_f, fox_w_out, ret_w_in, ret_w_out, ln1_g, ln1_b,
                    ln2_g, ln2_b, moe_w_group, moe_b_group, moe_w_router, moe_b_router, moe_w_gate,
                    moe_w_up, moe_w_down, ple_w_proj, ple_w_gate, ple_b_gate)
```

```python
import functools

import jax
import jax.numpy as jnp
from jax import lax
from jax.experimental import pallas as pl
from jax.experimental.pallas import tpu as pltpu

F32 = jnp.float32
BF16 = jnp.bfloat16

D_MODEL = 1024
PLE_DIM = 256
FOX_HEADS = 16
FOX_HEAD_DIM = D_MODEL // FOX_HEADS
RET_HEADS = 4
RET_KEY_DIM = D_MODEL // RET_HEADS
RET_VAL_DIM = 2 * D_MODEL // RET_HEADS
ROPE_BASE = 10000.0
N_GROUPS = 4
EXPERTS_PER_GROUP = 4
N_EXPERTS = N_GROUPS * EXPERTS_PER_GROUP
D_EXPERT = 512
DEPTH = 2
DEEPNORM_ALPHA = (2.0 * DEPTH) ** 0.25
LN_EPS = 1e-5

LANES = 128
EXT = LANES
ROW_W = D_MODEL + EXT
GIDX_LANE = 0
CW_LANE0 = 4
NEG_BIG = -1e30
LOG2E = 1.4426950408889634
VMEM_LIMIT = 56 * 1024 * 1024

TM_PROJ = 1024
TS_GATE = 512
TQ_ATTN = 512
ATTN_ROW_BLOCK = 128
TM_OUT = 1024
TM_ROWS = 1024
TM_MOE = 1024
RET_CHUNK = 256


def _cparams(sem):
    return pltpu.CompilerParams(dimension_semantics=sem, vmem_limit_bytes=VMEM_LIMIT)


def _proj_kernel(a_ref, w_ref, o_ref):
    a = a_ref[...].astype(BF16)
    o_ref[...] = jnp.dot(a, w_ref[...], preferred_element_type=F32).astype(o_ref.dtype)


def _proj(a, w, tn):
    m, k = a.shape
    n = w.shape[1]
    tm = min(TM_PROJ, m)
    return pl.pallas_call(
        _proj_kernel,
        out_shape=jax.ShapeDtypeStruct((m, n), BF16),
        grid=(m // tm, n // tn),
        in_specs=[pl.BlockSpec((tm, k), lambda i, j: (i, 0)),
                  pl.BlockSpec((k, tn), lambda i, j: (0, j))],
        out_specs=pl.BlockSpec((tm, tn), lambda i, j: (i, j)),
        compiler_params=_cparams(("parallel", "arbitrary")),
        name="proj",
    )(a, w)


def _split_bf16(x):
    hi = x.astype(BF16)
    return hi, (x - hi.astype(F32)).astype(BF16)


def _split_weight(w):
    hi, lo = _split_bf16(w)
    return jnp.concatenate([hi, lo], axis=1), hi


def _dot_split(x, w_hilo, w_hi):
    x_hi, x_lo = _split_bf16(x)
    r = jnp.dot(x_hi, w_hilo, preferred_element_type=F32)
    return r[:, :LANES] + r[:, LANES:] + jnp.dot(x_lo, w_hi, preferred_element_type=F32)


def _fgate_kernel(x_ref, whl_ref, wh_ref, b_ref, c_ref, carry_ref):
    @pl.when(pl.program_id(1) == 0)
    def _():
        carry_ref[...] = jnp.zeros_like(carry_ref)

    ts = x_ref.shape[0]
    z = _dot_split(x_ref[...], whl_ref[...], wh_ref[...]) + b_ref[...]
    logf = (jnp.minimum(z, 0.0) - jnp.log1p(jnp.exp(-jnp.abs(z)))) * LOG2E
    l1 = logf.astype(BF16)
    rest = logf - l1.astype(F32)
    l2, l3 = _split_bf16(rest)
    row = lax.broadcasted_iota(jnp.int32, (ts, ts), 0)
    col = lax.broadcasted_iota(jnp.int32, (ts, ts), 1)
    tri = jnp.where(row >= col, 1.0, 0.0).astype(BF16)
    parts = jnp.dot(tri, jnp.concatenate([l1, l2, l3], axis=1), preferred_element_type=F32)
    c = parts[:, :LANES] + parts[:, LANES:2 * LANES] + parts[:, 2 * LANES:] + carry_ref[...]
    c_ref[...] = c[:, :c_ref.shape[1]]
    carry_ref[...] = c[ts - 1:ts, :]


def _fgate(x2d, w_f, b_f, batch, seq):
    ts = min(TS_GATE, seq)
    ns = seq // ts
    h = w_f.shape[1]
    w_hilo, w_hi = _split_weight(jnp.pad(w_f, ((0, 0), (0, LANES - h))))
    b_pad = jnp.pad(b_f, ((0, 0), (0, LANES - h)))
    return pl.pallas_call(
        _fgate_kernel,
        out_shape=jax.ShapeDtypeStruct((batch * seq, h), F32),
        grid=(batch, ns),
        in_specs=[pl.BlockSpec((ts, D_MODEL), lambda b, s: (b * ns + s, 0)),
                  pl.BlockSpec((D_MODEL, 2 * LANES), lambda b, s: (0, 0)),
                  pl.BlockSpec((D_MODEL, LANES), lambda b, s: (0, 0)),
                  pl.BlockSpec((1, LANES), lambda b, s: (0, 0))],
        out_specs=pl.BlockSpec((ts, h), lambda b, s: (b * ns + s, 0)),
        scratch_shapes=[pltpu.VMEM((1, LANES), F32)],
        compiler_params=_cparams(("parallel", "arbitrary")),
        name="fgate",
    )(x2d, w_hilo, w_hi, b_pad)


def _fox_attn_kernel(q_ref, k_ref, v_ref, cq_ref, ck_ref, o_ref, m_ref, acc_ref, sa_ref, sb_ref, *, tq):
    i = pl.program_id(2)
    lane = lax.broadcasted_iota(jnp.int32, (tq, LANES), 1)
    row = lax.broadcasted_iota(jnp.int32, (tq, tq), 0)
    col = lax.broadcasted_iota(jnp.int32, (tq, tq), 1)
    causal = row >= col
    head0 = lane < FOX_HEAD_DIM
    m_ref[...] = jnp.full(m_ref.shape, NEG_BIG, F32)
    acc_ref[...] = jnp.zeros(acc_ref.shape, F32)
    lane_k = lax.broadcasted_iota(jnp.int32, (tq, LANES), 1)
    lane_h = lax.broadcasted_iota(jnp.int32, (tq, FOX_HEADS), 1)
    q2 = []
    cq_heads = []
    for tile in range(2):
        q = q_ref[tile * tq:(tile + 1) * tq, :]
        q2.append(jnp.concatenate([jnp.where(head0, q, jnp.zeros_like(q)),
                                   jnp.where(head0, jnp.zeros_like(q), q)], axis=0))
        c_tok = cq_ref[tile * tq:(tile + 1) * tq, :]
        cq_heads.append([jnp.sum(jnp.where(lane_h == 2 * pl.program_id(1) + h, c_tok, 0.0), axis=1, keepdims=True)
                         for h in range(2)])
    rb = ATTN_ROW_BLOCK

    def scores(tile, kb, s_ref):
        ks = pl.multiple_of(kb * tq, tq)
        s_ref[...] = lax.dot_general(q2[tile], k_ref[pl.ds(ks, tq), :], (((1,), (1,)), ((), ())),
                                     preferred_element_type=F32)

    def softmax_pv(tile, kb, s_ref, masked):
        ks = pl.multiple_of(kb * tq, tq)
        vblk = v_ref[pl.ds(ks, tq), :]
        for h in range(2):
            ck = ck_ref[h, :, pl.ds(ks, tq)]
            vh = jnp.where(lane_k < FOX_HEAD_DIM if h == 0 else lane_k >= FOX_HEAD_DIM,
                           vblk, jnp.ones_like(vblk))
            p_blocks = []
            m_blocks = []
            cq_all = cq_heads[tile][h]
            m_all = m_ref[tile, h]
            for r0 in range(0, tq, rb):
                t = s_ref[h * tq + r0:h * tq + r0 + rb, :] - ck
                if masked:
                    t = jnp.where(causal[r0:r0 + rb, :], t, NEG_BIG)
                cq = cq_all[r0:r0 + rb, :]
                m_new = jnp.maximum(m_all[r0:r0 + rb, :], cq + jnp.max(t, axis=1, keepdims=True))
                m_blocks.append(m_new)
                p_blocks.append(jnp.exp2(t + (cq - m_new)).astype(BF16))
            m_new_all = jnp.concatenate(m_blocks, axis=0)
            pv = jnp.dot(jnp.concatenate(p_blocks, axis=0), vh, preferred_element_type=F32)
            acc_ref[tile, h] = jnp.exp2(m_all - m_new_all) * acc_ref[tile, h] + pv
            m_ref[tile, h] = m_new_all

    scores(0, 0, sa_ref)

    def pair_a(j, _):
        scores(0, 2 * j + 1, sb_ref)
        softmax_pv(0, 2 * j, sa_ref, False)
        scores(0, 2 * j + 2, sa_ref)
        softmax_pv(0, 2 * j + 1, sb_ref, False)
        return 0

    lax.fori_loop(0, i, pair_a, 0)
    scores(1, 0, sb_ref)
    softmax_pv(0, 2 * i, sa_ref, True)
    scores(1, 1, sa_ref)
    softmax_pv(1, 0, sb_ref, False)

    def pair_b(j, _):
        scores(1, 2 * j + 2, sb_ref)
        softmax_pv(1, 2 * j + 1, sa_ref, False)
        scores(1, 2 * j + 3, sa_ref)
        softmax_pv(1, 2 * j + 2, sb_ref, False)
        return 0

    lax.fori_loop(0, i, pair_b, 0)
    softmax_pv(1, 2 * i + 1, sa_ref, True)

    for tile in range(2):
        a0 = acc_ref[tile, 0]
        a1 = acc_ref[tile, 1]
        o0 = a0 / pltpu.roll(a0, FOX_HEAD_DIM, 1)
        o1 = a1 / pltpu.roll(a1, FOX_HEAD_DIM, 1)
        o_ref[tile * tq:(tile + 1) * tq, :] = jnp.where(head0, o0, o1).astype(o_ref.dtype)


def _fox_attention(qkv, cq, ck, batch, seq):
    tq = min(TQ_ATTN, seq // 2)
    assert seq % (2 * tq) == 0, "a grid step covers two query tiles"
    nq2 = seq // (2 * tq)
    hp = FOX_HEADS // 2
    return pl.pallas_call(
        functools.partial(_fox_attn_kernel, tq=tq),
        out_shape=jax.ShapeDtypeStruct((batch * seq, D_MODEL), BF16),
        grid=(batch, hp, nq2),
        in_specs=[pl.BlockSpec((2 * tq, LANES), lambda b, p, i: (b * nq2 + i, p)),
                  pl.BlockSpec((seq, LANES), lambda b, p, i: (b, hp + p)),
                  pl.BlockSpec((seq, LANES), lambda b, p, i: (b, 2 * hp + p)),
                  pl.BlockSpec((2 * tq, FOX_HEADS), lambda b, p, i: (b * nq2 + i, 0)),
                  pl.BlockSpec((None, 2, 1, seq), lambda b, p, i: (b, p, 0, 0))],
        out_specs=pl.BlockSpec((2 * tq, LANES), lambda b, p, i: (b * nq2 + i, p)),
        scratch_shapes=[pltpu.VMEM((2, 2, tq, 1), F32), pltpu.VMEM((2, 2, tq, LANES), F32),
                        pltpu.VMEM((2 * tq, tq), F32), pltpu.VMEM((2 * tq, tq), F32)],
        compiler_params=_cparams(("parallel", "parallel", "arbitrary")),
        name="fox_attn",
    )(qkv, qkv, qkv, cq, ck)


def _rope_kernel(pos_ref, freq_ref, cos_ref, sin_ref):
    ang = pos_ref[...] * freq_ref[...]
    cos_ref[...] = jnp.cos(ang)
    sin_ref[...] = jnp.sin(ang)


def _rope_tables(pos_f, inv_freq):
    t = pos_f.shape[0]
    half = inv_freq.shape[1]
    tm = min(1024, t)
    return pl.pallas_call(
        _rope_kernel,
        out_shape=(jax.ShapeDtypeStruct((t, half), F32), jax.ShapeDtypeStruct((t, half), F32)),
        grid=(t // tm,),
        in_specs=[pl.BlockSpec((tm, 1), lambda i: (i, 0)),
                  pl.BlockSpec((1, half), lambda i: (0, 0))],
        out_specs=(pl.BlockSpec((tm, half), lambda i: (i, 0)),
                   pl.BlockSpec((tm, half), lambda i: (i, 0))),
        compiler_params=_cparams(("parallel",)),
        name="rope_tables",
    )(pos_f, inv_freq)


def _ret_proj_kernel(a_ref, w_ref, cos_ref, sin_ref, o_ref):
    j = pl.program_id(1)
    a = a_ref[...].astype(BF16)
    acc = jnp.dot(a, w_ref[...], preferred_element_type=F32)

    @pl.when(j < 2)
    def _():
        c = cos_ref[...]
        s = sin_ref[...]
        half = RET_KEY_DIM // 2
        for h in range(RET_HEADS):
            x1 = acc[:, h * RET_KEY_DIM:h * RET_KEY_DIM + half]
            x2 = acc[:, h * RET_KEY_DIM + half:(h + 1) * RET_KEY_DIM]
            o_ref[:, h * RET_KEY_DIM:h * RET_KEY_DIM + half] = (x1 * c - x2 * s).astype(o_ref.dtype)
            o_ref[:, h * RET_KEY_DIM + half:(h + 1) * RET_KEY_DIM] = (x2 * c + x1 * s).astype(o_ref.dtype)

    @pl.when(j >= 2)
    def _():
        o_ref[...] = acc.astype(o_ref.dtype)


def _ret_proj(x2d, w, cos, sin):
    t = x2d.shape[0]
    n = w.shape[1]
    tm = min(TM_PROJ, t)
    tn = D_MODEL
    half = RET_KEY_DIM // 2
    return pl.pallas_call(
        _ret_proj_kernel,
        out_shape=jax.ShapeDtypeStruct((t, n), BF16),
        grid=(t // tm, n // tn),
        in_specs=[pl.BlockSpec((tm, D_MODEL), lambda i, j: (i, 0)),
                  pl.BlockSpec((D_MODEL, tn), lambda i, j: (0, j)),
                  pl.BlockSpec((tm, half), lambda i, j: (i, 0)),
                  pl.BlockSpec((tm, half), lambda i, j: (i, 0))],
        out_specs=pl.BlockSpec((tm, tn), lambda i, j: (i, j)),
        compiler_params=_cparams(("parallel", "arbitrary")),
        name="ret_proj",
    )(x2d, w, cos, sin)


def _retention_kernel(lg_ref, q_ref, k_ref, v_ref, g_ref, o_ref, state_ref, decay_ref, *, chunk):
    dk, dv = RET_KEY_DIM, RET_VAL_DIM

    @pl.when(pl.program_id(1) == 0)
    def _():
        state_ref[...] = jnp.zeros_like(state_ref)
        row = lax.broadcasted_iota(jnp.int32, (chunk, chunk), 0)
        col = lax.broadcasted_iota(jnp.int32, (chunk, chunk), 1)
        diff = (row - col).astype(F32)
        for h in range(RET_HEADS):
            decay_ref[h] = jnp.where(diff >= 0, jnp.exp(jnp.maximum(diff, 0.0) * lg_ref[h]), 0.0)

    idx = lax.broadcasted_iota(jnp.int32, (chunk, 1), 0).astype(F32)
    for h in range(RET_HEADS):
        lg = lg_ref[h]
        q = q_ref[:, h * dk:(h + 1) * dk]
        k = k_ref[:, h * dk:(h + 1) * dk]
        v = v_ref[:, h * dv:(h + 1) * dv]
        scores = lax.dot_general(q, k, (((1,), (1,)), ((), ())), preferred_element_type=F32) * decay_ref[h]
        o = jnp.dot(scores.astype(BF16), v, preferred_element_type=F32)
        state = state_ref[h]
        qd = (q.astype(F32) * jnp.exp((idx + 1.0) * lg)).astype(BF16)
        o = o + jnp.dot(qd, state.astype(BF16), preferred_element_type=F32)
        kd = (k.astype(F32) * jnp.exp((chunk - 1.0 - idx) * lg)).astype(BF16)
        state_ref[h] = state * jnp.exp(chunk * lg) + lax.dot_general(
            kd, v, (((0,), (0,)), ((), ())), preferred_element_type=F32)
        mu = jnp.mean(o, axis=1, keepdims=True)
        oc = o - mu
        var = jnp.mean(oc * oc, axis=1, keepdims=True)
        on = oc * lax.rsqrt(var + LN_EPS)
        g = g_ref[:, h * dv:(h + 1) * dv].astype(F32)
        o_ref[:, h * dv:(h + 1) * dv] = (g * jax.nn.sigmoid(g) * on).astype(o_ref.dtype)


def _retention(proj, log_gamma, batch, seq):
    chunk = min(RET_CHUNK, seq)
    nc = seq // chunk
    dq = RET_HEADS * RET_KEY_DIM
    dvt = RET_HEADS * RET_VAL_DIM
    grid_spec = pltpu.PrefetchScalarGridSpec(
        num_scalar_prefetch=1,
        grid=(batch, nc),
        in_specs=[pl.BlockSpec((chunk, dq), lambda b, c, lg: (b * nc + c, 0)),
                  pl.BlockSpec((chunk, dq), lambda b, c, lg: (b * nc + c, 1)),
                  pl.BlockSpec((chunk, dvt), lambda b, c, lg: (b * nc + c, 1)),
                  pl.BlockSpec((chunk, dvt), lambda b, c, lg: (b * nc + c, 2))],
        out_specs=pl.BlockSpec((chunk, dvt), lambda b, c, lg: (b * nc + c, 0)),
        scratch_shapes=[pltpu.VMEM((RET_HEADS, RET_KEY_DIM, RET_VAL_DIM), F32),
                        pltpu.VMEM((RET_HEADS, chunk, chunk), F32)],
    )
    return pl.pallas_call(
        functools.partial(_retention_kernel, chunk=chunk),
        out_shape=jax.ShapeDtypeStruct((batch * seq, dvt), BF16),
        grid_spec=grid_spec,
        compiler_params=_cparams(("parallel", "arbitrary")),
        name="retention",
    )(log_gamma, proj, proj, proj, proj)


def _layer_norm(y, g, b):
    mu = jnp.mean(y, axis=1, keepdims=True)
    yc = y - mu
    var = jnp.mean(yc * yc, axis=1, keepdims=True)
    return yc * lax.rsqrt(var + LN_EPS) * g + b


def _out_router_kernel(a_ref, w_ref, x_ref, g_ref, b_ref, wrhl_ref, wrh_ref, br_ref, o_ref, ext_ref):
    tm = a_ref.shape[0]
    h = jnp.dot(a_ref[...], w_ref[...], preferred_element_type=F32)
    x1 = _layer_norm(DEEPNORM_ALPHA * x_ref[...] + h, g_ref[...], b_ref[...])
    o_ref[:, :D_MODEL] = x1

    logits = _dot_split(x1, wrhl_ref[...], wrh_ref[...]) + br_ref[...]
    lane = lax.broadcasted_iota(jnp.int32, (tm, LANES), 1)

    def first_argmax(vals, vmax):
        return jnp.min(jnp.where(vals == vmax, lane, LANES), axis=1, keepdims=True)

    gl = jnp.where(lane < N_GROUPS, logits, NEG_BIG)
    gmax = jnp.max(gl, axis=1, keepdims=True)
    gsum = jnp.sum(jnp.where(lane < N_GROUPS, jnp.exp(gl - gmax), 0.0), axis=1, keepdims=True)
    g_val = 1.0 / gsum
    g_idx = first_argmax(gl, gmax)
    lo = CW_LANE0 + EXPERTS_PER_GROUP * g_idx
    in_group = (lane >= lo) & (lane < lo + EXPERTS_PER_GROUP)
    el = jnp.where(in_group, logits, NEG_BIG)
    e1 = jnp.max(el, axis=1, keepdims=True)
    i1 = first_argmax(el, e1)
    el2 = jnp.where(lane == i1, NEG_BIG, el)
    e2 = jnp.max(el2, axis=1, keepdims=True)
    i2 = first_argmax(el2, e2)
    r = jnp.exp(e2 - e1)
    w1 = g_val / (1.0 + r)
    w2 = g_val * r / (1.0 + r)
    ext = jnp.where(lane == i1, w1, 0.0) + jnp.where(lane == i2, w2, 0.0)
    ext = jnp.where(lane == GIDX_LANE, g_idx.astype(F32), ext)
    o_ref[:, D_MODEL:] = ext
    ext_ref[...] = ext


def _out_router(a, w_out, x2d, ln_g, ln_b, w_r, b_r):
    t, din = a.shape
    tm = min(TM_OUT, t)
    w_r_hilo, w_r_hi = _split_weight(w_r)
    return pl.pallas_call(
        _out_router_kernel,
        out_shape=(jax.ShapeDtypeStruct((t, ROW_W), F32), jax.ShapeDtypeStruct((t, EXT), F32)),
        grid=(t // tm,),
        in_specs=[pl.BlockSpec((tm, din), lambda i: (i, 0)),
                  pl.BlockSpec((din, D_MODEL), lambda i: (0, 0)),
                  pl.BlockSpec((tm, D_MODEL), lambda i: (i, 0)),
                  pl.BlockSpec((1, D_MODEL), lambda i: (0, 0)),
                  pl.BlockSpec((1, D_MODEL), lambda i: (0, 0)),
                  pl.BlockSpec((D_MODEL, 2 * LANES), lambda i: (0, 0)),
                  pl.BlockSpec((D_MODEL, LANES), lambda i: (0, 0)),
                  pl.BlockSpec((1, LANES), lambda i: (0, 0))],
        out_specs=(pl.BlockSpec((tm, ROW_W), lambda i: (i, 0)), pl.BlockSpec((tm, EXT), lambda i: (i, 0))),
        compiler_params=_cparams(("parallel",)),
        name="out_router",
    )(a, w_out, x2d, ln_g, ln_b, w_r_hilo, w_r_hi, b_r)


def _row_copy(src_ref, src_row, dst_ref, dst_row, sem):
    return pltpu.make_async_copy(src_ref.at[pl.ds(src_row, 1)], dst_ref.at[pl.ds(dst_row, 1)], sem)


def _dispatch_kernel(pos_ref, x_ref, init_ref, xs_ref, sem):
    del init_ref
    tm = x_ref.shape[0]
    base = pl.program_id(0) * tm

    def issue(r, _):
        _row_copy(x_ref, r, xs_ref, pos_ref[base + r], sem).start()
        return 0

    lax.fori_loop(0, tm, issue, 0, unroll=8)
    pltpu.make_async_copy(x_ref, xs_ref.at[pl.ds(0, tm)], sem).wait()


def _dispatch(pos, x1ext, n_rows):
    t = x1ext.shape[0]
    tm = min(TM_ROWS, t)
    grid_spec = pltpu.PrefetchScalarGridSpec(
        num_scalar_prefetch=1,
        grid=(t // tm,),
        in_specs=[pl.BlockSpec((tm, ROW_W), lambda i, pos: (i, 0)),
                  pl.BlockSpec(memory_space=pl.ANY)],
        out_specs=pl.BlockSpec(memory_space=pl.ANY),
        scratch_shapes=[pltpu.SemaphoreType.DMA(())],
    )
    return pl.pallas_call(
        _dispatch_kernel,
        out_shape=jax.ShapeDtypeStruct((n_rows, ROW_W), F32),
        grid_spec=grid_spec,
        input_output_aliases={2: 0},
        compiler_params=_cparams(("arbitrary",)),
        name="dispatch",
    )(pos, x1ext, jnp.zeros((n_rows, ROW_W), F32))


def _moe_kernel(tg_ref, xs_ref, wg_ref, wu_ref, wd_ref, g_ref, b_ref, o_ref, xb_ref, acc_ref):
    i = pl.program_id(0)
    e = pl.program_id(1)
    tm = xs_ref.shape[0]
    group = tg_ref[i]
    valid = group < N_GROUPS

    @pl.when(valid)
    def _():
        @pl.when(e == 0)
        def _():
            xb_ref[...] = xs_ref[:, :D_MODEL].astype(BF16)
            acc_ref[...] = jnp.zeros_like(acc_ref)

        lane = lax.broadcasted_iota(jnp.int32, (tm, EXT), 1)
        cw_lane = CW_LANE0 + EXPERTS_PER_GROUP * group + e
        cw = jnp.sum(jnp.where(lane == cw_lane, xs_ref[:, D_MODEL:], 0.0), axis=1, keepdims=True)
        xb = xb_ref[...]
        hg = jnp.dot(xb, wg_ref[...].astype(BF16), preferred_element_type=F32)
        hu = jnp.dot(xb, wu_ref[...].astype(BF16), preferred_element_type=F32)
        hh = (hg * jax.nn.sigmoid(hg) * hu * cw).astype(BF16)
        acc_ref[...] += jnp.dot(hh, wd_ref[...].astype(BF16), preferred_element_type=F32)

        @pl.when(e == EXPERTS_PER_GROUP - 1)
        def _():
            y = DEEPNORM_ALPHA * xs_ref[:, :D_MODEL] + acc_ref[...]
            o_ref[...] = _layer_norm(y, g_ref[...], b_ref[...])

    @pl.when(jnp.logical_not(valid) & (e == EXPERTS_PER_GROUP - 1))
    def _():
        o_ref[...] = jnp.zeros_like(o_ref)


def _moe(tile_group, xs, w_gate, w_up, w_down, ln_g, ln_b, layer):
    n_rows = xs.shape[0]
    tm = TM_MOE
    epg = EXPERTS_PER_GROUP

    def expert(i, e, tg):
        return (layer, jnp.where(tg[i] < N_GROUPS, tg[i] * epg + e, N_EXPERTS - 1), 0, 0)

    grid_spec = pltpu.PrefetchScalarGridSpec(
        num_scalar_prefetch=1,
        grid=(n_rows // tm, epg),
        in_specs=[pl.BlockSpec((tm, ROW_W), lambda i, e, tg: (i, 0)),
                  pl.BlockSpec((None, None, D_MODEL, D_EXPERT), expert),
                  pl.BlockSpec((None, None, D_MODEL, D_EXPERT), expert),
                  pl.BlockSpec((None, None, D_EXPERT, D_MODEL), expert),
                  pl.BlockSpec((1, D_MODEL), lambda i, e, tg: (0, 0)),
                  pl.BlockSpec((1, D_MODEL), lambda i, e, tg: (0, 0))],
        out_specs=pl.BlockSpec((tm, D_MODEL), lambda i, e, tg: (i, 0)),
        scratch_shapes=[pltpu.VMEM((tm, D_MODEL), BF16), pltpu.VMEM((tm, D_MODEL), F32)],
    )
    return pl.pallas_call(
        _moe_kernel,
        out_shape=jax.ShapeDtypeStruct((n_rows, D_MODEL), F32),
        grid_spec=grid_spec,
        compiler_params=_cparams(("parallel", "arbitrary")),
        name="moe",
    )(tile_group, xs, w_gate, w_up, w_down, ln_g, ln_b)


def _combine_ple_kernel(pos_ref, x2s_ref, p_ref, wg_ref, bg_ref, wp_ref, o_ref, buf_ref, sem):
    tm = o_ref.shape[0]
    i = pl.program_id(0)

    def gather(step, slot):
        base = step * tm

        def issue(r, _):
            _row_copy(x2s_ref, pos_ref[base + r], buf_ref.at[slot], r, sem.at[slot]).start()
            return 0

        lax.fori_loop(0, tm, issue, 0, unroll=8)

    @pl.when(i == 0)
    def _():
        gather(0, 0)

    @pl.when(i + 1 < pl.num_programs(0))
    def _():
        gather(i + 1, (i + 1) % 2)

    slot = i % 2
    pltpu.make_async_copy(x2s_ref.at[pl.ds(0, tm)], buf_ref.at[slot], sem.at[slot]).wait()
    x2 = buf_ref[slot]
    gate = jax.nn.sigmoid(jnp.dot(x2.astype(BF16), wg_ref[...], preferred_element_type=F32) + bg_ref[...])
    pp = jnp.dot(p_ref[...].astype(BF16), wp_ref[...], preferred_element_type=F32)
    o_ref[...] = x2 + gate * pp


def _combine_ple(pos, x2s, p2d, w_gate, b_gate, w_proj):
    t = p2d.shape[0]
    tm = min(TM_ROWS, t)
    grid_spec = pltpu.PrefetchScalarGridSpec(
        num_scalar_prefetch=1,
        grid=(t // tm,),
        in_specs=[pl.BlockSpec(memory_space=pl.ANY),
                  pl.BlockSpec((tm, PLE_DIM), lambda i, pos: (i, 0)),
                  pl.BlockSpec((D_MODEL, D_MODEL), lambda i, pos: (0, 0)),
                  pl.BlockSpec((1, D_MODEL), lambda i, pos: (0, 0)),
                  pl.BlockSpec((PLE_DIM, D_MODEL), lambda i, pos: (0, 0))],
        out_specs=pl.BlockSpec((tm, D_MODEL), lambda i, pos: (i, 0)),
        scratch_shapes=[pltpu.VMEM((2, tm, D_MODEL), F32), pltpu.SemaphoreType.DMA((2,))],
    )
    return pl.pallas_call(
        _combine_ple_kernel,
        out_shape=jax.ShapeDtypeStruct((t, D_MODEL), F32),
        grid_spec=grid_spec,
        compiler_params=_cparams(("arbitrary",)),
        name="combine_ple",
    )(pos, x2s, p2d, w_gate, b_gate, w_proj)


def _sorted_positions(g_idx, n_tiles):
    onehot = (g_idx[:, None] == jnp.arange(N_GROUPS, dtype=jnp.int32)[None, :]).astype(jnp.int32)
    counts = jnp.sum(onehot, axis=0)
    tiles = (counts + TM_MOE - 1) // TM_MOE
    tile_end = jnp.cumsum(tiles)
    tile_start = tile_end - tiles
    rank = jnp.sum((jnp.cumsum(onehot, axis=0) - onehot) * onehot, axis=1)
    pos = jnp.sum(onehot * (tile_start * TM_MOE)[None, :], axis=1) + rank
    tile_ids = jnp.arange(n_tiles, dtype=jnp.int32)
    tile_group = jnp.sum((tile_ids[:, None] >= tile_end[None, :]).astype(jnp.int32), axis=1)
    return pos.astype(jnp.int32), tile_group.astype(jnp.int32)


def _moe_block(a, w_out, x2d, p2d, ln1_g, ln1_b, ln2_g, ln2_b, w_group, b_group, w_router, b_router,
               w_gate, w_up, w_down, ple_w_proj, ple_w_gate, ple_b_gate, layer):
    t = x2d.shape[0]
    n_tiles = t // TM_MOE + N_GROUPS
    pad = LANES - CW_LANE0 - N_EXPERTS
    w_r = jnp.concatenate([w_group, w_router, jnp.zeros((D_MODEL, pad), F32)], axis=1)
    b_r = jnp.concatenate([b_group, b_router, jnp.zeros((pad,), F32)])[None, :]
    x1ext, ext = _out_router(a, w_out.astype(BF16), x2d, ln1_g[None, :], ln1_b[None, :], w_r, b_r)
    g_idx = ext[:, GIDX_LANE].astype(jnp.int32)
    pos, tile_group = _sorted_positions(g_idx, n_tiles)
    xs = _dispatch(pos, x1ext, n_tiles * TM_MOE)
    x2s = _moe(tile_group, xs, w_gate, w_up, w_down, ln2_g[None, :], ln2_b[None, :], layer)
    return _combine_ple(pos, x2s, p2d, ple_w_gate.astype(BF16), ple_b_gate[None, :],
                        ple_w_proj.astype(BF16))


@jax.jit
def _forward(x, p, positions, fox_w_in, fox_b_f, fox_w_out, ret_w_in, ret_w_out, ln1_g, ln1_b, ln2_g, ln2_b,
             moe_w_group, moe_b_group, moe_w_router, moe_b_router, moe_w_gate, moe_w_up, moe_w_down,
             ple_w_proj, ple_w_gate, ple_b_gate):
    batch, seq, d = x.shape
    t = batch * seq
    x2d = x.reshape(t, d)

    def moe_args(i):
        return (ln1_g[i], ln1_b[i], ln2_g[i], ln2_b[i], moe_w_group[i], moe_b_group[i], moe_w_router[i],
                moe_b_router[i], moe_w_gate, moe_w_up, moe_w_down, ple_w_proj[i], ple_w_gate[i],
                ple_b_gate[i], i)

    w_in = fox_w_in[0]
    scale = jnp.concatenate([jnp.full((D_MODEL,), FOX_HEAD_DIM ** -0.5 * LOG2E, F32),
                             jnp.ones((2 * D_MODEL,), F32)])
    w_qkv = (w_in[:, :3 * D_MODEL] * scale[None, :]).astype(BF16)
    qkv = _proj(x2d, w_qkv, D_MODEL)
    c = _fgate(x2d, w_in[:, 3 * D_MODEL:], fox_b_f[0][None, :], batch, seq)
    c3 = c.reshape(batch, seq, FOX_HEADS).transpose(0, 2, 1)
    attn = _fox_attention(qkv, c, c3[:, :, None, :], batch, seq)
    x2d = _moe_block(attn, fox_w_out[0], x2d, p[0].reshape(t, PLE_DIM), *moe_args(0))

    w_in = ret_w_in[0]
    scale = jnp.concatenate([jnp.ones((D_MODEL,), F32), jnp.full((D_MODEL,), RET_KEY_DIM ** -0.5, F32),
                             jnp.ones((4 * D_MODEL,), F32)])
    inv_freq = ROPE_BASE ** (-jnp.arange(0, RET_KEY_DIM, 2, dtype=F32) / RET_KEY_DIM)
    cos, sin = _rope_tables(positions.astype(F32).reshape(t, 1), inv_freq[None, :])
    proj = _ret_proj(x2d, (w_in * scale[None, :]).astype(BF16), cos, sin)
    log_gamma = jnp.log(1.0 - 2.0 ** (-5.0 - jnp.arange(RET_HEADS, dtype=F32)))
    ret = _retention(proj, log_gamma, batch, seq)
    x2d = _moe_block(ret, ret_w_out[0], x2d, p[1].reshape(t, PLE_DIM), *moe_args(1))
    return x2d.reshape(batch, seq, d)


def kernel(x, p, positions, fox_w_in, fox_b_f, fox_w_out, ret_w_in, ret_w_out, ln1_g, ln1_b, ln2_g, ln2_b,
           moe_w_group, moe_b_group, moe_w_router, moe_b_router, moe_w_gate, moe_w_up, moe_w_down,
           ple_w_proj, ple_w_gate, ple_b_gate):
    return _forward(x, p, positions, fox_w_in, fox_b_f, fox_w_out, ret_w_in, ret_w_out, ln1_g, ln1_b,
                    ln2_g, ln2_b, moe_w_group, moe_b_group, moe_w_router, moe_b_router, moe_w_gate,
                    moe_w_up, moe_w_down, ple_w_proj, ple_w_gate, ple_b_gate)
```

```python
import functools

import jax
import jax.numpy as jnp
from jax import lax
from jax.experimental import pallas as pl
from jax.experimental.pallas import tpu as pltpu

F32 = jnp.float32
BF16 = jnp.bfloat16

D_MODEL = 1024
PLE_DIM = 256
FOX_HEADS = 16
FOX_HEAD_DIM = D_MODEL // FOX_HEADS
RET_HEADS = 4
RET_KEY_DIM = D_MODEL // RET_HEADS
RET_VAL_DIM = 2 * D_MODEL // RET_HEADS
ROPE_BASE = 10000.0
N_GROUPS = 4
EXPERTS_PER_GROUP = 4
N_EXPERTS = N_GROUPS * EXPERTS_PER_GROUP
D_EXPERT = 512
DEPTH = 2
DEEPNORM_ALPHA = (2.0 * DEPTH) ** 0.25
LN_EPS = 1e-5

LANES = 128
EXT = LANES
ROW_W = D_MODEL + EXT
GIDX_LANE = 0
CW_LANE0 = 4
NEG_BIG = -1e30
LOG2E = 1.4426950408889634
VMEM_LIMIT = 56 * 1024 * 1024

TM_PROJ = 1024
TS_GATE = 512
TQ_ATTN = 512
ATTN_TILES_PER_STEP = 2
ATTN_ROW_BLOCK = 128
TM_OUT = 1024
TM_ROWS = 1024
TM_MOE = 1024
RET_CHUNK = 256


def _cparams(sem):
    return pltpu.CompilerParams(dimension_semantics=sem, vmem_limit_bytes=VMEM_LIMIT)


def _proj_kernel(a_ref, w_ref, o_ref):
    a = a_ref[...].astype(BF16)
    o_ref[...] = jnp.dot(a, w_ref[...], preferred_element_type=F32).astype(o_ref.dtype)


def _proj(a, w, tn):
    m, k = a.shape
    n = w.shape[1]
    tm = min(TM_PROJ, m)
    return pl.pallas_call(
        _proj_kernel,
        out_shape=jax.ShapeDtypeStruct((m, n), BF16),
        grid=(m // tm, n // tn),
        in_specs=[pl.BlockSpec((tm, k), lambda i, j: (i, 0)),
                  pl.BlockSpec((k, tn), lambda i, j: (0, j))],
        out_specs=pl.BlockSpec((tm, tn), lambda i, j: (i, j)),
        compiler_params=_cparams(("parallel", "arbitrary")),
        name="proj",
    )(a, w)


def _split_bf16(x):
    hi = x.astype(BF16)
    return hi, (x - hi.astype(F32)).astype(BF16)


def _split_weight(w):
    hi, lo = _split_bf16(w)
    return jnp.concatenate([hi, lo], axis=1), hi


def _dot_split(x, w_hilo, w_hi):
    x_hi, x_lo = _split_bf16(x)
    r = jnp.dot(x_hi, w_hilo, preferred_element_type=F32)
    return r[:, :LANES] + r[:, LANES:] + jnp.dot(x_lo, w_hi, preferred_element_type=F32)


def _fgate_kernel(x_ref, whl_ref, wh_ref, b_ref, c_ref, carry_ref):
    @pl.when(pl.program_id(1) == 0)
    def _():
        carry_ref[...] = jnp.zeros_like(carry_ref)

    ts = x_ref.shape[0]
    z = _dot_split(x_ref[...], whl_ref[...], wh_ref[...]) + b_ref[...]
    logf = (jnp.minimum(z, 0.0) - jnp.log1p(jnp.exp(-jnp.abs(z)))) * LOG2E
    l1 = logf.astype(BF16)
    rest = logf - l1.astype(F32)
    l2, l3 = _split_bf16(rest)
    row = lax.broadcasted_iota(jnp.int32, (ts, ts), 0)
    col = lax.broadcasted_iota(jnp.int32, (ts, ts), 1)
    tri = jnp.where(row >= col, 1.0, 0.0).astype(BF16)
    parts = jnp.dot(tri, jnp.concatenate([l1, l2, l3], axis=1), preferred_element_type=F32)
    c = parts[:, :LANES] + parts[:, LANES:2 * LANES] + parts[:, 2 * LANES:] + carry_ref[...]
    c_ref[...] = c[:, :c_ref.shape[1]]
    carry_ref[...] = c[ts - 1:ts, :]


def _fgate(x2d, w_f, b_f, batch, seq):
    ts = min(TS_GATE, seq)
    ns = seq // ts
    h = w_f.shape[1]
    w_hilo, w_hi = _split_weight(jnp.pad(w_f, ((0, 0), (0, LANES - h))))
    b_pad = jnp.pad(b_f, ((0, 0), (0, LANES - h)))
    return pl.pallas_call(
        _fgate_kernel,
        out_shape=jax.ShapeDtypeStruct((batch * seq, h), F32),
        grid=(batch, ns),
        in_specs=[pl.BlockSpec((ts, D_MODEL), lambda b, s: (b * ns + s, 0)),
                  pl.BlockSpec((D_MODEL, 2 * LANES), lambda b, s: (0, 0)),
                  pl.BlockSpec((D_MODEL, LANES), lambda b, s: (0, 0)),
                  pl.BlockSpec((1, LANES), lambda b, s: (0, 0))],
        out_specs=pl.BlockSpec((ts, h), lambda b, s: (b * ns + s, 0)),
        scratch_shapes=[pltpu.VMEM((1, LANES), F32)],
        compiler_params=_cparams(("parallel", "arbitrary")),
        name="fgate",
    )(x2d, w_hilo, w_hi, b_pad)


def _fox_attn_kernel(q_ref, k_ref, v_ref, cq_ref, ck_ref, o_ref, m_ref, acc_ref, sa_ref, sb_ref, *, tq, nt):
    i = pl.program_id(2)
    lane = lax.broadcasted_iota(jnp.int32, (tq, LANES), 1)
    row = lax.broadcasted_iota(jnp.int32, (tq, tq), 0)
    col = lax.broadcasted_iota(jnp.int32, (tq, tq), 1)
    causal = row >= col
    head0 = lane < FOX_HEAD_DIM
    m_ref[...] = jnp.full(m_ref.shape, NEG_BIG, F32)
    acc_ref[...] = jnp.zeros(acc_ref.shape, F32)
    lane_k = lax.broadcasted_iota(jnp.int32, (tq, LANES), 1)
    lane_h = lax.broadcasted_iota(jnp.int32, (tq, FOX_HEADS), 1)
    q2 = []
    cq_heads = []
    for tile in range(nt):
        q = q_ref[tile * tq:(tile + 1) * tq, :]
        q2.append(jnp.concatenate([jnp.where(head0, q, jnp.zeros_like(q)),
                                   jnp.where(head0, jnp.zeros_like(q), q)], axis=0))
        c_tok = cq_ref[tile * tq:(tile + 1) * tq, :]
        cq_heads.append([jnp.sum(jnp.where(lane_h == 2 * pl.program_id(1) + h, c_tok, 0.0), axis=1, keepdims=True)
                         for h in range(2)])
    rb = ATTN_ROW_BLOCK

    def scores(tile, kb, s_ref):
        ks = pl.multiple_of(kb * tq, tq)
        s_ref[...] = lax.dot_general(q2[tile], k_ref[pl.ds(ks, tq), :], (((1,), (1,)), ((), ())),
                                     preferred_element_type=F32)

    def softmax_pv(tile, kb, s_ref, masked):
        ks = pl.multiple_of(kb * tq, tq)
        vblk = v_ref[pl.ds(ks, tq), :]
        for h in range(2):
            ck = ck_ref[h, :, pl.ds(ks, tq)]
            vh = jnp.where(lane_k < FOX_HEAD_DIM if h == 0 else lane_k >= FOX_HEAD_DIM,
                           vblk, jnp.ones_like(vblk))
            p_blocks = []
            m_blocks = []
            cq_all = cq_heads[tile][h]
            m_all = m_ref[tile, h]
            for r0 in range(0, tq, rb):
                t = s_ref[h * tq + r0:h * tq + r0 + rb, :] - ck
                if masked:
                    t = jnp.where(causal[r0:r0 + rb, :], t, NEG_BIG)
                cq = cq_all[r0:r0 + rb, :]
                m_new = jnp.maximum(m_all[r0:r0 + rb, :], cq + jnp.max(t, axis=1, keepdims=True))
                m_blocks.append(m_new)
                p_blocks.append(jnp.exp2(t + (cq - m_new)).astype(BF16))
            m_new_all = jnp.concatenate(m_blocks, axis=0)
            pv = jnp.dot(jnp.concatenate(p_blocks, axis=0), vh, preferred_element_type=F32)
            acc_ref[tile, h] = jnp.exp2(m_all - m_new_all) * acc_ref[tile, h] + pv
            m_ref[tile, h] = m_new_all

    bufs = (sa_ref, sb_ref)
    scores(0, 0, bufs[0])
    par = 0
    for u in range(nt):
        g = nt * i + u
        lo = u % 2
        if lo:
            scores(u, 1, bufs[1 - par])
            softmax_pv(u, 0, bufs[par], False)
            par = 1 - par
        cur, oth = bufs[par], bufs[1 - par]

        def pair(j, _, u=u, lo=lo, cur=cur, oth=oth):
            kb = lo + 2 * j
            scores(u, kb + 1, oth)
            softmax_pv(u, kb, cur, False)
            scores(u, kb + 2, cur)
            softmax_pv(u, kb + 1, oth, False)
            return 0

        lax.fori_loop(0, (g - lo) // 2, pair, 0)
        if u + 1 < nt:
            scores(u + 1, 0, oth)
        softmax_pv(u, g, cur, True)
        par = 1 - par

    for tile in range(nt):
        a0 = acc_ref[tile, 0]
        a1 = acc_ref[tile, 1]
        o0 = a0 / pltpu.roll(a0, FOX_HEAD_DIM, 1)
        o1 = a1 / pltpu.roll(a1, FOX_HEAD_DIM, 1)
        o_ref[tile * tq:(tile + 1) * tq, :] = jnp.where(head0, o0, o1).astype(o_ref.dtype)


def _fox_attention(qkv, cq, ck, batch, seq):
    nt = min(ATTN_TILES_PER_STEP, seq // min(TQ_ATTN, seq // 2))
    tq = min(TQ_ATTN, seq // nt)
    assert nt % 2 == 0 and seq % (nt * tq) == 0, "a grid step covers an even number of query tiles"
    ns = seq // (nt * tq)
    hp = FOX_HEADS // 2
    return pl.pallas_call(
        functools.partial(_fox_attn_kernel, tq=tq, nt=nt),
        out_shape=jax.ShapeDtypeStruct((batch * seq, D_MODEL), BF16),
        grid=(batch, hp, ns),
        in_specs=[pl.BlockSpec((nt * tq, LANES), lambda b, p, i: (b * ns + i, p)),
                  pl.BlockSpec((seq, LANES), lambda b, p, i: (b, hp + p)),
                  pl.BlockSpec((seq, LANES), lambda b, p, i: (b, 2 * hp + p)),
                  pl.BlockSpec((nt * tq, FOX_HEADS), lambda b, p, i: (b * ns + i, 0)),
                  pl.BlockSpec((None, 2, 1, seq), lambda b, p, i: (b, p, 0, 0))],
        out_specs=pl.BlockSpec((nt * tq, LANES), lambda b, p, i: (b * ns + i, p)),
        scratch_shapes=[pltpu.VMEM((nt, 2, tq, 1), F32), pltpu.VMEM((nt, 2, tq, LANES), F32),
                        pltpu.VMEM((2 * tq, tq), F32), pltpu.VMEM((2 * tq, tq), F32)],
        compiler_params=_cparams(("parallel", "parallel", "arbitrary")),
        name="fox_attn",
    )(qkv, qkv, qkv, cq, ck)


def _rope_kernel(pos_ref, freq_ref, cos_ref, sin_ref):
    ang = pos_ref[...] * freq_ref[...]
    cos_ref[...] = jnp.cos(ang)
    sin_ref[...] = jnp.sin(ang)


def _rope_tables(pos_f, inv_freq):
    t = pos_f.shape[0]
    half = inv_freq.shape[1]
    tm = min(1024, t)
    return pl.pallas_call(
        _rope_kernel,
        out_shape=(jax.ShapeDtypeStruct((t, half), F32), jax.ShapeDtypeStruct((t, half), F32)),
        grid=(t // tm,),
        in_specs=[pl.BlockSpec((tm, 1), lambda i: (i, 0)),
                  pl.BlockSpec((1, half), lambda i: (0, 0))],
        out_specs=(pl.BlockSpec((tm, half), lambda i: (i, 0)),
                   pl.BlockSpec((tm, half), lambda i: (i, 0))),
        compiler_params=_cparams(("parallel",)),
        name="rope_tables",
    )(pos_f, inv_freq)


def _ret_proj_kernel(a_ref, w_ref, cos_ref, sin_ref, o_ref):
    j = pl.program_id(1)
    a = a_ref[...].astype(BF16)
    acc = jnp.dot(a, w_ref[...], preferred_element_type=F32)

    @pl.when(j < 2)
    def _():
        c = cos_ref[...]
        s = sin_ref[...]
        half = RET_KEY_DIM // 2
        for h in range(RET_HEADS):
            x1 = acc[:, h * RET_KEY_DIM:h * RET_KEY_DIM + half]
            x2 = acc[:, h * RET_KEY_DIM + half:(h + 1) * RET_KEY_DIM]
            o_ref[:, h * RET_KEY_DIM:h * RET_KEY_DIM + half] = (x1 * c - x2 * s).astype(o_ref.dtype)
            o_ref[:, h * RET_KEY_DIM + half:(h + 1) * RET_KEY_DIM] = (x2 * c + x1 * s).astype(o_ref.dtype)

    @pl.when(j >= 2)
    def _():
        o_ref[...] = acc.astype(o_ref.dtype)


def _ret_proj(x2d, w, cos, sin):
    t = x2d.shape[0]
    n = w.shape[1]
    tm = min(TM_PROJ, t)
    tn = D_MODEL
    half = RET_KEY_DIM // 2
    return pl.pallas_call(
        _ret_proj_kernel,
        out_shape=jax.ShapeDtypeStruct((t, n), BF16),
        grid=(t // tm, n // tn),
        in_specs=[pl.BlockSpec((tm, D_MODEL), lambda i, j: (i, 0)),
                  pl.BlockSpec((D_MODEL, tn), lambda i, j: (0, j)),
                  pl.BlockSpec((tm, half), lambda i, j: (i, 0)),
                  pl.BlockSpec((tm, half), lambda i, j: (i, 0))],
        out_specs=pl.BlockSpec((tm, tn), lambda i, j: (i, j)),
        compiler_params=_cparams(("parallel", "arbitrary")),
        name="ret_proj",
    )(x2d, w, cos, sin)


def _retention_kernel(lg_ref, q_ref, k_ref, v_ref, g_ref, o_ref, state_ref, decay_ref, *, chunk):
    dk, dv = RET_KEY_DIM, RET_VAL_DIM

    @pl.when(pl.program_id(1) == 0)
    def _():
        state_ref[...] = jnp.zeros_like(state_ref)
        row = lax.broadcasted_iota(jnp.int32, (chunk, chunk), 0)
        col = lax.broadcasted_iota(jnp.int32, (chunk, chunk), 1)
        diff = (row - col).astype(F32)
        for h in range(RET_HEADS):
            decay_ref[h] = jnp.where(diff >= 0, jnp.exp(jnp.maximum(diff, 0.0) * lg_ref[h]), 0.0)

    idx = lax.broadcasted_iota(jnp.int32, (chunk, 1), 0).astype(F32)
    for h in range(RET_HEADS):
        lg = lg_ref[h]
        q = q_ref[:, h * dk:(h + 1) * dk]
        k = k_ref[:, h * dk:(h + 1) * dk]
        v = v_ref[:, h * dv:(h + 1) * dv]
        scores = lax.dot_general(q, k, (((1,), (1,)), ((), ())), preferred_element_type=F32) * decay_ref[h]
        o = jnp.dot(scores.astype(BF16), v, preferred_element_type=F32)
        state = state_ref[h]
        qd = (q.astype(F32) * jnp.exp((idx + 1.0) * lg)).astype(BF16)
        o = o + jnp.dot(qd, state.astype(BF16), preferred_element_type=F32)
        kd = (k.astype(F32) * jnp.exp((chunk - 1.0 - idx) * lg)).astype(BF16)
        state_ref[h] = state * jnp.exp(chunk * lg) + lax.dot_general(
            kd, v, (((0,), (0,)), ((), ())), preferred_element_type=F32)
        mu = jnp.mean(o, axis=1, keepdims=True)
        oc = o - mu
        var = jnp.mean(oc * oc, axis=1, keepdims=True)
        on = oc * lax.rsqrt(var + LN_EPS)
        g = g_ref[:, h * dv:(h + 1) * dv].astype(F32)
        o_ref[:, h * dv:(h + 1) * dv] = (g * jax.nn.sigmoid(g) * on).astype(o_ref.dtype)


def _retention(proj, log_gamma, batch, seq):
    chunk = min(RET_CHUNK, seq)
    nc = seq // chunk
    dq = RET_HEADS * RET_KEY_DIM
    dvt = RET_HEADS * RET_VAL_DIM
    grid_spec = pltpu.PrefetchScalarGridSpec(
        num_scalar_prefetch=1,
        grid=(batch, nc),
        in_specs=[pl.BlockSpec((chunk, dq), lambda b, c, lg: (b * nc + c, 0)),
                  pl.BlockSpec((chunk, dq), lambda b, c, lg: (b * nc + c, 1)),
                  pl.BlockSpec((chunk, dvt), lambda b, c, lg: (b * nc + c, 1)),
                  pl.BlockSpec((chunk, dvt), lambda b, c, lg: (b * nc + c, 2))],
        out_specs=pl.BlockSpec((chunk, dvt), lambda b, c, lg: (b * nc + c, 0)),
        scratch_shapes=[pltpu.VMEM((RET_HEADS, RET_KEY_DIM, RET_VAL_DIM), F32),
                        pltpu.VMEM((RET_HEADS, chunk, chunk), F32)],
    )
    return pl.pallas_call(
        functools.partial(_retention_kernel, chunk=chunk),
        out_shape=jax.ShapeDtypeStruct((batch * seq, dvt), BF16),
        grid_spec=grid_spec,
        compiler_params=_cparams(("parallel", "arbitrary")),
        name="retention",
    )(log_gamma, proj, proj, proj, proj)


def _layer_norm(y, g, b):
    mu = jnp.mean(y, axis=1, keepdims=True)
    yc = y - mu
    var = jnp.mean(yc * yc, axis=1, keepdims=True)
    return yc * lax.rsqrt(var + LN_EPS) * g + b


def _out_router_kernel(a_ref, w_ref, x_ref, g_ref, b_ref, wrhl_ref, wrh_ref, br_ref, o_ref, ext_ref):
    tm = a_ref.shape[0]
    h = jnp.dot(a_ref[...], w_ref[...], preferred_element_type=F32)
    x1 = _layer_norm(DEEPNORM_ALPHA * x_ref[...] + h, g_ref[...], b_ref[...])
    o_ref[:, :D_MODEL] = x1

    logits = _dot_split(x1, wrhl_ref[...], wrh_ref[...]) + br_ref[...]
    lane = lax.broadcasted_iota(jnp.int32, (tm, LANES), 1)

    def first_argmax(vals, vmax):
        return jnp.min(jnp.where(vals == vmax, lane, LANES), axis=1, keepdims=True)

    gl = jnp.where(lane < N_GROUPS, logits, NEG_BIG)
    gmax = jnp.max(gl, axis=1, keepdims=True)
    gsum = jnp.sum(jnp.where(lane < N_GROUPS, jnp.exp(gl - gmax), 0.0), axis=1, keepdims=True)
    g_val = 1.0 / gsum
    g_idx = first_argmax(gl, gmax)
    lo = CW_LANE0 + EXPERTS_PER_GROUP * g_idx
    in_group = (lane >= lo) & (lane < lo + EXPERTS_PER_GROUP)
    el = jnp.where(in_group, logits, NEG_BIG)
    e1 = jnp.max(el, axis=1, keepdims=True)
    i1 = first_argmax(el, e1)
    el2 = jnp.where(lane == i1, NEG_BIG, el)
    e2 = jnp.max(el2, axis=1, keepdims=True)
    i2 = first_argmax(el2, e2)
    r = jnp.exp(e2 - e1)
    w1 = g_val / (1.0 + r)
    w2 = g_val * r / (1.0 + r)
    ext = jnp.where(lane == i1, w1, 0.0) + jnp.where(lane == i2, w2, 0.0)
    ext = jnp.where(lane == GIDX_LANE, g_idx.astype(F32), ext)
    o_ref[:, D_MODEL:] = ext
    ext_ref[...] = ext


def _out_router(a, w_out, x2d, ln_g, ln_b, w_r, b_r):
    t, din = a.shape
    tm = min(TM_OUT, t)
    w_r_hilo, w_r_hi = _split_weight(w_r)
    return pl.pallas_call(
        _out_router_kernel,
        out_shape=(jax.ShapeDtypeStruct((t, ROW_W), F32), jax.ShapeDtypeStruct((t, EXT), F32)),
        grid=(t // tm,),
        in_specs=[pl.BlockSpec((tm, din), lambda i: (i, 0)),
                  pl.BlockSpec((din, D_MODEL), lambda i: (0, 0)),
                  pl.BlockSpec((tm, D_MODEL), lambda i: (i, 0)),
                  pl.BlockSpec((1, D_MODEL), lambda i: (0, 0)),
                  pl.BlockSpec((1, D_MODEL), lambda i: (0, 0)),
                  pl.BlockSpec((D_MODEL, 2 * LANES), lambda i: (0, 0)),
                  pl.BlockSpec((D_MODEL, LANES), lambda i: (0, 0)),
                  pl.BlockSpec((1, LANES), lambda i: (0, 0))],
        out_specs=(pl.BlockSpec((tm, ROW_W), lambda i: (i, 0)), pl.BlockSpec((tm, EXT), lambda i: (i, 0))),
        compiler_params=_cparams(("parallel",)),
        name="out_router",
    )(a, w_out, x2d, ln_g, ln_b, w_r_hilo, w_r_hi, b_r)


def _row_copy(src_ref, src_row, dst_ref, dst_row, sem):
    return pltpu.make_async_copy(src_ref.at[pl.ds(src_row, 1)], dst_ref.at[pl.ds(dst_row, 1)], sem)


def _dispatch_kernel(pos_ref, x_ref, init_ref, xs_ref, sem):
    del init_ref
    tm = x_ref.shape[0]
    base = pl.program_id(0) * tm

    def issue(r, _):
        _row_copy(x_ref, r, xs_ref, pos_ref[base + r], sem).start()
        return 0

    lax.fori_loop(0, tm, issue, 0, unroll=8)
    pltpu.make_async_copy(x_ref, xs_ref.at[pl.ds(0, tm)], sem).wait()


def _dispatch(pos, x1ext, n_rows):
    t = x1ext.shape[0]
    tm = min(TM_ROWS, t)
    grid_spec = pltpu.PrefetchScalarGridSpec(
        num_scalar_prefetch=1,
        grid=(t // tm,),
        in_specs=[pl.BlockSpec((tm, ROW_W), lambda i, pos: (i, 0)),
                  pl.BlockSpec(memory_space=pl.ANY)],
        out_specs=pl.BlockSpec(memory_space=pl.ANY),
        scratch_shapes=[pltpu.SemaphoreType.DMA(())],
    )
    return pl.pallas_call(
        _dispatch_kernel,
        out_shape=jax.ShapeDtypeStruct((n_rows, ROW_W), F32),
        grid_spec=grid_spec,
        input_output_aliases={2: 0},
        compiler_params=_cparams(("arbitrary",)),
        name="dispatch",
    )(pos, x1ext, jnp.zeros((n_rows, ROW_W), F32))


def _moe_kernel(tg_ref, xs_ref, wg_ref, wu_ref, wd_ref, g_ref, b_ref, o_ref, xb_ref, acc_ref):
    i = pl.program_id(0)
    e = pl.program_id(1)
    tm = xs_ref.shape[0]
    group = tg_ref[i]
    valid = group < N_GROUPS

    @pl.when(valid)
    def _():
        @pl.when(e == 0)
        def _():
            xb_ref[...] = xs_ref[:, :D_MODEL].astype(BF16)
            acc_ref[...] = jnp.zeros_like(acc_ref)

        lane = lax.broadcasted_iota(jnp.int32, (tm, EXT), 1)
        cw_lane = CW_LANE0 + EXPERTS_PER_GROUP * group + e
        cw = jnp.sum(jnp.where(lane == cw_lane, xs_ref[:, D_MODEL:], 0.0), axis=1, keepdims=True)
        xb = xb_ref[...]
        hg = jnp.dot(xb, wg_ref[...].astype(BF16), preferred_element_type=F32)
        hu = jnp.dot(xb, wu_ref[...].astype(BF16), preferred_element_type=F32)
        hh = (hg * jax.nn.sigmoid(hg) * hu * cw).astype(BF16)
        acc_ref[...] += jnp.dot(hh, wd_ref[...].astype(BF16), preferred_element_type=F32)

        @pl.when(e == EXPERTS_PER_GROUP - 1)
        def _():
            y = DEEPNORM_ALPHA * xs_ref[:, :D_MODEL] + acc_ref[...]
            o_ref[...] = _layer_norm(y, g_ref[...], b_ref[...])

    @pl.when(jnp.logical_not(valid) & (e == EXPERTS_PER_GROUP - 1))
    def _():
        o_ref[...] = jnp.zeros_like(o_ref)


def _moe(tile_group, xs, w_gate, w_up, w_down, ln_g, ln_b, layer):
    n_rows = xs.shape[0]
    tm = TM_MOE
    epg = EXPERTS_PER_GROUP

    def expert(i, e, tg):
        return (layer, jnp.where(tg[i] < N_GROUPS, tg[i] * epg + e, N_EXPERTS - 1), 0, 0)

    grid_spec = pltpu.PrefetchScalarGridSpec(
        num_scalar_prefetch=1,
        grid=(n_rows // tm, epg),
        in_specs=[pl.BlockSpec((tm, ROW_W), lambda i, e, tg: (i, 0)),
                  pl.BlockSpec((None, None, D_MODEL, D_EXPERT), expert),
                  pl.BlockSpec((None, None, D_MODEL, D_EXPERT), expert),
                  pl.BlockSpec((None, None, D_EXPERT, D_MODEL), expert),
                  pl.BlockSpec((1, D_MODEL), lambda i, e, tg: (0, 0)),
                  pl.BlockSpec((1, D_MODEL), lambda i, e, tg: (0, 0))],
        out_specs=pl.BlockSpec((tm, D_MODEL), lambda i, e, tg: (i, 0)),
        scratch_shapes=[pltpu.VMEM((tm, D_MODEL), BF16), pltpu.VMEM((tm, D_MODEL), F32)],
    )
    return pl.pallas_call(
        _moe_kernel,
        out_shape=jax.ShapeDtypeStruct((n_rows, D_MODEL), F32),
        grid_spec=grid_spec,
        compiler_params=_cparams(("parallel", "arbitrary")),
        name="moe",
    )(tile_group, xs, w_gate, w_up, w_down, ln_g, ln_b)


def _combine_ple_kernel(pos_ref, x2s_ref, p_ref, wg_ref, bg_ref, wp_ref, o_ref, buf_ref, sem):
    tm = o_ref.shape[0]
    i = pl.program_id(0)
    last = pl.num_programs(0) - 1
    slot = i % 2

    def wait_slot(s):
        pltpu.make_async_copy(x2s_ref.at[pl.ds(0, tm)], buf_ref.at[s], sem.at[s]).wait()

    @pl.when(i == 0)
    def _():
        def issue(r, _):
            _row_copy(x2s_ref, pos_ref[r], buf_ref.at[0], r, sem.at[0]).start()
            return 0

        lax.fori_loop(0, tm, issue, 0, unroll=8)

    wait_slot(slot)
    base = jnp.minimum(i + 1, last) * tm
    for r in range(tm):
        _row_copy(x2s_ref, pos_ref[base + r], buf_ref.at[1 - slot], r, sem.at[1 - slot]).start()
    x2 = buf_ref[slot]
    gate = jax.nn.sigmoid(jnp.dot(x2.astype(BF16), wg_ref[...], preferred_element_type=F32) + bg_ref[...])
    pp = jnp.dot(p_ref[...].astype(BF16), wp_ref[...], preferred_element_type=F32)
    o_ref[...] = x2 + gate * pp

    @pl.when(i == last)
    def _():
        wait_slot(1 - slot)


def _combine_ple(pos, x2s, p2d, w_gate, b_gate, w_proj):
    t = p2d.shape[0]
    tm = min(TM_ROWS, t)
    grid_spec = pltpu.PrefetchScalarGridSpec(
        num_scalar_prefetch=1,
        grid=(t // tm,),
        in_specs=[pl.BlockSpec(memory_space=pl.ANY),
                  pl.BlockSpec((tm, PLE_DIM), lambda i, pos: (i, 0)),
                  pl.BlockSpec((D_MODEL, D_MODEL), lambda i, pos: (0, 0)),
                  pl.BlockSpec((1, D_MODEL), lambda i, pos: (0, 0)),
                  pl.BlockSpec((PLE_DIM, D_MODEL), lambda i, pos: (0, 0))],
        out_specs=pl.BlockSpec((tm, D_MODEL), lambda i, pos: (i, 0)),
        scratch_shapes=[pltpu.VMEM((2, tm, D_MODEL), F32), pltpu.SemaphoreType.DMA((2,))],
    )
    return pl.pallas_call(
        _combine_ple_kernel,
        out_shape=jax.ShapeDtypeStruct((t, D_MODEL), F32),
        grid_spec=grid_spec,
        compiler_params=_cparams(("arbitrary",)),
        name="combine_ple",
    )(pos, x2s, p2d, w_gate, b_gate, w_proj)


def _sorted_positions(g_idx, n_tiles):
    onehot = (g_idx[:, None] == jnp.arange(N_GROUPS, dtype=jnp.int32)[None, :]).astype(jnp.int32)
    counts = jnp.sum(onehot, axis=0)
    tiles = (counts + TM_MOE - 1) // TM_MOE
    tile_end = jnp.cumsum(tiles)
    tile_start = tile_end - tiles
    rank = jnp.sum((jnp.cumsum(onehot, axis=0) - onehot) * onehot, axis=1)
    pos = jnp.sum(onehot * (tile_start * TM_MOE)[None, :], axis=1) + rank
    tile_ids = jnp.arange(n_tiles, dtype=jnp.int32)
    tile_group = jnp.sum((tile_ids[:, None] >= tile_end[None, :]).astype(jnp.int32), axis=1)
    return pos.astype(jnp.int32), tile_group.astype(jnp.int32)


def _moe_block(a, w_out, x2d, p2d, ln1_g, ln1_b, ln2_g, ln2_b, w_group, b_group, w_router, b_router,
               w_gate, w_up, w_down, ple_w_proj, ple_w_gate, ple_b_gate, layer):
    t = x2d.shape[0]
    n_tiles = t // TM_MOE + N_GROUPS
    pad = LANES - CW_LANE0 - N_EXPERTS
    w_r = jnp.concatenate([w_group, w_router, jnp.zeros((D_MODEL, pad), F32)], axis=1)
    b_r = jnp.concatenate([b_group, b_router, jnp.zeros((pad,), F32)])[None, :]
    x1ext, ext = _out_router(a, w_out.astype(BF16), x2d, ln1_g[None, :], ln1_b[None, :], w_r, b_r)
    g_idx = ext[:, GIDX_LANE].astype(jnp.int32)
    pos, tile_group = _sorted_positions(g_idx, n_tiles)
    xs = _dispatch(pos, x1ext, n_tiles * TM_MOE)
    x2s = _moe(tile_group, xs, w_gate, w_up, w_down, ln2_g[None, :], ln2_b[None, :], layer)
    return _combine_ple(pos, x2s, p2d, ple_w_gate.astype(BF16), ple_b_gate[None, :],
                        ple_w_proj.astype(BF16))


@jax.jit
def _forward(x, p, positions, fox_w_in, fox_b_f, fox_w_out, ret_w_in, ret_w_out, ln1_g, ln1_b, ln2_g, ln2_b,
             moe_w_group, moe_b_group, moe_w_router, moe_b_router, moe_w_gate, moe_w_up, moe_w_down,
             ple_w_proj, ple_w_gate, ple_b_gate):
    batch, seq, d = x.shape
    t = batch * seq
    x2d = x.reshape(t, d)

    def moe_args(i):
        return (ln1_g[i], ln1_b[i], ln2_g[i], ln2_b[i], moe_w_group[i], moe_b_group[i], moe_w_router[i],
                moe_b_router[i], moe_w_gate, moe_w_up, moe_w_down, ple_w_proj[i], ple_w_gate[i],
                ple_b_gate[i], i)

    w_in = fox_w_in[0]
    scale = jnp.concatenate([jnp.full((D_MODEL,), FOX_HEAD_DIM ** -0.5 * LOG2E, F32),
                             jnp.ones((2 * D_MODEL,), F32)])
    w_qkv = (w_in[:, :3 * D_MODEL] * scale[None, :]).astype(BF16)
    qkv = _proj(x2d, w_qkv, D_MODEL)
    c = _fgate(x2d, w_in[:, 3 * D_MODEL:], fox_b_f[0][None, :], batch, seq)
    c3 = c.reshape(batch, seq, FOX_HEADS).transpose(0, 2, 1)
    attn = _fox_attention(qkv, c, c3[:, :, None, :], batch, seq)
    x2d = _moe_block(attn, fox_w_out[0], x2d, p[0].reshape(t, PLE_DIM), *moe_args(0))

    w_in = ret_w_in[0]
    scale = jnp.concatenate([jnp.ones((D_MODEL,), F32), jnp.full((D_MODEL,), RET_KEY_DIM ** -0.5, F32),
                             jnp.ones((4 * D_MODEL,), F32)])
    inv_freq = ROPE_BASE ** (-jnp.arange(0, RET_KEY_DIM, 2, dtype=F32) / RET_KEY_DIM)
    cos, sin = _rope_tables(positions.astype(F32).reshape(t, 1), inv_freq[None, :])
    proj = _ret_proj(x2d, (w_in * scale[None, :]).astype(BF16), cos, sin)
    log_gamma = jnp.log(1.0 - 2.0 ** (-5.0 - jnp.arange(RET_HEADS, dtype=F32)))
    ret = _retention(proj, log_gamma, batch, seq)
    x2d = _moe_block(ret, ret_w_out[0], x2d, p[1].reshape(t, PLE_DIM), *moe_args(1))
    return x2d.reshape(batch, seq, d)


def kernel(x, p, positions, fox_w_in, fox_b_f, fox_w_out, ret_w_in, ret_w_out, ln1_g, ln1_b, ln2_g, ln2_b,
           moe_w_group, moe_b_group, moe_w_router, moe_b_router, moe_w_gate, moe_w_up, moe_w_down,
           ple_w_proj, ple_w_gate, ple_b_gate):
    return _forward(x, p, positions, fox_w_in, fox_b_f, fox_w_out, ret_w_in, ret_w_out, ln1_g, ln1_b,
                    ln2_g, ln2_b, moe_w_group, moe_b_group, moe_w_router, moe_b_router, moe_w_gate,
                    moe_w_up, moe_w_down, ple_w_proj, ple_w_gate, ple_b_gate)
```

```python
import functools

import jax
import jax.numpy as jnp
from jax import lax
from jax.experimental import pallas as pl
from jax.experimental.pallas import tpu as pltpu

F32 = jnp.float32
BF16 = jnp.bfloat16

D_MODEL = 1024
PLE_DIM = 256
FOX_HEADS = 16
FOX_HEAD_DIM = D_MODEL // FOX_HEADS
RET_HEADS = 4
RET_KEY_DIM = D_MODEL // RET_HEADS
RET_VAL_DIM = 2 * D_MODEL // RET_HEADS
ROPE_BASE = 10000.0
N_GROUPS = 4
EXPERTS_PER_GROUP = 4
N_EXPERTS = N_GROUPS * EXPERTS_PER_GROUP
D_EXPERT = 512
DEPTH = 2
DEEPNORM_ALPHA = (2.0 * DEPTH) ** 0.25
LN_EPS = 1e-5

LANES = 128
EXT = LANES
ROW_W = D_MODEL + EXT
GIDX_LANE = 0
CW_LANE0 = 4
NEG_BIG = -1e30
LOG2E = 1.4426950408889634
VMEM_LIMIT = 56 * 1024 * 1024

TM_PROJ = 1024
TS_GATE = 512
TQ_ATTN = 512
ATTN_TILES_PER_STEP = 2
ATTN_ROW_BLOCK = 128
TM_OUT = 1024
TM_ROWS = 1024
TM_MOE = 1024
RET_CHUNK = 256


def _cparams(sem):
    return pltpu.CompilerParams(dimension_semantics=sem, vmem_limit_bytes=VMEM_LIMIT)


def _proj_kernel(a_ref, w_ref, o_ref):
    a = a_ref[...].astype(BF16)
    o_ref[...] = jnp.dot(a, w_ref[...], preferred_element_type=F32).astype(o_ref.dtype)


def _proj(a, w, tn):
    m, k = a.shape
    n = w.shape[1]
    tm = min(TM_PROJ, m)
    return pl.pallas_call(
        _proj_kernel,
        out_shape=jax.ShapeDtypeStruct((m, n), BF16),
        grid=(m // tm, n // tn),
        in_specs=[pl.BlockSpec((tm, k), lambda i, j: (i, 0)),
                  pl.BlockSpec((k, tn), lambda i, j: (0, j))],
        out_specs=pl.BlockSpec((tm, tn), lambda i, j: (i, j)),
        compiler_params=_cparams(("parallel", "arbitrary")),
        name="proj",
    )(a, w)


def _split_bf16(x):
    hi = x.astype(BF16)
    return hi, (x - hi.astype(F32)).astype(BF16)


def _split_weight(w):
    hi, lo = _split_bf16(w)
    return jnp.concatenate([hi, lo], axis=1), hi


def _dot_split(x, w_hilo, w_hi):
    x_hi, x_lo = _split_bf16(x)
    r = jnp.dot(x_hi, w_hilo, preferred_element_type=F32)
    return r[:, :LANES] + r[:, LANES:] + jnp.dot(x_lo, w_hi, preferred_element_type=F32)


def _fgate_kernel(x_ref, whl_ref, wh_ref, b_ref, c_ref, carry_ref):
    @pl.when(pl.program_id(1) == 0)
    def _():
        carry_ref[...] = jnp.zeros_like(carry_ref)

    ts = x_ref.shape[0]
    z = _dot_split(x_ref[...], whl_ref[...], wh_ref[...]) + b_ref[...]
    logf = (jnp.minimum(z, 0.0) - jnp.log1p(jnp.exp(-jnp.abs(z)))) * LOG2E
    l1 = logf.astype(BF16)
    rest = logf - l1.astype(F32)
    l2, l3 = _split_bf16(rest)
    row = lax.broadcasted_iota(jnp.int32, (ts, ts), 0)
    col = lax.broadcasted_iota(jnp.int32, (ts, ts), 1)
    tri = jnp.where(row >= col, 1.0, 0.0).astype(BF16)
    parts = jnp.dot(tri, jnp.concatenate([l1, l2, l3], axis=1), preferred_element_type=F32)
    c = parts[:, :LANES] + parts[:, LANES:2 * LANES] + parts[:, 2 * LANES:] + carry_ref[...]
    c_ref[...] = c[:, :c_ref.shape[1]]
    carry_ref[...] = c[ts - 1:ts, :]


def _fgate(x2d, w_f, b_f, batch, seq):
    ts = min(TS_GATE, seq)
    ns = seq // ts
    h = w_f.shape[1]
    w_hilo, w_hi = _split_weight(jnp.pad(w_f, ((0, 0), (0, LANES - h))))
    b_pad = jnp.pad(b_f, ((0, 0), (0, LANES - h)))
    return pl.pallas_call(
        _fgate_kernel,
        out_shape=jax.ShapeDtypeStruct((batch * seq, h), F32),
        grid=(batch, ns),
        in_specs=[pl.BlockSpec((ts, D_MODEL), lambda b, s: (b * ns + s, 0)),
                  pl.BlockSpec((D_MODEL, 2 * LANES), lambda b, s: (0, 0)),
                  pl.BlockSpec((D_MODEL, LANES), lambda b, s: (0, 0)),
                  pl.BlockSpec((1, LANES), lambda b, s: (0, 0))],
        out_specs=pl.BlockSpec((ts, h), lambda b, s: (b * ns + s, 0)),
        scratch_shapes=[pltpu.VMEM((1, LANES), F32)],
        compiler_params=_cparams(("parallel", "arbitrary")),
        name="fgate",
    )(x2d, w_hilo, w_hi, b_pad)


def _fox_attn_kernel(q_ref, k_ref, v_ref, cq_ref, ck_ref, o_ref, m_ref, acc_ref, sa_ref, sb_ref, *, tq, nt):
    i = pl.program_id(2)
    lane = lax.broadcasted_iota(jnp.int32, (tq, LANES), 1)
    row = lax.broadcasted_iota(jnp.int32, (tq, tq), 0)
    col = lax.broadcasted_iota(jnp.int32, (tq, tq), 1)
    causal = row >= col
    head0 = lane < FOX_HEAD_DIM
    m_ref[...] = jnp.full(m_ref.shape, NEG_BIG, F32)
    acc_ref[...] = jnp.zeros(acc_ref.shape, F32)
    lane_k = lax.broadcasted_iota(jnp.int32, (tq, LANES), 1)
    lane_h = lax.broadcasted_iota(jnp.int32, (tq, FOX_HEADS), 1)
    q2 = []
    cq_heads = []
    for tile in range(nt):
        q = q_ref[tile * tq:(tile + 1) * tq, :]
        q2.append(jnp.concatenate([jnp.where(head0, q, jnp.zeros_like(q)),
                                   jnp.where(head0, jnp.zeros_like(q), q)], axis=0))
        c_tok = cq_ref[tile * tq:(tile + 1) * tq, :]
        cq_heads.append([jnp.sum(jnp.where(lane_h == 2 * pl.program_id(1) + h, c_tok, 0.0), axis=1, keepdims=True)
                         for h in range(2)])
    rb = ATTN_ROW_BLOCK

    def scores(tile, kb, s_ref):
        ks = pl.multiple_of(kb * tq, tq)
        s_ref[...] = lax.dot_general(q2[tile], k_ref[pl.ds(ks, tq), :], (((1,), (1,)), ((), ())),
                                     preferred_element_type=F32)

    def softmax_pv(tile, kb, s_ref, masked):
        ks = pl.multiple_of(kb * tq, tq)
        vblk = v_ref[pl.ds(ks, tq), :]
        for h in range(2):
            ck = ck_ref[h, :, pl.ds(ks, tq)]
            vh = jnp.where(lane_k < FOX_HEAD_DIM if h == 0 else lane_k >= FOX_HEAD_DIM,
                           vblk, jnp.ones_like(vblk))
            p_blocks = []
            m_blocks = []
            cq_all = cq_heads[tile][h]
            m_all = m_ref[tile, h]
            for r0 in range(0, tq, rb):
                t = s_ref[h * tq + r0:h * tq + r0 + rb, :] - ck
                if masked:
                    t = jnp.where(causal[r0:r0 + rb, :], t, NEG_BIG)
                cq = cq_all[r0:r0 + rb, :]
                m_new = jnp.maximum(m_all[r0:r0 + rb, :], cq + jnp.max(t, axis=1, keepdims=True))
                m_blocks.append(m_new)
                p_blocks.append(jnp.exp2(t + (cq - m_new)).astype(BF16))
            m_new_all = jnp.concatenate(m_blocks, axis=0)
            pv = jnp.dot(jnp.concatenate(p_blocks, axis=0), vh, preferred_element_type=F32)
            acc_ref[tile, h] = jnp.exp2(m_all - m_new_all) * acc_ref[tile, h] + pv
            m_ref[tile, h] = m_new_all

    bufs = (sa_ref, sb_ref)
    scores(0, 0, bufs[0])
    par = 0
    for u in range(nt):
        g = nt * i + u
        lo = u % 2
        if lo:
            scores(u, 1, bufs[1 - par])
            softmax_pv(u, 0, bufs[par], False)
            par = 1 - par
        cur, oth = bufs[par], bufs[1 - par]

        def pair(j, _, u=u, lo=lo, cur=cur, oth=oth):
            kb = lo + 2 * j
            scores(u, kb + 1, oth)
            softmax_pv(u, kb, cur, False)
            scores(u, kb + 2, cur)
            softmax_pv(u, kb + 1, oth, False)
            return 0

        lax.fori_loop(0, (g - lo) // 2, pair, 0)
        if u + 1 < nt:
            scores(u + 1, 0, oth)
        softmax_pv(u, g, cur, True)
        par = 1 - par

    for tile in range(nt):
        a0 = acc_ref[tile, 0]
        a1 = acc_ref[tile, 1]
        o0 = a0 / pltpu.roll(a0, FOX_HEAD_DIM, 1)
        o1 = a1 / pltpu.roll(a1, FOX_HEAD_DIM, 1)
        o_ref[tile * tq:(tile + 1) * tq, :] = jnp.where(head0, o0, o1).astype(o_ref.dtype)


def _fox_attention(qkv, cq, ck, batch, seq):
    nt = min(ATTN_TILES_PER_STEP, seq // min(TQ_ATTN, seq // 2))
    tq = min(TQ_ATTN, seq // nt)
    assert nt % 2 == 0 and seq % (nt * tq) == 0, "a grid step covers an even number of query tiles"
    ns = seq // (nt * tq)
    hp = FOX_HEADS // 2
    return pl.pallas_call(
        functools.partial(_fox_attn_kernel, tq=tq, nt=nt),
        out_shape=jax.ShapeDtypeStruct((batch * seq, D_MODEL), BF16),
        grid=(batch, hp, ns),
        in_specs=[pl.BlockSpec((nt * tq, LANES), lambda b, p, i: (b * ns + i, p)),
                  pl.BlockSpec((seq, LANES), lambda b, p, i: (b, hp + p)),
                  pl.BlockSpec((seq, LANES), lambda b, p, i: (b, 2 * hp + p)),
                  pl.BlockSpec((nt * tq, FOX_HEADS), lambda b, p, i: (b * ns + i, 0)),
                  pl.BlockSpec((None, 2, 1, seq), lambda b, p, i: (b, p, 0, 0))],
        out_specs=pl.BlockSpec((nt * tq, LANES), lambda b, p, i: (b * ns + i, p)),
        scratch_shapes=[pltpu.VMEM((nt, 2, tq, 1), F32), pltpu.VMEM((nt, 2, tq, LANES), F32),
                        pltpu.VMEM((2 * tq, tq), F32), pltpu.VMEM((2 * tq, tq), F32)],
        compiler_params=_cparams(("parallel", "parallel", "arbitrary")),
        name="fox_attn",
    )(qkv, qkv, qkv, cq, ck)


def _rope_kernel(pos_ref, freq_ref, cos_ref, sin_ref):
    ang = pos_ref[...] * freq_ref[...]
    cos_ref[...] = jnp.cos(ang)
    sin_ref[...] = jnp.sin(ang)


def _rope_tables(pos_f, inv_freq):
    t = pos_f.shape[0]
    half = inv_freq.shape[1]
    tm = min(1024, t)
    return pl.pallas_call(
        _rope_kernel,
        out_shape=(jax.ShapeDtypeStruct((t, half), F32), jax.ShapeDtypeStruct((t, half), F32)),
        grid=(t // tm,),
        in_specs=[pl.BlockSpec((tm, 1), lambda i: (i, 0)),
                  pl.BlockSpec((1, half), lambda i: (0, 0))],
        out_specs=(pl.BlockSpec((tm, half), lambda i: (i, 0)),
                   pl.BlockSpec((tm, half), lambda i: (i, 0))),
        compiler_params=_cparams(("parallel",)),
        name="rope_tables",
    )(pos_f, inv_freq)


def _ret_proj_kernel(a_ref, w_ref, cos_ref, sin_ref, o_ref):
    j = pl.program_id(1)
    a = a_ref[...].astype(BF16)
    acc = jnp.dot(a, w_ref[...], preferred_element_type=F32)

    @pl.when(j < 2)
    def _():
        c = cos_ref[...]
        s = sin_ref[...]
        half = RET_KEY_DIM // 2
        for h in range(RET_HEADS):
            x1 = acc[:, h * RET_KEY_DIM:h * RET_KEY_DIM + half]
            x2 = acc[:, h * RET_KEY_DIM + half:(h + 1) * RET_KEY_DIM]
            o_ref[:, h * RET_KEY_DIM:h * RET_KEY_DIM + half] = (x1 * c - x2 * s).astype(o_ref.dtype)
            o_ref[:, h * RET_KEY_DIM + half:(h + 1) * RET_KEY_DIM] = (x2 * c + x1 * s).astype(o_ref.dtype)

    @pl.when(j >= 2)
    def _():
        o_ref[...] = acc.astype(o_ref.dtype)


def _ret_proj(x2d, w, cos, sin):
    t = x2d.shape[0]
    n = w.shape[1]
    tm = min(TM_PROJ, t)
    tn = D_MODEL
    half = RET_KEY_DIM // 2
    return pl.pallas_call(
        _ret_proj_kernel,
        out_shape=jax.ShapeDtypeStruct((t, n), BF16),
        grid=(t // tm, n // tn),
        in_specs=[pl.BlockSpec((tm, D_MODEL), lambda i, j: (i, 0)),
                  pl.BlockSpec((D_MODEL, tn), lambda i, j: (0, j)),
                  pl.BlockSpec((tm, half), lambda i, j: (i, 0)),
                  pl.BlockSpec((tm, half), lambda i, j: (i, 0))],
        out_specs=pl.BlockSpec((tm, tn), lambda i, j: (i, j)),
        compiler_params=_cparams(("parallel", "arbitrary")),
        name="ret_proj",
    )(x2d, w, cos, sin)


def _retention_kernel(lg_ref, q_ref, k_ref, v_ref, g_ref, o_ref, state_ref, decay_ref, *, chunk):
    dk, dv = RET_KEY_DIM, RET_VAL_DIM

    @pl.when(pl.program_id(1) == 0)
    def _():
        state_ref[...] = jnp.zeros_like(state_ref)
        row = lax.broadcasted_iota(jnp.int32, (chunk, chunk), 0)
        col = lax.broadcasted_iota(jnp.int32, (chunk, chunk), 1)
        diff = (row - col).astype(F32)
        for h in range(RET_HEADS):
            decay_ref[h] = jnp.where(diff >= 0, jnp.exp(jnp.maximum(diff, 0.0) * lg_ref[h]), 0.0)

    idx = lax.broadcasted_iota(jnp.int32, (chunk, 1), 0).astype(F32)
    for h in range(RET_HEADS):
        lg = lg_ref[h]
        q = q_ref[:, h * dk:(h + 1) * dk]
        k = k_ref[:, h * dk:(h + 1) * dk]
        v = v_ref[:, h * dv:(h + 1) * dv]
        scores = lax.dot_general(q, k, (((1,), (1,)), ((), ())), preferred_element_type=F32) * decay_ref[h]
        o = jnp.dot(scores.astype(BF16), v, preferred_element_type=F32)
        state = state_ref[h]
        qd = (q.astype(F32) * jnp.exp((idx + 1.0) * lg)).astype(BF16)
        o = o + jnp.dot(qd, state.astype(BF16), preferred_element_type=F32)
        kd = (k.astype(F32) * jnp.exp((chunk - 1.0 - idx) * lg)).astype(BF16)
        state_ref[h] = state * jnp.exp(chunk * lg) + lax.dot_general(
            kd, v, (((0,), (0,)), ((), ())), preferred_element_type=F32)
        mu = jnp.mean(o, axis=1, keepdims=True)
        oc = o - mu
        var = jnp.mean(oc * oc, axis=1, keepdims=True)
        on = oc * lax.rsqrt(var + LN_EPS)
        g = g_ref[:, h * dv:(h + 1) * dv].astype(F32)
        o_ref[:, h * dv:(h + 1) * dv] = (g * jax.nn.sigmoid(g) * on).astype(o_ref.dtype)


def _retention(proj, log_gamma, batch, seq):
    chunk = min(RET_CHUNK, seq)
    nc = seq // chunk
    dq = RET_HEADS * RET_KEY_DIM
    dvt = RET_HEADS * RET_VAL_DIM
    grid_spec = pltpu.PrefetchScalarGridSpec(
        num_scalar_prefetch=1,
        grid=(batch, nc),
        in_specs=[pl.BlockSpec((chunk, dq), lambda b, c, lg: (b * nc + c, 0)),
                  pl.BlockSpec((chunk, dq), lambda b, c, lg: (b * nc + c, 1)),
                  pl.BlockSpec((chunk, dvt), lambda b, c, lg: (b * nc + c, 1)),
                  pl.BlockSpec((chunk, dvt), lambda b, c, lg: (b * nc + c, 2))],
        out_specs=pl.BlockSpec((chunk, dvt), lambda b, c, lg: (b * nc + c, 0)),
        scratch_shapes=[pltpu.VMEM((RET_HEADS, RET_KEY_DIM, RET_VAL_DIM), F32),
                        pltpu.VMEM((RET_HEADS, chunk, chunk), F32)],
    )
    return pl.pallas_call(
        functools.partial(_retention_kernel, chunk=chunk),
        out_shape=jax.ShapeDtypeStruct((batch * seq, dvt), BF16),
        grid_spec=grid_spec,
        compiler_params=_cparams(("parallel", "arbitrary")),
        name="retention",
    )(log_gamma, proj, proj, proj, proj)


def _layer_norm(y, g, b):
    mu = jnp.mean(y, axis=1, keepdims=True)
    yc = y - mu
    var = jnp.mean(yc * yc, axis=1, keepdims=True)
    return yc * lax.rsqrt(var + LN_EPS) * g + b


def _out_router_kernel(a_ref, w_ref, x_ref, g_ref, b_ref, wrhl_ref, wrh_ref, br_ref, o_ref, ext_ref):
    tm = a_ref.shape[0]
    h = jnp.dot(a_ref[...], w_ref[...], preferred_element_type=F32)
    x1 = _layer_norm(DEEPNORM_ALPHA * x_ref[...] + h, g_ref[...], b_ref[...])
    o_ref[:, :D_MODEL] = x1

    logits = _dot_split(x1, wrhl_ref[...], wrh_ref[...]) + br_ref[...]
    lane = lax.broadcasted_iota(jnp.int32, (tm, LANES), 1)

    def first_argmax(vals, vmax):
        return jnp.min(jnp.where(vals == vmax, lane, LANES), axis=1, keepdims=True)

    gl = jnp.where(lane < N_GROUPS, logits, NEG_BIG)
    gmax = jnp.max(gl, axis=1, keepdims=True)
    gsum = jnp.sum(jnp.where(lane < N_GROUPS, jnp.exp(gl - gmax), 0.0), axis=1, keepdims=True)
    g_val = 1.0 / gsum
    g_idx = first_argmax(gl, gmax)
    lo = CW_LANE0 + EXPERTS_PER_GROUP * g_idx
    in_group = (lane >= lo) & (lane < lo + EXPERTS_PER_GROUP)
    el = jnp.where(in_group, logits, NEG_BIG)
    e1 = jnp.max(el, axis=1, keepdims=True)
    i1 = first_argmax(el, e1)
    el2 = jnp.where(lane == i1, NEG_BIG, el)
    e2 = jnp.max(el2, axis=1, keepdims=True)
    i2 = first_argmax(el2, e2)
    r = jnp.exp(e2 - e1)
    w1 = g_val / (1.0 + r)
    w2 = g_val * r / (1.0 + r)
    ext = jnp.where(lane == i1, w1, 0.0) + jnp.where(lane == i2, w2, 0.0)
    ext = jnp.where(lane == GIDX_LANE, g_idx.astype(F32), ext)
    o_ref[:, D_MODEL:] = ext
    ext_ref[...] = ext


def _out_router(a, w_out, x2d, ln_g, ln_b, w_r, b_r):
    t, din = a.shape
    tm = min(TM_OUT, t)
    w_r_hilo, w_r_hi = _split_weight(w_r)
    return pl.pallas_call(
        _out_router_kernel,
        out_shape=(jax.ShapeDtypeStruct((t, ROW_W), F32), jax.ShapeDtypeStruct((t, EXT), F32)),
        grid=(t // tm,),
        in_specs=[pl.BlockSpec((tm, din), lambda i: (i, 0)),
                  pl.BlockSpec((din, D_MODEL), lambda i: (0, 0)),
                  pl.BlockSpec((tm, D_MODEL), lambda i: (i, 0)),
                  pl.BlockSpec((1, D_MODEL), lambda i: (0, 0)),
                  pl.BlockSpec((1, D_MODEL), lambda i: (0, 0)),
                  pl.BlockSpec((D_MODEL, 2 * LANES), lambda i: (0, 0)),
                  pl.BlockSpec((D_MODEL, LANES), lambda i: (0, 0)),
                  pl.BlockSpec((1, LANES), lambda i: (0, 0))],
        out_specs=(pl.BlockSpec((tm, ROW_W), lambda i: (i, 0)), pl.BlockSpec((tm, EXT), lambda i: (i, 0))),
        compiler_params=_cparams(("parallel",)),
        name="out_router",
    )(a, w_out, x2d, ln_g, ln_b, w_r_hilo, w_r_hi, b_r)


def _row_copy(src_ref, src_row, dst_ref, dst_row, sem):
    return pltpu.make_async_copy(src_ref.at[pl.ds(src_row, 1)], dst_ref.at[pl.ds(dst_row, 1)], sem)


def _dispatch_kernel(pos_ref, x_ref, init_ref, xs_ref, sem):
    del init_ref
    tm = x_ref.shape[0]
    base = pl.program_id(0) * tm

    for r in range(tm):
        _row_copy(x_ref, r, xs_ref, pos_ref[base + r], sem).start()
    pltpu.make_async_copy(x_ref, xs_ref.at[pl.ds(0, tm)], sem).wait()


def _dispatch(pos, x1ext, n_rows):
    t = x1ext.shape[0]
    tm = min(TM_ROWS, t)
    grid_spec = pltpu.PrefetchScalarGridSpec(
        num_scalar_prefetch=1,
        grid=(t // tm,),
        in_specs=[pl.BlockSpec((tm, ROW_W), lambda i, pos: (i, 0)),
                  pl.BlockSpec(memory_space=pl.ANY)],
        out_specs=pl.BlockSpec(memory_space=pl.ANY),
        scratch_shapes=[pltpu.SemaphoreType.DMA(())],
    )
    return pl.pallas_call(
        _dispatch_kernel,
        out_shape=jax.ShapeDtypeStruct((n_rows, ROW_W), F32),
        grid_spec=grid_spec,
        input_output_aliases={2: 0},
        compiler_params=_cparams(("arbitrary",)),
        name="dispatch",
    )(pos, x1ext, jnp.zeros((n_rows, ROW_W), F32))


def _moe_kernel(tg_ref, xs_ref, wg_ref, wu_ref, wd_ref, g_ref, b_ref, o_ref, xb_ref, acc_ref):
    i = pl.program_id(0)
    e = pl.program_id(1)
    tm = xs_ref.shape[0]
    group = tg_ref[i]
    valid = group < N_GROUPS

    @pl.when(valid)
    def _():
        @pl.when(e == 0)
        def _():
            xb_ref[...] = xs_ref[:, :D_MODEL].astype(BF16)
            acc_ref[...] = jnp.zeros_like(acc_ref)

        lane = lax.broadcasted_iota(jnp.int32, (tm, EXT), 1)
        cw_lane = CW_LANE0 + EXPERTS_PER_GROUP * group + e
        cw = jnp.sum(jnp.where(lane == cw_lane, xs_ref[:, D_MODEL:], 0.0), axis=1, keepdims=True)
        xb = xb_ref[...]
        hg = jnp.dot(xb, wg_ref[...].astype(BF16), preferred_element_type=F32)
        hu = jnp.dot(xb, wu_ref[...].astype(BF16), preferred_element_type=F32)
        hh = (hg * jax.nn.sigmoid(hg) * hu * cw).astype(BF16)
        acc_ref[...] += jnp.dot(hh, wd_ref[...].astype(BF16), preferred_element_type=F32)

        @pl.when(e == EXPERTS_PER_GROUP - 1)
        def _():
            y = DEEPNORM_ALPHA * xs_ref[:, :D_MODEL] + acc_ref[...]
            o_ref[...] = _layer_norm(y, g_ref[...], b_ref[...])

    @pl.when(jnp.logical_not(valid) & (e == EXPERTS_PER_GROUP - 1))
    def _():
        o_ref[...] = jnp.zeros_like(o_ref)


def _moe(tile_group, xs, w_gate, w_up, w_down, ln_g, ln_b, layer):
    n_rows = xs.shape[0]
    tm = TM_MOE
    epg = EXPERTS_PER_GROUP

    def expert(i, e, tg):
        return (layer, jnp.where(tg[i] < N_GROUPS, tg[i] * epg + e, N_EXPERTS - 1), 0, 0)

    grid_spec = pltpu.PrefetchScalarGridSpec(
        num_scalar_prefetch=1,
        grid=(n_rows // tm, epg),
        in_specs=[pl.BlockSpec((tm, ROW_W), lambda i, e, tg: (i, 0)),
                  pl.BlockSpec((None, None, D_MODEL, D_EXPERT), expert),
                  pl.BlockSpec((None, None, D_MODEL, D_EXPERT), expert),
                  pl.BlockSpec((None, None, D_EXPERT, D_MODEL), expert),
                  pl.BlockSpec((1, D_MODEL), lambda i, e, tg: (0, 0)),
                  pl.BlockSpec((1, D_MODEL), lambda i, e, tg: (0, 0))],
        out_specs=pl.BlockSpec((tm, D_MODEL), lambda i, e, tg: (i, 0)),
        scratch_shapes=[pltpu.VMEM((tm, D_MODEL), BF16), pltpu.VMEM((tm, D_MODEL), F32)],
    )
    return pl.pallas_call(
        _moe_kernel,
        out_shape=jax.ShapeDtypeStruct((n_rows, D_MODEL), F32),
        grid_spec=grid_spec,
        compiler_params=_cparams(("parallel", "arbitrary")),
        name="moe",
    )(tile_group, xs, w_gate, w_up, w_down, ln_g, ln_b)


def _combine_ple_kernel(pos_ref, x2s_ref, p_ref, wg_ref, bg_ref, wp_ref, o_ref, buf_ref, sem):
    tm = o_ref.shape[0]
    i = pl.program_id(0)
    last = pl.num_programs(0) - 1
    slot = i % 2

    def wait_slot(s):
        pltpu.make_async_copy(x2s_ref.at[pl.ds(0, tm)], buf_ref.at[s], sem.at[s]).wait()

    @pl.when(i == 0)
    def _():
        def issue(r, _):
            _row_copy(x2s_ref, pos_ref[r], buf_ref.at[0], r, sem.at[0]).start()
            return 0

        lax.fori_loop(0, tm, issue, 0, unroll=8)

    wait_slot(slot)
    base = jnp.minimum(i + 1, last) * tm
    for r in range(tm):
        _row_copy(x2s_ref, pos_ref[base + r], buf_ref.at[1 - slot], r, sem.at[1 - slot]).start()
    x2 = buf_ref[slot]
    gate = jax.nn.sigmoid(jnp.dot(x2.astype(BF16), wg_ref[...], preferred_element_type=F32) + bg_ref[...])
    pp = jnp.dot(p_ref[...].astype(BF16), wp_ref[...], preferred_element_type=F32)
    o_ref[...] = x2 + gate * pp

    @pl.when(i == last)
    def _():
        wait_slot(1 - slot)


def _combine_ple(pos, x2s, p2d, w_gate, b_gate, w_proj):
    t = p2d.shape[0]
    tm = min(TM_ROWS, t)
    grid_spec = pltpu.PrefetchScalarGridSpec(
        num_scalar_prefetch=1,
        grid=(t // tm,),
        in_specs=[pl.BlockSpec(memory_space=pl.ANY),
                  pl.BlockSpec((tm, PLE_DIM), lambda i, pos: (i, 0)),
                  pl.BlockSpec((D_MODEL, D_MODEL), lambda i, pos: (0, 0)),
                  pl.BlockSpec((1, D_MODEL), lambda i, pos: (0, 0)),
                  pl.BlockSpec((PLE_DIM, D_MODEL), lambda i, pos: (0, 0))],
        out_specs=pl.BlockSpec((tm, D_MODEL), lambda i, pos: (i, 0)),
        scratch_shapes=[pltpu.VMEM((2, tm, D_MODEL), F32), pltpu.SemaphoreType.DMA((2,))],
    )
    return pl.pallas_call(
        _combine_ple_kernel,
        out_shape=jax.ShapeDtypeStruct((t, D_MODEL), F32),
        grid_spec=grid_spec,
        compiler_params=_cparams(("arbitrary",)),
        name="combine_ple",
    )(pos, x2s, p2d, w_gate, b_gate, w_proj)


def _sorted_positions(g_idx, n_tiles):
    onehot = (g_idx[:, None] == jnp.arange(N_GROUPS, dtype=jnp.int32)[None, :]).astype(jnp.int32)
    counts = jnp.sum(onehot, axis=0)
    tiles = (counts + TM_MOE - 1) // TM_MOE
    tile_end = jnp.cumsum(tiles)
    tile_start = tile_end - tiles
    rank = jnp.sum((jnp.cumsum(onehot, axis=0) - onehot) * onehot, axis=1)
    pos = jnp.sum(onehot * (tile_start * TM_MOE)[None, :], axis=1) + rank
    tile_ids = jnp.arange(n_tiles, dtype=jnp.int32)
    tile_group = jnp.sum((tile_ids[:, None] >= tile_end[None, :]).astype(jnp.int32), axis=1)
    return pos.astype(jnp.int32), tile_group.astype(jnp.int32)


def _moe_block(a, w_out, x2d, p2d, ln1_g, ln1_b, ln2_g, ln2_b, w_group, b_group, w_router, b_router,
               w_gate, w_up, w_down, ple_w_proj, ple_w_gate, ple_b_gate, layer):
    t = x2d.shape[0]
    n_tiles = t // TM_MOE + N_GROUPS
    pad = LANES - CW_LANE0 - N_EXPERTS
    w_r = jnp.concatenate([w_group, w_router, jnp.zeros((D_MODEL, pad), F32)], axis=1)
    b_r = jnp.concatenate([b_group, b_router, jnp.zeros((pad,), F32)])[None, :]
    x1ext, ext = _out_router(a, w_out.astype(BF16), x2d, ln1_g[None, :], ln1_b[None, :], w_r, b_r)
    g_idx = ext[:, GIDX_LANE].astype(jnp.int32)
    pos, tile_group = _sorted_positions(g_idx, n_tiles)
    xs = _dispatch(pos, x1ext, n_tiles * TM_MOE)
    x2s = _moe(tile_group, xs, w_gate, w_up, w_down, ln2_g[None, :], ln2_b[None, :], layer)
    return _combine_ple(pos, x2s, p2d, ple_w_gate.astype(BF16), ple_b_gate[None, :],
                        ple_w_proj.astype(BF16))


@jax.jit
def _forward(x, p, positions, fox_w_in, fox_b_f, fox_w_out, ret_w_in, ret_w_out, ln1_g, ln1_b, ln2_g, ln2_b,
             moe_w_group, moe_b_group, moe_w_router, moe_b_router, moe_w_gate, moe_w_up, moe_w_down,
             ple_w_proj, ple_w_gate, ple_b_gate):
    batch, seq, d = x.shape
    t = batch * seq
    x2d = x.reshape(t, d)

    def moe_args(i):
        return (ln1_g[i], ln1_b[i], ln2_g[i], ln2_b[i], moe_w_group[i], moe_b_group[i], moe_w_router[i],
                moe_b_router[i], moe_w_gate, moe_w_up, moe_w_down, ple_w_proj[i], ple_w_gate[i],
                ple_b_gate[i], i)

    w_in = fox_w_in[0]
    scale = jnp.concatenate([jnp.full((D_MODEL,), FOX_HEAD_DIM ** -0.5 * LOG2E, F32),
                             jnp.ones((2 * D_MODEL,), F32)])
    w_qkv = (w_in[:, :3 * D_MODEL] * scale[None, :]).astype(BF16)
    qkv = _proj(x2d, w_qkv, D_MODEL)
    c = _fgate(x2d, w_in[:, 3 * D_MODEL:], fox_b_f[0][None, :], batch, seq)
    c3 = c.reshape(batch, seq, FOX_HEADS).transpose(0, 2, 1)
    attn = _fox_attention(qkv, c, c3[:, :, None, :], batch, seq)
    x2d = _moe_block(attn, fox_w_out[0], x2d, p[0].reshape(t, PLE_DIM), *moe_args(0))

    w_in = ret_w_in[0]
    scale = jnp.concatenate([jnp.ones((D_MODEL,), F32), jnp.full((D_MODEL,), RET_KEY_DIM ** -0.5, F32),
                             jnp.ones((4 * D_MODEL,), F32)])
    inv_freq = ROPE_BASE ** (-jnp.arange(0, RET_KEY_DIM, 2, dtype=F32) / RET_KEY_DIM)
    cos, sin = _rope_tables(positions.astype(F32).reshape(t, 1), inv_freq[None, :])
    proj = _ret_proj(x2d, (w_in * scale[None, :]).astype(BF16), cos, sin)
    log_gamma = jnp.log(1.0 - 2.0 ** (-5.0 - jnp.arange(RET_HEADS, dtype=F32)))
    ret = _retention(proj, log_gamma, batch, seq)
    x2d = _moe_block(ret, ret_w_out[0], x2d, p[1].reshape(t, PLE_DIM), *moe_args(1))
    return x2d.reshape(batch, seq, d)


def kernel(x, p, positions, fox_w_in, fox_b_f, fox_w_out, ret_w_in, ret_w_out, ln1_g, ln1_b, ln2_g, ln2_b,
           moe_w_group, moe_b_group, moe_w_router, moe_b_router, moe_w_gate, moe_w_up, moe_w_down,
           ple_w_proj, ple_w_gate, ple_b_gate):
    return _forward(x, p, positions, fox_w_in, fox_b_f, fox_w_out, ret_w_in, ret_w_out, ln1_g, ln1_b,
                    ln2_g, ln2_b, moe_w_group, moe_b_group, moe_w_router, moe_b_router, moe_w_gate,
                    moe_w_up, moe_w_down, ple_w_proj, ple_w_gate, ple_b_gate)
```

```python
import functools

import jax
import jax.numpy as jnp
from jax import lax
from jax.experimental import pallas as pl
from jax.experimental.pallas import tpu as pltpu

F32 = jnp.float32
BF16 = jnp.bfloat16

D_MODEL = 1024
PLE_DIM = 256
FOX_HEADS = 16
FOX_HEAD_DIM = D_MODEL // FOX_HEADS
RET_HEADS = 4
RET_KEY_DIM = D_MODEL // RET_HEADS
RET_VAL_DIM = 2 * D_MODEL // RET_HEADS
ROPE_BASE = 10000.0
N_GROUPS = 4
EXPERTS_PER_GROUP = 4
N_EXPERTS = N_GROUPS * EXPERTS_PER_GROUP
D_EXPERT = 512
DEPTH = 2
DEEPNORM_ALPHA = (2.0 * DEPTH) ** 0.25
LN_EPS = 1e-5

LANES = 128
EXT = LANES
ROW_W = D_MODEL + EXT
GIDX_LANE = 0
CW_LANE0 = 4
NEG_BIG = -1e30
LOG2E = 1.4426950408889634
VMEM_LIMIT = 56 * 1024 * 1024

TM_PROJ = 1024
TS_GATE = 512
TQ_ATTN = 512
ATTN_TILES_PER_STEP = 2
ATTN_ROW_BLOCK = 128
TM_OUT = 1024
TM_ROWS = 1024
TM_MOE = 1024
RET_CHUNK = 256


def _cparams(sem):
    return pltpu.CompilerParams(dimension_semantics=sem, vmem_limit_bytes=VMEM_LIMIT)


def _proj_kernel(a_ref, w_ref, o_ref, *, scale0):
    acc = jnp.dot(a_ref[...].astype(BF16), w_ref[...].astype(BF16), preferred_element_type=F32)
    o_ref[...] = (acc * jnp.where(pl.program_id(1) == 0, scale0, 1.0)).astype(o_ref.dtype)


def _proj(a, w3, n, tn, scale0):
    m, k = a.shape
    tm = min(TM_PROJ, m)
    return pl.pallas_call(
        functools.partial(_proj_kernel, scale0=scale0),
        out_shape=jax.ShapeDtypeStruct((m, n), BF16),
        grid=(m // tm, n // tn),
        in_specs=[pl.BlockSpec((tm, k), lambda i, j: (i, 0)),
                  pl.BlockSpec((None, k, tn), lambda i, j: (0, 0, j))],
        out_specs=pl.BlockSpec((tm, tn), lambda i, j: (i, j)),
        compiler_params=_cparams(("parallel", "arbitrary")),
        name="proj",
    )(a, w3)


def _split_bf16(x):
    hi = x.astype(BF16)
    return hi, (x - hi.astype(F32)).astype(BF16)


def _split_weight(w):
    hi, lo = _split_bf16(w)
    return jnp.concatenate([hi, lo], axis=1), hi


def _dot_split(x, w_hilo, w_hi):
    x_hi, x_lo = _split_bf16(x)
    r = jnp.dot(x_hi, w_hilo, preferred_element_type=F32)
    return r[:, :LANES] + r[:, LANES:] + jnp.dot(x_lo, w_hi, preferred_element_type=F32)


def _fgate_kernel(x_ref, whl_ref, wh_ref, b_ref, c_ref, carry_ref):
    @pl.when(pl.program_id(1) == 0)
    def _():
        carry_ref[...] = jnp.zeros_like(carry_ref)

    ts = x_ref.shape[0]
    z = _dot_split(x_ref[...], whl_ref[...], wh_ref[...]) + b_ref[...]
    logf = (jnp.minimum(z, 0.0) - jnp.log1p(jnp.exp(-jnp.abs(z)))) * LOG2E
    l1 = logf.astype(BF16)
    rest = logf - l1.astype(F32)
    l2, l3 = _split_bf16(rest)
    row = lax.broadcasted_iota(jnp.int32, (ts, ts), 0)
    col = lax.broadcasted_iota(jnp.int32, (ts, ts), 1)
    tri = jnp.where(row >= col, 1.0, 0.0).astype(BF16)
    parts = jnp.dot(tri, jnp.concatenate([l1, l2, l3], axis=1), preferred_element_type=F32)
    c = parts[:, :LANES] + parts[:, LANES:2 * LANES] + parts[:, 2 * LANES:] + carry_ref[...]
    c_ref[...] = c[:, :c_ref.shape[1]]
    carry_ref[...] = c[ts - 1:ts, :]


def _fgate(x2d, w_f, b_f, batch, seq):
    ts = min(TS_GATE, seq)
    ns = seq // ts
    h = w_f.shape[1]
    w_hilo, w_hi = _split_weight(jnp.pad(w_f, ((0, 0), (0, LANES - h))))
    b_pad = jnp.pad(b_f, ((0, 0), (0, LANES - h)))
    return pl.pallas_call(
        _fgate_kernel,
        out_shape=jax.ShapeDtypeStruct((batch * seq, h), F32),
        grid=(batch, ns),
        in_specs=[pl.BlockSpec((ts, D_MODEL), lambda b, s: (b * ns + s, 0)),
                  pl.BlockSpec((D_MODEL, 2 * LANES), lambda b, s: (0, 0)),
                  pl.BlockSpec((D_MODEL, LANES), lambda b, s: (0, 0)),
                  pl.BlockSpec((1, LANES), lambda b, s: (0, 0))],
        out_specs=pl.BlockSpec((ts, h), lambda b, s: (b * ns + s, 0)),
        scratch_shapes=[pltpu.VMEM((1, LANES), F32)],
        compiler_params=_cparams(("parallel", "arbitrary")),
        name="fgate",
    )(x2d, w_hilo, w_hi, b_pad)


def _fox_attn_kernel(q_ref, k_ref, v_ref, cq_ref, ck_ref, o_ref, m_ref, acc_ref, sa_ref, sb_ref, *, tq, nt):
    i = pl.program_id(2)
    lane = lax.broadcasted_iota(jnp.int32, (tq, LANES), 1)
    row = lax.broadcasted_iota(jnp.int32, (tq, tq), 0)
    col = lax.broadcasted_iota(jnp.int32, (tq, tq), 1)
    causal = row >= col
    head0 = lane < FOX_HEAD_DIM
    m_ref[...] = jnp.full(m_ref.shape, NEG_BIG, F32)
    acc_ref[...] = jnp.zeros(acc_ref.shape, F32)
    lane_k = lax.broadcasted_iota(jnp.int32, (tq, LANES), 1)
    lane_h = lax.broadcasted_iota(jnp.int32, (tq, FOX_HEADS), 1)
    q2 = []
    cq_heads = []
    for tile in range(nt):
        q = q_ref[tile * tq:(tile + 1) * tq, :]
        q2.append(jnp.concatenate([jnp.where(head0, q, jnp.zeros_like(q)),
                                   jnp.where(head0, jnp.zeros_like(q), q)], axis=0))
        c_tok = cq_ref[tile * tq:(tile + 1) * tq, :]
        cq_heads.append([jnp.sum(jnp.where(lane_h == 2 * pl.program_id(1) + h, c_tok, 0.0), axis=1, keepdims=True)
                         for h in range(2)])
    rb = ATTN_ROW_BLOCK

    def scores(tile, kb, s_ref):
        ks = pl.multiple_of(kb * tq, tq)
        s_ref[...] = lax.dot_general(q2[tile], k_ref[pl.ds(ks, tq), :], (((1,), (1,)), ((), ())),
                                     preferred_element_type=F32)

    def softmax_pv(tile, kb, s_ref, masked):
        ks = pl.multiple_of(kb * tq, tq)
        vblk = v_ref[pl.ds(ks, tq), :]
        for h in range(2):
            ck = ck_ref[h, :, pl.ds(ks, tq)]
            vh = jnp.where(lane_k < FOX_HEAD_DIM if h == 0 else lane_k >= FOX_HEAD_DIM,
                           vblk, jnp.ones_like(vblk))
            p_blocks = []
            m_blocks = []
            cq_all = cq_heads[tile][h]
            m_all = m_ref[tile, h]
            for r0 in range(0, tq, rb):
                t = s_ref[h * tq + r0:h * tq + r0 + rb, :] - ck
                if masked:
                    t = jnp.where(causal[r0:r0 + rb, :], t, NEG_BIG)
                cq = cq_all[r0:r0 + rb, :]
                m_new = jnp.maximum(m_all[r0:r0 + rb, :], cq + jnp.max(t, axis=1, keepdims=True))
                m_blocks.append(m_new)
                p_blocks.append(jnp.exp2(t + (cq - m_new)).astype(BF16))
            m_new_all = jnp.concatenate(m_blocks, axis=0)
            pv = jnp.dot(jnp.concatenate(p_blocks, axis=0), vh, preferred_element_type=F32)
            acc_ref[tile, h] = jnp.exp2(m_all - m_new_all) * acc_ref[tile, h] + pv
            m_ref[tile, h] = m_new_all

    bufs = (sa_ref, sb_ref)
    scores(0, 0, bufs[0])
    par = 0
    for u in range(nt):
        g = nt * i + u
        lo = u % 2
        if lo:
            scores(u, 1, bufs[1 - par])
            softmax_pv(u, 0, bufs[par], False)
            par = 1 - par
        cur, oth = bufs[par], bufs[1 - par]

        def pair(j, _, u=u, lo=lo, cur=cur, oth=oth):
            kb = lo + 2 * j
            scores(u, kb + 1, oth)
            softmax_pv(u, kb, cur, False)
            scores(u, kb + 2, cur)
            softmax_pv(u, kb + 1, oth, False)
            return 0

        lax.fori_loop(0, (g - lo) // 2, pair, 0)
        if u + 1 < nt:
            scores(u + 1, 0, oth)
        softmax_pv(u, g, cur, True)
        par = 1 - par

    for tile in range(nt):
        a0 = acc_ref[tile, 0]
        a1 = acc_ref[tile, 1]
        o0 = a0 / pltpu.roll(a0, FOX_HEAD_DIM, 1)
        o1 = a1 / pltpu.roll(a1, FOX_HEAD_DIM, 1)
        o_ref[tile * tq:(tile + 1) * tq, :] = jnp.where(head0, o0, o1).astype(o_ref.dtype)


def _fox_attention(qkv, cq, ck, batch, seq):
    nt = min(ATTN_TILES_PER_STEP, seq // min(TQ_ATTN, seq // 2))
    tq = min(TQ_ATTN, seq // nt)
    assert nt % 2 == 0 and seq % (nt * tq) == 0, "a grid step covers an even number of query tiles"
    ns = seq // (nt * tq)
    hp = FOX_HEADS // 2
    return pl.pallas_call(
        functools.partial(_fox_attn_kernel, tq=tq, nt=nt),
        out_shape=jax.ShapeDtypeStruct((batch * seq, D_MODEL), BF16),
        grid=(batch, hp, ns),
        in_specs=[pl.BlockSpec((nt * tq, LANES), lambda b, p, i: (b * ns + i, p)),
                  pl.BlockSpec((seq, LANES), lambda b, p, i: (b, hp + p)),
                  pl.BlockSpec((seq, LANES), lambda b, p, i: (b, 2 * hp + p)),
                  pl.BlockSpec((nt * tq, FOX_HEADS), lambda b, p, i: (b * ns + i, 0)),
                  pl.BlockSpec((None, 2, 1, seq), lambda b, p, i: (b, p, 0, 0))],
        out_specs=pl.BlockSpec((nt * tq, LANES), lambda b, p, i: (b * ns + i, p)),
        scratch_shapes=[pltpu.VMEM((nt, 2, tq, 1), F32), pltpu.VMEM((nt, 2, tq, LANES), F32),
                        pltpu.VMEM((2 * tq, tq), F32), pltpu.VMEM((2 * tq, tq), F32)],
        compiler_params=_cparams(("parallel", "parallel", "arbitrary")),
        name="fox_attn",
    )(qkv, qkv, qkv, cq, ck)


def _rope_kernel(pos_ref, freq_ref, cos_ref, sin_ref):
    ang = pos_ref[...] * freq_ref[...]
    cos_ref[...] = jnp.cos(ang)
    sin_ref[...] = jnp.sin(ang)


def _rope_tables(pos_f, inv_freq):
    t = pos_f.shape[0]
    half = inv_freq.shape[1]
    tm = min(1024, t)
    return pl.pallas_call(
        _rope_kernel,
        out_shape=(jax.ShapeDtypeStruct((t, half), F32), jax.ShapeDtypeStruct((t, half), F32)),
        grid=(t // tm,),
        in_specs=[pl.BlockSpec((tm, 1), lambda i: (i, 0)),
                  pl.BlockSpec((1, half), lambda i: (0, 0))],
        out_specs=(pl.BlockSpec((tm, half), lambda i: (i, 0)),
                   pl.BlockSpec((tm, half), lambda i: (i, 0))),
        compiler_params=_cparams(("parallel",)),
        name="rope_tables",
    )(pos_f, inv_freq)


def _ret_proj_kernel(a_ref, w_ref, cos_ref, sin_ref, o_ref):
    j = pl.program_id(1)
    a = a_ref[...].astype(BF16)
    acc = jnp.dot(a, w_ref[...].astype(BF16), preferred_element_type=F32)

    @pl.when(j < 2)
    def _():
        key_scale = jnp.where(j == 1, RET_KEY_DIM ** -0.5, 1.0)
        c = cos_ref[...] * key_scale
        s = sin_ref[...] * key_scale
        half = RET_KEY_DIM // 2
        for h in range(RET_HEADS):
            x1 = acc[:, h * RET_KEY_DIM:h * RET_KEY_DIM + half]
            x2 = acc[:, h * RET_KEY_DIM + half:(h + 1) * RET_KEY_DIM]
            o_ref[:, h * RET_KEY_DIM:h * RET_KEY_DIM + half] = (x1 * c - x2 * s).astype(o_ref.dtype)
            o_ref[:, h * RET_KEY_DIM + half:(h + 1) * RET_KEY_DIM] = (x2 * c + x1 * s).astype(o_ref.dtype)

    @pl.when(j >= 2)
    def _():
        o_ref[...] = acc.astype(o_ref.dtype)


def _ret_proj(x2d, w3, cos, sin):
    t = x2d.shape[0]
    n = w3.shape[2]
    tm = min(TM_PROJ, t)
    tn = D_MODEL
    half = RET_KEY_DIM // 2
    return pl.pallas_call(
        _ret_proj_kernel,
        out_shape=jax.ShapeDtypeStruct((t, n), BF16),
        grid=(t // tm, n // tn),
        in_specs=[pl.BlockSpec((tm, D_MODEL), lambda i, j: (i, 0)),
                  pl.BlockSpec((None, D_MODEL, tn), lambda i, j: (0, 0, j)),
                  pl.BlockSpec((tm, half), lambda i, j: (i, 0)),
                  pl.BlockSpec((tm, half), lambda i, j: (i, 0))],
        out_specs=pl.BlockSpec((tm, tn), lambda i, j: (i, j)),
        compiler_params=_cparams(("parallel", "arbitrary")),
        name="ret_proj",
    )(x2d, w3, cos, sin)


def _retention_kernel(lg_ref, q_ref, k_ref, v_ref, g_ref, o_ref, state_ref, decay_ref, *, chunk):
    dk, dv = RET_KEY_DIM, RET_VAL_DIM

    @pl.when(pl.program_id(1) == 0)
    def _():
        state_ref[...] = jnp.zeros_like(state_ref)
        row = lax.broadcasted_iota(jnp.int32, (chunk, chunk), 0)
        col = lax.broadcasted_iota(jnp.int32, (chunk, chunk), 1)
        diff = (row - col).astype(F32)
        for h in range(RET_HEADS):
            decay_ref[h] = jnp.where(diff >= 0, jnp.exp(jnp.maximum(diff, 0.0) * lg_ref[h]), 0.0)

    idx = lax.broadcasted_iota(jnp.int32, (chunk, 1), 0).astype(F32)
    for h in range(RET_HEADS):
        lg = lg_ref[h]
        q = q_ref[:, h * dk:(h + 1) * dk]
        k = k_ref[:, h * dk:(h + 1) * dk]
        v = v_ref[:, h * dv:(h + 1) * dv]
        scores = lax.dot_general(q, k, (((1,), (1,)), ((), ())), preferred_element_type=F32) * decay_ref[h]
        o = jnp.dot(scores.astype(BF16), v, preferred_element_type=F32)
        state = state_ref[h]
        qd = (q.astype(F32) * jnp.exp((idx + 1.0) * lg)).astype(BF16)
        o = o + jnp.dot(qd, state.astype(BF16), preferred_element_type=F32)
        kd = (k.astype(F32) * jnp.exp((chunk - 1.0 - idx) * lg)).astype(BF16)
        state_ref[h] = state * jnp.exp(chunk * lg) + lax.dot_general(
            kd, v, (((0,), (0,)), ((), ())), preferred_element_type=F32)
        mu = jnp.mean(o, axis=1, keepdims=True)
        oc = o - mu
        var = jnp.mean(oc * oc, axis=1, keepdims=True)
        on = oc * lax.rsqrt(var + LN_EPS)
        g = g_ref[:, h * dv:(h + 1) * dv].astype(F32)
        o_ref[:, h * dv:(h + 1) * dv] = (g * jax.nn.sigmoid(g) * on).astype(o_ref.dtype)


def _retention(proj, log_gamma, batch, seq):
    chunk = min(RET_CHUNK, seq)
    nc = seq // chunk
    dq = RET_HEADS * RET_KEY_DIM
    dvt = RET_HEADS * RET_VAL_DIM
    grid_spec = pltpu.PrefetchScalarGridSpec(
        num_scalar_prefetch=1,
        grid=(batch, nc),
        in_specs=[pl.BlockSpec((chunk, dq), lambda b, c, lg: (b * nc + c, 0)),
                  pl.BlockSpec((chunk, dq), lambda b, c, lg: (b * nc + c, 1)),
                  pl.BlockSpec((chunk, dvt), lambda b, c, lg: (b * nc + c, 1)),
                  pl.BlockSpec((chunk, dvt), lambda b, c, lg: (b * nc + c, 2))],
        out_specs=pl.BlockSpec((chunk, dvt), lambda b, c, lg: (b * nc + c, 0)),
        scratch_shapes=[pltpu.VMEM((RET_HEADS, RET_KEY_DIM, RET_VAL_DIM), F32),
                        pltpu.VMEM((RET_HEADS, chunk, chunk), F32)],
    )
    return pl.pallas_call(
        functools.partial(_retention_kernel, chunk=chunk),
        out_shape=jax.ShapeDtypeStruct((batch * seq, dvt), BF16),
        grid_spec=grid_spec,
        compiler_params=_cparams(("parallel", "arbitrary")),
        name="retention",
    )(log_gamma, proj, proj, proj, proj)


def _layer_norm(y, g, b):
    mu = jnp.mean(y, axis=1, keepdims=True)
    yc = y - mu
    var = jnp.mean(yc * yc, axis=1, keepdims=True)
    return yc * lax.rsqrt(var + LN_EPS) * g + b


def _out_router_kernel(a_ref, w_ref, x_ref, g_ref, b_ref, wrhl_ref, wrh_ref, br_ref, o_ref, ext_ref):
    tm = a_ref.shape[0]
    h = jnp.dot(a_ref[...], w_ref[...], preferred_element_type=F32)
    x1 = _layer_norm(DEEPNORM_ALPHA * x_ref[...] + h, g_ref[...], b_ref[...])
    o_ref[:, :D_MODEL] = x1

    logits = _dot_split(x1, wrhl_ref[...], wrh_ref[...]) + br_ref[...]
    lane = lax.broadcasted_iota(jnp.int32, (tm, LANES), 1)

    def first_argmax(vals, vmax):
        return jnp.min(jnp.where(vals == vmax, lane, LANES), axis=1, keepdims=True)

    gl = jnp.where(lane < N_GROUPS, logits, NEG_BIG)
    gmax = jnp.max(gl, axis=1, keepdims=True)
    gsum = jnp.sum(jnp.where(lane < N_GROUPS, jnp.exp(gl - gmax), 0.0), axis=1, keepdims=True)
    g_val = 1.0 / gsum
    g_idx = first_argmax(gl, gmax)
    lo = CW_LANE0 + EXPERTS_PER_GROUP * g_idx
    in_group = (lane >= lo) & (lane < lo + EXPERTS_PER_GROUP)
    el = jnp.where(in_group, logits, NEG_BIG)
    e1 = jnp.max(el, axis=1, keepdims=True)
    i1 = first_argmax(el, e1)
    el2 = jnp.where(lane == i1, NEG_BIG, el)
    e2 = jnp.max(el2, axis=1, keepdims=True)
    i2 = first_argmax(el2, e2)
    r = jnp.exp(e2 - e1)
    w1 = g_val / (1.0 + r)
    w2 = g_val * r / (1.0 + r)
    ext = jnp.where(lane == i1, w1, 0.0) + jnp.where(lane == i2, w2, 0.0)
    ext = jnp.where(lane == GIDX_LANE, g_idx.astype(F32), ext)
    o_ref[:, D_MODEL:] = ext
    ext_ref[...] = ext


def _out_router(a, w_out, x2d, ln_g, ln_b, w_r, b_r):
    t, din = a.shape
    tm = min(TM_OUT, t)
    w_r_hilo, w_r_hi = _split_weight(w_r)
    return pl.pallas_call(
        _out_router_kernel,
        out_shape=(jax.ShapeDtypeStruct((t, ROW_W), F32), jax.ShapeDtypeStruct((t, EXT), F32)),
        grid=(t // tm,),
        in_specs=[pl.BlockSpec((tm, din), lambda i: (i, 0)),
                  pl.BlockSpec((din, D_MODEL), lambda i: (0, 0)),
                  pl.BlockSpec((tm, D_MODEL), lambda i: (i, 0)),
                  pl.BlockSpec((1, D_MODEL), lambda i: (0, 0)),
                  pl.BlockSpec((1, D_MODEL), lambda i: (0, 0)),
                  pl.BlockSpec((D_MODEL, 2 * LANES), lambda i: (0, 0)),
                  pl.BlockSpec((D_MODEL, LANES), lambda i: (0, 0)),
                  pl.BlockSpec((1, LANES), lambda i: (0, 0))],
        out_specs=(pl.BlockSpec((tm, ROW_W), lambda i: (i, 0)), pl.BlockSpec((tm, EXT), lambda i: (i, 0))),
        compiler_params=_cparams(("parallel",)),
        name="out_router",
    )(a, w_out, x2d, ln_g, ln_b, w_r_hilo, w_r_hi, b_r)


def _row_copy(src_ref, src_row, dst_ref, dst_row, sem):
    return pltpu.make_async_copy(src_ref.at[pl.ds(src_row, 1)], dst_ref.at[pl.ds(dst_row, 1)], sem)


def _dispatch_kernel(pos_ref, x_ref, init_ref, xs_ref, sem):
    del init_ref
    tm = x_ref.shape[0]
    base = pl.program_id(0) * tm

    for r in range(tm):
        _row_copy(x_ref, r, xs_ref, pos_ref[base + r], sem).start()
    pltpu.make_async_copy(x_ref, xs_ref.at[pl.ds(0, tm)], sem).wait()


def _dispatch(pos, x1ext, n_rows):
    t = x1ext.shape[0]
    tm = min(TM_ROWS, t)
    grid_spec = pltpu.PrefetchScalarGridSpec(
        num_scalar_prefetch=1,
        grid=(t // tm,),
        in_specs=[pl.BlockSpec((tm, ROW_W), lambda i, pos: (i, 0)),
                  pl.BlockSpec(memory_space=pl.ANY)],
        out_specs=pl.BlockSpec(memory_space=pl.ANY),
        scratch_shapes=[pltpu.SemaphoreType.DMA(())],
    )
    return pl.pallas_call(
        _dispatch_kernel,
        out_shape=jax.ShapeDtypeStruct((n_rows, ROW_W), F32),
        grid_spec=grid_spec,
        input_output_aliases={2: 0},
        compiler_params=_cparams(("arbitrary",)),
        name="dispatch",
    )(pos, x1ext, jnp.zeros((n_rows, ROW_W), F32))


def _moe_kernel(tg_ref, xs_ref, wg_ref, wu_ref, wd_ref, o_ref, xb_ref, acc_ref):
    i = pl.program_id(0)
    e = pl.program_id(1)
    tm = xs_ref.shape[0]
    group = tg_ref[i]
    valid = group < N_GROUPS

    @pl.when(valid)
    def _():
        @pl.when(e == 0)
        def _():
            xb_ref[...] = xs_ref[:, :D_MODEL].astype(BF16)
            acc_ref[...] = jnp.zeros_like(acc_ref)

        lane = lax.broadcasted_iota(jnp.int32, (tm, EXT), 1)
        cw_lane = CW_LANE0 + EXPERTS_PER_GROUP * group + e
        cw = jnp.sum(jnp.where(lane == cw_lane, xs_ref[:, D_MODEL:], 0.0), axis=1, keepdims=True)
        xb = xb_ref[...]
        hg = jnp.dot(xb, wg_ref[...].astype(BF16), preferred_element_type=F32)
        hu = jnp.dot(xb, wu_ref[...].astype(BF16), preferred_element_type=F32)
        hh = (hg * jax.nn.sigmoid(hg) * hu * cw).astype(BF16)
        acc_ref[...] += jnp.dot(hh, wd_ref[...].astype(BF16), preferred_element_type=F32)

        @pl.when(e == EXPERTS_PER_GROUP - 1)
        def _():
            o_ref[...] = DEEPNORM_ALPHA * xs_ref[:, :D_MODEL] + acc_ref[...]

    @pl.when(jnp.logical_not(valid) & (e == EXPERTS_PER_GROUP - 1))
    def _():
        o_ref[...] = jnp.zeros_like(o_ref)


def _moe(tile_group, xs, w_gate, w_up, w_down, layer):
    n_rows = xs.shape[0]
    tm = TM_MOE
    epg = EXPERTS_PER_GROUP

    def expert(i, e, tg):
        return (layer, jnp.where(tg[i] < N_GROUPS, tg[i] * epg + e, N_EXPERTS - 1), 0, 0)

    grid_spec = pltpu.PrefetchScalarGridSpec(
        num_scalar_prefetch=1,
        grid=(n_rows // tm, epg),
        in_specs=[pl.BlockSpec((tm, ROW_W), lambda i, e, tg: (i, 0)),
                  pl.BlockSpec((None, None, D_MODEL, D_EXPERT), expert),
                  pl.BlockSpec((None, None, D_MODEL, D_EXPERT), expert),
                  pl.BlockSpec((None, None, D_EXPERT, D_MODEL), expert)],
        out_specs=pl.BlockSpec((tm, D_MODEL), lambda i, e, tg: (i, 0)),
        scratch_shapes=[pltpu.VMEM((tm, D_MODEL), BF16), pltpu.VMEM((tm, D_MODEL), F32)],
    )
    return pl.pallas_call(
        _moe_kernel,
        out_shape=jax.ShapeDtypeStruct((n_rows, D_MODEL), F32),
        grid_spec=grid_spec,
        compiler_params=_cparams(("parallel", "arbitrary")),
        name="moe",
    )(tile_group, xs, w_gate, w_up, w_down)


def _combine_ple_kernel(pos_ref, x2s_ref, p_ref, g_ref, b_ref, wg_ref, bg_ref, wp_ref, o_ref, buf_ref, sem):
    tm = o_ref.shape[0]
    i = pl.program_id(0)
    last = pl.num_programs(0) - 1
    slot = i % 2

    def wait_slot(s):
        pltpu.make_async_copy(x2s_ref.at[pl.ds(0, tm)], buf_ref.at[s], sem.at[s]).wait()

    @pl.when(i == 0)
    def _():
        def issue(r, _):
            _row_copy(x2s_ref, pos_ref[r], buf_ref.at[0], r, sem.at[0]).start()
            return 0

        lax.fori_loop(0, tm, issue, 0, unroll=8)

    wait_slot(slot)
    base = jnp.minimum(i + 1, last) * tm
    for r in range(tm):
        _row_copy(x2s_ref, pos_ref[base + r], buf_ref.at[1 - slot], r, sem.at[1 - slot]).start()
    x2 = _layer_norm(buf_ref[slot], g_ref[...], b_ref[...])
    gate = jax.nn.sigmoid(jnp.dot(x2.astype(BF16), wg_ref[...], preferred_element_type=F32) + bg_ref[...])
    pp = jnp.dot(p_ref[...].astype(BF16), wp_ref[...], preferred_element_type=F32)
    o_ref[...] = x2 + gate * pp

    @pl.when(i == last)
    def _():
        wait_slot(1 - slot)


def _combine_ple(pos, x2s, p_all, layer, ln_g, ln_b, w_gate, b_gate, w_proj):
    t = pos.shape[0]
    tm = min(TM_ROWS, t)
    steps = t // tm
    grid_spec = pltpu.PrefetchScalarGridSpec(
        num_scalar_prefetch=1,
        grid=(steps,),
        in_specs=[pl.BlockSpec(memory_space=pl.ANY),
                  pl.BlockSpec((tm, PLE_DIM), lambda i, pos: (layer * steps + i, 0)),
                  pl.BlockSpec((1, D_MODEL), lambda i, pos: (0, 0)),
                  pl.BlockSpec((1, D_MODEL), lambda i, pos: (0, 0)),
                  pl.BlockSpec((D_MODEL, D_MODEL), lambda i, pos: (0, 0)),
                  pl.BlockSpec((1, D_MODEL), lambda i, pos: (0, 0)),
                  pl.BlockSpec((PLE_DIM, D_MODEL), lambda i, pos: (0, 0))],
        out_specs=pl.BlockSpec((tm, D_MODEL), lambda i, pos: (i, 0)),
        scratch_shapes=[pltpu.VMEM((2, tm, D_MODEL), F32), pltpu.SemaphoreType.DMA((2,))],
    )
    return pl.pallas_call(
        _combine_ple_kernel,
        out_shape=jax.ShapeDtypeStruct((t, D_MODEL), F32),
        grid_spec=grid_spec,
        compiler_params=_cparams(("arbitrary",)),
        name="combine_ple",
    )(pos, x2s, p_all, ln_g, ln_b, w_gate, b_gate, w_proj)


def _sorted_positions(g_idx, n_tiles):
    onehot = (g_idx[:, None] == jnp.arange(N_GROUPS, dtype=jnp.int32)[None, :]).astype(jnp.int32)
    counts = jnp.sum(onehot, axis=0)
    tiles = (counts + TM_MOE - 1) // TM_MOE
    tile_end = jnp.cumsum(tiles)
    tile_start = tile_end - tiles
    rank = jnp.sum((jnp.cumsum(onehot, axis=0) - onehot) * onehot, axis=1)
    pos = jnp.sum(onehot * (tile_start * TM_MOE)[None, :], axis=1) + rank
    tile_ids = jnp.arange(n_tiles, dtype=jnp.int32)
    tile_group = jnp.sum((tile_ids[:, None] >= tile_end[None, :]).astype(jnp.int32), axis=1)
    return pos.astype(jnp.int32), tile_group.astype(jnp.int32)


def _moe_block(a, w_out, x2d, p_all, ln1_g, ln1_b, ln2_g, ln2_b, w_group, b_group, w_router, b_router,
               w_gate, w_up, w_down, ple_w_proj, ple_w_gate, ple_b_gate, layer):
    t = x2d.shape[0]
    n_tiles = t // TM_MOE + N_GROUPS
    pad = LANES - CW_LANE0 - N_EXPERTS
    w_r = jnp.concatenate([w_group, w_router, jnp.zeros((D_MODEL, pad), F32)], axis=1)
    b_r = jnp.concatenate([b_group, b_router, jnp.zeros((pad,), F32)])[None, :]
    x1ext, ext = _out_router(a, w_out.astype(BF16), x2d, ln1_g[None, :], ln1_b[None, :], w_r, b_r)
    g_idx = ext[:, GIDX_LANE].astype(jnp.int32)
    pos, tile_group = _sorted_positions(g_idx, n_tiles)
    xs = _dispatch(pos, x1ext, n_tiles * TM_MOE)
    x2s = _moe(tile_group, xs, w_gate, w_up, w_down, layer)
    return _combine_ple(pos, x2s, p_all, layer, ln2_g[None, :], ln2_b[None, :], ple_w_gate.astype(BF16),
                        ple_b_gate[None, :], ple_w_proj.astype(BF16))


@jax.jit
def _forward(x, p, positions, fox_w_in, fox_b_f, fox_w_out, ret_w_in, ret_w_out, ln1_g, ln1_b, ln2_g, ln2_b,
             moe_w_group, moe_b_group, moe_w_router, moe_b_router, moe_w_gate, moe_w_up, moe_w_down,
             ple_w_proj, ple_w_gate, ple_b_gate):
    batch, seq, d = x.shape
    t = batch * seq
    x2d = x.reshape(t, d)
    p_all = p.reshape(DEPTH * t, PLE_DIM)

    def moe_args(i):
        return (ln1_g[i], ln1_b[i], ln2_g[i], ln2_b[i], moe_w_group[i], moe_b_group[i], moe_w_router[i],
                moe_b_router[i], moe_w_gate, moe_w_up, moe_w_down, ple_w_proj[i], ple_w_gate[i],
                ple_b_gate[i], i)

    w_in = fox_w_in[0]
    qkv = _proj(x2d, fox_w_in, 3 * D_MODEL, D_MODEL, FOX_HEAD_DIM ** -0.5 * LOG2E)
    c = _fgate(x2d, w_in[:, 3 * D_MODEL:], fox_b_f[0][None, :], batch, seq)
    c3 = c.reshape(batch, seq, FOX_HEADS).transpose(0, 2, 1)
    attn = _fox_attention(qkv, c, c3[:, :, None, :], batch, seq)
    x2d = _moe_block(attn, fox_w_out[0], x2d, p_all, *moe_args(0))

    inv_freq = ROPE_BASE ** (-jnp.arange(0, RET_KEY_DIM, 2, dtype=F32) / RET_KEY_DIM)
    cos, sin = _rope_tables(positions.astype(F32).reshape(t, 1), inv_freq[None, :])
    proj = _ret_proj(x2d, ret_w_in, cos, sin)
    log_gamma = jnp.log(1.0 - 2.0 ** (-5.0 - jnp.arange(RET_HEADS, dtype=F32)))
    ret = _retention(proj, log_gamma, batch, seq)
    x2d = _moe_block(ret, ret_w_out[0], x2d, p_all, *moe_args(1))
    return x2d.reshape(batch, seq, d)


def kernel(x, p, positions, fox_w_in, fox_b_f, fox_w_out, ret_w_in, ret_w_out, ln1_g, ln1_b, ln2_g, ln2_b,
           moe_w_group, moe_b_group, moe_w_router, moe_b_router, moe_w_gate, moe_w_up, moe_w_down,
           ple_w_proj, ple_w_gate, ple_b_gate):
    return _forward(x, p, positions, fox_w_in, fox_b_f, fox_w_out, ret_w_in, ret_w_out, ln1_g, ln1_b,
                    ln2_g, ln2_b, moe_w_group, moe_b_group, moe_w_router, moe_b_router, moe_w_gate,
                    moe_w_up, moe_w_down, ple_w_proj, ple_w_gate, ple_b_gate)
```

```python
import functools

import jax
import jax.numpy as jnp
from jax import lax
from jax.experimental import pallas as pl
from jax.experimental.pallas import tpu as pltpu

F32 = jnp.float32
BF16 = jnp.bfloat16

D_MODEL = 1024
PLE_DIM = 256
FOX_HEADS = 16
FOX_HEAD_DIM = D_MODEL // FOX_HEADS
RET_HEADS = 4
RET_KEY_DIM = D_MODEL // RET_HEADS
RET_VAL_DIM = 2 * D_MODEL // RET_HEADS
ROPE_BASE = 10000.0
N_GROUPS = 4
EXPERTS_PER_GROUP = 4
N_EXPERTS = N_GROUPS * EXPERTS_PER_GROUP
D_EXPERT = 512
DEPTH = 2
DEEPNORM_ALPHA = (2.0 * DEPTH) ** 0.25
LN_EPS = 1e-5

LANES = 128
EXT = LANES
ROW_W = D_MODEL + EXT
GIDX_LANE = 0
CW_LANE0 = 4
NEG_BIG = -1e30
LOG2E = 1.4426950408889634
VMEM_LIMIT = 56 * 1024 * 1024

TM_PROJ = 1024
TS_GATE = 512
TQ_ATTN = 512
ATTN_TILES_PER_STEP = 2
ATTN_ROW_BLOCK = 128
TM_OUT = 1024
TM_ROWS = 1024
TM_MOE = 1024
RET_CHUNK = 256


def _cparams(sem):
    return pltpu.CompilerParams(dimension_semantics=sem, vmem_limit_bytes=VMEM_LIMIT)


def _proj_kernel(a_ref, w_ref, o_ref):
    a = a_ref[...].astype(BF16)
    o_ref[...] = jnp.dot(a, w_ref[...], preferred_element_type=F32).astype(o_ref.dtype)


def _proj(a, w, tn):
    m, k = a.shape
    n = w.shape[1]
    tm = min(TM_PROJ, m)
    return pl.pallas_call(
        _proj_kernel,
        out_shape=jax.ShapeDtypeStruct((m, n), BF16),
        grid=(m // tm, n // tn),
        in_specs=[pl.BlockSpec((tm, k), lambda i, j: (i, 0)),
                  pl.BlockSpec((k, tn), lambda i, j: (0, j))],
        out_specs=pl.BlockSpec((tm, tn), lambda i, j: (i, j)),
        compiler_params=_cparams(("parallel", "arbitrary")),
        name="proj",
    )(a, w)


def _split_bf16(x):
    hi = x.astype(BF16)
    return hi, (x - hi.astype(F32)).astype(BF16)


def _split_weight(w):
    hi, lo = _split_bf16(w)
    return jnp.concatenate([hi, lo], axis=1), hi


def _dot_split(x, w_hilo, w_hi):
    x_hi, x_lo = _split_bf16(x)
    r = jnp.dot(x_hi, w_hilo, preferred_element_type=F32)
    return r[:, :LANES] + r[:, LANES:] + jnp.dot(x_lo, w_hi, preferred_element_type=F32)


def _fgate_kernel(x_ref, whl_ref, wh_ref, b_ref, c_ref, carry_ref):
    @pl.when(pl.program_id(1) == 0)
    def _():
        carry_ref[...] = jnp.zeros_like(carry_ref)

    ts = x_ref.shape[0]
    z = _dot_split(x_ref[...], whl_ref[...], wh_ref[...]) + b_ref[...]
    logf = (jnp.minimum(z, 0.0) - jnp.log1p(jnp.exp(-jnp.abs(z)))) * LOG2E
    l1 = logf.astype(BF16)
    rest = logf - l1.astype(F32)
    l2, l3 = _split_bf16(rest)
    row = lax.broadcasted_iota(jnp.int32, (ts, ts), 0)
    col = lax.broadcasted_iota(jnp.int32, (ts, ts), 1)
    tri = jnp.where(row >= col, 1.0, 0.0).astype(BF16)
    parts = jnp.dot(tri, jnp.concatenate([l1, l2, l3], axis=1), preferred_element_type=F32)
    c = parts[:, :LANES] + parts[:, LANES:2 * LANES] + parts[:, 2 * LANES:] + carry_ref[...]
    c_ref[...] = c[:, :c_ref.shape[1]]
    carry_ref[...] = c[ts - 1:ts, :]


def _fgate(x2d, w_f, b_f, batch, seq):
    ts = min(TS_GATE, seq)
    ns = seq // ts
    h = w_f.shape[1]
    w_hilo, w_hi = _split_weight(jnp.pad(w_f, ((0, 0), (0, LANES - h))))
    b_pad = jnp.pad(b_f, ((0, 0), (0, LANES - h)))
    return pl.pallas_call(
        _fgate_kernel,
        out_shape=jax.ShapeDtypeStruct((batch * seq, h), F32),
        grid=(batch, ns),
        in_specs=[pl.BlockSpec((ts, D_MODEL), lambda b, s: (b * ns + s, 0)),
                  pl.BlockSpec((D_MODEL, 2 * LANES), lambda b, s: (0, 0)),
                  pl.BlockSpec((D_MODEL, LANES), lambda b, s: (0, 0)),
                  pl.BlockSpec((1, LANES), lambda b, s: (0, 0))],
        out_specs=pl.BlockSpec((ts, h), lambda b, s: (b * ns + s, 0)),
        scratch_shapes=[pltpu.VMEM((1, LANES), F32)],
        compiler_params=_cparams(("parallel", "arbitrary")),
        name="fgate",
    )(x2d, w_hilo, w_hi, b_pad)


def _fox_attn_kernel(q_ref, k_ref, v_ref, cq_ref, ck_ref, o_ref, m_ref, acc_ref, sa_ref, sb_ref, *, tq, nt):
    i = pl.program_id(2)
    lane = lax.broadcasted_iota(jnp.int32, (tq, LANES), 1)
    row = lax.broadcasted_iota(jnp.int32, (tq, tq), 0)
    col = lax.broadcasted_iota(jnp.int32, (tq, tq), 1)
    causal = row >= col
    head0 = lane < FOX_HEAD_DIM
    m_ref[...] = jnp.full(m_ref.shape, NEG_BIG, F32)
    acc_ref[...] = jnp.zeros(acc_ref.shape, F32)
    lane_k = lax.broadcasted_iota(jnp.int32, (tq, LANES), 1)
    lane_h = lax.broadcasted_iota(jnp.int32, (tq, FOX_HEADS), 1)
    q2 = []
    cq_heads = []
    for tile in range(nt):
        q = q_ref[tile * tq:(tile + 1) * tq, :]
        q2.append(jnp.concatenate([jnp.where(head0, q, jnp.zeros_like(q)),
                                   jnp.where(head0, jnp.zeros_like(q), q)], axis=0))
        c_tok = cq_ref[tile * tq:(tile + 1) * tq, :]
        cq_heads.append([jnp.sum(jnp.where(lane_h == 2 * pl.program_id(1) + h, c_tok, 0.0), axis=1, keepdims=True)
                         for h in range(2)])
    rb = ATTN_ROW_BLOCK

    def scores(tile, kb, s_ref):
        ks = pl.multiple_of(kb * tq, tq)
        s_ref[...] = lax.dot_general(q2[tile], k_ref[pl.ds(ks, tq), :], (((1,), (1,)), ((), ())),
                                     preferred_element_type=F32)

    def softmax_pv(tile, kb, s_ref, masked):
        ks = pl.multiple_of(kb * tq, tq)
        vblk = v_ref[pl.ds(ks, tq), :]
        for h in range(2):
            ck = ck_ref[h, :, pl.ds(ks, tq)]
            vh = jnp.where(lane_k < FOX_HEAD_DIM if h == 0 else lane_k >= FOX_HEAD_DIM,
                           vblk, jnp.ones_like(vblk))
            p_blocks = []
            m_blocks = []
            cq_all = cq_heads[tile][h]
            m_all = m_ref[tile, h]
            for r0 in range(0, tq, rb):
                t = s_ref[h * tq + r0:h * tq + r0 + rb, :] - ck
                if masked:
                    t = jnp.where(causal[r0:r0 + rb, :], t, NEG_BIG)
                cq = cq_all[r0:r0 + rb, :]
                m_new = jnp.maximum(m_all[r0:r0 + rb, :], cq + jnp.max(t, axis=1, keepdims=True))
                m_blocks.append(m_new)
                p_blocks.append(jnp.exp2(t + (cq - m_new)).astype(BF16))
            m_new_all = jnp.concatenate(m_blocks, axis=0)
            pv = jnp.dot(jnp.concatenate(p_blocks, axis=0), vh, preferred_element_type=F32)
            acc_ref[tile, h] = jnp.exp2(m_all - m_new_all) * acc_ref[tile, h] + pv
            m_ref[tile, h] = m_new_all

    bufs = (sa_ref, sb_ref)
    scores(0, 0, bufs[0])
    par = 0
    for u in range(nt):
        g = nt * i + u
        lo = u % 2
        if lo:
            scores(u, 1, bufs[1 - par])
            softmax_pv(u, 0, bufs[par], False)
            par = 1 - par
        cur, oth = bufs[par], bufs[1 - par]

        def pair(j, _, u=u, lo=lo, cur=cur, oth=oth):
            kb = lo + 2 * j
            scores(u, kb + 1, oth)
            softmax_pv(u, kb, cur, False)
            scores(u, kb + 2, cur)
            softmax_pv(u, kb + 1, oth, False)
            return 0

        lax.fori_loop(0, (g - lo) // 2, pair, 0)
        if u + 1 < nt:
            scores(u + 1, 0, oth)
        softmax_pv(u, g, cur, True)
        par = 1 - par

    for tile in range(nt):
        a0 = acc_ref[tile, 0]
        a1 = acc_ref[tile, 1]
        o0 = a0 / pltpu.roll(a0, FOX_HEAD_DIM, 1)
        o1 = a1 / pltpu.roll(a1, FOX_HEAD_DIM, 1)
        o_ref[tile * tq:(tile + 1) * tq, :] = jnp.where(head0, o0, o1).astype(o_ref.dtype)


def _fox_attention(qkv, cq, ck, batch, seq):
    nt = min(ATTN_TILES_PER_STEP, seq // min(TQ_ATTN, seq // 2))
    tq = min(TQ_ATTN, seq // nt)
    assert nt % 2 == 0 and seq % (nt * tq) == 0, "a grid step covers an even number of query tiles"
    ns = seq // (nt * tq)
    hp = FOX_HEADS // 2
    return pl.pallas_call(
        functools.partial(_fox_attn_kernel, tq=tq, nt=nt),
        out_shape=jax.ShapeDtypeStruct((batch * seq, D_MODEL), BF16),
        grid=(batch, hp, ns),
        in_specs=[pl.BlockSpec((nt * tq, LANES), lambda b, p, i: (b * ns + i, p)),
                  pl.BlockSpec((seq, LANES), lambda b, p, i: (b, hp + p)),
                  pl.BlockSpec((seq, LANES), lambda b, p, i: (b, 2 * hp + p)),
                  pl.BlockSpec((nt * tq, FOX_HEADS), lambda b, p, i: (b * ns + i, 0)),
                  pl.BlockSpec((None, 2, 1, seq), lambda b, p, i: (b, p, 0, 0))],
        out_specs=pl.BlockSpec((nt * tq, LANES), lambda b, p, i: (b * ns + i, p)),
        scratch_shapes=[pltpu.VMEM((nt, 2, tq, 1), F32), pltpu.VMEM((nt, 2, tq, LANES), F32),
                        pltpu.VMEM((2 * tq, tq), F32), pltpu.VMEM((2 * tq, tq), F32)],
        compiler_params=_cparams(("parallel", "parallel", "arbitrary")),
        name="fox_attn",
    )(qkv, qkv, qkv, cq, ck)


def _rope_kernel(pos_ref, freq_ref, cos_ref, sin_ref):
    ang = pos_ref[...] * freq_ref[...]
    cos_ref[...] = jnp.cos(ang)
    sin_ref[...] = jnp.sin(ang)


def _rope_tables(pos_f, inv_freq):
    t = pos_f.shape[0]
    half = inv_freq.shape[1]
    tm = min(1024, t)
    return pl.pallas_call(
        _rope_kernel,
        out_shape=(jax.ShapeDtypeStruct((t, half), F32), jax.ShapeDtypeStruct((t, half), F32)),
        grid=(t // tm,),
        in_specs=[pl.BlockSpec((tm, 1), lambda i: (i, 0)),
                  pl.BlockSpec((1, half), lambda i: (0, 0))],
        out_specs=(pl.BlockSpec((tm, half), lambda i: (i, 0)),
                   pl.BlockSpec((tm, half), lambda i: (i, 0))),
        compiler_params=_cparams(("parallel",)),
        name="rope_tables",
    )(pos_f, inv_freq)


def _ret_proj_kernel(a_ref, w_ref, cos_ref, sin_ref, o_ref):
    j = pl.program_id(1)
    a = a_ref[...].astype(BF16)
    acc = jnp.dot(a, w_ref[...], preferred_element_type=F32)

    @pl.when(j < 2)
    def _():
        c = cos_ref[...]
        s = sin_ref[...]
        half = RET_KEY_DIM // 2
        for h in range(RET_HEADS):
            x1 = acc[:, h * RET_KEY_DIM:h * RET_KEY_DIM + half]
            x2 = acc[:, h * RET_KEY_DIM + half:(h + 1) * RET_KEY_DIM]
            o_ref[:, h * RET_KEY_DIM:h * RET_KEY_DIM + half] = (x1 * c - x2 * s).astype(o_ref.dtype)
            o_ref[:, h * RET_KEY_DIM + half:(h + 1) * RET_KEY_DIM] = (x2 * c + x1 * s).astype(o_ref.dtype)

    @pl.when(j >= 2)
    def _():
        o_ref[...] = acc.astype(o_ref.dtype)


def _ret_proj(x2d, w, cos, sin):
    t = x2d.shape[0]
    n = w.shape[1]
    tm = min(TM_PROJ, t)
    tn = D_MODEL
    half = RET_KEY_DIM // 2
    return pl.pallas_call(
        _ret_proj_kernel,
        out_shape=jax.ShapeDtypeStruct((t, n), BF16),
        grid=(t // tm, n // tn),
        in_specs=[pl.BlockSpec((tm, D_MODEL), lambda i, j: (i, 0)),
                  pl.BlockSpec((D_MODEL, tn), lambda i, j: (0, j)),
                  pl.BlockSpec((tm, half), lambda i, j: (i, 0)),
                  pl.BlockSpec((tm, half), lambda i, j: (i, 0))],
        out_specs=pl.BlockSpec((tm, tn), lambda i, j: (i, j)),
        compiler_params=_cparams(("parallel", "arbitrary")),
        name="ret_proj",
    )(x2d, w, cos, sin)


def _retention_kernel(lg_ref, q_ref, k_ref, v_ref, g_ref, o_ref, state_ref, decay_ref, *, chunk):
    dk, dv = RET_KEY_DIM, RET_VAL_DIM

    @pl.when(pl.program_id(1) == 0)
    def _():
        state_ref[...] = jnp.zeros_like(state_ref)
        row = lax.broadcasted_iota(jnp.int32, (chunk, chunk), 0)
        col = lax.broadcasted_iota(jnp.int32, (chunk, chunk), 1)
        diff = (row - col).astype(F32)
        for h in range(RET_HEADS):
            decay_ref[h] = jnp.where(diff >= 0, jnp.exp(jnp.maximum(diff, 0.0) * lg_ref[h]), 0.0)

    idx = lax.broadcasted_iota(jnp.int32, (chunk, 1), 0).astype(F32)
    for h in range(RET_HEADS):
        lg = lg_ref[h]
        q = q_ref[:, h * dk:(h + 1) * dk]
        k = k_ref[:, h * dk:(h + 1) * dk]
        v = v_ref[:, h * dv:(h + 1) * dv]
        scores = lax.dot_general(q, k, (((1,), (1,)), ((), ())), preferred_element_type=F32) * decay_ref[h]
        o = jnp.dot(scores.astype(BF16), v, preferred_element_type=F32)
        state = state_ref[h]
        qd = (q.astype(F32) * jnp.exp((idx + 1.0) * lg)).astype(BF16)
        o = o + jnp.dot(qd, state.astype(BF16), preferred_element_type=F32)
        kd = (k.astype(F32) * jnp.exp((chunk - 1.0 - idx) * lg)).astype(BF16)
        state_ref[h] = state * jnp.exp(chunk * lg) + lax.dot_general(
            kd, v, (((0,), (0,)), ((), ())), preferred_element_type=F32)
        mu = jnp.mean(o, axis=1, keepdims=True)
        oc = o - mu
        var = jnp.mean(oc * oc, axis=1, keepdims=True)
        on = oc * lax.rsqrt(var + LN_EPS)
        g = g_ref[:, h * dv:(h + 1) * dv].astype(F32)
        o_ref[:, h * dv:(h + 1) * dv] = (g * jax.nn.sigmoid(g) * on).astype(o_ref.dtype)


def _retention(proj, log_gamma, batch, seq):
    chunk = min(RET_CHUNK, seq)
    nc = seq // chunk
    dq = RET_HEADS * RET_KEY_DIM
    dvt = RET_HEADS * RET_VAL_DIM
    grid_spec = pltpu.PrefetchScalarGridSpec(
        num_scalar_prefetch=1,
        grid=(batch, nc),
        in_specs=[pl.BlockSpec((chunk, dq), lambda b, c, lg: (b * nc + c, 0)),
                  pl.BlockSpec((chunk, dq), lambda b, c, lg: (b * nc + c, 1)),
                  pl.BlockSpec((chunk, dvt), lambda b, c, lg: (b * nc + c, 1)),
                  pl.BlockSpec((chunk, dvt), lambda b, c, lg: (b * nc + c, 2))],
        out_specs=pl.BlockSpec((chunk, dvt), lambda b, c, lg: (b * nc + c, 0)),
        scratch_shapes=[pltpu.VMEM((RET_HEADS, RET_KEY_DIM, RET_VAL_DIM), F32),
                        pltpu.VMEM((RET_HEADS, chunk, chunk), F32)],
    )
    return pl.pallas_call(
        functools.partial(_retention_kernel, chunk=chunk),
        out_shape=jax.ShapeDtypeStruct((batch * seq, dvt), BF16),
        grid_spec=grid_spec,
        compiler_params=_cparams(("parallel", "arbitrary")),
        name="retention",
    )(log_gamma, proj, proj, proj, proj)


def _layer_norm(y, g, b):
    mu = jnp.mean(y, axis=1, keepdims=True)
    yc = y - mu
    var = jnp.mean(yc * yc, axis=1, keepdims=True)
    return yc * lax.rsqrt(var + LN_EPS) * g + b


def _out_router_kernel(a_ref, w_ref, x_ref, g_ref, b_ref, wrhl_ref, wrh_ref, br_ref, o_ref, ext_ref):
    tm = a_ref.shape[0]
    h = jnp.dot(a_ref[...], w_ref[...], preferred_element_type=F32)
    x1 = _layer_norm(DEEPNORM_ALPHA * x_ref[...] + h, g_ref[...], b_ref[...])
    o_ref[:, :D_MODEL] = x1

    logits = _dot_split(x1, wrhl_ref[...], wrh_ref[...]) + br_ref[...]
    lane = lax.broadcasted_iota(jnp.int32, (tm, LANES), 1)

    def first_argmax(vals, vmax):
        return jnp.min(jnp.where(vals == vmax, lane, LANES), axis=1, keepdims=True)

    gl = jnp.where(lane < N_GROUPS, logits, NEG_BIG)
    gmax = jnp.max(gl, axis=1, keepdims=True)
    gsum = jnp.sum(jnp.where(lane < N_GROUPS, jnp.exp(gl - gmax), 0.0), axis=1, keepdims=True)
    g_val = 1.0 / gsum
    g_idx = first_argmax(gl, gmax)
    lo = CW_LANE0 + EXPERTS_PER_GROUP * g_idx
    in_group = (lane >= lo) & (lane < lo + EXPERTS_PER_GROUP)
    el = jnp.where(in_group, logits, NEG_BIG)
    e1 = jnp.max(el, axis=1, keepdims=True)
    i1 = first_argmax(el, e1)
    el2 = jnp.where(lane == i1, NEG_BIG, el)
    e2 = jnp.max(el2, axis=1, keepdims=True)
    i2 = first_argmax(el2, e2)
    r = jnp.exp(e2 - e1)
    w1 = g_val / (1.0 + r)
    w2 = g_val * r / (1.0 + r)
    ext = jnp.where(lane == i1, w1, 0.0) + jnp.where(lane == i2, w2, 0.0)
    ext = jnp.where(lane == GIDX_LANE, g_idx.astype(F32), ext)
    o_ref[:, D_MODEL:] = ext
    ext_ref[...] = ext


def _out_router(a, w_out, x2d, ln_g, ln_b, w_r, b_r):
    t, din = a.shape
    tm = min(TM_OUT, t)
    w_r_hilo, w_r_hi = _split_weight(w_r)
    return pl.pallas_call(
        _out_router_kernel,
        out_shape=(jax.ShapeDtypeStruct((t, ROW_W), F32), jax.ShapeDtypeStruct((t, EXT), F32)),
        grid=(t // tm,),
        in_specs=[pl.BlockSpec((tm, din), lambda i: (i, 0)),
                  pl.BlockSpec((din, D_MODEL), lambda i: (0, 0)),
                  pl.BlockSpec((tm, D_MODEL), lambda i: (i, 0)),
                  pl.BlockSpec((1, D_MODEL), lambda i: (0, 0)),
                  pl.BlockSpec((1, D_MODEL), lambda i: (0, 0)),
                  pl.BlockSpec((D_MODEL, 2 * LANES), lambda i: (0, 0)),
                  pl.BlockSpec((D_MODEL, LANES), lambda i: (0, 0)),
                  pl.BlockSpec((1, LANES), lambda i: (0, 0))],
        out_specs=(pl.BlockSpec((tm, ROW_W), lambda i: (i, 0)), pl.BlockSpec((tm, EXT), lambda i: (i, 0))),
        compiler_params=_cparams(("parallel",)),
        name="out_router",
    )(a, w_out, x2d, ln_g, ln_b, w_r_hilo, w_r_hi, b_r)


def _row_copy(src_ref, src_row, dst_ref, dst_row, sem):
    return pltpu.make_async_copy(src_ref.at[pl.ds(src_row, 1)], dst_ref.at[pl.ds(dst_row, 1)], sem)


def _dispatch_kernel(pos_ref, x_ref, init_ref, xs_ref, sem):
    del init_ref
    tm = x_ref.shape[0]
    base = pl.program_id(0) * tm

    for r in range(tm):
        _row_copy(x_ref, r, xs_ref, pos_ref[base + r], sem).start()
    pltpu.make_async_copy(x_ref, xs_ref.at[pl.ds(0, tm)], sem).wait()


def _dispatch(pos, x1ext, n_rows):
    t = x1ext.shape[0]
    tm = min(TM_ROWS, t)
    grid_spec = pltpu.PrefetchScalarGridSpec(
        num_scalar_prefetch=1,
        grid=(t // tm,),
        in_specs=[pl.BlockSpec((tm, ROW_W), lambda i, pos: (i, 0)),
                  pl.BlockSpec(memory_space=pl.ANY)],
        out_specs=pl.BlockSpec(memory_space=pl.ANY),
        scratch_shapes=[pltpu.SemaphoreType.DMA(())],
    )
    return pl.pallas_call(
        _dispatch_kernel,
        out_shape=jax.ShapeDtypeStruct((n_rows, ROW_W), F32),
        grid_spec=grid_spec,
        input_output_aliases={2: 0},
        compiler_params=_cparams(("arbitrary",)),
        name="dispatch",
    )(pos, x1ext, jnp.zeros((n_rows, ROW_W), F32))


def _moe_kernel(tg_ref, xs_ref, wg_ref, wu_ref, wd_ref, o_ref, xb_ref, acc_ref):
    i = pl.program_id(0)
    e = pl.program_id(1)
    tm = xs_ref.shape[0]
    group = tg_ref[i]
    valid = group < N_GROUPS

    @pl.when(valid)
    def _():
        @pl.when(e == 0)
        def _():
            xb_ref[...] = xs_ref[:, :D_MODEL].astype(BF16)
            acc_ref[...] = jnp.zeros_like(acc_ref)

        lane = lax.broadcasted_iota(jnp.int32, (tm, EXT), 1)
        cw_lane = CW_LANE0 + EXPERTS_PER_GROUP * group + e
        cw = jnp.sum(jnp.where(lane == cw_lane, xs_ref[:, D_MODEL:], 0.0), axis=1, keepdims=True)
        xb = xb_ref[...]
        hg = jnp.dot(xb, wg_ref[...].astype(BF16), preferred_element_type=F32)
        hu = jnp.dot(xb, wu_ref[...].astype(BF16), preferred_element_type=F32)
        hh = (hg * jax.nn.sigmoid(hg) * hu * cw).astype(BF16)
        acc_ref[...] += jnp.dot(hh, wd_ref[...].astype(BF16), preferred_element_type=F32)

        @pl.when(e == EXPERTS_PER_GROUP - 1)
        def _():
            o_ref[...] = DEEPNORM_ALPHA * xs_ref[:, :D_MODEL] + acc_ref[...]

    @pl.when(jnp.logical_not(valid) & (e == EXPERTS_PER_GROUP - 1))
    def _():
        o_ref[...] = jnp.zeros_like(o_ref)


def _moe(tile_group, xs, w_gate, w_up, w_down, layer):
    n_rows = xs.shape[0]
    tm = TM_MOE
    epg = EXPERTS_PER_GROUP

    def expert(i, e, tg):
        return (layer, jnp.where(tg[i] < N_GROUPS, tg[i] * epg + e, N_EXPERTS - 1), 0, 0)

    grid_spec = pltpu.PrefetchScalarGridSpec(
        num_scalar_prefetch=1,
        grid=(n_rows // tm, epg),
        in_specs=[pl.BlockSpec((tm, ROW_W), lambda i, e, tg: (i, 0)),
                  pl.BlockSpec((None, None, D_MODEL, D_EXPERT), expert),
                  pl.BlockSpec((None, None, D_MODEL, D_EXPERT), expert),
                  pl.BlockSpec((None, None, D_EXPERT, D_MODEL), expert)],
        out_specs=pl.BlockSpec((tm, D_MODEL), lambda i, e, tg: (i, 0)),
        scratch_shapes=[pltpu.VMEM((tm, D_MODEL), BF16), pltpu.VMEM((tm, D_MODEL), F32)],
    )
    return pl.pallas_call(
        _moe_kernel,
        out_shape=jax.ShapeDtypeStruct((n_rows, D_MODEL), F32),
        grid_spec=grid_spec,
        compiler_params=_cparams(("parallel", "arbitrary")),
        name="moe",
    )(tile_group, xs, w_gate, w_up, w_down)


def _combine_ple_kernel(pos_ref, x2s_ref, p_ref, g_ref, b_ref, wg_ref, bg_ref, wp_ref, o_ref, buf_ref, sem):
    tm = o_ref.shape[0]
    i = pl.program_id(0)
    last = pl.num_programs(0) - 1
    slot = i % 2

    def wait_slot(s):
        pltpu.make_async_copy(x2s_ref.at[pl.ds(0, tm)], buf_ref.at[s], sem.at[s]).wait()

    @pl.when(i == 0)
    def _():
        def issue(r, _):
            _row_copy(x2s_ref, pos_ref[r], buf_ref.at[0], r, sem.at[0]).start()
            return 0

        lax.fori_loop(0, tm, issue, 0, unroll=8)

    wait_slot(slot)
    base = jnp.minimum(i + 1, last) * tm
    for r in range(tm):
        _row_copy(x2s_ref, pos_ref[base + r], buf_ref.at[1 - slot], r, sem.at[1 - slot]).start()
    x2 = _layer_norm(buf_ref[slot], g_ref[...], b_ref[...])
    gate = jax.nn.sigmoid(jnp.dot(x2.astype(BF16), wg_ref[...], preferred_element_type=F32) + bg_ref[...])
    pp = jnp.dot(p_ref[...].astype(BF16), wp_ref[...], preferred_element_type=F32)
    o_ref[...] = x2 + gate * pp

    @pl.when(i == last)
    def _():
        wait_slot(1 - slot)


def _combine_ple(pos, x2s, p_all, layer, ln_g, ln_b, w_gate, b_gate, w_proj):
    t = pos.shape[0]
    tm = min(TM_ROWS, t)
    steps = t // tm
    grid_spec = pltpu.PrefetchScalarGridSpec(
        num_scalar_prefetch=1,
        grid=(steps,),
        in_specs=[pl.BlockSpec(memory_space=pl.ANY),
                  pl.BlockSpec((tm, PLE_DIM), lambda i, pos: (layer * steps + i, 0)),
                  pl.BlockSpec((1, D_MODEL), lambda i, pos: (0, 0)),
                  pl.BlockSpec((1, D_MODEL), lambda i, pos: (0, 0)),
                  pl.BlockSpec((D_MODEL, D_MODEL), lambda i, pos: (0, 0)),
                  pl.BlockSpec((1, D_MODEL), lambda i, pos: (0, 0)),
                  pl.BlockSpec((PLE_DIM, D_MODEL), lambda i, pos: (0, 0))],
        out_specs=pl.BlockSpec((tm, D_MODEL), lambda i, pos: (i, 0)),
        scratch_shapes=[pltpu.VMEM((2, tm, D_MODEL), F32), pltpu.SemaphoreType.DMA((2,))],
    )
    return pl.pallas_call(
        _combine_ple_kernel,
        out_shape=jax.ShapeDtypeStruct((t, D_MODEL), F32),
        grid_spec=grid_spec,
        compiler_params=_cparams(("arbitrary",)),
        name="combine_ple",
    )(pos, x2s, p_all, ln_g, ln_b, w_gate, b_gate, w_proj)


def _sorted_positions(g_idx, n_tiles):
    onehot = (g_idx[:, None] == jnp.arange(N_GROUPS, dtype=jnp.int32)[None, :]).astype(jnp.int32)
    counts = jnp.sum(onehot, axis=0)
    tiles = (counts + TM_MOE - 1) // TM_MOE
    tile_end = jnp.cumsum(tiles)
    tile_start = tile_end - tiles
    rank = jnp.sum((jnp.cumsum(onehot, axis=0) - onehot) * onehot, axis=1)
    pos = jnp.sum(onehot * (tile_start * TM_MOE)[None, :], axis=1) + rank
    tile_ids = jnp.arange(n_tiles, dtype=jnp.int32)
    tile_group = jnp.sum((tile_ids[:, None] >= tile_end[None, :]).astype(jnp.int32), axis=1)
    return pos.astype(jnp.int32), tile_group.astype(jnp.int32)


def _moe_block(a, w_out, x2d, p_all, ln1_g, ln1_b, ln2_g, ln2_b, w_group, b_group, w_router, b_router,
               w_gate, w_up, w_down, ple_w_proj, ple_w_gate, ple_b_gate, layer):
    t = x2d.shape[0]
    n_tiles = t // TM_MOE + N_GROUPS
    pad = LANES - CW_LANE0 - N_EXPERTS
    w_r = jnp.concatenate([w_group, w_router, jnp.zeros((D_MODEL, pad), F32)], axis=1)
    b_r = jnp.concatenate([b_group, b_router, jnp.zeros((pad,), F32)])[None, :]
    x1ext, ext = _out_router(a, w_out.astype(BF16), x2d, ln1_g[None, :], ln1_b[None, :], w_r, b_r)
    g_idx = ext[:, GIDX_LANE].astype(jnp.int32)
    pos, tile_group = _sorted_positions(g_idx, n_tiles)
    xs = _dispatch(pos, x1ext, n_tiles * TM_MOE)
    x2s = _moe(tile_group, xs, w_gate, w_up, w_down, layer)
    return _combine_ple(pos, x2s, p_all, layer, ln2_g[None, :], ln2_b[None, :], ple_w_gate.astype(BF16),
                        ple_b_gate[None, :], ple_w_proj.astype(BF16))


@jax.jit
def _forward(x, p, positions, fox_w_in, fox_b_f, fox_w_out, ret_w_in, ret_w_out, ln1_g, ln1_b, ln2_g, ln2_b,
             moe_w_group, moe_b_group, moe_w_router, moe_b_router, moe_w_gate, moe_w_up, moe_w_down,
             ple_w_proj, ple_w_gate, ple_b_gate):
    batch, seq, d = x.shape
    t = batch * seq
    x2d = x.reshape(t, d)
    p_all = p.reshape(DEPTH * t, PLE_DIM)

    def moe_args(i):
        return (ln1_g[i], ln1_b[i], ln2_g[i], ln2_b[i], moe_w_group[i], moe_b_group[i], moe_w_router[i],
                moe_b_router[i], moe_w_gate, moe_w_up, moe_w_down, ple_w_proj[i], ple_w_gate[i],
                ple_b_gate[i], i)

    w_in = fox_w_in[0]
    scale = jnp.concatenate([jnp.full((D_MODEL,), FOX_HEAD_DIM ** -0.5 * LOG2E, F32),
                             jnp.ones((2 * D_MODEL,), F32)])
    w_qkv = (w_in[:, :3 * D_MODEL] * scale[None, :]).astype(BF16)
    qkv = _proj(x2d, w_qkv, D_MODEL)
    c = _fgate(x2d, w_in[:, 3 * D_MODEL:], fox_b_f[0][None, :], batch, seq)
    c3 = c.reshape(batch, seq, FOX_HEADS).transpose(0, 2, 1)
    attn = _fox_attention(qkv, c, c3[:, :, None, :], batch, seq)
    x2d = _moe_block(attn, fox_w_out[0], x2d, p_all, *moe_args(0))

    w_in = ret_w_in[0]
    scale = jnp.concatenate([jnp.ones((D_MODEL,), F32), jnp.full((D_MODEL,), RET_KEY_DIM ** -0.5, F32),
                             jnp.ones((4 * D_MODEL,), F32)])
    inv_freq = ROPE_BASE ** (-jnp.arange(0, RET_KEY_DIM, 2, dtype=F32) / RET_KEY_DIM)
    cos, sin = _rope_tables(positions.astype(F32).reshape(t, 1), inv_freq[None, :])
    proj = _ret_proj(x2d, (w_in * scale[None, :]).astype(BF16), cos, sin)
    log_gamma = jnp.log(1.0 - 2.0 ** (-5.0 - jnp.arange(RET_HEADS, dtype=F32)))
    ret = _retention(proj, log_gamma, batch, seq)
    x2d = _moe_block(ret, ret_w_out[0], x2d, p_all, *moe_args(1))
    return x2d.reshape(batch, seq, d)


def kernel(x, p, positions, fox_w_in, fox_b_f, fox_w_out, ret_w_in, ret_w_out, ln1_g, ln1_b, ln2_g, ln2_b,
           moe_w_group, moe_b_group, moe_w_router, moe_b_router, moe_w_gate, moe_w_up, moe_w_down,
           ple_w_proj, ple_w_gate, ple_b_gate):
    return _forward(x, p, positions, fox_w_in, fox_b_f, fox_w_out, ret_w_in, ret_w_out, ln1_g, ln1_b,
                    ln2_g, ln2_b, moe_w_group, moe_b_group, moe_w_router, moe_b_router, moe_w_gate,
                    moe_w_up, moe_w_down, ple_w_proj, ple_w_gate, ple_b_gate)
```

```python
import functools

import jax
import jax.numpy as jnp
from jax import lax
from jax.experimental import pallas as pl
from jax.experimental.pallas import tpu as pltpu

F32 = jnp.float32
BF16 = jnp.bfloat16

D_MODEL = 1024
PLE_DIM = 256
FOX_HEADS = 16
FOX_HEAD_DIM = D_MODEL // FOX_HEADS
RET_HEADS = 4
RET_KEY_DIM = D_MODEL // RET_HEADS
RET_VAL_DIM = 2 * D_MODEL // RET_HEADS
ROPE_BASE = 10000.0
N_GROUPS = 4
EXPERTS_PER_GROUP = 4
N_EXPERTS = N_GROUPS * EXPERTS_PER_GROUP
D_EXPERT = 512
DEPTH = 2
DEEPNORM_ALPHA = (2.0 * DEPTH) ** 0.25
LN_EPS = 1e-5

LANES = 128
EXT = LANES
ROW_W = D_MODEL + EXT
GIDX_LANE = 0
CW_LANE0 = 4
NEG_BIG = -1e30
LOG2E = 1.4426950408889634
VMEM_LIMIT = 56 * 1024 * 1024

TM_PROJ = 1024
TS_GATE = 512
TQ_ATTN = 512
ATTN_TILES_PER_STEP = 2
ATTN_ROW_BLOCK = 128
TM_OUT = 1024
TM_ROWS = 1024
TM_MOE = 1024
RET_CHUNK = 256


def _cparams(sem):
    return pltpu.CompilerParams(dimension_semantics=sem, vmem_limit_bytes=VMEM_LIMIT)


def _proj_kernel(a_ref, w_ref, o_ref):
    a = a_ref[...].astype(BF16)
    o_ref[...] = jnp.dot(a, w_ref[...], preferred_element_type=F32).astype(o_ref.dtype)


def _proj(a, w, tn):
    m, k = a.shape
    n = w.shape[1]
    tm = min(TM_PROJ, m)
    return pl.pallas_call(
        _proj_kernel,
        out_shape=jax.ShapeDtypeStruct((m, n), BF16),
        grid=(m // tm, n // tn),
        in_specs=[pl.BlockSpec((tm, k), lambda i, j: (i, 0)),
                  pl.BlockSpec((k, tn), lambda i, j: (0, j))],
        out_specs=pl.BlockSpec((tm, tn), lambda i, j: (i, j)),
        compiler_params=_cparams(("parallel", "arbitrary")),
        name="proj",
    )(a, w)


def _split_bf16(x):
    hi = x.astype(BF16)
    return hi, (x - hi.astype(F32)).astype(BF16)


def _split_weight(w):
    hi, lo = _split_bf16(w)
    return jnp.concatenate([hi, lo], axis=1), hi


def _dot_split(x, w_hilo, w_hi):
    x_hi, x_lo = _split_bf16(x)
    r = jnp.dot(x_hi, w_hilo, preferred_element_type=F32)
    return r[:, :LANES] + r[:, LANES:] + jnp.dot(x_lo, w_hi, preferred_element_type=F32)


def _fgate_kernel(x_ref, whl_ref, wh_ref, b_ref, c_ref, carry_ref):
    @pl.when(pl.program_id(1) == 0)
    def _():
        carry_ref[...] = jnp.zeros_like(carry_ref)

    ts = x_ref.shape[0]
    z = _dot_split(x_ref[...], whl_ref[...], wh_ref[...]) + b_ref[...]
    logf = (jnp.minimum(z, 0.0) - jnp.log1p(jnp.exp(-jnp.abs(z)))) * LOG2E
    l1 = logf.astype(BF16)
    rest = logf - l1.astype(F32)
    l2, l3 = _split_bf16(rest)
    row = lax.broadcasted_iota(jnp.int32, (ts, ts), 0)
    col = lax.broadcasted_iota(jnp.int32, (ts, ts), 1)
    tri = jnp.where(row >= col, 1.0, 0.0).astype(BF16)
    parts = jnp.dot(tri, jnp.concatenate([l1, l2, l3], axis=1), preferred_element_type=F32)
    c = parts[:, :LANES] + parts[:, LANES:2 * LANES] + parts[:, 2 * LANES:] + carry_ref[...]
    c_ref[...] = c[:, :c_ref.shape[1]]
    carry_ref[...] = c[ts - 1:ts, :]


def _fgate(x2d, w_f, b_f, batch, seq):
    ts = min(TS_GATE, seq)
    ns = seq // ts
    h = w_f.shape[1]
    w_hilo, w_hi = _split_weight(jnp.pad(w_f, ((0, 0), (0, LANES - h))))
    b_pad = jnp.pad(b_f, ((0, 0), (0, LANES - h)))
    return pl.pallas_call(
        _fgate_kernel,
        out_shape=jax.ShapeDtypeStruct((batch * seq, h), F32),
        grid=(batch, ns),
        in_specs=[pl.BlockSpec((ts, D_MODEL), lambda b, s: (b * ns + s, 0)),
                  pl.BlockSpec((D_MODEL, 2 * LANES), lambda b, s: (0, 0)),
                  pl.BlockSpec((D_MODEL, LANES), lambda b, s: (0, 0)),
                  pl.BlockSpec((1, LANES), lambda b, s: (0, 0))],
        out_specs=pl.BlockSpec((ts, h), lambda b, s: (b * ns + s, 0)),
        scratch_shapes=[pltpu.VMEM((1, LANES), F32)],
        compiler_params=_cparams(("parallel", "arbitrary")),
        name="fgate",
    )(x2d, w_hilo, w_hi, b_pad)


def _fox_attn_kernel(q_ref, k_ref, v_ref, cq_ref, ck_ref, o_ref, m_ref, acc_ref, sa_ref, sb_ref, *, tq, nt):
    i = pl.program_id(2)
    lane = lax.broadcasted_iota(jnp.int32, (tq, LANES), 1)
    row = lax.broadcasted_iota(jnp.int32, (tq, tq), 0)
    col = lax.broadcasted_iota(jnp.int32, (tq, tq), 1)
    causal = row >= col
    head0 = lane < FOX_HEAD_DIM
    m_ref[...] = jnp.full(m_ref.shape, NEG_BIG, F32)
    acc_ref[...] = jnp.zeros(acc_ref.shape, F32)
    lane_k = lax.broadcasted_iota(jnp.int32, (tq, LANES), 1)
    lane_h = lax.broadcasted_iota(jnp.int32, (tq, FOX_HEADS), 1)
    q2 = []
    cq_heads = []
    for tile in range(nt):
        q = q_ref[tile * tq:(tile + 1) * tq, :]
        q2.append(jnp.concatenate([jnp.where(head0, q, jnp.zeros_like(q)),
                                   jnp.where(head0, jnp.zeros_like(q), q)], axis=0))
        c_tok = cq_ref[tile * tq:(tile + 1) * tq, :]
        cq_heads.append([jnp.sum(jnp.where(lane_h == 2 * pl.program_id(1) + h, c_tok, 0.0), axis=1, keepdims=True)
                         for h in range(2)])
    rb = ATTN_ROW_BLOCK

    def scores(tile, kb, s_ref):
        ks = pl.multiple_of(kb * tq, tq)
        s_ref[...] = lax.dot_general(q2[tile], k_ref[pl.ds(ks, tq), :], (((1,), (1,)), ((), ())),
                                     preferred_element_type=F32)

    def softmax_pv(tile, kb, s_ref, masked):
        ks = pl.multiple_of(kb * tq, tq)
        vblk = v_ref[pl.ds(ks, tq), :]
        for h in range(2):
            ck = ck_ref[h, :, pl.ds(ks, tq)]
            vh = jnp.where(lane_k < FOX_HEAD_DIM if h == 0 else lane_k >= FOX_HEAD_DIM,
                           vblk, jnp.ones_like(vblk))
            p_blocks = []
            m_blocks = []
            cq_all = cq_heads[tile][h]
            m_all = m_ref[tile, h]
            for r0 in range(0, tq, rb):
                t = s_ref[h * tq + r0:h * tq + r0 + rb, :] - ck
                if masked:
                    t = jnp.where(causal[r0:r0 + rb, :], t, NEG_BIG)
                cq = cq_all[r0:r0 + rb, :]
                m_new = jnp.maximum(m_all[r0:r0 + rb, :], cq + jnp.max(t, axis=1, keepdims=True))
                m_blocks.append(m_new)
                p_blocks.append(jnp.exp2(t + (cq - m_new)).astype(BF16))
            m_new_all = jnp.concatenate(m_blocks, axis=0)
            pv = jnp.dot(jnp.concatenate(p_blocks, axis=0), vh, preferred_element_type=F32)
            acc_ref[tile, h] = jnp.exp2(m_all - m_new_all) * acc_ref[tile, h] + pv
            m_ref[tile, h] = m_new_all

    bufs = (sa_ref, sb_ref)
    scores(0, 0, bufs[0])
    par = 0
    for u in range(nt):
        g = nt * i + u
        lo = u % 2
        if lo:
            scores(u, 1, bufs[1 - par])
            softmax_pv(u, 0, bufs[par], False)
            par = 1 - par
        cur, oth = bufs[par], bufs[1 - par]

        def pair(j, _, u=u, lo=lo, cur=cur, oth=oth):
            kb = lo + 2 * j
            scores(u, kb + 1, oth)
            softmax_pv(u, kb, cur, False)
            scores(u, kb + 2, cur)
            softmax_pv(u, kb + 1, oth, False)
            return 0

        lax.fori_loop(0, (g - lo) // 2, pair, 0)
        if u + 1 < nt:
            scores(u + 1, 0, oth)
        softmax_pv(u, g, cur, True)
        par = 1 - par

    for tile in range(nt):
        a0 = acc_ref[tile, 0]
        a1 = acc_ref[tile, 1]
        o0 = a0 / pltpu.roll(a0, FOX_HEAD_DIM, 1)
        o1 = a1 / pltpu.roll(a1, FOX_HEAD_DIM, 1)
        o_ref[tile * tq:(tile + 1) * tq, :] = jnp.where(head0, o0, o1).astype(o_ref.dtype)


def _fox_attention(qkv, cq, ck, batch, seq):
    nt = min(ATTN_TILES_PER_STEP, seq // min(TQ_ATTN, seq // 2))
    tq = min(TQ_ATTN, seq // nt)
    assert nt % 2 == 0 and seq % (nt * tq) == 0, "a grid step covers an even number of query tiles"
    ns = seq // (nt * tq)
    hp = FOX_HEADS // 2
    return pl.pallas_call(
        functools.partial(_fox_attn_kernel, tq=tq, nt=nt),
        out_shape=jax.ShapeDtypeStruct((batch * seq, D_MODEL), BF16),
        grid=(batch, hp, ns),
        in_specs=[pl.BlockSpec((nt * tq, LANES), lambda b, p, i: (b * ns + i, p)),
                  pl.BlockSpec((seq, LANES), lambda b, p, i: (b, hp + p)),
                  pl.BlockSpec((seq, LANES), lambda b, p, i: (b, 2 * hp + p)),
                  pl.BlockSpec((nt * tq, FOX_HEADS), lambda b, p, i: (b * ns + i, 0)),
                  pl.BlockSpec((None, 2, 1, seq), lambda b, p, i: (b, p, 0, 0))],
        out_specs=pl.BlockSpec((nt * tq, LANES), lambda b, p, i: (b * ns + i, p)),
        scratch_shapes=[pltpu.VMEM((nt, 2, tq, 1), F32), pltpu.VMEM((nt, 2, tq, LANES), F32),
                        pltpu.VMEM((2 * tq, tq), F32), pltpu.VMEM((2 * tq, tq), F32)],
        compiler_params=_cparams(("parallel", "parallel", "arbitrary")),
        name="fox_attn",
    )(qkv, qkv, qkv, cq, ck)


def _rope_kernel(pos_ref, freq_ref, cos_ref, sin_ref):
    ang = pos_ref[...] * freq_ref[...]
    cos_ref[...] = jnp.cos(ang)
    sin_ref[...] = jnp.sin(ang)


def _rope_tables(pos_f, inv_freq):
    t = pos_f.shape[0]
    half = inv_freq.shape[1]
    tm = min(1024, t)
    return pl.pallas_call(
        _rope_kernel,
        out_shape=(jax.ShapeDtypeStruct((t, half), F32), jax.ShapeDtypeStruct((t, half), F32)),
        grid=(t // tm,),
        in_specs=[pl.BlockSpec((tm, 1), lambda i: (i, 0)),
                  pl.BlockSpec((1, half), lambda i: (0, 0))],
        out_specs=(pl.BlockSpec((tm, half), lambda i: (i, 0)),
                   pl.BlockSpec((tm, half), lambda i: (i, 0))),
        compiler_params=_cparams(("parallel",)),
        name="rope_tables",
    )(pos_f, inv_freq)


def _ret_proj_kernel(a_ref, w_ref, cos_ref, sin_ref, o_ref):
    j = pl.program_id(1)
    a = a_ref[...].astype(BF16)
    acc = jnp.dot(a, w_ref[...], preferred_element_type=F32)

    @pl.when(j < 2)
    def _():
        c = cos_ref[...]
        s = sin_ref[...]
        half = RET_KEY_DIM // 2
        for h in range(RET_HEADS):
            x1 = acc[:, h * RET_KEY_DIM:h * RET_KEY_DIM + half]
            x2 = acc[:, h * RET_KEY_DIM + half:(h + 1) * RET_KEY_DIM]
            o_ref[:, h * RET_KEY_DIM:h * RET_KEY_DIM + half] = (x1 * c - x2 * s).astype(o_ref.dtype)
            o_ref[:, h * RET_KEY_DIM + half:(h + 1) * RET_KEY_DIM] = (x2 * c + x1 * s).astype(o_ref.dtype)

    @pl.when(j >= 2)
    def _():
        o_ref[...] = acc.astype(o_ref.dtype)


def _ret_proj(x2d, w, cos, sin):
    t = x2d.shape[0]
    n = w.shape[1]
    tm = min(TM_PROJ, t)
    tn = D_MODEL
    half = RET_KEY_DIM // 2
    return pl.pallas_call(
        _ret_proj_kernel,
        out_shape=jax.ShapeDtypeStruct((t, n), BF16),
        grid=(t // tm, n // tn),
        in_specs=[pl.BlockSpec((tm, D_MODEL), lambda i, j: (i, 0)),
                  pl.BlockSpec((D_MODEL, tn), lambda i, j: (0, j)),
                  pl.BlockSpec((tm, half), lambda i, j: (i, 0)),
                  pl.BlockSpec((tm, half), lambda i, j: (i, 0))],
        out_specs=pl.BlockSpec((tm, tn), lambda i, j: (i, j)),
        compiler_params=_cparams(("parallel", "arbitrary")),
        name="ret_proj",
    )(x2d, w, cos, sin)


def _retention_kernel(lg_ref, q_ref, k_ref, v_ref, g_ref, o_ref, state_ref, decay_ref, *, chunk):
    dk, dv = RET_KEY_DIM, RET_VAL_DIM

    @pl.when(pl.program_id(1) == 0)
    def _():
        state_ref[...] = jnp.zeros_like(state_ref)
        row = lax.broadcasted_iota(jnp.int32, (chunk, chunk), 0)
        col = lax.broadcasted_iota(jnp.int32, (chunk, chunk), 1)
        diff = (row - col).astype(F32)
        for h in range(RET_HEADS):
            decay_ref[h] = jnp.where(diff >= 0, jnp.exp(jnp.maximum(diff, 0.0) * lg_ref[h]), 0.0)

    idx = lax.broadcasted_iota(jnp.int32, (chunk, 1), 0).astype(F32)
    for h in range(RET_HEADS):
        lg = lg_ref[h]
        q = q_ref[:, h * dk:(h + 1) * dk]
        k = k_ref[:, h * dk:(h + 1) * dk]
        v = v_ref[:, h * dv:(h + 1) * dv]
        scores = lax.dot_general(q, k, (((1,), (1,)), ((), ())), preferred_element_type=F32) * decay_ref[h]
        o = jnp.dot(scores.astype(BF16), v, preferred_element_type=F32)
        state = state_ref[h]
        qd = (q.astype(F32) * jnp.exp((idx + 1.0) * lg)).astype(BF16)
        o = o + jnp.dot(qd, state.astype(BF16), preferred_element_type=F32)
        kd = (k.astype(F32) * jnp.exp((chunk - 1.0 - idx) * lg)).astype(BF16)
        state_ref[h] = state * jnp.exp(chunk * lg) + lax.dot_general(
            kd, v, (((0,), (0,)), ((), ())), preferred_element_type=F32)
        mu = jnp.mean(o, axis=1, keepdims=True)
        oc = o - mu
        var = jnp.mean(oc * oc, axis=1, keepdims=True)
        on = oc * lax.rsqrt(var + LN_EPS)
        g = g_ref[:, h * dv:(h + 1) * dv].astype(F32)
        o_ref[:, h * dv:(h + 1) * dv] = (g * jax.nn.sigmoid(g) * on).astype(o_ref.dtype)


def _retention(proj, log_gamma, batch, seq):
    chunk = min(RET_CHUNK, seq)
    nc = seq // chunk
    dq = RET_HEADS * RET_KEY_DIM
    dvt = RET_HEADS * RET_VAL_DIM
    grid_spec = pltpu.PrefetchScalarGridSpec(
        num_scalar_prefetch=1,
        grid=(batch, nc),
        in_specs=[pl.BlockSpec((chunk, dq), lambda b, c, lg: (b * nc + c, 0)),
                  pl.BlockSpec((chunk, dq), lambda b, c, lg: (b * nc + c, 1)),
                  pl.BlockSpec((chunk, dvt), lambda b, c, lg: (b * nc + c, 1)),
                  pl.BlockSpec((chunk, dvt), lambda b, c, lg: (b * nc + c, 2))],
        out_specs=pl.BlockSpec((chunk, dvt), lambda b, c, lg: (b * nc + c, 0)),
        scratch_shapes=[pltpu.VMEM((RET_HEADS, RET_KEY_DIM, RET_VAL_DIM), F32),
                        pltpu.VMEM((RET_HEADS, chunk, chunk), F32)],
    )
    return pl.pallas_call(
        functools.partial(_retention_kernel, chunk=chunk),
        out_shape=jax.ShapeDtypeStruct((batch * seq, dvt), BF16),
        grid_spec=grid_spec,
        compiler_params=_cparams(("parallel", "arbitrary")),
        name="retention",
    )(log_gamma, proj, proj, proj, proj)


def _layer_norm(y, g, b):
    mu = jnp.mean(y, axis=1, keepdims=True)
    yc = y - mu
    var = jnp.mean(yc * yc, axis=1, keepdims=True)
    return yc * lax.rsqrt(var + LN_EPS) * g + b


def _out_router_kernel(a_ref, w_ref, x_ref, g_ref, b_ref, wrhl_ref, wrh_ref, br_ref, o_ref, ext_ref):
    tm = a_ref.shape[0]
    h = jnp.dot(a_ref[...], w_ref[...], preferred_element_type=F32)
    x1 = _layer_norm(DEEPNORM_ALPHA * x_ref[...] + h, g_ref[...], b_ref[...])
    o_ref[:, :D_MODEL] = x1

    logits = _dot_split(x1, wrhl_ref[...], wrh_ref[...]) + br_ref[...]
    lane = lax.broadcasted_iota(jnp.int32, (tm, LANES), 1)

    def first_argmax(vals, vmax):
        return jnp.min(jnp.where(vals == vmax, lane, LANES), axis=1, keepdims=True)

    gl = jnp.where(lane < N_GROUPS, logits, NEG_BIG)
    gmax = jnp.max(gl, axis=1, keepdims=True)
    gsum = jnp.sum(jnp.where(lane < N_GROUPS, jnp.exp(gl - gmax), 0.0), axis=1, keepdims=True)
    g_val = 1.0 / gsum
    g_idx = first_argmax(gl, gmax)
    lo = CW_LANE0 + EXPERTS_PER_GROUP * g_idx
    in_group = (lane >= lo) & (lane < lo + EXPERTS_PER_GROUP)
    el = jnp.where(in_group, logits, NEG_BIG)
    e1 = jnp.max(el, axis=1, keepdims=True)
    i1 = first_argmax(el, e1)
    el2 = jnp.where(lane == i1, NEG_BIG, el)
    e2 = jnp.max(el2, axis=1, keepdims=True)
    i2 = first_argmax(el2, e2)
    r = jnp.exp(e2 - e1)
    w1 = g_val / (1.0 + r)
    w2 = g_val * r / (1.0 + r)
    ext = jnp.where(lane == i1, w1, 0.0) + jnp.where(lane == i2, w2, 0.0)
    ext = jnp.where(lane == GIDX_LANE, g_idx.astype(F32), ext)
    o_ref[:, D_MODEL:] = ext
    ext_ref[...] = ext


def _out_router(a, w_out, x2d, ln_g, ln_b, w_r, b_r):
    t, din = a.shape
    tm = min(TM_OUT, t)
    w_r_hilo, w_r_hi = _split_weight(w_r)
    return pl.pallas_call(
        _out_router_kernel,
        out_shape=(jax.ShapeDtypeStruct((t, ROW_W), F32), jax.ShapeDtypeStruct((t, EXT), F32)),
        grid=(t // tm,),
        in_specs=[pl.BlockSpec((tm, din), lambda i: (i, 0)),
                  pl.BlockSpec((din, D_MODEL), lambda i: (0, 0)),
                  pl.BlockSpec((tm, D_MODEL), lambda i: (i, 0)),
                  pl.BlockSpec((1, D_MODEL), lambda i: (0, 0)),
                  pl.BlockSpec((1, D_MODEL), lambda i: (0, 0)),
                  pl.BlockSpec((D_MODEL, 2 * LANES), lambda i: (0, 0)),
                  pl.BlockSpec((D_MODEL, LANES), lambda i: (0, 0)),
                  pl.BlockSpec((1, LANES), lambda i: (0, 0))],
        out_specs=(pl.BlockSpec((tm, ROW_W), lambda i: (i, 0)), pl.BlockSpec((tm, EXT), lambda i: (i, 0))),
        compiler_params=_cparams(("parallel",)),
        name="out_router",
    )(a, w_out, x2d, ln_g, ln_b, w_r_hilo, w_r_hi, b_r)


def _row_copy(src_ref, src_row, dst_ref, dst_row, sem):
    return pltpu.make_async_copy(src_ref.at[pl.ds(src_row, 1)], dst_ref.at[pl.ds(dst_row, 1)], sem)


def _dispatch_kernel(pos_ref, zt_ref, x_ref, xs_ref, zero_ref, sem, zsem):
    tm = x_ref.shape[0]
    base = pl.program_id(0) * tm

    @pl.when(pl.program_id(0) == 0)
    def _():
        zero_ref[...] = jnp.zeros_like(zero_ref)

        def zero_tile(k):
            return pltpu.make_async_copy(zero_ref, xs_ref.at[pl.ds(zt_ref[k] * TM_MOE, TM_MOE)], zsem)

        for k in range(zt_ref.shape[0]):
            @pl.when(zt_ref[k] >= 0)
            def _(k=k):
                zero_tile(k).start()

        for k in range(zt_ref.shape[0]):
            @pl.when(zt_ref[k] >= 0)
            def _(k=k):
                zero_tile(k).wait()

    for r in range(tm):
        _row_copy(x_ref, r, xs_ref, pos_ref[base + r], sem).start()
    pltpu.make_async_copy(x_ref, xs_ref.at[pl.ds(0, tm)], sem).wait()


def _dispatch(pos, zero_tiles, x1ext, n_rows):
    t = x1ext.shape[0]
    tm = min(TM_ROWS, t)
    grid_spec = pltpu.PrefetchScalarGridSpec(
        num_scalar_prefetch=2,
        grid=(t // tm,),
        in_specs=[pl.BlockSpec((tm, ROW_W), lambda i, pos, zt: (i, 0))],
        out_specs=pl.BlockSpec(memory_space=pl.ANY),
        scratch_shapes=[pltpu.VMEM((TM_MOE, ROW_W), F32), pltpu.SemaphoreType.DMA(()),
                        pltpu.SemaphoreType.DMA(())],
    )
    return pl.pallas_call(
        _dispatch_kernel,
        out_shape=jax.ShapeDtypeStruct((n_rows, ROW_W), F32),
        grid_spec=grid_spec,
        compiler_params=_cparams(("arbitrary",)),
        name="dispatch",
    )(pos, zero_tiles, x1ext)


def _moe_kernel(tg_ref, xs_ref, wg_ref, wu_ref, wd_ref, o_ref, xb_ref, acc_ref):
    i = pl.program_id(0)
    e = pl.program_id(1)
    tm = xs_ref.shape[0]
    group = tg_ref[i]
    valid = group < N_GROUPS

    @pl.when(valid)
    def _():
        @pl.when(e == 0)
        def _():
            xb_ref[...] = xs_ref[:, :D_MODEL].astype(BF16)
            acc_ref[...] = jnp.zeros_like(acc_ref)

        lane = lax.broadcasted_iota(jnp.int32, (tm, EXT), 1)
        cw_lane = CW_LANE0 + EXPERTS_PER_GROUP * group + e
        cw = jnp.sum(jnp.where(lane == cw_lane, xs_ref[:, D_MODEL:], 0.0), axis=1, keepdims=True)
        xb = xb_ref[...]
        hg = jnp.dot(xb, wg_ref[...].astype(BF16), preferred_element_type=F32)
        hu = jnp.dot(xb, wu_ref[...].astype(BF16), preferred_element_type=F32)
        hh = (hg * jax.nn.sigmoid(hg) * hu * cw).astype(BF16)
        acc_ref[...] += jnp.dot(hh, wd_ref[...].astype(BF16), preferred_element_type=F32)

        @pl.when(e == EXPERTS_PER_GROUP - 1)
        def _():
            o_ref[...] = DEEPNORM_ALPHA * xs_ref[:, :D_MODEL] + acc_ref[...]

    @pl.when(jnp.logical_not(valid) & (e == EXPERTS_PER_GROUP - 1))
    def _():
        o_ref[...] = jnp.zeros_like(o_ref)


def _moe(tile_group, xs, w_gate, w_up, w_down, layer):
    n_rows = xs.shape[0]
    tm = TM_MOE
    epg = EXPERTS_PER_GROUP

    def expert(i, e, tg):
        return (layer, jnp.where(tg[i] < N_GROUPS, tg[i] * epg + e, N_EXPERTS - 1), 0, 0)

    grid_spec = pltpu.PrefetchScalarGridSpec(
        num_scalar_prefetch=1,
        grid=(n_rows // tm, epg),
        in_specs=[pl.BlockSpec((tm, ROW_W), lambda i, e, tg: (i, 0)),
                  pl.BlockSpec((None, None, D_MODEL, D_EXPERT), expert),
                  pl.BlockSpec((None, None, D_MODEL, D_EXPERT), expert),
                  pl.BlockSpec((None, None, D_EXPERT, D_MODEL), expert)],
        out_specs=pl.BlockSpec((tm, D_MODEL), lambda i, e, tg: (i, 0)),
        scratch_shapes=[pltpu.VMEM((tm, D_MODEL), BF16), pltpu.VMEM((tm, D_MODEL), F32)],
    )
    return pl.pallas_call(
        _moe_kernel,
        out_shape=jax.ShapeDtypeStruct((n_rows, D_MODEL), F32),
        grid_spec=grid_spec,
        compiler_params=_cparams(("parallel", "arbitrary")),
        name="moe",
    )(tile_group, xs, w_gate, w_up, w_down)


def _combine_ple_kernel(pos_ref, x2s_ref, p_ref, g_ref, b_ref, wg_ref, bg_ref, wp_ref, o_ref, buf_ref, sem):
    tm = o_ref.shape[0]
    i = pl.program_id(0)
    last = pl.num_programs(0) - 1
    slot = i % 2

    def wait_slot(s):
        pltpu.make_async_copy(x2s_ref.at[pl.ds(0, tm)], buf_ref.at[s], sem.at[s]).wait()

    @pl.when(i == 0)
    def _():
        def issue(r, _):
            _row_copy(x2s_ref, pos_ref[r], buf_ref.at[0], r, sem.at[0]).start()
            return 0

        lax.fori_loop(0, tm, issue, 0, unroll=8)

    wait_slot(slot)
    base = jnp.minimum(i + 1, last) * tm
    for r in range(tm):
        _row_copy(x2s_ref, pos_ref[base + r], buf_ref.at[1 - slot], r, sem.at[1 - slot]).start()
    x2 = _layer_norm(buf_ref[slot], g_ref[...], b_ref[...])
    gate = jax.nn.sigmoid(jnp.dot(x2.astype(BF16), wg_ref[...], preferred_element_type=F32) + bg_ref[...])
    pp = jnp.dot(p_ref[...].astype(BF16), wp_ref[...], preferred_element_type=F32)
    o_ref[...] = x2 + gate * pp

    @pl.when(i == last)
    def _():
        wait_slot(1 - slot)


def _combine_ple(pos, x2s, p_all, layer, ln_g, ln_b, w_gate, b_gate, w_proj):
    t = pos.shape[0]
    tm = min(TM_ROWS, t)
    steps = t // tm
    grid_spec = pltpu.PrefetchScalarGridSpec(
        num_scalar_prefetch=1,
        grid=(steps,),
        in_specs=[pl.BlockSpec(memory_space=pl.ANY),
                  pl.BlockSpec((tm, PLE_DIM), lambda i, pos: (layer * steps + i, 0)),
                  pl.BlockSpec((1, D_MODEL), lambda i, pos: (0, 0)),
                  pl.BlockSpec((1, D_MODEL), lambda i, pos: (0, 0)),
                  pl.BlockSpec((D_MODEL, D_MODEL), lambda i, pos: (0, 0)),
                  pl.BlockSpec((1, D_MODEL), lambda i, pos: (0, 0)),
                  pl.BlockSpec((PLE_DIM, D_MODEL), lambda i, pos: (0, 0))],
        out_specs=pl.BlockSpec((tm, D_MODEL), lambda i, pos: (i, 0)),
        scratch_shapes=[pltpu.VMEM((2, tm, D_MODEL), F32), pltpu.SemaphoreType.DMA((2,))],
    )
    return pl.pallas_call(
        _combine_ple_kernel,
        out_shape=jax.ShapeDtypeStruct((t, D_MODEL), F32),
        grid_spec=grid_spec,
        compiler_params=_cparams(("arbitrary",)),
        name="combine_ple",
    )(pos, x2s, p_all, ln_g, ln_b, w_gate, b_gate, w_proj)


def _sorted_positions(g_idx, n_tiles):
    onehot = (g_idx[:, None] == jnp.arange(N_GROUPS, dtype=jnp.int32)[None, :]).astype(jnp.int32)
    counts = jnp.sum(onehot, axis=0)
    tiles = (counts + TM_MOE - 1) // TM_MOE
    tile_end = jnp.cumsum(tiles)
    tile_start = tile_end - tiles
    rank = jnp.sum((jnp.cumsum(onehot, axis=0) - onehot) * onehot, axis=1)
    pos = jnp.sum(onehot * (tile_start * TM_MOE)[None, :], axis=1) + rank
    tile_ids = jnp.arange(n_tiles, dtype=jnp.int32)
    tile_group = jnp.sum((tile_ids[:, None] >= tile_end[None, :]).astype(jnp.int32), axis=1)
    trailing = tile_end[-1] + jnp.arange(N_GROUPS, dtype=jnp.int32)
    zero_tiles = jnp.concatenate([jnp.where(tiles > 0, tile_end - 1, -1),
                                  jnp.where(trailing < n_tiles, trailing, -1)])
    return pos.astype(jnp.int32), tile_group.astype(jnp.int32), zero_tiles.astype(jnp.int32)


def _moe_block(a, w_out, x2d, p_all, ln1_g, ln1_b, ln2_g, ln2_b, w_group, b_group, w_router, b_router,
               w_gate, w_up, w_down, ple_w_proj, ple_w_gate, ple_b_gate, layer):
    t = x2d.shape[0]
    n_tiles = t // TM_MOE + N_GROUPS
    pad = LANES - CW_LANE0 - N_EXPERTS
    w_r = jnp.concatenate([w_group, w_router, jnp.zeros((D_MODEL, pad), F32)], axis=1)
    b_r = jnp.concatenate([b_group, b_router, jnp.zeros((pad,), F32)])[None, :]
    x1ext, ext = _out_router(a, w_out.astype(BF16), x2d, ln1_g[None, :], ln1_b[None, :], w_r, b_r)
    g_idx = ext[:, GIDX_LANE].astype(jnp.int32)
    pos, tile_group, zero_tiles = _sorted_positions(g_idx, n_tiles)
    xs = _dispatch(pos, zero_tiles, x1ext, n_tiles * TM_MOE)
    x2s = _moe(tile_group, xs, w_gate, w_up, w_down, layer)
    return _combine_ple(pos, x2s, p_all, layer, ln2_g[None, :], ln2_b[None, :], ple_w_gate.astype(BF16),
                        ple_b_gate[None, :], ple_w_proj.astype(BF16))


@jax.jit
def _forward(x, p, positions, fox_w_in, fox_b_f, fox_w_out, ret_w_in, ret_w_out, ln1_g, ln1_b, ln2_g, ln2_b,
             moe_w_group, moe_b_group, moe_w_router, moe_b_router, moe_w_gate, moe_w_up, moe_w_down,
             ple_w_proj, ple_w_gate, ple_b_gate):
    batch, seq, d = x.shape
    t = batch * seq
    x2d = x.reshape(t, d)
    p_all = p.reshape(DEPTH * t, PLE_DIM)

    def moe_args(i):
        return (ln1_g[i], ln1_b[i], ln2_g[i], ln2_b[i], moe_w_group[i], moe_b_group[i], moe_w_router[i],
                moe_b_router[i], moe_w_gate, moe_w_up, moe_w_down, ple_w_proj[i], ple_w_gate[i],
                ple_b_gate[i], i)

    w_in = fox_w_in[0]
    scale = jnp.concatenate([jnp.full((D_MODEL,), FOX_HEAD_DIM ** -0.5 * LOG2E, F32),
                             jnp.ones((2 * D_MODEL,), F32)])
    w_qkv = (w_in[:, :3 * D_MODEL] * scale[None, :]).astype(BF16)
    qkv = _proj(x2d, w_qkv, D_MODEL)
    c = _fgate(x2d, w_in[:, 3 * D_MODEL:], fox_b_f[0][None, :], batch, seq)
    c3 = c.reshape(batch, seq, FOX_HEADS).transpose(0, 2, 1)
    attn = _fox_attention(qkv, c, c3[:, :, None, :], batch, seq)
    x2d = _moe_block(attn, fox_w_out[0], x2d, p_all, *moe_args(0))

    w_in = ret_w_in[0]
    scale = jnp.concatenate([jnp.ones((D_MODEL,), F32), jnp.full((D_MODEL,), RET_KEY_DIM ** -0.5, F32),
                             jnp.ones((4 * D_MODEL,), F32)])
    inv_freq = ROPE_BASE ** (-jnp.arange(0, RET_KEY_DIM, 2, dtype=F32) / RET_KEY_DIM)
    cos, sin = _rope_tables(positions.astype(F32).reshape(t, 1), inv_freq[None, :])
    proj = _ret_proj(x2d, (w_in * scale[None, :]).astype(BF16), cos, sin)
    log_gamma = jnp.log(1.0 - 2.0 ** (-5.0 - jnp.arange(RET_HEADS, dtype=F32)))
    ret = _retention(proj, log_gamma, batch, seq)
    x2d = _moe_block(ret, ret_w_out[0], x2d, p_all, *moe_args(1))
    return x2d.reshape(batch, seq, d)


def kernel(x, p, positions, fox_w_in, fox_b_f, fox_w_out, ret_w_in, ret_w_out, ln1_g, ln1_b, ln2_g, ln2_b,
           moe_w_group, moe_b_group, moe_w_router, moe_b_router, moe_w_gate, moe_w_up, moe_w_down,
           ple_w_proj, ple_w_gate, ple_b_gate):
    return _forward(x, p, positions, fox_w_in, fox_b_f, fox_w_out, ret_w_in, ret_w_out, ln1_g, ln1_b,
                    ln2_g, ln2_b, moe_w_group, moe_b_group, moe_w_router, moe_b_router, moe_w_gate,
                    moe_w_up, moe_w_down, ple_w_proj, ple_w_gate, ple_b_gate)
```

```python
import functools

import jax
import jax.numpy as jnp
from jax import lax
from jax.experimental import pallas as pl
from jax.experimental.pallas import tpu as pltpu

F32 = jnp.float32
BF16 = jnp.bfloat16

D_MODEL = 1024
PLE_DIM = 256
FOX_HEADS = 16
FOX_HEAD_DIM = D_MODEL // FOX_HEADS
RET_HEADS = 4
RET_KEY_DIM = D_MODEL // RET_HEADS
RET_VAL_DIM = 2 * D_MODEL // RET_HEADS
ROPE_BASE = 10000.0
N_GROUPS = 4
EXPERTS_PER_GROUP = 4
N_EXPERTS = N_GROUPS * EXPERTS_PER_GROUP
D_EXPERT = 512
DEPTH = 2
DEEPNORM_ALPHA = (2.0 * DEPTH) ** 0.25
LN_EPS = 1e-5

LANES = 128
EXT = LANES
ROW_W = D_MODEL + EXT
GIDX_LANE = 0
CW_LANE0 = 4
NEG_BIG = -1e30
LOG2E = 1.4426950408889634
VMEM_LIMIT = 56 * 1024 * 1024
N_DMA_THREADS = 2

TM_PROJ = 1024
TS_GATE = 512
TQ_ATTN = 512
ATTN_TILES_PER_STEP = 2
ATTN_ROW_BLOCK = 128
TM_OUT = 1024
TM_ROWS = 1024
TM_MOE = 1024
RET_CHUNK = 256


def _cparams(sem):
    return pltpu.CompilerParams(dimension_semantics=sem, vmem_limit_bytes=VMEM_LIMIT)


def _proj_kernel(a_ref, w_ref, o_ref):
    a = a_ref[...].astype(BF16)
    o_ref[...] = jnp.dot(a, w_ref[...], preferred_element_type=F32).astype(o_ref.dtype)


def _proj(a, w, tn):
    m, k = a.shape
    n = w.shape[1]
    tm = min(TM_PROJ, m)
    return pl.pallas_call(
        _proj_kernel,
        out_shape=jax.ShapeDtypeStruct((m, n), BF16),
        grid=(m // tm, n // tn),
        in_specs=[pl.BlockSpec((tm, k), lambda i, j: (i, 0)),
                  pl.BlockSpec((k, tn), lambda i, j: (0, j))],
        out_specs=pl.BlockSpec((tm, tn), lambda i, j: (i, j)),
        compiler_params=_cparams(("parallel", "arbitrary")),
        name="proj",
    )(a, w)


def _split_bf16(x):
    hi = x.astype(BF16)
    return hi, (x - hi.astype(F32)).astype(BF16)


def _split_weight(w):
    hi, lo = _split_bf16(w)
    return jnp.concatenate([hi, lo], axis=1), hi


def _dot_split(x, w_hilo, w_hi):
    x_hi, x_lo = _split_bf16(x)
    r = jnp.dot(x_hi, w_hilo, preferred_element_type=F32)
    return r[:, :LANES] + r[:, LANES:] + jnp.dot(x_lo, w_hi, preferred_element_type=F32)


def _fgate_kernel(x_ref, whl_ref, wh_ref, b_ref, c_ref, carry_ref):
    @pl.when(pl.program_id(1) == 0)
    def _():
        carry_ref[...] = jnp.zeros_like(carry_ref)

    ts = x_ref.shape[0]
    z = _dot_split(x_ref[...], whl_ref[...], wh_ref[...]) + b_ref[...]
    logf = (jnp.minimum(z, 0.0) - jnp.log1p(jnp.exp(-jnp.abs(z)))) * LOG2E
    l1 = logf.astype(BF16)
    rest = logf - l1.astype(F32)
    l2, l3 = _split_bf16(rest)
    row = lax.broadcasted_iota(jnp.int32, (ts, ts), 0)
    col = lax.broadcasted_iota(jnp.int32, (ts, ts), 1)
    tri = jnp.where(row >= col, 1.0, 0.0).astype(BF16)
    parts = jnp.dot(tri, jnp.concatenate([l1, l2, l3], axis=1), preferred_element_type=F32)
    c = parts[:, :LANES] + parts[:, LANES:2 * LANES] + parts[:, 2 * LANES:] + carry_ref[...]
    c_ref[...] = c[:, :c_ref.shape[1]]
    carry_ref[...] = c[ts - 1:ts, :]


def _fgate(x2d, w_f, b_f, batch, seq):
    ts = min(TS_GATE, seq)
    ns = seq // ts
    h = w_f.shape[1]
    w_hilo, w_hi = _split_weight(jnp.pad(w_f, ((0, 0), (0, LANES - h))))
    b_pad = jnp.pad(b_f, ((0, 0), (0, LANES - h)))
    return pl.pallas_call(
        _fgate_kernel,
        out_shape=jax.ShapeDtypeStruct((batch * seq, h), F32),
        grid=(batch, ns),
        in_specs=[pl.BlockSpec((ts, D_MODEL), lambda b, s: (b * ns + s, 0)),
                  pl.BlockSpec((D_MODEL, 2 * LANES), lambda b, s: (0, 0)),
                  pl.BlockSpec((D_MODEL, LANES), lambda b, s: (0, 0)),
                  pl.BlockSpec((1, LANES), lambda b, s: (0, 0))],
        out_specs=pl.BlockSpec((ts, h), lambda b, s: (b * ns + s, 0)),
        scratch_shapes=[pltpu.VMEM((1, LANES), F32)],
        compiler_params=_cparams(("parallel", "arbitrary")),
        name="fgate",
    )(x2d, w_hilo, w_hi, b_pad)


def _fox_attn_kernel(q_ref, k_ref, v_ref, cq_ref, ck_ref, o_ref, m_ref, acc_ref, sa_ref, sb_ref, *, tq, nt):
    i = pl.program_id(2)
    lane = lax.broadcasted_iota(jnp.int32, (tq, LANES), 1)
    row = lax.broadcasted_iota(jnp.int32, (tq, tq), 0)
    col = lax.broadcasted_iota(jnp.int32, (tq, tq), 1)
    causal = row >= col
    head0 = lane < FOX_HEAD_DIM
    m_ref[...] = jnp.full(m_ref.shape, NEG_BIG, F32)
    acc_ref[...] = jnp.zeros(acc_ref.shape, F32)
    lane_k = lax.broadcasted_iota(jnp.int32, (tq, LANES), 1)
    lane_h = lax.broadcasted_iota(jnp.int32, (tq, FOX_HEADS), 1)
    q2 = []
    cq_heads = []
    for tile in range(nt):
        q = q_ref[tile * tq:(tile + 1) * tq, :]
        q2.append(jnp.concatenate([jnp.where(head0, q, jnp.zeros_like(q)),
                                   jnp.where(head0, jnp.zeros_like(q), q)], axis=0))
        c_tok = cq_ref[tile * tq:(tile + 1) * tq, :]
        cq_heads.append([jnp.sum(jnp.where(lane_h == 2 * pl.program_id(1) + h, c_tok, 0.0), axis=1, keepdims=True)
                         for h in range(2)])
    rb = ATTN_ROW_BLOCK

    def scores(tile, kb, s_ref):
        ks = pl.multiple_of(kb * tq, tq)
        s_ref[...] = lax.dot_general(q2[tile], k_ref[pl.ds(ks, tq), :], (((1,), (1,)), ((), ())),
                                     preferred_element_type=F32)

    def softmax_pv(tile, kb, s_ref, masked):
        ks = pl.multiple_of(kb * tq, tq)
        vblk = v_ref[pl.ds(ks, tq), :]
        for h in range(2):
            ck = ck_ref[h, :, pl.ds(ks, tq)]
            vh = jnp.where(lane_k < FOX_HEAD_DIM if h == 0 else lane_k >= FOX_HEAD_DIM,
                           vblk, jnp.ones_like(vblk))
            p_blocks = []
            m_blocks = []
            cq_all = cq_heads[tile][h]
            m_all = m_ref[tile, h]
            for r0 in range(0, tq, rb):
                t = s_ref[h * tq + r0:h * tq + r0 + rb, :] - ck
                if masked:
                    t = jnp.where(causal[r0:r0 + rb, :], t, NEG_BIG)
                cq = cq_all[r0:r0 + rb, :]
                m_new = jnp.maximum(m_all[r0:r0 + rb, :], cq + jnp.max(t, axis=1, keepdims=True))
                m_blocks.append(m_new)
                p_blocks.append(jnp.exp2(t + (cq - m_new)).astype(BF16))
            m_new_all = jnp.concatenate(m_blocks, axis=0)
            pv = jnp.dot(jnp.concatenate(p_blocks, axis=0), vh, preferred_element_type=F32)
            acc_ref[tile, h] = jnp.exp2(m_all - m_new_all) * acc_ref[tile, h] + pv
            m_ref[tile, h] = m_new_all

    bufs = (sa_ref, sb_ref)
    scores(0, 0, bufs[0])
    par = 0
    for u in range(nt):
        g = nt * i + u
        lo = u % 2
        if lo:
            scores(u, 1, bufs[1 - par])
            softmax_pv(u, 0, bufs[par], False)
            par = 1 - par
        cur, oth = bufs[par], bufs[1 - par]

        def pair(j, _, u=u, lo=lo, cur=cur, oth=oth):
            kb = lo + 2 * j
            scores(u, kb + 1, oth)
            softmax_pv(u, kb, cur, False)
            scores(u, kb + 2, cur)
            softmax_pv(u, kb + 1, oth, False)
            return 0

        lax.fori_loop(0, (g - lo) // 2, pair, 0)
        if u + 1 < nt:
            scores(u + 1, 0, oth)
        softmax_pv(u, g, cur, True)
        par = 1 - par

    for tile in range(nt):
        a0 = acc_ref[tile, 0]
        a1 = acc_ref[tile, 1]
        o0 = a0 / pltpu.roll(a0, FOX_HEAD_DIM, 1)
        o1 = a1 / pltpu.roll(a1, FOX_HEAD_DIM, 1)
        o_ref[tile * tq:(tile + 1) * tq, :] = jnp.where(head0, o0, o1).astype(o_ref.dtype)


def _fox_attention(qkv, cq, ck, batch, seq):
    nt = min(ATTN_TILES_PER_STEP, seq // min(TQ_ATTN, seq // 2))
    tq = min(TQ_ATTN, seq // nt)
    assert nt % 2 == 0 and seq % (nt * tq) == 0, "a grid step covers an even number of query tiles"
    ns = seq // (nt * tq)
    hp = FOX_HEADS // 2
    return pl.pallas_call(
        functools.partial(_fox_attn_kernel, tq=tq, nt=nt),
        out_shape=jax.ShapeDtypeStruct((batch * seq, D_MODEL), BF16),
        grid=(batch, hp, ns),
        in_specs=[pl.BlockSpec((nt * tq, LANES), lambda b, p, i: (b * ns + i, p)),
                  pl.BlockSpec((seq, LANES), lambda b, p, i: (b, hp + p)),
                  pl.BlockSpec((seq, LANES), lambda b, p, i: (b, 2 * hp + p)),
                  pl.BlockSpec((nt * tq, FOX_HEADS), lambda b, p, i: (b * ns + i, 0)),
                  pl.BlockSpec((None, 2, 1, seq), lambda b, p, i: (b, p, 0, 0))],
        out_specs=pl.BlockSpec((nt * tq, LANES), lambda b, p, i: (b * ns + i, p)),
        scratch_shapes=[pltpu.VMEM((nt, 2, tq, 1), F32), pltpu.VMEM((nt, 2, tq, LANES), F32),
                        pltpu.VMEM((2 * tq, tq), F32), pltpu.VMEM((2 * tq, tq), F32)],
        compiler_params=_cparams(("parallel", "parallel", "arbitrary")),
        name="fox_attn",
    )(qkv, qkv, qkv, cq, ck)


def _rope_kernel(pos_ref, freq_ref, cos_ref, sin_ref):
    ang = pos_ref[...] * freq_ref[...]
    cos_ref[...] = jnp.cos(ang)
    sin_ref[...] = jnp.sin(ang)


def _rope_tables(pos_f, inv_freq):
    t = pos_f.shape[0]
    half = inv_freq.shape[1]
    tm = min(1024, t)
    return pl.pallas_call(
        _rope_kernel,
        out_shape=(jax.ShapeDtypeStruct((t, half), F32), jax.ShapeDtypeStruct((t, half), F32)),
        grid=(t // tm,),
        in_specs=[pl.BlockSpec((tm, 1), lambda i: (i, 0)),
                  pl.BlockSpec((1, half), lambda i: (0, 0))],
        out_specs=(pl.BlockSpec((tm, half), lambda i: (i, 0)),
                   pl.BlockSpec((tm, half), lambda i: (i, 0))),
        compiler_params=_cparams(("parallel",)),
        name="rope_tables",
    )(pos_f, inv_freq)


def _ret_proj_kernel(a_ref, w_ref, cos_ref, sin_ref, o_ref):
    j = pl.program_id(1)
    a = a_ref[...].astype(BF16)
    acc = jnp.dot(a, w_ref[...], preferred_element_type=F32)

    @pl.when(j < 2)
    def _():
        c = cos_ref[...]
        s = sin_ref[...]
        half = RET_KEY_DIM // 2
        for h in range(RET_HEADS):
            x1 = acc[:, h * RET_KEY_DIM:h * RET_KEY_DIM + half]
            x2 = acc[:, h * RET_KEY_DIM + half:(h + 1) * RET_KEY_DIM]
            o_ref[:, h * RET_KEY_DIM:h * RET_KEY_DIM + half] = (x1 * c - x2 * s).astype(o_ref.dtype)
            o_ref[:, h * RET_KEY_DIM + half:(h + 1) * RET_KEY_DIM] = (x2 * c + x1 * s).astype(o_ref.dtype)

    @pl.when(j >= 2)
    def _():
        o_ref[...] = acc.astype(o_ref.dtype)


def _ret_proj(x2d, w, cos, sin):
    t = x2d.shape[0]
    n = w.shape[1]
    tm = min(TM_PROJ, t)
    tn = D_MODEL
    half = RET_KEY_DIM // 2
    return pl.pallas_call(
        _ret_proj_kernel,
        out_shape=jax.ShapeDtypeStruct((t, n), BF16),
        grid=(t // tm, n // tn),
        in_specs=[pl.BlockSpec((tm, D_MODEL), lambda i, j: (i, 0)),
                  pl.BlockSpec((D_MODEL, tn), lambda i, j: (0, j)),
                  pl.BlockSpec((tm, half), lambda i, j: (i, 0)),
                  pl.BlockSpec((tm, half), lambda i, j: (i, 0))],
        out_specs=pl.BlockSpec((tm, tn), lambda i, j: (i, j)),
        compiler_params=_cparams(("parallel", "arbitrary")),
        name="ret_proj",
    )(x2d, w, cos, sin)


def _retention_kernel(lg_ref, q_ref, k_ref, v_ref, g_ref, o_ref, state_ref, decay_ref, *, chunk):
    dk, dv = RET_KEY_DIM, RET_VAL_DIM

    @pl.when(pl.program_id(1) == 0)
    def _():
        state_ref[...] = jnp.zeros_like(state_ref)
        row = lax.broadcasted_iota(jnp.int32, (chunk, chunk), 0)
        col = lax.broadcasted_iota(jnp.int32, (chunk, chunk), 1)
        diff = (row - col).astype(F32)
        for h in range(RET_HEADS):
            decay_ref[h] = jnp.where(diff >= 0, jnp.exp(jnp.maximum(diff, 0.0) * lg_ref[h]), 0.0)

    idx = lax.broadcasted_iota(jnp.int32, (chunk, 1), 0).astype(F32)
    for h in range(RET_HEADS):
        lg = lg_ref[h]
        q = q_ref[:, h * dk:(h + 1) * dk]
        k = k_ref[:, h * dk:(h + 1) * dk]
        v = v_ref[:, h * dv:(h + 1) * dv]
        scores = lax.dot_general(q, k, (((1,), (1,)), ((), ())), preferred_element_type=F32) * decay_ref[h]
        o = jnp.dot(scores.astype(BF16), v, preferred_element_type=F32)
        state = state_ref[h]
        qd = (q.astype(F32) * jnp.exp((idx + 1.0) * lg)).astype(BF16)
        o = o + jnp.dot(qd, state.astype(BF16), preferred_element_type=F32)
        kd = (k.astype(F32) * jnp.exp((chunk - 1.0 - idx) * lg)).astype(BF16)
        state_ref[h] = state * jnp.exp(chunk * lg) + lax.dot_general(
            kd, v, (((0,), (0,)), ((), ())), preferred_element_type=F32)
        mu = jnp.mean(o, axis=1, keepdims=True)
        oc = o - mu
        var = jnp.mean(oc * oc, axis=1, keepdims=True)
        on = oc * lax.rsqrt(var + LN_EPS)
        g = g_ref[:, h * dv:(h + 1) * dv].astype(F32)
        o_ref[:, h * dv:(h + 1) * dv] = (g * jax.nn.sigmoid(g) * on).astype(o_ref.dtype)


def _retention(proj, log_gamma, batch, seq):
    chunk = min(RET_CHUNK, seq)
    nc = seq // chunk
    dq = RET_HEADS * RET_KEY_DIM
    dvt = RET_HEADS * RET_VAL_DIM
    grid_spec = pltpu.PrefetchScalarGridSpec(
        num_scalar_prefetch=1,
        grid=(batch, nc),
        in_specs=[pl.BlockSpec((chunk, dq), lambda b, c, lg: (b * nc + c, 0)),
                  pl.BlockSpec((chunk, dq), lambda b, c, lg: (b * nc + c, 1)),
                  pl.BlockSpec((chunk, dvt), lambda b, c, lg: (b * nc + c, 1)),
                  pl.BlockSpec((chunk, dvt), lambda b, c, lg: (b * nc + c, 2))],
        out_specs=pl.BlockSpec((chunk, dvt), lambda b, c, lg: (b * nc + c, 0)),
        scratch_shapes=[pltpu.VMEM((RET_HEADS, RET_KEY_DIM, RET_VAL_DIM), F32),
                        pltpu.VMEM((RET_HEADS, chunk, chunk), F32)],
    )
    return pl.pallas_call(
        functools.partial(_retention_kernel, chunk=chunk),
        out_shape=jax.ShapeDtypeStruct((batch * seq, dvt), BF16),
        grid_spec=grid_spec,
        compiler_params=_cparams(("parallel", "arbitrary")),
        name="retention",
    )(log_gamma, proj, proj, proj, proj)


def _layer_norm(y, g, b):
    mu = jnp.mean(y, axis=1, keepdims=True)
    yc = y - mu
    var = jnp.mean(yc * yc, axis=1, keepdims=True)
    return yc * lax.rsqrt(var + LN_EPS) * g + b


def _out_router_kernel(a_ref, w_ref, x_ref, g_ref, b_ref, wrhl_ref, wrh_ref, br_ref, o_ref, ext_ref):
    tm = a_ref.shape[0]
    h = jnp.dot(a_ref[...], w_ref[...], preferred_element_type=F32)
    x1 = _layer_norm(DEEPNORM_ALPHA * x_ref[...] + h, g_ref[...], b_ref[...])
    o_ref[:, :D_MODEL] = x1

    logits = _dot_split(x1, wrhl_ref[...], wrh_ref[...]) + br_ref[...]
    lane = lax.broadcasted_iota(jnp.int32, (tm, LANES), 1)

    def first_argmax(vals, vmax):
        return jnp.min(jnp.where(vals == vmax, lane, LANES), axis=1, keepdims=True)

    gl = jnp.where(lane < N_GROUPS, logits, NEG_BIG)
    gmax = jnp.max(gl, axis=1, keepdims=True)
    gsum = jnp.sum(jnp.where(lane < N_GROUPS, jnp.exp(gl - gmax), 0.0), axis=1, keepdims=True)
    g_val = 1.0 / gsum
    g_idx = first_argmax(gl, gmax)
    lo = CW_LANE0 + EXPERTS_PER_GROUP * g_idx
    in_group = (lane >= lo) & (lane < lo + EXPERTS_PER_GROUP)
    el = jnp.where(in_group, logits, NEG_BIG)
    e1 = jnp.max(el, axis=1, keepdims=True)
    i1 = first_argmax(el, e1)
    el2 = jnp.where(lane == i1, NEG_BIG, el)
    e2 = jnp.max(el2, axis=1, keepdims=True)
    i2 = first_argmax(el2, e2)
    r = jnp.exp(e2 - e1)
    w1 = g_val / (1.0 + r)
    w2 = g_val * r / (1.0 + r)
    ext = jnp.where(lane == i1, w1, 0.0) + jnp.where(lane == i2, w2, 0.0)
    ext = jnp.where(lane == GIDX_LANE, g_idx.astype(F32), ext)
    o_ref[:, D_MODEL:] = ext
    ext_ref[...] = ext


def _out_router(a, w_out, x2d, ln_g, ln_b, w_r, b_r):
    t, din = a.shape
    tm = min(TM_OUT, t)
    w_r_hilo, w_r_hi = _split_weight(w_r)
    return pl.pallas_call(
        _out_router_kernel,
        out_shape=(jax.ShapeDtypeStruct((t, ROW_W), F32), jax.ShapeDtypeStruct((t, EXT), F32)),
        grid=(t // tm,),
        in_specs=[pl.BlockSpec((tm, din), lambda i: (i, 0)),
                  pl.BlockSpec((din, D_MODEL), lambda i: (0, 0)),
                  pl.BlockSpec((tm, D_MODEL), lambda i: (i, 0)),
                  pl.BlockSpec((1, D_MODEL), lambda i: (0, 0)),
                  pl.BlockSpec((1, D_MODEL), lambda i: (0, 0)),
                  pl.BlockSpec((D_MODEL, 2 * LANES), lambda i: (0, 0)),
                  pl.BlockSpec((D_MODEL, LANES), lambda i: (0, 0)),
                  pl.BlockSpec((1, LANES), lambda i: (0, 0))],
        out_specs=(pl.BlockSpec((tm, ROW_W), lambda i: (i, 0)), pl.BlockSpec((tm, EXT), lambda i: (i, 0))),
        compiler_params=_cparams(("parallel",)),
        name="out_router",
    )(a, w_out, x2d, ln_g, ln_b, w_r_hilo, w_r_hi, b_r)


def _row_copy(src_ref, src_row, dst_ref, dst_row, sem):
    return pltpu.make_async_copy(src_ref.at[pl.ds(src_row, 1)], dst_ref.at[pl.ds(dst_row, 1)], sem)


def _dispatch_kernel(pos_ref, zt_ref, x_ref, xs_ref, zero_ref, sem, zsem):
    tm = x_ref.shape[0]
    base = pl.program_id(0) * tm

    @pl.when(pl.program_id(0) == 0)
    def _():
        zero_ref[...] = jnp.zeros_like(zero_ref)

        def zero_tile(k):
            return pltpu.make_async_copy(zero_ref, xs_ref.at[pl.ds(zt_ref[k] * TM_MOE, TM_MOE)], zsem)

        for k in range(zt_ref.shape[0]):
            @pl.when(zt_ref[k] >= 0)
            def _(k=k):
                zero_tile(k).start()

        for k in range(zt_ref.shape[0]):
            @pl.when(zt_ref[k] >= 0)
            def _(k=k):
                zero_tile(k).wait()

    for r in range(tm):
        _row_copy(x_ref, r, xs_ref, pos_ref[base + r], sem).start(priority=r % N_DMA_THREADS)
    pltpu.make_async_copy(x_ref, xs_ref.at[pl.ds(0, tm)], sem).wait()


def _dispatch(pos, zero_tiles, x1ext, n_rows):
    t = x1ext.shape[0]
    tm = min(TM_ROWS, t)
    grid_spec = pltpu.PrefetchScalarGridSpec(
        num_scalar_prefetch=2,
        grid=(t // tm,),
        in_specs=[pl.BlockSpec((tm, ROW_W), lambda i, pos, zt: (i, 0))],
        out_specs=pl.BlockSpec(memory_space=pl.ANY),
        scratch_shapes=[pltpu.VMEM((TM_MOE, ROW_W), F32), pltpu.SemaphoreType.DMA(()),
                        pltpu.SemaphoreType.DMA(())],
    )
    return pl.pallas_call(
        _dispatch_kernel,
        out_shape=jax.ShapeDtypeStruct((n_rows, ROW_W), F32),
        grid_spec=grid_spec,
        compiler_params=_cparams(("arbitrary",)),
        name="dispatch",
    )(pos, zero_tiles, x1ext)


def _moe_kernel(tg_ref, xs_ref, wg_ref, wu_ref, wd_ref, o_ref, xb_ref, acc_ref):
    i = pl.program_id(0)
    e = pl.program_id(1)
    tm = xs_ref.shape[0]
    group = tg_ref[i]
    valid = group < N_GROUPS

    @pl.when(valid)
    def _():
        @pl.when(e == 0)
        def _():
            xb_ref[...] = xs_ref[:, :D_MODEL].astype(BF16)
            acc_ref[...] = jnp.zeros_like(acc_ref)

        lane = lax.broadcasted_iota(jnp.int32, (tm, EXT), 1)
        cw_lane = CW_LANE0 + EXPERTS_PER_GROUP * group + e
        cw = jnp.sum(jnp.where(lane == cw_lane, xs_ref[:, D_MODEL:], 0.0), axis=1, keepdims=True)
        xb = xb_ref[...]
        hg = jnp.dot(xb, wg_ref[...].astype(BF16), preferred_element_type=F32)
        hu = jnp.dot(xb, wu_ref[...].astype(BF16), preferred_element_type=F32)
        hh = (hg * jax.nn.sigmoid(hg) * hu * cw).astype(BF16)
        acc_ref[...] += jnp.dot(hh, wd_ref[...].astype(BF16), preferred_element_type=F32)

        @pl.when(e == EXPERTS_PER_GROUP - 1)
        def _():
            o_ref[...] = DEEPNORM_ALPHA * xs_ref[:, :D_MODEL] + acc_ref[...]

    @pl.when(jnp.logical_not(valid) & (e == EXPERTS_PER_GROUP - 1))
    def _():
        o_ref[...] = jnp.zeros_like(o_ref)


def _moe(tile_group, xs, w_gate, w_up, w_down, layer):
    n_rows = xs.shape[0]
    tm = TM_MOE
    epg = EXPERTS_PER_GROUP

    def expert(i, e, tg):
        return (layer, jnp.where(tg[i] < N_GROUPS, tg[i] * epg + e, N_EXPERTS - 1), 0, 0)

    grid_spec = pltpu.PrefetchScalarGridSpec(
        num_scalar_prefetch=1,
        grid=(n_rows // tm, epg),
        in_specs=[pl.BlockSpec((tm, ROW_W), lambda i, e, tg: (i, 0)),
                  pl.BlockSpec((None, None, D_MODEL, D_EXPERT), expert),
                  pl.BlockSpec((None, None, D_MODEL, D_EXPERT), expert),
                  pl.BlockSpec((None, None, D_EXPERT, D_MODEL), expert)],
        out_specs=pl.BlockSpec((tm, D_MODEL), lambda i, e, tg: (i, 0)),
        scratch_shapes=[pltpu.VMEM((tm, D_MODEL), BF16), pltpu.VMEM((tm, D_MODEL), F32)],
    )
    return pl.pallas_call(
        _moe_kernel,
        out_shape=jax.ShapeDtypeStruct((n_rows, D_MODEL), F32),
        grid_spec=grid_spec,
        compiler_params=_cparams(("parallel", "arbitrary")),
        name="moe",
    )(tile_group, xs, w_gate, w_up, w_down)


def _combine_ple_kernel(pos_ref, x2s_ref, p_ref, g_ref, b_ref, wg_ref, bg_ref, wp_ref, o_ref, buf_a, buf_b, sem):
    tm = buf_a.shape[0]
    i = pl.program_id(0)
    last = pl.num_programs(0) - 1
    bufs = (buf_a, buf_b)

    def wait_buf(s):
        pltpu.make_async_copy(x2s_ref.at[pl.ds(0, tm)], bufs[s], sem.at[s]).wait()

    def gather_inline(tile, s):
        base = tile * tm
        for r in range(tm):
            _row_copy(x2s_ref, pos_ref[base + r], bufs[s], r, sem.at[s]).start(priority=r % N_DMA_THREADS)

    def compute(s):
        x2 = _layer_norm(bufs[s][...], g_ref[...], b_ref[...])
        rows = slice(s * tm, (s + 1) * tm)
        gate = jax.nn.sigmoid(jnp.dot(x2.astype(BF16), wg_ref[...], preferred_element_type=F32) + bg_ref[...])
        pp = jnp.dot(p_ref[rows, :].astype(BF16), wp_ref[...], preferred_element_type=F32)
        o_ref[rows, :] = x2 + gate * pp

    @pl.when(i == 0)
    def _():
        def issue(r, _):
            _row_copy(x2s_ref, pos_ref[r], buf_a, r, sem.at[0]).start()
            return 0

        lax.fori_loop(0, tm, issue, 0, unroll=8)

    wait_buf(0)
    gather_inline(2 * i + 1, 1)
    compute(0)
    wait_buf(1)
    gather_inline(2 * jnp.minimum(i + 1, last), 0)
    compute(1)

    @pl.when(i == last)
    def _():
        wait_buf(0)


def _combine_ple(pos, x2s, p_all, layer, ln_g, ln_b, w_gate, b_gate, w_proj):
    t = pos.shape[0]
    tm = min(TM_ROWS, t // 2)
    steps = t // (2 * tm)
    grid_spec = pltpu.PrefetchScalarGridSpec(
        num_scalar_prefetch=1,
        grid=(steps,),
        in_specs=[pl.BlockSpec(memory_space=pl.ANY),
                  pl.BlockSpec((2 * tm, PLE_DIM), lambda i, pos: (layer * steps + i, 0)),
                  pl.BlockSpec((1, D_MODEL), lambda i, pos: (0, 0)),
                  pl.BlockSpec((1, D_MODEL), lambda i, pos: (0, 0)),
                  pl.BlockSpec((D_MODEL, D_MODEL), lambda i, pos: (0, 0)),
                  pl.BlockSpec((1, D_MODEL), lambda i, pos: (0, 0)),
                  pl.BlockSpec((PLE_DIM, D_MODEL), lambda i, pos: (0, 0))],
        out_specs=pl.BlockSpec((2 * tm, D_MODEL), lambda i, pos: (i, 0)),
        scratch_shapes=[pltpu.VMEM((tm, D_MODEL), F32), pltpu.VMEM((tm, D_MODEL), F32),
                        pltpu.SemaphoreType.DMA((2,))],
    )
    return pl.pallas_call(
        _combine_ple_kernel,
        out_shape=jax.ShapeDtypeStruct((t, D_MODEL), F32),
        grid_spec=grid_spec,
        compiler_params=_cparams(("arbitrary",)),
        name="combine_ple",
    )(pos, x2s, p_all, ln_g, ln_b, w_gate, b_gate, w_proj)


def _sorted_positions(g_idx, n_tiles):
    onehot = (g_idx[:, None] == jnp.arange(N_GROUPS, dtype=jnp.int32)[None, :]).astype(jnp.int32)
    counts = jnp.sum(onehot, axis=0)
    tiles = (counts + TM_MOE - 1) // TM_MOE
    tile_end = jnp.cumsum(tiles)
    tile_start = tile_end - tiles
    rank = jnp.sum((jnp.cumsum(onehot, axis=0) - onehot) * onehot, axis=1)
    pos = jnp.sum(onehot * (tile_start * TM_MOE)[None, :], axis=1) + rank
    tile_ids = jnp.arange(n_tiles, dtype=jnp.int32)
    tile_group = jnp.sum((tile_ids[:, None] >= tile_end[None, :]).astype(jnp.int32), axis=1)
    trailing = tile_end[-1] + jnp.arange(N_GROUPS, dtype=jnp.int32)
    zero_tiles = jnp.concatenate([jnp.where(tiles > 0, tile_end - 1, -1),
                                  jnp.where(trailing < n_tiles, trailing, -1)])
    return pos.astype(jnp.int32), tile_group.astype(jnp.int32), zero_tiles.astype(jnp.int32)


def _moe_block(a, w_out, x2d, p_all, ln1_g, ln1_b, ln2_g, ln2_b, w_group, b_group, w_router, b_router,
               w_gate, w_up, w_down, ple_w_proj, ple_w_gate, ple_b_gate, layer):
    t = x2d.shape[0]
    n_tiles = t // TM_MOE + N_GROUPS
    pad = LANES - CW_LANE0 - N_EXPERTS
    w_r = jnp.concatenate([w_group, w_router, jnp.zeros((D_MODEL, pad), F32)], axis=1)
    b_r = jnp.concatenate([b_group, b_router, jnp.zeros((pad,), F32)])[None, :]
    x1ext, ext = _out_router(a, w_out.astype(BF16), x2d, ln1_g[None, :], ln1_b[None, :], w_r, b_r)
    g_idx = ext[:, GIDX_LANE].astype(jnp.int32)
    pos, tile_group, zero_tiles = _sorted_positions(g_idx, n_tiles)
    xs = _dispatch(pos, zero_tiles, x1ext, n_tiles * TM_MOE)
    x2s = _moe(tile_group, xs, w_gate, w_up, w_down, layer)
    return _combine_ple(pos, x2s, p_all, layer, ln2_g[None, :], ln2_b[None, :], ple_w_gate.astype(BF16),
                        ple_b_gate[None, :], ple_w_proj.astype(BF16))


@jax.jit
def _forward(x, p, positions, fox_w_in, fox_b_f, fox_w_out, ret_w_in, ret_w_out, ln1_g, ln1_b, ln2_g, ln2_b,
             moe_w_group, moe_b_group, moe_w_router, moe_b_router, moe_w_gate, moe_w_up, moe_w_down,
             ple_w_proj, ple_w_gate, ple_b_gate):
    batch, seq, d = x.shape
    t = batch * seq
    x2d = x.reshape(t, d)
    p_all = p.reshape(DEPTH * t, PLE_DIM)

    def moe_args(i):
        return (ln1_g[i], ln1_b[i], ln2_g[i], ln2_b[i], moe_w_group[i], moe_b_group[i], moe_w_router[i],
                moe_b_router[i], moe_w_gate, moe_w_up, moe_w_down, ple_w_proj[i], ple_w_gate[i],
                ple_b_gate[i], i)

    w_in = fox_w_in[0]
    scale = jnp.concatenate([jnp.full((D_MODEL,), FOX_HEAD_DIM ** -0.5 * LOG2E, F32),
                             jnp.ones((2 * D_MODEL,), F32)])
    w_qkv = (w_in[:, :3 * D_MODEL] * scale[None, :]).astype(BF16)
    qkv = _proj(x2d, w_qkv, D_MODEL)
    c = _fgate(x2d, w_in[:, 3 * D_MODEL:], fox_b_f[0][None, :], batch, seq)
    c3 = c.reshape(batch, seq, FOX_HEADS).transpose(0, 2, 1)
    attn = _fox_attention(qkv, c, c3[:, :, None, :], batch, seq)
    x2d = _moe_block(attn, fox_w_out[0], x2d, p_all, *moe_args(0))

    w_in = ret_w_in[0]
    scale = jnp.concatenate([jnp.ones((D_MODEL,), F32), jnp.full((D_MODEL,), RET_KEY_DIM ** -0.5, F32),
                             jnp.ones((4 * D_MODEL,), F32)])
    inv_freq = ROPE_BASE ** (-jnp.arange(0, RET_KEY_DIM, 2, dtype=F32) / RET_KEY_DIM)
    cos, sin = _rope_tables(positions.astype(F32).reshape(t, 1), inv_freq[None, :])
    proj = _ret_proj(x2d, (w_in * scale[None, :]).astype(BF16), cos, sin)
    log_gamma = jnp.log(1.0 - 2.0 ** (-5.0 - jnp.arange(RET_HEADS, dtype=F32)))
    ret = _retention(proj, log_gamma, batch, seq)
    x2d = _moe_block(ret, ret_w_out[0], x2d, p_all, *moe_args(1))
    return x2d.reshape(batch, seq, d)


def kernel(x, p, positions, fox_w_in, fox_b_f, fox_w_out, ret_w_in, ret_w_out, ln1_g, ln1_b, ln2_g, ln2_b,
           moe_w_group, moe_b_group, moe_w_router, moe_b_router, moe_w_gate, moe_w_up, moe_w_down,
           ple_w_proj, ple_w_gate, ple_b_gate):
    return _forward(x, p, positions, fox_w_in, fox_b_f, fox_w_out, ret_w_in, ret_w_out, ln1_g, ln1_b,
                    ln2_g, ln2_b, moe_w_group, moe_b_group, moe_w_router, moe_b_router, moe_w_gate,
                    moe_w_up, moe_w_down, ple_w_proj, ple_w_gate, ple_b_gate)
```

```python
import functools

import jax
import jax.numpy as jnp
from jax import lax
from jax.experimental import pallas as pl
from jax.experimental.pallas import tpu as pltpu

F32 = jnp.float32
BF16 = jnp.bfloat16

D_MODEL = 1024
PLE_DIM = 256
FOX_HEADS = 16
FOX_HEAD_DIM = D_MODEL // FOX_HEADS
RET_HEADS = 4
RET_KEY_DIM = D_MODEL // RET_HEADS
RET_VAL_DIM = 2 * D_MODEL // RET_HEADS
ROPE_BASE = 10000.0
N_GROUPS = 4
EXPERTS_PER_GROUP = 4
N_EXPERTS = N_GROUPS * EXPERTS_PER_GROUP
D_EXPERT = 512
DEPTH = 2
DEEPNORM_ALPHA = (2.0 * DEPTH) ** 0.25
LN_EPS = 1e-5

LANES = 128
EXT = LANES
ROW_W = D_MODEL + EXT
GIDX_LANE = 0
CW_LANE0 = 4
NEG_BIG = -1e30
LOG2E = 1.4426950408889634
VMEM_LIMIT = 56 * 1024 * 1024
N_DMA_THREADS = 2

TM_PROJ = 2048
TS_GATE = 512
TQ_ATTN = 512
ATTN_TILES_PER_STEP = 2
ATTN_ROW_BLOCK = 128
TM_OUT = 1024
TM_ROWS = 1024
TM_MOE = 1024
RET_CHUNK = 256


def _cparams(sem):
    return pltpu.CompilerParams(dimension_semantics=sem, vmem_limit_bytes=VMEM_LIMIT)


def _proj_kernel(a_ref, w_ref, o_ref):
    a = a_ref[...].astype(BF16)
    o_ref[...] = jnp.dot(a, w_ref[...], preferred_element_type=F32).astype(o_ref.dtype)


def _proj(a, w, tn):
    m, k = a.shape
    n = w.shape[1]
    tm = min(TM_PROJ, m)
    return pl.pallas_call(
        _proj_kernel,
        out_shape=jax.ShapeDtypeStruct((m, n), BF16),
        grid=(m // tm, n // tn),
        in_specs=[pl.BlockSpec((tm, k), lambda i, j: (i, 0)),
                  pl.BlockSpec((k, tn), lambda i, j: (0, j))],
        out_specs=pl.BlockSpec((tm, tn), lambda i, j: (i, j)),
        compiler_params=_cparams(("parallel", "arbitrary")),
        name="proj",
    )(a, w)


def _split_bf16(x):
    hi = x.astype(BF16)
    return hi, (x - hi.astype(F32)).astype(BF16)


def _split_weight(w):
    hi, lo = _split_bf16(w)
    return jnp.concatenate([hi, lo], axis=1), hi


def _dot_split(x, w_hilo, w_hi):
    x_hi, x_lo = _split_bf16(x)
    r = jnp.dot(x_hi, w_hilo, preferred_element_type=F32)
    return r[:, :LANES] + r[:, LANES:] + jnp.dot(x_lo, w_hi, preferred_element_type=F32)


def _fgate_kernel(x_ref, whl_ref, wh_ref, b_ref, c_ref, carry_ref):
    @pl.when(pl.program_id(1) == 0)
    def _():
        carry_ref[...] = jnp.zeros_like(carry_ref)

    ts = x_ref.shape[0]
    z = _dot_split(x_ref[...], whl_ref[...], wh_ref[...]) + b_ref[...]
    logf = (jnp.minimum(z, 0.0) - jnp.log1p(jnp.exp(-jnp.abs(z)))) * LOG2E
    l1 = logf.astype(BF16)
    rest = logf - l1.astype(F32)
    l2, l3 = _split_bf16(rest)
    row = lax.broadcasted_iota(jnp.int32, (ts, ts), 0)
    col = lax.broadcasted_iota(jnp.int32, (ts, ts), 1)
    tri = jnp.where(row >= col, 1.0, 0.0).astype(BF16)
    parts = jnp.dot(tri, jnp.concatenate([l1, l2, l3], axis=1), preferred_element_type=F32)
    c = parts[:, :LANES] + parts[:, LANES:2 * LANES] + parts[:, 2 * LANES:] + carry_ref[...]
    c_ref[...] = c[:, :c_ref.shape[1]]
    carry_ref[...] = c[ts - 1:ts, :]


def _fgate(x2d, w_f, b_f, batch, seq):
    ts = min(TS_GATE, seq)
    ns = seq // ts
    h = w_f.shape[1]
    w_hilo, w_hi = _split_weight(jnp.pad(w_f, ((0, 0), (0, LANES - h))))
    b_pad = jnp.pad(b_f, ((0, 0), (0, LANES - h)))
    return pl.pallas_call(
        _fgate_kernel,
        out_shape=jax.ShapeDtypeStruct((batch * seq, h), F32),
        grid=(batch, ns),
        in_specs=[pl.BlockSpec((ts, D_MODEL), lambda b, s: (b * ns + s, 0)),
                  pl.BlockSpec((D_MODEL, 2 * LANES), lambda b, s: (0, 0)),
                  pl.BlockSpec((D_MODEL, LANES), lambda b, s: (0, 0)),
                  pl.BlockSpec((1, LANES), lambda b, s: (0, 0))],
        out_specs=pl.BlockSpec((ts, h), lambda b, s: (b * ns + s, 0)),
        scratch_shapes=[pltpu.VMEM((1, LANES), F32)],
        compiler_params=_cparams(("parallel", "arbitrary")),
        name="fgate",
    )(x2d, w_hilo, w_hi, b_pad)


def _fox_attn_kernel(q_ref, k_ref, v_ref, cq_ref, ck_ref, o_ref, m_ref, acc_ref, sa_ref, sb_ref, *, tq, nt):
    i = pl.program_id(2)
    lane = lax.broadcasted_iota(jnp.int32, (tq, LANES), 1)
    row = lax.broadcasted_iota(jnp.int32, (tq, tq), 0)
    col = lax.broadcasted_iota(jnp.int32, (tq, tq), 1)
    causal = row >= col
    head0 = lane < FOX_HEAD_DIM
    m_ref[...] = jnp.full(m_ref.shape, NEG_BIG, F32)
    acc_ref[...] = jnp.zeros(acc_ref.shape, F32)
    lane_k = lax.broadcasted_iota(jnp.int32, (tq, LANES), 1)
    lane_h = lax.broadcasted_iota(jnp.int32, (tq, FOX_HEADS), 1)
    q2 = []
    cq_heads = []
    for tile in range(nt):
        q = q_ref[tile * tq:(tile + 1) * tq, :]
        q2.append(jnp.concatenate([jnp.where(head0, q, jnp.zeros_like(q)),
                                   jnp.where(head0, jnp.zeros_like(q), q)], axis=0))
        c_tok = cq_ref[tile * tq:(tile + 1) * tq, :]
        cq_heads.append([jnp.sum(jnp.where(lane_h == 2 * pl.program_id(1) + h, c_tok, 0.0), axis=1, keepdims=True)
                         for h in range(2)])
    rb = ATTN_ROW_BLOCK

    def scores(tile, kb, s_ref):
        ks = pl.multiple_of(kb * tq, tq)
        s_ref[...] = lax.dot_general(q2[tile], k_ref[pl.ds(ks, tq), :], (((1,), (1,)), ((), ())),
                                     preferred_element_type=F32)

    def softmax_pv(tile, kb, s_ref, masked):
        ks = pl.multiple_of(kb * tq, tq)
        vblk = v_ref[pl.ds(ks, tq), :]
        for h in range(2):
            ck = ck_ref[h, :, pl.ds(ks, tq)]
            vh = jnp.where(lane_k < FOX_HEAD_DIM if h == 0 else lane_k >= FOX_HEAD_DIM,
                           vblk, jnp.ones_like(vblk))
            p_blocks = []
            m_blocks = []
            cq_all = cq_heads[tile][h]
            m_all = m_ref[tile, h]
            for r0 in range(0, tq, rb):
                t = s_ref[h * tq + r0:h * tq + r0 + rb, :] - ck
                if masked:
                    t = jnp.where(causal[r0:r0 + rb, :], t, NEG_BIG)
                cq = cq_all[r0:r0 + rb, :]
                m_new = jnp.maximum(m_all[r0:r0 + rb, :], cq + jnp.max(t, axis=1, keepdims=True))
                m_blocks.append(m_new)
                p_blocks.append(jnp.exp2(t + (cq - m_new)).astype(BF16))
            m_new_all = jnp.concatenate(m_blocks, axis=0)
            pv = jnp.dot(jnp.concatenate(p_blocks, axis=0), vh, preferred_element_type=F32)
            acc_ref[tile, h] = jnp.exp2(m_all - m_new_all) * acc_ref[tile, h] + pv
            m_ref[tile, h] = m_new_all

    bufs = (sa_ref, sb_ref)
    scores(0, 0, bufs[0])
    par = 0
    for u in range(nt):
        g = nt * i + u
        lo = u % 2
        if lo:
            scores(u, 1, bufs[1 - par])
            softmax_pv(u, 0, bufs[par], False)
            par = 1 - par
        cur, oth = bufs[par], bufs[1 - par]

        def pair(j, _, u=u, lo=lo, cur=cur, oth=oth):
            kb = lo + 2 * j
            scores(u, kb + 1, oth)
            softmax_pv(u, kb, cur, False)
            scores(u, kb + 2, cur)
            softmax_pv(u, kb + 1, oth, False)
            return 0

        lax.fori_loop(0, (g - lo) // 2, pair, 0)
        if u + 1 < nt:
            scores(u + 1, 0, oth)
        softmax_pv(u, g, cur, True)
        par = 1 - par

    for tile in range(nt):
        a0 = acc_ref[tile, 0]
        a1 = acc_ref[tile, 1]
        o0 = a0 / pltpu.roll(a0, FOX_HEAD_DIM, 1)
        o1 = a1 / pltpu.roll(a1, FOX_HEAD_DIM, 1)
        o_ref[tile * tq:(tile + 1) * tq, :] = jnp.where(head0, o0, o1).astype(o_ref.dtype)


def _fox_attention(qkv, cq, ck, batch, seq):
    nt = min(ATTN_TILES_PER_STEP, seq // min(TQ_ATTN, seq // 2))
    tq = min(TQ_ATTN, seq // nt)
    assert nt % 2 == 0 and seq % (nt * tq) == 0, "a grid step covers an even number of query tiles"
    ns = seq // (nt * tq)
    hp = FOX_HEADS // 2
    return pl.pallas_call(
        functools.partial(_fox_attn_kernel, tq=tq, nt=nt),
        out_shape=jax.ShapeDtypeStruct((batch * seq, D_MODEL), BF16),
        grid=(batch, hp, ns),
        in_specs=[pl.BlockSpec((nt * tq, LANES), lambda b, p, i: (b * ns + i, p)),
                  pl.BlockSpec((seq, LANES), lambda b, p, i: (b, hp + p)),
                  pl.BlockSpec((seq, LANES), lambda b, p, i: (b, 2 * hp + p)),
                  pl.BlockSpec((nt * tq, FOX_HEADS), lambda b, p, i: (b * ns + i, 0)),
                  pl.BlockSpec((None, 2, 1, seq), lambda b, p, i: (b, p, 0, 0))],
        out_specs=pl.BlockSpec((nt * tq, LANES), lambda b, p, i: (b * ns + i, p)),
        scratch_shapes=[pltpu.VMEM((nt, 2, tq, 1), F32), pltpu.VMEM((nt, 2, tq, LANES), F32),
                        pltpu.VMEM((2 * tq, tq), F32), pltpu.VMEM((2 * tq, tq), F32)],
        compiler_params=_cparams(("parallel", "parallel", "arbitrary")),
        name="fox_attn",
    )(qkv, qkv, qkv, cq, ck)


def _rope_kernel(pos_ref, freq_ref, cos_ref, sin_ref):
    ang = pos_ref[...] * freq_ref[...]
    cos_ref[...] = jnp.cos(ang)
    sin_ref[...] = jnp.sin(ang)


def _rope_tables(pos_f, inv_freq):
    t = pos_f.shape[0]
    half = inv_freq.shape[1]
    tm = min(1024, t)
    return pl.pallas_call(
        _rope_kernel,
        out_shape=(jax.ShapeDtypeStruct((t, half), F32), jax.ShapeDtypeStruct((t, half), F32)),
        grid=(t // tm,),
        in_specs=[pl.BlockSpec((tm, 1), lambda i: (i, 0)),
                  pl.BlockSpec((1, half), lambda i: (0, 0))],
        out_specs=(pl.BlockSpec((tm, half), lambda i: (i, 0)),
                   pl.BlockSpec((tm, half), lambda i: (i, 0))),
        compiler_params=_cparams(("parallel",)),
        name="rope_tables",
    )(pos_f, inv_freq)


def _ret_proj_kernel(a_ref, w_ref, cos_ref, sin_ref, o_ref):
    j = pl.program_id(1)
    a = a_ref[...].astype(BF16)
    acc = jnp.dot(a, w_ref[...], preferred_element_type=F32)

    @pl.when(j < 2)
    def _():
        c = cos_ref[...]
        s = sin_ref[...]
        half = RET_KEY_DIM // 2
        for h in range(RET_HEADS):
            x1 = acc[:, h * RET_KEY_DIM:h * RET_KEY_DIM + half]
            x2 = acc[:, h * RET_KEY_DIM + half:(h + 1) * RET_KEY_DIM]
            o_ref[:, h * RET_KEY_DIM:h * RET_KEY_DIM + half] = (x1 * c - x2 * s).astype(o_ref.dtype)
            o_ref[:, h * RET_KEY_DIM + half:(h + 1) * RET_KEY_DIM] = (x2 * c + x1 * s).astype(o_ref.dtype)

    @pl.when(j >= 2)
    def _():
        o_ref[...] = acc.astype(o_ref.dtype)


def _ret_proj(x2d, w, cos, sin):
    t = x2d.shape[0]
    n = w.shape[1]
    tm = min(TM_PROJ, t)
    tn = D_MODEL
    half = RET_KEY_DIM // 2
    return pl.pallas_call(
        _ret_proj_kernel,
        out_shape=jax.ShapeDtypeStruct((t, n), BF16),
        grid=(t // tm, n // tn),
        in_specs=[pl.BlockSpec((tm, D_MODEL), lambda i, j: (i, 0)),
                  pl.BlockSpec((D_MODEL, tn), lambda i, j: (0, j)),
                  pl.BlockSpec((tm, half), lambda i, j: (i, 0)),
                  pl.BlockSpec((tm, half), lambda i, j: (i, 0))],
        out_specs=pl.BlockSpec((tm, tn), lambda i, j: (i, j)),
        compiler_params=_cparams(("parallel", "arbitrary")),
        name="ret_proj",
    )(x2d, w, cos, sin)


def _retention_kernel(lg_ref, q_ref, k_ref, v_ref, g_ref, o_ref, state_ref, decay_ref, *, chunk):
    dk, dv = RET_KEY_DIM, RET_VAL_DIM

    @pl.when(pl.program_id(1) == 0)
    def _():
        state_ref[...] = jnp.zeros_like(state_ref)
        row = lax.broadcasted_iota(jnp.int32, (chunk, chunk), 0)
        col = lax.broadcasted_iota(jnp.int32, (chunk, chunk), 1)
        diff = (row - col).astype(F32)
        for h in range(RET_HEADS):
            decay_ref[h] = jnp.where(diff >= 0, jnp.exp(jnp.maximum(diff, 0.0) * lg_ref[h]), 0.0)

    idx = lax.broadcasted_iota(jnp.int32, (chunk, 1), 0).astype(F32)
    for h in range(RET_HEADS):
        lg = lg_ref[h]
        q = q_ref[:, h * dk:(h + 1) * dk]
        k = k_ref[:, h * dk:(h + 1) * dk]
        v = v_ref[:, h * dv:(h + 1) * dv]
        scores = lax.dot_general(q, k, (((1,), (1,)), ((), ())), preferred_element_type=F32) * decay_ref[h]
        o = jnp.dot(scores.astype(BF16), v, preferred_element_type=F32)
        state = state_ref[h]
        qd = (q.astype(F32) * jnp.exp((idx + 1.0) * lg)).astype(BF16)
        o = o + jnp.dot(qd, state.astype(BF16), preferred_element_type=F32)
        kd = (k.astype(F32) * jnp.exp((chunk - 1.0 - idx) * lg)).astype(BF16)
        state_ref[h] = state * jnp.exp(chunk * lg) + lax.dot_general(
            kd, v, (((0,), (0,)), ((), ())), preferred_element_type=F32)
        mu = jnp.mean(o, axis=1, keepdims=True)
        oc = o - mu
        var = jnp.mean(oc * oc, axis=1, keepdims=True)
        on = oc * lax.rsqrt(var + LN_EPS)
        g = g_ref[:, h * dv:(h + 1) * dv].astype(F32)
        o_ref[:, h * dv:(h + 1) * dv] = (g * jax.nn.sigmoid(g) * on).astype(o_ref.dtype)


def _retention(proj, log_gamma, batch, seq):
    chunk = min(RET_CHUNK, seq)
    nc = seq // chunk
    dq = RET_HEADS * RET_KEY_DIM
    dvt = RET_HEADS * RET_VAL_DIM
    grid_spec = pltpu.PrefetchScalarGridSpec(
        num_scalar_prefetch=1,
        grid=(batch, nc),
        in_specs=[pl.BlockSpec((chunk, dq), lambda b, c, lg: (b * nc + c, 0)),
                  pl.BlockSpec((chunk, dq), lambda b, c, lg: (b * nc + c, 1)),
                  pl.BlockSpec((chunk, dvt), lambda b, c, lg: (b * nc + c, 1)),
                  pl.BlockSpec((chunk, dvt), lambda b, c, lg: (b * nc + c, 2))],
        out_specs=pl.BlockSpec((chunk, dvt), lambda b, c, lg: (b * nc + c, 0)),
        scratch_shapes=[pltpu.VMEM((RET_HEADS, RET_KEY_DIM, RET_VAL_DIM), F32),
                        pltpu.VMEM((RET_HEADS, chunk, chunk), F32)],
    )
    return pl.pallas_call(
        functools.partial(_retention_kernel, chunk=chunk),
        out_shape=jax.ShapeDtypeStruct((batch * seq, dvt), BF16),
        grid_spec=grid_spec,
        compiler_params=_cparams(("parallel", "arbitrary")),
        name="retention",
    )(log_gamma, proj, proj, proj, proj)


def _layer_norm(y, g, b):
    mu = jnp.mean(y, axis=1, keepdims=True)
    yc = y - mu
    var = jnp.mean(yc * yc, axis=1, keepdims=True)
    return yc * lax.rsqrt(var + LN_EPS) * g + b


def _out_router_kernel(a_ref, w_ref, x_ref, g_ref, b_ref, wrhl_ref, wrh_ref, br_ref, o_ref, ext_ref):
    tm = a_ref.shape[0]
    h = jnp.dot(a_ref[...], w_ref[...], preferred_element_type=F32)
    x1 = _layer_norm(DEEPNORM_ALPHA * x_ref[...] + h, g_ref[...], b_ref[...])
    o_ref[:, :D_MODEL] = x1

    logits = _dot_split(x1, wrhl_ref[...], wrh_ref[...]) + br_ref[...]
    lane = lax.broadcasted_iota(jnp.int32, (tm, LANES), 1)

    def first_argmax(vals, vmax):
        return jnp.min(jnp.where(vals == vmax, lane, LANES), axis=1, keepdims=True)

    gl = jnp.where(lane < N_GROUPS, logits, NEG_BIG)
    gmax = jnp.max(gl, axis=1, keepdims=True)
    gsum = jnp.sum(jnp.where(lane < N_GROUPS, jnp.exp(gl - gmax), 0.0), axis=1, keepdims=True)
    g_val = 1.0 / gsum
    g_idx = first_argmax(gl, gmax)
    lo = CW_LANE0 + EXPERTS_PER_GROUP * g_idx
    in_group = (lane >= lo) & (lane < lo + EXPERTS_PER_GROUP)
    el = jnp.where(in_group, logits, NEG_BIG)
    e1 = jnp.max(el, axis=1, keepdims=True)
    i1 = first_argmax(el, e1)
    el2 = jnp.where(lane == i1, NEG_BIG, el)
    e2 = jnp.max(el2, axis=1, keepdims=True)
    i2 = first_argmax(el2, e2)
    r = jnp.exp(e2 - e1)
    w1 = g_val / (1.0 + r)
    w2 = g_val * r / (1.0 + r)
    ext = jnp.where(lane == i1, w1, 0.0) + jnp.where(lane == i2, w2, 0.0)
    ext = jnp.where(lane == GIDX_LANE, g_idx.astype(F32), ext)
    o_ref[:, D_MODEL:] = ext
    ext_ref[...] = ext


def _out_router(a, w_out, x2d, ln_g, ln_b, w_r, b_r):
    t, din = a.shape
    tm = min(TM_OUT, t)
    w_r_hilo, w_r_hi = _split_weight(w_r)
    return pl.pallas_call(
        _out_router_kernel,
        out_shape=(jax.ShapeDtypeStruct((t, ROW_W), F32), jax.ShapeDtypeStruct((t, EXT), F32)),
        grid=(t // tm,),
        in_specs=[pl.BlockSpec((tm, din), lambda i: (i, 0)),
                  pl.BlockSpec((din, D_MODEL), lambda i: (0, 0)),
                  pl.BlockSpec((tm, D_MODEL), lambda i: (i, 0)),
                  pl.BlockSpec((1, D_MODEL), lambda i: (0, 0)),
                  pl.BlockSpec((1, D_MODEL), lambda i: (0, 0)),
                  pl.BlockSpec((D_MODEL, 2 * LANES), lambda i: (0, 0)),
                  pl.BlockSpec((D_MODEL, LANES), lambda i: (0, 0)),
                  pl.BlockSpec((1, LANES), lambda i: (0, 0))],
        out_specs=(pl.BlockSpec((tm, ROW_W), lambda i: (i, 0)), pl.BlockSpec((tm, EXT), lambda i: (i, 0))),
        compiler_params=_cparams(("parallel",)),
        name="out_router",
    )(a, w_out, x2d, ln_g, ln_b, w_r_hilo, w_r_hi, b_r)


def _row_copy(src_ref, src_row, dst_ref, dst_row, sem):
    return pltpu.make_async_copy(src_ref.at[pl.ds(src_row, 1)], dst_ref.at[pl.ds(dst_row, 1)], sem)


def _dispatch_kernel(pos_ref, zt_ref, x_ref, xs_ref, zero_ref, sem, zsem):
    tm = x_ref.shape[0]
    base = pl.program_id(0) * tm

    @pl.when(pl.program_id(0) == 0)
    def _():
        zero_ref[...] = jnp.zeros_like(zero_ref)

        def zero_tile(k):
            return pltpu.make_async_copy(zero_ref, xs_ref.at[pl.ds(zt_ref[k] * TM_MOE, TM_MOE)], zsem)

        for k in range(zt_ref.shape[0]):
            @pl.when(zt_ref[k] >= 0)
            def _(k=k):
                zero_tile(k).start()

        for k in range(zt_ref.shape[0]):
            @pl.when(zt_ref[k] >= 0)
            def _(k=k):
                zero_tile(k).wait()

    for r in range(tm):
        _row_copy(x_ref, r, xs_ref, pos_ref[base + r], sem).start(priority=r % N_DMA_THREADS)
    pltpu.make_async_copy(x_ref, xs_ref.at[pl.ds(0, tm)], sem).wait()


def _dispatch(pos, zero_tiles, x1ext, n_rows):
    t = x1ext.shape[0]
    tm = min(TM_ROWS, t)
    grid_spec = pltpu.PrefetchScalarGridSpec(
        num_scalar_prefetch=2,
        grid=(t // tm,),
        in_specs=[pl.BlockSpec((tm, ROW_W), lambda i, pos, zt: (i, 0))],
        out_specs=pl.BlockSpec(memory_space=pl.ANY),
        scratch_shapes=[pltpu.VMEM((TM_MOE, ROW_W), F32), pltpu.SemaphoreType.DMA(()),
                        pltpu.SemaphoreType.DMA(())],
    )
    return pl.pallas_call(
        _dispatch_kernel,
        out_shape=jax.ShapeDtypeStruct((n_rows, ROW_W), F32),
        grid_spec=grid_spec,
        compiler_params=_cparams(("arbitrary",)),
        name="dispatch",
    )(pos, zero_tiles, x1ext)


def _moe_kernel(tg_ref, xs_ref, wg_ref, wu_ref, wd_ref, o_ref, xb_ref, acc_ref):
    i = pl.program_id(0)
    e = pl.program_id(1)
    tm = xs_ref.shape[0]
    group = tg_ref[i]
    valid = group < N_GROUPS

    @pl.when(valid)
    def _():
        @pl.when(e == 0)
        def _():
            xb_ref[...] = xs_ref[:, :D_MODEL].astype(BF16)
            acc_ref[...] = jnp.zeros_like(acc_ref)

        lane = lax.broadcasted_iota(jnp.int32, (tm, EXT), 1)
        cw_lane = CW_LANE0 + EXPERTS_PER_GROUP * group + e
        cw = jnp.sum(jnp.where(lane == cw_lane, xs_ref[:, D_MODEL:], 0.0), axis=1, keepdims=True)
        xb = xb_ref[...]
        hg = jnp.dot(xb, wg_ref[...].astype(BF16), preferred_element_type=F32)
        hu = jnp.dot(xb, wu_ref[...].astype(BF16), preferred_element_type=F32)
        hh = (hg * jax.nn.sigmoid(hg) * hu * cw).astype(BF16)
        acc_ref[...] += jnp.dot(hh, wd_ref[...].astype(BF16), preferred_element_type=F32)

        @pl.when(e == EXPERTS_PER_GROUP - 1)
        def _():
            o_ref[...] = DEEPNORM_ALPHA * xs_ref[:, :D_MODEL] + acc_ref[...]

    @pl.when(jnp.logical_not(valid) & (e == EXPERTS_PER_GROUP - 1))
    def _():
        o_ref[...] = jnp.zeros_like(o_ref)


def _moe(tile_group, xs, w_gate, w_up, w_down, layer):
    n_rows = xs.shape[0]
    tm = TM_MOE
    epg = EXPERTS_PER_GROUP

    def expert(i, e, tg):
        return (layer, jnp.where(tg[i] < N_GROUPS, tg[i] * epg + e, N_EXPERTS - 1), 0, 0)

    grid_spec = pltpu.PrefetchScalarGridSpec(
        num_scalar_prefetch=1,
        grid=(n_rows // tm, epg),
        in_specs=[pl.BlockSpec((tm, ROW_W), lambda i, e, tg: (i, 0)),
                  pl.BlockSpec((None, None, D_MODEL, D_EXPERT), expert),
                  pl.BlockSpec((None, None, D_MODEL, D_EXPERT), expert),
                  pl.BlockSpec((None, None, D_EXPERT, D_MODEL), expert)],
        out_specs=pl.BlockSpec((tm, D_MODEL), lambda i, e, tg: (i, 0)),
        scratch_shapes=[pltpu.VMEM((tm, D_MODEL), BF16), pltpu.VMEM((tm, D_MODEL), F32)],
    )
    return pl.pallas_call(
        _moe_kernel,
        out_shape=jax.ShapeDtypeStruct((n_rows, D_MODEL), F32),
        grid_spec=grid_spec,
        compiler_params=_cparams(("parallel", "arbitrary")),
        name="moe",
    )(tile_group, xs, w_gate, w_up, w_down)


def _combine_ple_kernel(pos_ref, x2s_ref, p_ref, g_ref, b_ref, wg_ref, bg_ref, wp_ref, o_ref, buf_a, buf_b, sem):
    tm = buf_a.shape[0]
    i = pl.program_id(0)
    last = pl.num_programs(0) - 1
    bufs = (buf_a, buf_b)

    def wait_buf(s):
        pltpu.make_async_copy(x2s_ref.at[pl.ds(0, tm)], bufs[s], sem.at[s]).wait()

    def gather_inline(tile, s):
        base = tile * tm
        for r in range(tm):
            _row_copy(x2s_ref, pos_ref[base + r], bufs[s], r, sem.at[s]).start(priority=r % N_DMA_THREADS)

    def compute(s):
        x2 = _layer_norm(bufs[s][...], g_ref[...], b_ref[...])
        rows = slice(s * tm, (s + 1) * tm)
        gate = jax.nn.sigmoid(jnp.dot(x2.astype(BF16), wg_ref[...], preferred_element_type=F32) + bg_ref[...])
        pp = jnp.dot(p_ref[rows, :].astype(BF16), wp_ref[...], preferred_element_type=F32)
        o_ref[rows, :] = x2 + gate * pp

    @pl.when(i == 0)
    def _():
        def issue(r, _):
            _row_copy(x2s_ref, pos_ref[r], buf_a, r, sem.at[0]).start()
            return 0

        lax.fori_loop(0, tm, issue, 0, unroll=8)

    wait_buf(0)
    gather_inline(2 * i + 1, 1)
    compute(0)
    wait_buf(1)
    gather_inline(2 * jnp.minimum(i + 1, last), 0)
    compute(1)

    @pl.when(i == last)
    def _():
        wait_buf(0)


def _combine_ple(pos, x2s, p_all, layer, ln_g, ln_b, w_gate, b_gate, w_proj):
    t = pos.shape[0]
    tm = min(TM_ROWS, t // 2)
    steps = t // (2 * tm)
    grid_spec = pltpu.PrefetchScalarGridSpec(
        num_scalar_prefetch=1,
        grid=(steps,),
        in_specs=[pl.BlockSpec(memory_space=pl.ANY),
                  pl.BlockSpec((2 * tm, PLE_DIM), lambda i, pos: (layer * steps + i, 0)),
                  pl.BlockSpec((1, D_MODEL), lambda i, pos: (0, 0)),
                  pl.BlockSpec((1, D_MODEL), lambda i, pos: (0, 0)),
                  pl.BlockSpec((D_MODEL, D_MODEL), lambda i, pos: (0, 0)),
                  pl.BlockSpec((1, D_MODEL), lambda i, pos: (0, 0)),
                  pl.BlockSpec((PLE_DIM, D_MODEL), lambda i, pos: (0, 0))],
        out_specs=pl.BlockSpec((2 * tm, D_MODEL), lambda i, pos: (i, 0)),
        scratch_shapes=[pltpu.VMEM((tm, D_MODEL), F32), pltpu.VMEM((tm, D_MODEL), F32),
                        pltpu.SemaphoreType.DMA((2,))],
    )
    return pl.pallas_call(
        _combine_ple_kernel,
        out_shape=jax.ShapeDtypeStruct((t, D_MODEL), F32),
        grid_spec=grid_spec,
        compiler_params=_cparams(("arbitrary",)),
        name="combine_ple",
    )(pos, x2s, p_all, ln_g, ln_b, w_gate, b_gate, w_proj)


def _sorted_positions(g_idx, n_tiles):
    onehot = (g_idx[:, None] == jnp.arange(N_GROUPS, dtype=jnp.int32)[None, :]).astype(jnp.int32)
    counts = jnp.sum(onehot, axis=0)
    tiles = (counts + TM_MOE - 1) // TM_MOE
    tile_end = jnp.cumsum(tiles)
    tile_start = tile_end - tiles
    rank = jnp.sum((jnp.cumsum(onehot, axis=0) - onehot) * onehot, axis=1)
    pos = jnp.sum(onehot * (tile_start * TM_MOE)[None, :], axis=1) + rank
    tile_ids = jnp.arange(n_tiles, dtype=jnp.int32)
    tile_group = jnp.sum((tile_ids[:, None] >= tile_end[None, :]).astype(jnp.int32), axis=1)
    trailing = tile_end[-1] + jnp.arange(N_GROUPS, dtype=jnp.int32)
    zero_tiles = jnp.concatenate([jnp.where(tiles > 0, tile_end - 1, -1),
                                  jnp.where(trailing < n_tiles, trailing, -1)])
    return pos.astype(jnp.int32), tile_group.astype(jnp.int32), zero_tiles.astype(jnp.int32)


def _moe_block(a, w_out, x2d, p_all, ln1_g, ln1_b, ln2_g, ln2_b, w_group, b_group, w_router, b_router,
               w_gate, w_up, w_down, ple_w_proj, ple_w_gate, ple_b_gate, layer):
    t = x2d.shape[0]
    n_tiles = t // TM_MOE + N_GROUPS
    pad = LANES - CW_LANE0 - N_EXPERTS
    w_r = jnp.concatenate([w_group, w_router, jnp.zeros((D_MODEL, pad), F32)], axis=1)
    b_r = jnp.concatenate([b_group, b_router, jnp.zeros((pad,), F32)])[None, :]
    x1ext, ext = _out_router(a, w_out.astype(BF16), x2d, ln1_g[None, :], ln1_b[None, :], w_r, b_r)
    g_idx = ext[:, GIDX_LANE].astype(jnp.int32)
    pos, tile_group, zero_tiles = _sorted_positions(g_idx, n_tiles)
    xs = _dispatch(pos, zero_tiles, x1ext, n_tiles * TM_MOE)
    x2s = _moe(tile_group, xs, w_gate, w_up, w_down, layer)
    return _combine_ple(pos, x2s, p_all, layer, ln2_g[None, :], ln2_b[None, :], ple_w_gate.astype(BF16),
                        ple_b_gate[None, :], ple_w_proj.astype(BF16))


@jax.jit
def _forward(x, p, positions, fox_w_in, fox_b_f, fox_w_out, ret_w_in, ret_w_out, ln1_g, ln1_b, ln2_g, ln2_b,
             moe_w_group, moe_b_group, moe_w_router, moe_b_router, moe_w_gate, moe_w_up, moe_w_down,
             ple_w_proj, ple_w_gate, ple_b_gate):
    batch, seq, d = x.shape
    t = batch * seq
    x2d = x.reshape(t, d)
    p_all = p.reshape(DEPTH * t, PLE_DIM)

    def moe_args(i):
        return (ln1_g[i], ln1_b[i], ln2_g[i], ln2_b[i], moe_w_group[i], moe_b_group[i], moe_w_router[i],
                moe_b_router[i], moe_w_gate, moe_w_up, moe_w_down, ple_w_proj[i], ple_w_gate[i],
                ple_b_gate[i], i)

    w_in = fox_w_in[0]
    scale = jnp.concatenate([jnp.full((D_MODEL,), FOX_HEAD_DIM ** -0.5 * LOG2E, F32),
                             jnp.ones((2 * D_MODEL,), F32)])
    w_qkv = (w_in[:, :3 * D_MODEL] * scale[None, :]).astype(BF16)
    qkv = _proj(x2d, w_qkv, D_MODEL)
    c = _fgate(x2d, w_in[:, 3 * D_MODEL:], fox_b_f[0][None, :], batch, seq)
    c3 = c.reshape(batch, seq, FOX_HEADS).transpose(0, 2, 1)
    attn = _fox_attention(qkv, c, c3[:, :, None, :], batch, seq)
    x2d = _moe_block(attn, fox_w_out[0], x2d, p_all, *moe_args(0))

    w_in = ret_w_in[0]
    scale = jnp.concatenate([jnp.ones((D_MODEL,), F32), jnp.full((D_MODEL,), RET_KEY_DIM ** -0.5, F32),
                             jnp.ones((4 * D_MODEL,), F32)])
    inv_freq = ROPE_BASE ** (-jnp.arange(0, RET_KEY_DIM, 2, dtype=F32) / RET_KEY_DIM)
    cos, sin = _rope_tables(positions.astype(F32).reshape(t, 1), inv_freq[None, :])
    proj = _ret_proj(x2d, (w_in * scale[None, :]).astype(BF16), cos, sin)
    log_gamma = jnp.log(1.0 - 2.0 ** (-5.0 - jnp.arange(RET_HEADS, dtype=F32)))
    ret = _retention(proj, log_gamma, batch, seq)
    x2d = _moe_block(ret, ret_w_out[0], x2d, p_all, *moe_args(1))
    return x2d.reshape(batch, seq, d)


def kernel(x, p, positions, fox_w_in, fox_b_f, fox_w_out, ret_w_in, ret_w_out, ln1_g, ln1_b, ln2_g, ln2_b,
           moe_w_group, moe_b_group, moe_w_router, moe_b_router, moe_w_gate, moe_w_up, moe_w_down,
           ple_w_proj, ple_w_gate, ple_b_gate):
    return _forward(x, p, positions, fox_w_in, fox_b_f, fox_w_out, ret_w_in, ret_w_out, ln1_g, ln1_b,
                    ln2_g, ln2_b, moe_w_group, moe_b_group, moe_w_router, moe_b_router, moe_w_gate,
                    moe_w_up, moe_w_down, ple_w_proj, ple_w_gate, ple_b_gate)
```

```python
import functools

import jax
import jax.numpy as jnp
from jax import lax
from jax.experimental import pallas as pl
from jax.experimental.pallas import tpu as pltpu

F32 = jnp.float32
BF16 = jnp.bfloat16

D_MODEL = 1024
PLE_DIM = 256
FOX_HEADS = 16
FOX_HEAD_DIM = D_MODEL // FOX_HEADS
RET_HEADS = 4
RET_KEY_DIM = D_MODEL // RET_HEADS
RET_VAL_DIM = 2 * D_MODEL // RET_HEADS
ROPE_BASE = 10000.0
N_GROUPS = 4
EXPERTS_PER_GROUP = 4
N_EXPERTS = N_GROUPS * EXPERTS_PER_GROUP
D_EXPERT = 512
DEPTH = 2
DEEPNORM_ALPHA = (2.0 * DEPTH) ** 0.25
LN_EPS = 1e-5

LANES = 128
EXT = LANES
ROW_W = D_MODEL + EXT
GIDX_LANE = 0
CW_LANE0 = 4
NEG_BIG = -1e30
LOG2E = 1.4426950408889634
VMEM_LIMIT = 56 * 1024 * 1024
N_DMA_THREADS = 2

TM_PROJ = 2048
TS_GATE = 512
TQ_ATTN = 512
ATTN_TILES_PER_STEP = 2
ATTN_ROW_BLOCK = 128
TM_OUT = 1024
TM_ROWS = 1024
TM_MOE = 1024
RET_CHUNK = 256


def _cparams(sem):
    return pltpu.CompilerParams(dimension_semantics=sem, vmem_limit_bytes=VMEM_LIMIT)


def _proj_kernel(a_ref, w_ref, o_ref):
    a = a_ref[...].astype(BF16)
    o_ref[...] = jnp.dot(a, w_ref[...], preferred_element_type=F32).astype(o_ref.dtype)


def _proj(a, w, tn):
    m, k = a.shape
    n = w.shape[1]
    tm = min(TM_PROJ, m)
    return pl.pallas_call(
        _proj_kernel,
        out_shape=jax.ShapeDtypeStruct((m, n), BF16),
        grid=(m // tm, n // tn),
        in_specs=[pl.BlockSpec((tm, k), lambda i, j: (i, 0)),
                  pl.BlockSpec((k, tn), lambda i, j: (0, j))],
        out_specs=pl.BlockSpec((tm, tn), lambda i, j: (i, j)),
        compiler_params=_cparams(("parallel", "arbitrary")),
        name="proj",
    )(a, w)


def _split_bf16(x):
    hi = x.astype(BF16)
    return hi, (x - hi.astype(F32)).astype(BF16)


def _split_weight(w):
    hi, lo = _split_bf16(w)
    return jnp.concatenate([hi, lo], axis=1), hi


def _dot_split(x, w_hilo, w_hi):
    x_hi, x_lo = _split_bf16(x)
    r = jnp.dot(x_hi, w_hilo, preferred_element_type=F32)
    return r[:, :LANES] + r[:, LANES:] + jnp.dot(x_lo, w_hi, preferred_element_type=F32)


def _fgate_kernel(x_ref, whl_ref, wh_ref, b_ref, c_ref, ct_ref, carry_ref):
    @pl.when(pl.program_id(1) == 0)
    def _():
        carry_ref[...] = jnp.zeros_like(carry_ref)

    ts = x_ref.shape[0]
    z = _dot_split(x_ref[...], whl_ref[...], wh_ref[...]) + b_ref[...]
    logf = (jnp.minimum(z, 0.0) - jnp.log1p(jnp.exp(-jnp.abs(z)))) * LOG2E
    l1 = logf.astype(BF16)
    rest = logf - l1.astype(F32)
    l2, l3 = _split_bf16(rest)
    row = lax.broadcasted_iota(jnp.int32, (ts, ts), 0)
    col = lax.broadcasted_iota(jnp.int32, (ts, ts), 1)
    tri = jnp.where(row >= col, 1.0, 0.0).astype(BF16)
    parts = jnp.dot(tri, jnp.concatenate([l1, l2, l3], axis=1), preferred_element_type=F32)
    c = parts[:, :LANES] + parts[:, LANES:2 * LANES] + parts[:, 2 * LANES:] + carry_ref[...]
    c_ref[...] = c[:, :c_ref.shape[1]]
    ct = c.T
    for h in range(ct_ref.shape[0]):
        ct_ref[h] = ct[h:h + 1, :]
    carry_ref[...] = c[ts - 1:ts, :]


def _fgate(x2d, w_f, b_f, batch, seq):
    ts = min(TS_GATE, seq)
    ns = seq // ts
    h = w_f.shape[1]
    w_hilo, w_hi = _split_weight(jnp.pad(w_f, ((0, 0), (0, LANES - h))))
    b_pad = jnp.pad(b_f, ((0, 0), (0, LANES - h)))
    return pl.pallas_call(
        _fgate_kernel,
        out_shape=(jax.ShapeDtypeStruct((batch * seq, h), F32),
                   jax.ShapeDtypeStruct((batch, h, 1, seq), F32)),
        grid=(batch, ns),
        in_specs=[pl.BlockSpec((ts, D_MODEL), lambda b, s: (b * ns + s, 0)),
                  pl.BlockSpec((D_MODEL, 2 * LANES), lambda b, s: (0, 0)),
                  pl.BlockSpec((D_MODEL, LANES), lambda b, s: (0, 0)),
                  pl.BlockSpec((1, LANES), lambda b, s: (0, 0))],
        out_specs=(pl.BlockSpec((ts, h), lambda b, s: (b * ns + s, 0)),
                   pl.BlockSpec((None, h, 1, ts), lambda b, s: (b, 0, 0, s))),
        scratch_shapes=[pltpu.VMEM((1, LANES), F32)],
        compiler_params=_cparams(("parallel", "arbitrary")),
        name="fgate",
    )(x2d, w_hilo, w_hi, b_pad)


def _fox_attn_kernel(q_ref, k_ref, v_ref, cq_ref, ck_ref, o_ref, m_ref, acc_ref, sa_ref, sb_ref, *, tq, nt):
    i = pl.program_id(2)
    lane = lax.broadcasted_iota(jnp.int32, (tq, LANES), 1)
    row = lax.broadcasted_iota(jnp.int32, (tq, tq), 0)
    col = lax.broadcasted_iota(jnp.int32, (tq, tq), 1)
    causal = row >= col
    head0 = lane < FOX_HEAD_DIM
    m_ref[...] = jnp.full(m_ref.shape, NEG_BIG, F32)
    acc_ref[...] = jnp.zeros(acc_ref.shape, F32)
    lane_k = lax.broadcasted_iota(jnp.int32, (tq, LANES), 1)
    lane_h = lax.broadcasted_iota(jnp.int32, (tq, FOX_HEADS), 1)
    q2 = []
    cq_heads = []
    for tile in range(nt):
        q = q_ref[tile * tq:(tile + 1) * tq, :]
        q2.append(jnp.concatenate([jnp.where(head0, q, jnp.zeros_like(q)),
                                   jnp.where(head0, jnp.zeros_like(q), q)], axis=0))
        c_tok = cq_ref[tile * tq:(tile + 1) * tq, :]
        cq_heads.append([jnp.sum(jnp.where(lane_h == 2 * pl.program_id(1) + h, c_tok, 0.0), axis=1, keepdims=True)
                         for h in range(2)])
    rb = ATTN_ROW_BLOCK

    def scores(tile, kb, s_ref):
        ks = pl.multiple_of(kb * tq, tq)
        s_ref[...] = lax.dot_general(q2[tile], k_ref[pl.ds(ks, tq), :], (((1,), (1,)), ((), ())),
                                     preferred_element_type=F32)

    def softmax_pv(tile, kb, s_ref, masked):
        ks = pl.multiple_of(kb * tq, tq)
        vblk = v_ref[pl.ds(ks, tq), :]
        for h in range(2):
            ck = ck_ref[h, :, pl.ds(ks, tq)]
            vh = jnp.where(lane_k < FOX_HEAD_DIM if h == 0 else lane_k >= FOX_HEAD_DIM,
                           vblk, jnp.ones_like(vblk))
            p_blocks = []
            m_blocks = []
            cq_all = cq_heads[tile][h]
            m_all = m_ref[tile, h]
            for r0 in range(0, tq, rb):
                t = s_ref[h * tq + r0:h * tq + r0 + rb, :] - ck
                if masked:
                    t = jnp.where(causal[r0:r0 + rb, :], t, NEG_BIG)
                cq = cq_all[r0:r0 + rb, :]
                m_new = jnp.maximum(m_all[r0:r0 + rb, :], cq + jnp.max(t, axis=1, keepdims=True))
                m_blocks.append(m_new)
                p_blocks.append(jnp.exp2(t + (cq - m_new)).astype(BF16))
            m_new_all = jnp.concatenate(m_blocks, axis=0)
            pv = jnp.dot(jnp.concatenate(p_blocks, axis=0), vh, preferred_element_type=F32)
            acc_ref[tile, h] = jnp.exp2(m_all - m_new_all) * acc_ref[tile, h] + pv
            m_ref[tile, h] = m_new_all

    bufs = (sa_ref, sb_ref)
    scores(0, 0, bufs[0])
    par = 0
    for u in range(nt):
        g = nt * i + u
        lo = u % 2
        if lo:
            scores(u, 1, bufs[1 - par])
            softmax_pv(u, 0, bufs[par], False)
            par = 1 - par
        cur, oth = bufs[par], bufs[1 - par]

        def pair(j, _, u=u, lo=lo, cur=cur, oth=oth):
            kb = lo + 2 * j
            scores(u, kb + 1, oth)
            softmax_pv(u, kb, cur, False)
            scores(u, kb + 2, cur)
            softmax_pv(u, kb + 1, oth, False)
            return 0

        lax.fori_loop(0, (g - lo) // 2, pair, 0)
        if u + 1 < nt:
            scores(u + 1, 0, oth)
        softmax_pv(u, g, cur, True)
        par = 1 - par

    for tile in range(nt):
        a0 = acc_ref[tile, 0]
        a1 = acc_ref[tile, 1]
        o0 = a0 / pltpu.roll(a0, FOX_HEAD_DIM, 1)
        o1 = a1 / pltpu.roll(a1, FOX_HEAD_DIM, 1)
        o_ref[tile * tq:(tile + 1) * tq, :] = jnp.where(head0, o0, o1).astype(o_ref.dtype)


def _fox_attention(qkv, cq, ck, batch, seq):
    nt = min(ATTN_TILES_PER_STEP, seq // min(TQ_ATTN, seq // 2))
    tq = min(TQ_ATTN, seq // nt)
    assert nt % 2 == 0 and seq % (nt * tq) == 0, "a grid step covers an even number of query tiles"
    ns = seq // (nt * tq)
    hp = FOX_HEADS // 2
    return pl.pallas_call(
        functools.partial(_fox_attn_kernel, tq=tq, nt=nt),
        out_shape=jax.ShapeDtypeStruct((batch * seq, D_MODEL), BF16),
        grid=(batch, hp, ns),
        in_specs=[pl.BlockSpec((nt * tq, LANES), lambda b, p, i: (b * ns + i, p)),
                  pl.BlockSpec((seq, LANES), lambda b, p, i: (b, hp + p)),
                  pl.BlockSpec((seq, LANES), lambda b, p, i: (b, 2 * hp + p)),
                  pl.BlockSpec((nt * tq, FOX_HEADS), lambda b, p, i: (b * ns + i, 0)),
                  pl.BlockSpec((None, 2, 1, seq), lambda b, p, i: (b, p, 0, 0))],
        out_specs=pl.BlockSpec((nt * tq, LANES), lambda b, p, i: (b * ns + i, p)),
        scratch_shapes=[pltpu.VMEM((nt, 2, tq, 1), F32), pltpu.VMEM((nt, 2, tq, LANES), F32),
                        pltpu.VMEM((2 * tq, tq), F32), pltpu.VMEM((2 * tq, tq), F32)],
        compiler_params=_cparams(("parallel", "parallel", "arbitrary")),
        name="fox_attn",
    )(qkv, qkv, qkv, cq, ck)


def _rope_kernel(pos_ref, freq_ref, cos_ref, sin_ref):
    ang = pos_ref[...] * freq_ref[...]
    cos_ref[...] = jnp.cos(ang)
    sin_ref[...] = jnp.sin(ang)


def _rope_tables(pos_f, inv_freq):
    t = pos_f.shape[0]
    half = inv_freq.shape[1]
    tm = min(1024, t)
    return pl.pallas_call(
        _rope_kernel,
        out_shape=(jax.ShapeDtypeStruct((t, half), F32), jax.ShapeDtypeStruct((t, half), F32)),
        grid=(t // tm,),
        in_specs=[pl.BlockSpec((tm, 1), lambda i: (i, 0)),
                  pl.BlockSpec((1, half), lambda i: (0, 0))],
        out_specs=(pl.BlockSpec((tm, half), lambda i: (i, 0)),
                   pl.BlockSpec((tm, half), lambda i: (i, 0))),
        compiler_params=_cparams(("parallel",)),
        name="rope_tables",
    )(pos_f, inv_freq)


def _ret_proj_kernel(a_ref, w_ref, cos_ref, sin_ref, o_ref):
    j = pl.program_id(1)
    a = a_ref[...].astype(BF16)
    acc = jnp.dot(a, w_ref[...], preferred_element_type=F32)

    @pl.when(j < 2)
    def _():
        c = cos_ref[...]
        s = sin_ref[...]
        half = RET_KEY_DIM // 2
        for h in range(RET_HEADS):
            x1 = acc[:, h * RET_KEY_DIM:h * RET_KEY_DIM + half]
            x2 = acc[:, h * RET_KEY_DIM + half:(h + 1) * RET_KEY_DIM]
            o_ref[:, h * RET_KEY_DIM:h * RET_KEY_DIM + half] = (x1 * c - x2 * s).astype(o_ref.dtype)
            o_ref[:, h * RET_KEY_DIM + half:(h + 1) * RET_KEY_DIM] = (x2 * c + x1 * s).astype(o_ref.dtype)

    @pl.when(j >= 2)
    def _():
        o_ref[...] = acc.astype(o_ref.dtype)


def _ret_proj(x2d, w, cos, sin):
    t = x2d.shape[0]
    n = w.shape[1]
    tm = min(TM_PROJ, t)
    tn = D_MODEL
    half = RET_KEY_DIM // 2
    return pl.pallas_call(
        _ret_proj_kernel,
        out_shape=jax.ShapeDtypeStruct((t, n), BF16),
        grid=(t // tm, n // tn),
        in_specs=[pl.BlockSpec((tm, D_MODEL), lambda i, j: (i, 0)),
                  pl.BlockSpec((D_MODEL, tn), lambda i, j: (0, j)),
                  pl.BlockSpec((tm, half), lambda i, j: (i, 0)),
                  pl.BlockSpec((tm, half), lambda i, j: (i, 0))],
        out_specs=pl.BlockSpec((tm, tn), lambda i, j: (i, j)),
        compiler_params=_cparams(("parallel", "arbitrary")),
        name="ret_proj",
    )(x2d, w, cos, sin)


def _retention_kernel(lg_ref, q_ref, k_ref, v_ref, g_ref, o_ref, state_ref, decay_ref, *, chunk):
    dk, dv = RET_KEY_DIM, RET_VAL_DIM

    @pl.when(pl.program_id(1) == 0)
    def _():
        state_ref[...] = jnp.zeros_like(state_ref)
        row = lax.broadcasted_iota(jnp.int32, (chunk, chunk), 0)
        col = lax.broadcasted_iota(jnp.int32, (chunk, chunk), 1)
        diff = (row - col).astype(F32)
        for h in range(RET_HEADS):
            decay_ref[h] = jnp.where(diff >= 0, jnp.exp(jnp.maximum(diff, 0.0) * lg_ref[h]), 0.0)

    idx = lax.broadcasted_iota(jnp.int32, (chunk, 1), 0).astype(F32)
    for h in range(RET_HEADS):
        lg = lg_ref[h]
        q = q_ref[:, h * dk:(h + 1) * dk]
        k = k_ref[:, h * dk:(h + 1) * dk]
        v = v_ref[:, h * dv:(h + 1) * dv]
        scores = lax.dot_general(q, k, (((1,), (1,)), ((), ())), preferred_element_type=F32) * decay_ref[h]
        o = jnp.dot(scores.astype(BF16), v, preferred_element_type=F32)
        state = state_ref[h]
        qd = (q.astype(F32) * jnp.exp((idx + 1.0) * lg)).astype(BF16)
        o = o + jnp.dot(qd, state.astype(BF16), preferred_element_type=F32)
        kd = (k.astype(F32) * jnp.exp((chunk - 1.0 - idx) * lg)).astype(BF16)
        state_ref[h] = state * jnp.exp(chunk * lg) + lax.dot_general(
            kd, v, (((0,), (0,)), ((), ())), preferred_element_type=F32)
        mu = jnp.mean(o, axis=1, keepdims=True)
        oc = o - mu
        var = jnp.mean(oc * oc, axis=1, keepdims=True)
        on = oc * lax.rsqrt(var + LN_EPS)
        g = g_ref[:, h * dv:(h + 1) * dv].astype(F32)
        o_ref[:, h * dv:(h + 1) * dv] = (g * jax.nn.sigmoid(g) * on).astype(o_ref.dtype)


def _retention(proj, log_gamma, batch, seq):
    chunk = min(RET_CHUNK, seq)
    nc = seq // chunk
    dq = RET_HEADS * RET_KEY_DIM
    dvt = RET_HEADS * RET_VAL_DIM
    grid_spec = pltpu.PrefetchScalarGridSpec(
        num_scalar_prefetch=1,
        grid=(batch, nc),
        in_specs=[pl.BlockSpec((chunk, dq), lambda b, c, lg: (b * nc + c, 0)),
                  pl.BlockSpec((chunk, dq), lambda b, c, lg: (b * nc + c, 1)),
                  pl.BlockSpec((chunk, dvt), lambda b, c, lg: (b * nc + c, 1)),
                  pl.BlockSpec((chunk, dvt), lambda b, c, lg: (b * nc + c, 2))],
        out_specs=pl.BlockSpec((chunk, dvt), lambda b, c, lg: (b * nc + c, 0)),
        scratch_shapes=[pltpu.VMEM((RET_HEADS, RET_KEY_DIM, RET_VAL_DIM), F32),
                        pltpu.VMEM((RET_HEADS, chunk, chunk), F32)],
    )
    return pl.pallas_call(
        functools.partial(_retention_kernel, chunk=chunk),
        out_shape=jax.ShapeDtypeStruct((batch * seq, dvt), BF16),
        grid_spec=grid_spec,
        compiler_params=_cparams(("parallel", "arbitrary")),
        name="retention",
    )(log_gamma, proj, proj, proj, proj)


def _layer_norm(y, g, b):
    mu = jnp.mean(y, axis=1, keepdims=True)
    yc = y - mu
    var = jnp.mean(yc * yc, axis=1, keepdims=True)
    return yc * lax.rsqrt(var + LN_EPS) * g + b


def _out_router_kernel(a_ref, w_ref, x_ref, g_ref, b_ref, wrhl_ref, wrh_ref, br_ref, o_ref, ext_ref):
    tm = a_ref.shape[0]
    h = jnp.dot(a_ref[...], w_ref[...], preferred_element_type=F32)
    x1 = _layer_norm(DEEPNORM_ALPHA * x_ref[...] + h, g_ref[...], b_ref[...])
    o_ref[:, :D_MODEL] = x1

    logits = _dot_split(x1, wrhl_ref[...], wrh_ref[...]) + br_ref[...]
    lane = lax.broadcasted_iota(jnp.int32, (tm, LANES), 1)

    def first_argmax(vals, vmax):
        return jnp.min(jnp.where(vals == vmax, lane, LANES), axis=1, keepdims=True)

    gl = jnp.where(lane < N_GROUPS, logits, NEG_BIG)
    gmax = jnp.max(gl, axis=1, keepdims=True)
    gsum = jnp.sum(jnp.where(lane < N_GROUPS, jnp.exp(gl - gmax), 0.0), axis=1, keepdims=True)
    g_val = 1.0 / gsum
    g_idx = first_argmax(gl, gmax)
    lo = CW_LANE0 + EXPERTS_PER_GROUP * g_idx
    in_group = (lane >= lo) & (lane < lo + EXPERTS_PER_GROUP)
    el = jnp.where(in_group, logits, NEG_BIG)
    e1 = jnp.max(el, axis=1, keepdims=True)
    i1 = first_argmax(el, e1)
    el2 = jnp.where(lane == i1, NEG_BIG, el)
    e2 = jnp.max(el2, axis=1, keepdims=True)
    i2 = first_argmax(el2, e2)
    r = jnp.exp(e2 - e1)
    w1 = g_val / (1.0 + r)
    w2 = g_val * r / (1.0 + r)
    ext = jnp.where(lane == i1, w1, 0.0) + jnp.where(lane == i2, w2, 0.0)
    ext = jnp.where(lane == GIDX_LANE, g_idx.astype(F32), ext)
    o_ref[:, D_MODEL:] = ext
    ext_ref[...] = ext


def _out_router(a, w_out, x2d, ln_g, ln_b, w_r, b_r):
    t, din = a.shape
    tm = min(TM_OUT, t)
    w_r_hilo, w_r_hi = _split_weight(w_r)
    return pl.pallas_call(
        _out_router_kernel,
        out_shape=(jax.ShapeDtypeStruct((t, ROW_W), F32), jax.ShapeDtypeStruct((t, EXT), F32)),
        grid=(t // tm,),
        in_specs=[pl.BlockSpec((tm, din), lambda i: (i, 0)),
                  pl.BlockSpec((din, D_MODEL), lambda i: (0, 0)),
                  pl.BlockSpec((tm, D_MODEL), lambda i: (i, 0)),
                  pl.BlockSpec((1, D_MODEL), lambda i: (0, 0)),
                  pl.BlockSpec((1, D_MODEL), lambda i: (0, 0)),
                  pl.BlockSpec((D_MODEL, 2 * LANES), lambda i: (0, 0)),
                  pl.BlockSpec((D_MODEL, LANES), lambda i: (0, 0)),
                  pl.BlockSpec((1, LANES), lambda i: (0, 0))],
        out_specs=(pl.BlockSpec((tm, ROW_W), lambda i: (i, 0)), pl.BlockSpec((tm, EXT), lambda i: (i, 0))),
        compiler_params=_cparams(("parallel",)),
        name="out_router",
    )(a, w_out, x2d, ln_g, ln_b, w_r_hilo, w_r_hi, b_r)


def _row_copy(src_ref, src_row, dst_ref, dst_row, sem):
    return pltpu.make_async_copy(src_ref.at[pl.ds(src_row, 1)], dst_ref.at[pl.ds(dst_row, 1)], sem)


def _dispatch_kernel(pos_ref, zt_ref, x_ref, xs_ref, zero_ref, sem, zsem):
    tm = x_ref.shape[0]
    base = pl.program_id(0) * tm

    @pl.when(pl.program_id(0) == 0)
    def _():
        zero_ref[...] = jnp.zeros_like(zero_ref)

        def zero_tile(k):
            return pltpu.make_async_copy(zero_ref, xs_ref.at[pl.ds(zt_ref[k] * TM_MOE, TM_MOE)], zsem)

        for k in range(zt_ref.shape[0]):
            @pl.when(zt_ref[k] >= 0)
            def _(k=k):
                zero_tile(k).start()

        for k in range(zt_ref.shape[0]):
            @pl.when(zt_ref[k] >= 0)
            def _(k=k):
                zero_tile(k).wait()

    for r in range(tm):
        _row_copy(x_ref, r, xs_ref, pos_ref[base + r], sem).start(priority=r % N_DMA_THREADS)
    pltpu.make_async_copy(x_ref, xs_ref.at[pl.ds(0, tm)], sem).wait()


def _dispatch(pos, zero_tiles, x1ext, n_rows):
    t = x1ext.shape[0]
    tm = min(TM_ROWS, t)
    grid_spec = pltpu.PrefetchScalarGridSpec(
        num_scalar_prefetch=2,
        grid=(t // tm,),
        in_specs=[pl.BlockSpec((tm, ROW_W), lambda i, pos, zt: (i, 0))],
        out_specs=pl.BlockSpec(memory_space=pl.ANY),
        scratch_shapes=[pltpu.VMEM((TM_MOE, ROW_W), F32), pltpu.SemaphoreType.DMA(()),
                        pltpu.SemaphoreType.DMA(())],
    )
    return pl.pallas_call(
        _dispatch_kernel,
        out_shape=jax.ShapeDtypeStruct((n_rows, ROW_W), F32),
        grid_spec=grid_spec,
        compiler_params=_cparams(("arbitrary",)),
        name="dispatch",
    )(pos, zero_tiles, x1ext)


def _moe_kernel(tg_ref, xs_ref, wg_ref, wu_ref, wd_ref, o_ref, xb_ref, acc_ref):
    i = pl.program_id(0)
    e = pl.program_id(1)
    tm = xs_ref.shape[0]
    group = tg_ref[i]
    valid = group < N_GROUPS

    @pl.when(valid)
    def _():
        @pl.when(e == 0)
        def _():
            xb_ref[...] = xs_ref[:, :D_MODEL].astype(BF16)
            acc_ref[...] = jnp.zeros_like(acc_ref)

        lane = lax.broadcasted_iota(jnp.int32, (tm, EXT), 1)
        cw_lane = CW_LANE0 + EXPERTS_PER_GROUP * group + e
        cw = jnp.sum(jnp.where(lane == cw_lane, xs_ref[:, D_MODEL:], 0.0), axis=1, keepdims=True)
        xb = xb_ref[...]
        hg = jnp.dot(xb, wg_ref[...].astype(BF16), preferred_element_type=F32)
        hu = jnp.dot(xb, wu_ref[...].astype(BF16), preferred_element_type=F32)
        hh = (hg * jax.nn.sigmoid(hg) * hu * cw).astype(BF16)
        acc_ref[...] += jnp.dot(hh, wd_ref[...].astype(BF16), preferred_element_type=F32)

        @pl.when(e == EXPERTS_PER_GROUP - 1)
        def _():
            o_ref[...] = DEEPNORM_ALPHA * xs_ref[:, :D_MODEL] + acc_ref[...]

    @pl.when(jnp.logical_not(valid) & (e == EXPERTS_PER_GROUP - 1))
    def _():
        o_ref[...] = jnp.zeros_like(o_ref)


def _moe(tile_group, xs, w_gate, w_up, w_down, layer):
    n_rows = xs.shape[0]
    tm = TM_MOE
    epg = EXPERTS_PER_GROUP

    def expert(i, e, tg):
        return (layer, jnp.where(tg[i] < N_GROUPS, tg[i] * epg + e, N_EXPERTS - 1), 0, 0)

    grid_spec = pltpu.PrefetchScalarGridSpec(
        num_scalar_prefetch=1,
        grid=(n_rows // tm, epg),
        in_specs=[pl.BlockSpec((tm, ROW_W), lambda i, e, tg: (i, 0)),
                  pl.BlockSpec((None, None, D_MODEL, D_EXPERT), expert),
                  pl.BlockSpec((None, None, D_MODEL, D_EXPERT), expert),
                  pl.BlockSpec((None, None, D_EXPERT, D_MODEL), expert)],
        out_specs=pl.BlockSpec((tm, D_MODEL), lambda i, e, tg: (i, 0)),
        scratch_shapes=[pltpu.VMEM((tm, D_MODEL), BF16), pltpu.VMEM((tm, D_MODEL), F32)],
    )
    return pl.pallas_call(
        _moe_kernel,
        out_shape=jax.ShapeDtypeStruct((n_rows, D_MODEL), F32),
        grid_spec=grid_spec,
        compiler_params=_cparams(("parallel", "arbitrary")),
        name="moe",
    )(tile_group, xs, w_gate, w_up, w_down)


def _combine_ple_kernel(pos_ref, x2s_ref, p_ref, g_ref, b_ref, wg_ref, bg_ref, wp_ref, o_ref, buf_a, buf_b, sem):
    tm = buf_a.shape[0]
    i = pl.program_id(0)
    last = pl.num_programs(0) - 1
    bufs = (buf_a, buf_b)

    def wait_buf(s):
        pltpu.make_async_copy(x2s_ref.at[pl.ds(0, tm)], bufs[s], sem.at[s]).wait()

    def gather_inline(tile, s):
        base = tile * tm
        for r in range(tm):
            _row_copy(x2s_ref, pos_ref[base + r], bufs[s], r, sem.at[s]).start(priority=N_DMA_THREADS - 1)

    def compute(s):
        x2 = _layer_norm(bufs[s][...], g_ref[...], b_ref[...])
        rows = slice(s * tm, (s + 1) * tm)
        gate = jax.nn.sigmoid(jnp.dot(x2.astype(BF16), wg_ref[...], preferred_element_type=F32) + bg_ref[...])
        pp = jnp.dot(p_ref[rows, :].astype(BF16), wp_ref[...], preferred_element_type=F32)
        o_ref[rows, :] = x2 + gate * pp

    @pl.when(i == 0)
    def _():
        def issue(r, _):
            _row_copy(x2s_ref, pos_ref[r], buf_a, r, sem.at[0]).start()
            return 0

        lax.fori_loop(0, tm, issue, 0, unroll=8)

    wait_buf(0)
    gather_inline(2 * i + 1, 1)
    compute(0)
    wait_buf(1)
    gather_inline(2 * jnp.minimum(i + 1, last), 0)
    compute(1)

    @pl.when(i == last)
    def _():
        wait_buf(0)


def _combine_ple(pos, x2s, p_all, layer, ln_g, ln_b, w_gate, b_gate, w_proj):
    t = pos.shape[0]
    tm = min(TM_ROWS, t // 2)
    steps = t // (2 * tm)
    grid_spec = pltpu.PrefetchScalarGridSpec(
        num_scalar_prefetch=1,
        grid=(steps,),
        in_specs=[pl.BlockSpec(memory_space=pl.ANY),
                  pl.BlockSpec((2 * tm, PLE_DIM), lambda i, pos: (layer * steps + i, 0)),
                  pl.BlockSpec((1, D_MODEL), lambda i, pos: (0, 0)),
                  pl.BlockSpec((1, D_MODEL), lambda i, pos: (0, 0)),
                  pl.BlockSpec((D_MODEL, D_MODEL), lambda i, pos: (0, 0)),
                  pl.BlockSpec((1, D_MODEL), lambda i, pos: (0, 0)),
                  pl.BlockSpec((PLE_DIM, D_MODEL), lambda i, pos: (0, 0))],
        out_specs=pl.BlockSpec((2 * tm, D_MODEL), lambda i, pos: (i, 0)),
        scratch_shapes=[pltpu.VMEM((tm, D_MODEL), F32), pltpu.VMEM((tm, D_MODEL), F32),
                        pltpu.SemaphoreType.DMA((2,))],
    )
    return pl.pallas_call(
        _combine_ple_kernel,
        out_shape=jax.ShapeDtypeStruct((t, D_MODEL), F32),
        grid_spec=grid_spec,
        compiler_params=_cparams(("arbitrary",)),
        name="combine_ple",
    )(pos, x2s, p_all, ln_g, ln_b, w_gate, b_gate, w_proj)


def _sorted_positions(g_idx, n_tiles):
    onehot = (g_idx[:, None] == jnp.arange(N_GROUPS, dtype=jnp.int32)[None, :]).astype(jnp.int32)
    counts = jnp.sum(onehot, axis=0)
    tiles = (counts + TM_MOE - 1) // TM_MOE
    tile_end = jnp.cumsum(tiles)
    tile_start = tile_end - tiles
    rank = jnp.sum((jnp.cumsum(onehot, axis=0) - onehot) * onehot, axis=1)
    pos = jnp.sum(onehot * (tile_start * TM_MOE)[None, :], axis=1) + rank
    tile_ids = jnp.arange(n_tiles, dtype=jnp.int32)
    tile_group = jnp.sum((tile_ids[:, None] >= tile_end[None, :]).astype(jnp.int32), axis=1)
    trailing = tile_end[-1] + jnp.arange(N_GROUPS, dtype=jnp.int32)
    zero_tiles = jnp.concatenate([jnp.where(tiles > 0, tile_end - 1, -1),
                                  jnp.where(trailing < n_tiles, trailing, -1)])
    return pos.astype(jnp.int32), tile_group.astype(jnp.int32), zero_tiles.astype(jnp.int32)


def _moe_block(a, w_out, x2d, p_all, ln1_g, ln1_b, ln2_g, ln2_b, w_group, b_group, w_router, b_router,
               w_gate, w_up, w_down, ple_w_proj, ple_w_gate, ple_b_gate, layer):
    t = x2d.shape[0]
    n_tiles = t // TM_MOE + N_GROUPS
    pad = LANES - CW_LANE0 - N_EXPERTS
    w_r = jnp.concatenate([w_group, w_router, jnp.zeros((D_MODEL, pad), F32)], axis=1)
    b_r = jnp.concatenate([b_group, b_router, jnp.zeros((pad,), F32)])[None, :]
    x1ext, ext = _out_router(a, w_out.astype(BF16), x2d, ln1_g[None, :], ln1_b[None, :], w_r, b_r)
    g_idx = ext[:, GIDX_LANE].astype(jnp.int32)
    pos, tile_group, zero_tiles = _sorted_positions(g_idx, n_tiles)
    xs = _dispatch(pos, zero_tiles, x1ext, n_tiles * TM_MOE)
    x2s = _moe(tile_group, xs, w_gate, w_up, w_down, layer)
    return _combine_ple(pos, x2s, p_all, layer, ln2_g[None, :], ln2_b[None, :], ple_w_gate.astype(BF16),
                        ple_b_gate[None, :], ple_w_proj.astype(BF16))


@jax.jit
def _forward(x, p, positions, fox_w_in, fox_b_f, fox_w_out, ret_w_in, ret_w_out, ln1_g, ln1_b, ln2_g, ln2_b,
             moe_w_group, moe_b_group, moe_w_router, moe_b_router, moe_w_gate, moe_w_up, moe_w_down,
             ple_w_proj, ple_w_gate, ple_b_gate):
    batch, seq, d = x.shape
    t = batch * seq
    x2d = x.reshape(t, d)
    p_all = p.reshape(DEPTH * t, PLE_DIM)

    def moe_args(i):
        return (ln1_g[i], ln1_b[i], ln2_g[i], ln2_b[i], moe_w_group[i], moe_b_group[i], moe_w_router[i],
                moe_b_router[i], moe_w_gate, moe_w_up, moe_w_down, ple_w_proj[i], ple_w_gate[i],
                ple_b_gate[i], i)

    w_in = fox_w_in[0]
    scale = jnp.concatenate([jnp.full((D_MODEL,), FOX_HEAD_DIM ** -0.5 * LOG2E, F32),
                             jnp.ones((2 * D_MODEL,), F32)])
    w_qkv = (w_in[:, :3 * D_MODEL] * scale[None, :]).astype(BF16)
    qkv = _proj(x2d, w_qkv, D_MODEL)
    c, c_heads = _fgate(x2d, w_in[:, 3 * D_MODEL:], fox_b_f[0][None, :], batch, seq)
    attn = _fox_attention(qkv, c, c_heads, batch, seq)
    x2d = _moe_block(attn, fox_w_out[0], x2d, p_all, *moe_args(0))

    w_in = ret_w_in[0]
    scale = jnp.concatenate([jnp.ones((D_MODEL,), F32), jnp.full((D_MODEL,), RET_KEY_DIM ** -0.5, F32),
                             jnp.ones((4 * D_MODEL,), F32)])
    inv_freq = ROPE_BASE ** (-jnp.arange(0, RET_KEY_DIM, 2, dtype=F32) / RET_KEY_DIM)
    cos, sin = _rope_tables(positions.astype(F32).reshape(t, 1), inv_freq[None, :])
    proj = _ret_proj(x2d, (w_in * scale[None, :]).astype(BF16), cos, sin)
    log_gamma = jnp.log(1.0 - 2.0 ** (-5.0 - jnp.arange(RET_HEADS, dtype=F32)))
    ret = _retention(proj, log_gamma, batch, seq)
    x2d = _moe_block(ret, ret_w_out[0], x2d, p_all, *moe_args(1))
    return x2d.reshape(batch, seq, d)


def kernel(x, p, positions, fox_w_in, fox_b_f, fox_w_out, ret_w_in, ret_w_out, ln1_g, ln1_b, ln2_g, ln2_b,
           moe_w_group, moe_b_group, moe_w_router, moe_b_router, moe_w_gate, moe_w_up, moe_w_down,
           ple_w_proj, ple_w_gate, ple_b_gate):
    return _forward(x, p, positions, fox_w_in, fox_b_f, fox_w_out, ret_w_in, ret_w_out, ln1_g, ln1_b,
                    ln2_g, ln2_b, moe_w_group, moe_b_group, moe_w_router, moe_b_router, moe_w_gate,
                    moe_w_up, moe_w_down, ple_w_proj, ple_w_gate, ple_b_gate)
```
